```python
import jax
import jax.numpy as jnp
from jax import lax
import numpy as np

D_MODEL = 1024
BATCH = 8
SEQ = 4096
DEPTH = 1

D_CONV = D_MODEL // 2
D_LRU = D_MODEL // 2
D_MIX = D_CONV + D_LRU
CONV_WIDTH = 31
N_LRU_HEADS = 8
LRU_HEAD_DIM = D_LRU // N_LRU_HEADS
LRU_CONV_WIDTH = 4
LRU_C = 8.0
N_EXPERTS = 32
TOP_K = 4
D_EXPERT = D_MODEL
SWIGLU_ALPHA = 1.702
SWIGLU_LIMIT = 7.0
ROW_BLOCK = 128
N_MOD = 6
EPS = 1e-6

kernel_name = "hymba_conformer_rglru_moe_adaln"


def rms_norm(x, g):
    xf = x.astype(jnp.float32)
    y = xf * lax.rsqrt(jnp.mean(xf * xf, axis=-1, keepdims=True) + EPS)
    return (y * g.astype(jnp.float32)).astype(x.dtype)


def layer_norm(x, g, b):
    xf = x.astype(jnp.float32)
    mu = jnp.mean(xf, axis=-1, keepdims=True)
    var = jnp.mean(jnp.square(xf - mu), axis=-1, keepdims=True)
    y = (xf - mu) * lax.rsqrt(var + EPS)
    return (y * g.astype(jnp.float32) + b.astype(jnp.float32)).astype(x.dtype)


def causal_depthwise_conv(x, w, b):
    width, ch = w.shape
    xp = jnp.pad(x, ((0, 0), (width - 1, 0), (0, 0)))
    y = lax.conv_general_dilated(
        xp, w[:, None, :].astype(x.dtype), window_strides=(1,), padding="VALID",
        dimension_numbers=("NWC", "WIO", "NWC"), feature_group_count=ch)
    return y + b.astype(x.dtype)


def conformer_conv_group(u, conv_w, conv_b, ln_g, ln_b):
    val, gate = jnp.split(u, 2, axis=-1)
    v = val * jax.nn.sigmoid(gate)
    v = causal_depthwise_conv(v, conv_w, conv_b)
    v = layer_norm(v, ln_g, ln_b)
    return jax.nn.silu(v)


def rglru_group(u_gate, u_rec, conv_w, conv_b, w_a, b_a, w_x, b_x, lam):
    bsz, seq, _ = u_rec.shape
    xr = causal_depthwise_conv(u_rec, conv_w, conv_b)
    xh = xr.reshape(bsz, seq, N_LRU_HEADS, LRU_HEAD_DIM)
    r = jax.nn.sigmoid(jnp.einsum("bshi,hij->bshj", xh, w_a) + b_a).reshape(bsz, seq, D_LRU)
    i = jax.nn.sigmoid(jnp.einsum("bshi,hij->bshj", xh, w_x) + b_x).reshape(bsz, seq, D_LRU)
    log_a = -LRU_C * r.astype(jnp.float32) * jax.nn.softplus(-lam.astype(jnp.float32))
    a = jnp.exp(log_a)
    inp = jnp.sqrt(-jnp.expm1(2.0 * log_a)) * (i * xr).astype(jnp.float32)

    def combine(left, right):
        a1, b1 = left
        a2, b2 = right
        return a1 * a2, a2 * b1 + b2

    _, h = lax.associative_scan(combine, (a, inp), axis=1)
    return jax.nn.gelu(u_gate) * h.astype(u_gate.dtype)


def routed_experts(h, w_router, b_router, w_gate_up, b_gate_up, w_down, b_down):
    bsz, seq, dm = h.shape
    xt = h.reshape(bsz * seq, dm)
    n_tok = xt.shape[0]
    logits = (xt @ w_router + b_router).astype(jnp.float32)
    top_val, top_idx = lax.top_k(logits, TOP_K)
    gates = jax.nn.softmax(top_val, axis=-1)

    n_assign = n_tok * TOP_K
    flat_e = top_idx.reshape(-1)
    flat_g = gates.reshape(-1)
    order = jnp.argsort(flat_e)
    sorted_e = flat_e[order]
    counts = jnp.bincount(flat_e, length=N_EXPERTS)
    starts = jnp.cumsum(counts) - counts
    padded = (counts + ROW_BLOCK - 1) // ROW_BLOCK * ROW_BLOCK
    pad_ends = jnp.cumsum(padded)
    pad_starts = pad_ends - padded
    dest = pad_starts[sorted_e] + (jnp.arange(n_assign) - starts[sorted_e])
    n_rows = ((n_assign + ROW_BLOCK - 1) // ROW_BLOCK + N_EXPERTS) * ROW_BLOCK
    n_blocks = n_rows // ROW_BLOCK
    row_token = jnp.full((n_rows,), n_tok, jnp.int32).at[dest].set((order // TOP_K).astype(jnp.int32))
    row_gate = jnp.zeros((n_rows,), jnp.float32).at[dest].set(flat_g[order])
    block_expert = jnp.minimum(
        jnp.searchsorted(pad_ends, jnp.arange(n_blocks) * ROW_BLOCK, side="right"), N_EXPERTS - 1)

    x_pad = jnp.concatenate([xt, jnp.zeros((1, dm), xt.dtype)], axis=0)
    x_blocks = x_pad[row_token].reshape(n_blocks, ROW_BLOCK, dm)

    def expert_block(args):
        xb, e = args
        gu = xb @ w_gate_up[e] + b_gate_up[e]
        g = jnp.minimum(gu[:, :D_EXPERT], SWIGLU_LIMIT)
        u = jnp.clip(gu[:, D_EXPERT:], -SWIGLU_LIMIT, SWIGLU_LIMIT)
        act = (u + 1.0) * (g * jax.nn.sigmoid(SWIGLU_ALPHA * g))
        return act @ w_down[e] + b_down[e]

    y_blocks = lax.map(expert_block, (x_blocks, block_expert))
    y_rows = y_blocks.reshape(n_rows, dm) * row_gate[:, None].astype(h.dtype)
    y = jnp.zeros((n_tok + 1, dm), h.dtype).at[row_token].add(y_rows)[:n_tok]
    return y.reshape(bsz, seq, dm)


def setup_inputs(seed: int = 0) -> dict:
    key = jax.random.key(seed)
    ks = jax.random.split(key, 28)
    f32 = jnp.float32
    nrm = lambda k, shape, s: jax.random.normal(k, shape, f32) * s
    L = DEPTH
    a_c = jax.random.uniform(ks[16], (L, D_LRU), f32, minval=0.9, maxval=0.999)
    a0 = a_c ** (1.0 / LRU_C)
    lru_lambda = jnp.log(a0) - jnp.log1p(-a0)
    return {
        "x": nrm(ks[0], (BATCH, SEQ, D_MODEL), 1.0),
        "c": nrm(ks[1], (BATCH, D_MODEL), 1.0),
        "w_ada": nrm(ks[2], (L, D_MODEL, N_MOD * D_MODEL), D_MODEL ** -0.5),
        "b_ada": nrm(ks[3], (L, N_MOD * D_MODEL), 0.01),
        "norm1_g": 1.0 + nrm(ks[4], (L, D_MODEL), 0.02),
        "w_in": nrm(ks[5], (L, D_MODEL, 2 * D_MIX), D_MODEL ** -0.5),
        "conv_w": nrm(ks[6], (L, CONV_WIDTH, D_CONV), CONV_WIDTH ** -0.5),
        "conv_b": nrm(ks[7], (L, D_CONV), 0.01),
        "conv_ln_g": 1.0 + nrm(ks[8], (L, D_CONV), 0.02),
        "conv_ln_b": nrm(ks[9], (L, D_CONV), 0.01),
        "lru_conv_w": nrm(ks[10], (L, LRU_CONV_WIDTH, D_LRU), LRU_CONV_WIDTH ** -0.5),
        "lru_conv_b": nrm(ks[11], (L, D_LRU), 0.01),
        "lru_w_a": nrm(ks[12], (L, N_LRU_HEADS, LRU_HEAD_DIM, LRU_HEAD_DIM), LRU_HEAD_DIM ** -0.5),
        "lru_b_a": nrm(ks[13], (L, N_LRU_HEADS, LRU_HEAD_DIM), 0.01),
        "lru_w_x": nrm(ks[14], (L, N_LRU_HEADS, LRU_HEAD_DIM, LRU_HEAD_DIM), LRU_HEAD_DIM ** -0.5),
        "lru_b_x": nrm(ks[15], (L, N_LRU_HEADS, LRU_HEAD_DIM), 0.01),
        "lru_lambda": lru_lambda,
        "mix_norm_g": 1.0 + nrm(ks[17], (L, D_MIX), 0.02),
        "w_out": nrm(ks[18], (L, D_MIX, D_MODEL), D_MIX ** -0.5),
        "norm2_g": 1.0 + nrm(ks[19], (L, D_MODEL), 0.02),
        "w_router": nrm(ks[20], (L, D_MODEL, N_EXPERTS), D_MODEL ** -0.5),
        "b_router": nrm(ks[21], (L, N_EXPERTS), 0.01),
        "w_gate_up": nrm(ks[22], (L, N_EXPERTS, D_MODEL, 2 * D_EXPERT), D_MODEL ** -0.5),
        "b_gate_up": nrm(ks[23], (L, N_EXPERTS, 2 * D_EXPERT), 0.01),
        "w_down": nrm(ks[24], (L, N_EXPERTS, D_EXPERT, D_MODEL), D_EXPERT ** -0.5),
        "b_down": nrm(ks[25], (L, N_EXPERTS, D_MODEL), 0.01),
        "final_norm_g": 1.0 + nrm(ks[26], (D_MODEL,), 0.02),
    }


def reference(x, c, w_ada, b_ada, norm1_g, w_in, conv_w, conv_b, conv_ln_g, conv_ln_b,
              lru_conv_w, lru_conv_b, lru_w_a, lru_b_a, lru_w_x, lru_b_x, lru_lambda,
              mix_norm_g, w_out, norm2_g, w_router, b_router, w_gate_up, b_gate_up,
              w_down, b_down, final_norm_g):
    bsz, seq, dm = x.shape
    c_act = jax.nn.silu(c)
    for l in range(DEPTH):
        mod = (c_act @ w_ada[l] + b_ada[l]).reshape(bsz, N_MOD, 1, dm)
        shift1, scale1, gate1, shift2, scale2, gate2 = [mod[:, j] for j in range(N_MOD)]

        h = rms_norm(x, norm1_g[l]) * (1.0 + scale1) + shift1
        u = h @ w_in[l]
        u_conv = u[..., :2 * D_CONV]
        u_lru_gate = u[..., 2 * D_CONV:2 * D_CONV + D_LRU]
        u_lru_rec = u[..., 2 * D_CONV + D_LRU:]
        y_conv = conformer_conv_group(u_conv, conv_w[l], conv_b[l], conv_ln_g[l], conv_ln_b[l])
        y_lru = rglru_group(u_lru_gate, u_lru_rec, lru_conv_w[l], lru_conv_b[l], lru_w_a[l],
                            lru_b_a[l], lru_w_x[l], lru_b_x[l], lru_lambda[l])
        y = jnp.concatenate([y_conv, y_lru], axis=-1).reshape(bsz, seq, 2, D_MIX // 2)
        y = rms_norm(y, mix_norm_g[l].reshape(2, D_MIX // 2)).reshape(bsz, seq, D_MIX)
        x = x + gate1 * (y @ w_out[l])

        h = rms_norm(x, norm2_g[l]) * (1.0 + scale2) + shift2
        x = x + gate2 * routed_experts(h, w_router[l], b_router[l], w_gate_up[l], b_gate_up[l],
                                       w_down[l], b_down[l])
    return rms_norm(x, final_norm_g)
```

```python
import functools

import jax
import jax.numpy as jnp
from jax import lax
from jax.experimental import pallas as pl
from jax.experimental.pallas import tpu as pltpu

F32 = jnp.float32
BF16 = jnp.bfloat16
I32 = jnp.int32

EPS = 1e-6
N_MOD = 6
LRU_C = 8.0
TOP_K = 4
SWIGLU_ALPHA = 1.702
SWIGLU_LIMIT = 7.0

LANES = 128
SUBLANES = 8
VMEM_LIMIT_BYTES = 56 * 1024 * 1024

SEQ_TILE = 512
CONV_PAD = 32
MOE_BLOCK = 512
HI_MASK = -65536


def _sigmoid(x):
    return jax.nn.sigmoid(x)


def _pack_bf16_pair(lo_f32, hi_f32):
    lo_bits = lax.bitcast_convert_type(lo_f32.astype(BF16).astype(F32), I32)
    hi_bits = lax.bitcast_convert_type(hi_f32.astype(BF16).astype(F32), I32)
    return lax.shift_right_logical(lo_bits, 16) | hi_bits


def _unpack_bf16_pair(p):
    lo = lax.bitcast_convert_type(lax.shift_left(p, 16), F32)
    hi = lax.bitcast_convert_type(p & HI_MASK, F32)
    return lo, hi


def _ada_kernel(c_ref, w_ref, b_ref, o_ref):
    c = c_ref[...]
    ca = c * _sigmoid(c)
    w = w_ref[...]
    c_hi = ca.astype(BF16)
    c_lo = (ca - c_hi.astype(F32)).astype(BF16)
    w_hi = w.astype(BF16)
    w_lo = (w - w_hi.astype(F32)).astype(BF16)
    acc = jnp.dot(c_hi, w_hi, preferred_element_type=F32)
    acc += jnp.dot(c_lo, w_hi, preferred_element_type=F32)
    acc += jnp.dot(c_hi, w_lo, preferred_element_type=F32)
    o_ref[...] = acc + b_ref[...]


def _ada_call(c, w_ada, b_ada):
    bsz, dm = c.shape
    n_out = w_ada.shape[1]
    tn = 1024
    return pl.pallas_call(
        _ada_kernel,
        grid=(n_out // tn,),
        in_specs=[
            pl.BlockSpec((bsz, dm), lambda n: (0, 0)),
            pl.BlockSpec((dm, tn), lambda n: (0, n)),
            pl.BlockSpec((1, tn), lambda n: (0, n)),
        ],
        out_specs=pl.BlockSpec((bsz, tn), lambda n: (0, n)),
        out_shape=jax.ShapeDtypeStruct((bsz, n_out), F32),
        compiler_params=pltpu.CompilerParams(dimension_semantics=("arbitrary",)),
        name="ada_mod",
    )(c, w_ada, b_ada.reshape(1, n_out))


def _rms(x, eps=EPS):
    return x * lax.rsqrt(jnp.mean(x * x, axis=-1, keepdims=True) + eps)


def _gelu_tanh(x):
    return 0.5 * x * (1.0 + jnp.tanh(0.7978845608028654 * (x + 0.044715 * (x * x * x))))


def _mix_kernel(x_ref, mod_ref, g1_ref, win_ref, cw_ref, cb_ref, lng_ref, lnb_ref,
                lcw_ref, lcb_ref, wg_ref, bg_ref, lam_ref, mng_ref, wout_ref, g2_ref,
                wrt_ref, br_ref,
                x1_ref, h2p_ref, idx_ref, rank_ref, gt_ref, cnt_ref,
                vbuf, rbuf, a_s, b_s, hcar, cnt_s,
                *, ts, d_conv, d_lru, conv_w, lru_cw, n_exp):
    b = pl.program_id(0)
    j = pl.program_id(1)

    @pl.when(j == 0)
    def _():
        vbuf[pl.ds(0, CONV_PAD), :] = jnp.zeros((CONV_PAD, d_conv), F32)
        rbuf[pl.ds(0, SUBLANES), :] = jnp.zeros((SUBLANES, d_lru), F32)
        hcar[...] = jnp.zeros_like(hcar)

    @pl.when((b == 0) & (j == 0))
    def _():
        cnt_s[...] = jnp.zeros_like(cnt_s)

    x = x_ref[...]
    mod = mod_ref[...]
    shift1, scale1, gate1 = mod[0:1], mod[1:2], mod[2:3]
    shift2, scale2, gate2 = mod[3:4], mod[4:5], mod[5:6]
    del gate2

    h = _rms(x) * g1_ref[...] * (1.0 + scale1) + shift1
    u = jnp.dot(h.astype(BF16), win_ref[...], preferred_element_type=F32)

    v = u[:, :d_conv] * _sigmoid(u[:, d_conv:2 * d_conv])
    vbuf[pl.ds(CONV_PAD, ts), :] = v
    acc = jnp.broadcast_to(cb_ref[...], (ts, d_conv))
    for k in range(conv_w):
        acc = acc + cw_ref[pl.ds(k, 1), :] * vbuf[pl.ds(CONV_PAD - (conv_w - 1) + k, ts), :]
    vbuf[pl.ds(0, CONV_PAD), :] = vbuf[pl.ds(ts, CONV_PAD), :]
    mu = jnp.mean(acc, axis=-1, keepdims=True)
    cen = acc - mu
    var = jnp.mean(cen * cen, axis=-1, keepdims=True)
    yc = cen * lax.rsqrt(var + EPS) * lng_ref[...] + lnb_ref[...]
    yc = yc * _sigmoid(yc)

    u_gate = u[:, 2 * d_conv:2 * d_conv + d_lru]
    rbuf[pl.ds(SUBLANES, ts), :] = u[:, 2 * d_conv + d_lru:]
    xr = jnp.broadcast_to(lcb_ref[...], (ts, d_lru))
    for k in range(lru_cw):
        xr = xr + lcw_ref[pl.ds(k, 1), :] * rbuf[pl.ds(SUBLANES - (lru_cw - 1) + k, ts), :]
    rbuf[pl.ds(0, SUBLANES), :] = rbuf[pl.ds(ts, SUBLANES), :]
    gates = jnp.dot(xr.astype(BF16), wg_ref[...], preferred_element_type=F32) + bg_ref[...]
    r = _sigmoid(gates[:, :d_lru])
    i_g = _sigmoid(gates[:, d_lru:])
    lam = lam_ref[...]
    softplus_neg_lam = jnp.maximum(-lam, 0.0) + jnp.log1p(jnp.exp(-jnp.abs(lam)))
    log_a = (-LRU_C) * r * softplus_neg_lam
    a = jnp.exp(log_a)
    inp = jnp.sqrt(1.0 - jnp.exp(2.0 * log_a)) * (i_g * xr)

    n_grp = ts // SUBLANES
    a3 = a.reshape(n_grp, SUBLANES, d_lru)
    b3 = inp.reshape(n_grp, SUBLANES, d_lru)
    sub = lax.broadcasted_iota(I32, (n_grp, SUBLANES, d_lru), 1)
    for s in (1, 2, 4):
        a_sh = pltpu.roll(a3, s, axis=1)
        b_sh = pltpu.roll(b3, s, axis=1)
        m = sub >= s
        b3 = jnp.where(m, a3 * b_sh + b3, b3)
        a3 = jnp.where(m, a3 * a_sh, a3)
    a_s[...] = a3.reshape(ts, d_lru)
    b_s[...] = b3.reshape(ts, d_lru)

    def grp_body(g, carry):
        rows = pl.ds(pl.multiple_of(g * SUBLANES, SUBLANES), SUBLANES)
        hg = b_s[rows, :] + a_s[rows, :] * carry
        b_s[rows, :] = hg
        return jnp.broadcast_to(hg[SUBLANES - 1:SUBLANES, :], (SUBLANES, d_lru))

    hcar[...] = lax.fori_loop(0, n_grp, grp_body, hcar[...])
    yl = _gelu_tanh(u_gate) * b_s[...]

    mng = mng_ref[...]
    yc_n = _rms(yc) * mng[:, :d_conv]
    yl_n = _rms(yl) * mng[:, d_conv:]
    mixo = jnp.dot(yc_n.astype(BF16), wout_ref[pl.ds(0, d_conv), :], preferred_element_type=F32)
    mixo += jnp.dot(yl_n.astype(BF16), wout_ref[pl.ds(d_conv, d_lru), :], preferred_element_type=F32)
    x1 = x + gate1 * mixo
    x1_ref[...] = x1

    h2 = _rms(x1) * g2_ref[...] * (1.0 + scale2) + shift2
    dh = h2.shape[1] // 2
    h2_hi = h2.astype(BF16)
    h2p_ref[...] = _pack_bf16_pair(h2[:, :dh], h2[:, dh:])
    h2_lo = (h2 - h2_hi.astype(F32)).astype(BF16)
    nt_dims = (((1,), (1,)), ((), ()))
    wrt = wrt_ref[...]
    lg = lax.dot_general(wrt, h2_hi, nt_dims, preferred_element_type=F32)
    lg2 = lax.dot_general(wrt[:n_exp], h2_lo, nt_dims, preferred_element_type=F32)
    logits = lg[:n_exp] + lg[n_exp:] + lg2 + br_ref[...]

    eidx = lax.broadcasted_iota(I32, (n_exp, ts), 0)
    neg_inf = jnp.float32(-jnp.inf)
    work = logits
    vals, idxs, hots = [], [], []
    for _ in range(TOP_K):
        mval = jnp.max(work, axis=0, keepdims=True)
        midx = jnp.min(jnp.where(work == mval, eidx, n_exp), axis=0, keepdims=True)
        hot = eidx == midx
        vals.append(mval)
        idxs.append(midx)
        hots.append(hot)
        work = jnp.where(hot, neg_inf, work)
    exps = [jnp.exp(vk - vals[0]) for vk in vals]
    denom = exps[0] + exps[1] + exps[2] + exps[3]
    gate_rows = [ek / denom for ek in exps]

    sel = jnp.zeros((n_exp, ts), F32)
    for hot in hots:
        sel = sel + hot.astype(F32)
    tri = (lax.broadcasted_iota(I32, (ts, ts), 0) < lax.broadcasted_iota(I32, (ts, ts), 1)).astype(BF16)
    before = jnp.dot(sel.astype(BF16), tri, preferred_element_type=F32) + cnt_s[...]
    rank_rows = [jnp.sum(jnp.where(hot, before, 0.0), axis=0, keepdims=True) for hot in hots]
    cnt_s[...] = cnt_s[...] + jnp.sum(sel, axis=1, keepdims=True)
    cnt_ref[...] = cnt_s[:, :LANES]

    idx_ref[...] = jnp.concatenate(idxs, axis=0)
    rank_ref[...] = jnp.concatenate(rank_rows, axis=0).astype(I32)
    g4 = jnp.concatenate(gate_rows, axis=0)
    g_pad = jnp.concatenate([g4, jnp.zeros((LANES - TOP_K, ts), F32)], axis=0)
    gt_ref[...] = g_pad.T


def _mix_call(x, mod, p):
    bsz, seq, dm = x.shape
    ts = min(SEQ_TILE, seq)
    nj = seq // ts
    d_conv = p["conv_w"].shape[1]
    d_lru = p["lru_conv_w"].shape[1]
    conv_w = p["conv_w"].shape[0]
    lru_cw = p["lru_conv_w"].shape[0]
    n_exp = p["wrt"].shape[0] // 2
    assert seq % ts == 0 and ts % LANES == 0 and conv_w - 1 <= CONV_PAD and lru_cw - 1 <= SUBLANES

    def full(a):
        return pl.BlockSpec(a.shape, lambda b, j: (0,) * a.ndim)

    weights = [p["g1"], p["win"], p["conv_w"], p["conv_b"], p["ln_g"], p["ln_b"], p["lru_conv_w"],
               p["lru_conv_b"], p["wg"], p["bg"], p["lam"], p["mng"], p["wout"], p["g2"], p["wrt"], p["br"]]
    kern = functools.partial(_mix_kernel, ts=ts, d_conv=d_conv, d_lru=d_lru, conv_w=conv_w,
                             lru_cw=lru_cw, n_exp=n_exp)
    out_shape = (
        jax.ShapeDtypeStruct((bsz, seq, dm), F32),
        jax.ShapeDtypeStruct((bsz, seq, dm // 2), I32),
        jax.ShapeDtypeStruct((bsz * nj, TOP_K, ts), I32),
        jax.ShapeDtypeStruct((bsz * nj, TOP_K, ts), I32),
        jax.ShapeDtypeStruct((bsz, seq, LANES), F32),
        jax.ShapeDtypeStruct((n_exp, LANES), F32),
    )
    return pl.pallas_call(
        kern,
        grid=(bsz, nj),
        in_specs=[pl.BlockSpec((None, ts, dm), lambda b, j: (b, j, 0)),
                  pl.BlockSpec((None, N_MOD, dm), lambda b, j: (b, 0, 0))] + [full(w) for w in weights],
        out_specs=(
            pl.BlockSpec((None, ts, dm), lambda b, j: (b, j, 0)),
            pl.BlockSpec((None, ts, dm // 2), lambda b, j: (b, j, 0)),
            pl.BlockSpec((None, TOP_K, ts), lambda b, j: (b * nj + j, 0, 0)),
            pl.BlockSpec((None, TOP_K, ts), lambda b, j: (b * nj + j, 0, 0)),
            pl.BlockSpec((None, ts, LANES), lambda b, j: (b, j, 0)),
            pl.BlockSpec((n_exp, LANES), lambda b, j: (0, 0)),
        ),
        out_shape=out_shape,
        scratch_shapes=[
            pltpu.VMEM((CONV_PAD + ts, d_conv), F32),
            pltpu.VMEM((SUBLANES + ts, d_lru), F32),
            pltpu.VMEM((ts, d_lru), F32),
            pltpu.VMEM((ts, d_lru), F32),
            pltpu.VMEM((SUBLANES, d_lru), F32),
            pltpu.VMEM((n_exp, ts), F32),
        ],
        compiler_params=pltpu.CompilerParams(dimension_semantics=("arbitrary", "arbitrary"),
                                             vmem_limit_bytes=VMEM_LIMIT_BYTES),
        name="token_mix_route",
    )(x, mod, *weights)


def _dispatch_kernel(dest_ref, h_ref, xs_ref, sem, *, ts):
    def row_copy(t, k):
        return pltpu.make_async_copy(h_ref.at[pl.ds(t, 1), :], xs_ref.at[pl.ds(dest_ref[k, t], 1), :], sem)

    def issue(t, c):
        for k in range(TOP_K):
            row_copy(t, k).start()
        return c

    lax.fori_loop(0, ts, issue, 0)

    def drain(t, c):
        for k in range(TOP_K):
            row_copy(t, k).wait()
        return c

    lax.fori_loop(0, ts, drain, 0)


def _dispatch_call(h2p, dest, n_rows):
    n_tok, dh = h2p.shape
    n_tiles, _, ts = dest.shape
    return pl.pallas_call(
        functools.partial(_dispatch_kernel, ts=ts),
        grid=(n_tiles,),
        in_specs=[pl.BlockSpec((None, TOP_K, ts), lambda i: (i, 0, 0), memory_space=pltpu.SMEM),
                  pl.BlockSpec((ts, dh), lambda i: (i, 0))],
        out_specs=pl.BlockSpec(memory_space=pl.ANY),
        out_shape=jax.ShapeDtypeStruct((n_rows, dh), I32),
        scratch_shapes=[pltpu.SemaphoreType.DMA],
        compiler_params=pltpu.CompilerParams(dimension_semantics=("arbitrary",)),
        name="moe_dispatch",
    )(dest, h2p)


def _moe_kernel(be_ref, nb_ref, xs_ref, wgu_ref, bgu_ref, wd_ref, bd_ref, ys_ref, *, d_ff):
    del be_ref

    @pl.when(pl.program_id(0) < nb_ref[0])
    def _():
        x_lo, x_hi = _unpack_bf16_pair(xs_ref[...])
        dh = x_lo.shape[1]
        gu = jnp.dot(x_lo.astype(BF16), wgu_ref[pl.ds(0, dh), :], preferred_element_type=F32)
        gu += jnp.dot(x_hi.astype(BF16), wgu_ref[pl.ds(dh, dh), :], preferred_element_type=F32)
        gu += bgu_ref[...]
        g = jnp.minimum(gu[:, :d_ff], SWIGLU_LIMIT)
        u = jnp.clip(gu[:, d_ff:], -SWIGLU_LIMIT, SWIGLU_LIMIT)
        act = (u + 1.0) * (g * _sigmoid(SWIGLU_ALPHA * g))
        y = jnp.dot(act.astype(BF16), wd_ref[...], preferred_element_type=F32) + bd_ref[...]
        dm2 = y.shape[1] // 2
        ys_ref[...] = _pack_bf16_pair(y[:, :dm2], y[:, dm2:])


def _moe_call(xs, block_expert, n_used, wgu, bgu, wd, bd):
    n_rows, dh = xs.shape
    n_exp, dm, d_ff2 = wgu.shape
    d_ff = d_ff2 // 2
    bm = MOE_BLOCK
    n_blocks = n_rows // bm

    def row_map(i, be, nb):
        return (jnp.minimum(i, nb[0] - 1), 0)

    def w_map(i, be, nb):
        return (be[i], 0, 0)

    grid_spec = pltpu.PrefetchScalarGridSpec(
        num_scalar_prefetch=2,
        grid=(n_blocks,),
        in_specs=[
            pl.BlockSpec((bm, dh), row_map),
            pl.BlockSpec((None, dm, d_ff2), w_map),
            pl.BlockSpec((None, 1, d_ff2), w_map),
            pl.BlockSpec((None, d_ff, dm), w_map),
            pl.BlockSpec((None, 1, dm), w_map),
        ],
        out_specs=pl.BlockSpec((bm, dm // 2), row_map),
    )
    return pl.pallas_call(
        functools.partial(_moe_kernel, d_ff=d_ff),
        grid_spec=grid_spec,
        out_shape=jax.ShapeDtypeStruct((n_rows, dm // 2), I32),
        compiler_params=pltpu.CompilerParams(dimension_semantics=("arbitrary",),
                                             vmem_limit_bytes=VMEM_LIMIT_BYTES),
        name="moe_experts",
    )(block_expert, n_used, xs, wgu, bgu, wd, bd)


def _combine_kernel(dest_ref, x1_ref, gt_ref, mod_ref, gf_ref, ys_ref, o_ref, ybuf, sem, *, ts):
    def row_copy(t, k):
        return pltpu.make_async_copy(ys_ref.at[pl.ds(dest_ref[k, t], 1), :],
                                     ybuf.at[pl.ds(k * ts + t, 1), :], sem)

    def issue(t, c):
        for k in range(TOP_K):
            row_copy(t, k).start()
        return c

    lax.fori_loop(0, ts, issue, 0)

    def drain(t, c):
        for k in range(TOP_K):
            row_copy(t, k).wait()
        return c

    lax.fori_loop(0, ts, drain, 0)

    gt = gt_ref[...]
    dh = ybuf.shape[1]
    acc_lo = jnp.zeros((ts, dh), F32)
    acc_hi = jnp.zeros((ts, dh), F32)
    for k in range(TOP_K):
        lo, hi = _unpack_bf16_pair(ybuf[pl.ds(k * ts, ts), :])
        gk = gt[:, k:k + 1]
        acc_lo += gk * lo
        acc_hi += gk * hi
    y = jnp.concatenate([acc_lo, acc_hi], axis=1)
    gate2 = mod_ref[...][5:6]
    x2 = x1_ref[...] + gate2 * y
    o_ref[...] = _rms(x2) * gf_ref[...]


def _combine_call(x1, gt, mod, gf, ys, dest):
    bsz, seq, dm = x1.shape
    n_tiles, _, ts = dest.shape
    nj = seq // ts
    return pl.pallas_call(
        functools.partial(_combine_kernel, ts=ts),
        grid=(bsz, nj),
        in_specs=[
            pl.BlockSpec((None, TOP_K, ts), lambda b, j: (b * nj + j, 0, 0), memory_space=pltpu.SMEM),
            pl.BlockSpec((None, ts, dm), lambda b, j: (b, j, 0)),
            pl.BlockSpec((None, ts, LANES), lambda b, j: (b, j, 0)),
            pl.BlockSpec((None, N_MOD, dm), lambda b, j: (b, 0, 0)),
            pl.BlockSpec((1, dm), lambda b, j: (0, 0)),
            pl.BlockSpec(memory_space=pl.ANY),
        ],
        out_specs=pl.BlockSpec((None, ts, dm), lambda b, j: (b, j, 0)),
        out_shape=jax.ShapeDtypeStruct((bsz, seq, dm), F32),
        scratch_shapes=[pltpu.VMEM((TOP_K * ts, dm // 2), I32), pltpu.SemaphoreType.DMA],
        compiler_params=pltpu.CompilerParams(dimension_semantics=("arbitrary", "arbitrary"),
                                             vmem_limit_bytes=VMEM_LIMIT_BYTES),
        name="moe_combine",
    )(dest, x1, gt, mod, gf, ys)


def _block_diag(w):
    n_h, d, _ = w.shape
    eye = jnp.eye(n_h, dtype=w.dtype)
    return (eye[:, None, :, None] * w[:, :, None, :]).reshape(n_h * d, n_h * d)


def _layer(x, mod, l, w_in, norm1_g, conv_w, conv_b, conv_ln_g, conv_ln_b, lru_conv_w, lru_conv_b,
           lru_w_a, lru_b_a, lru_w_x, lru_b_x, lru_lambda, mix_norm_g, w_out, norm2_g, w_router,
           b_router, w_gate_up, b_gate_up, w_down, b_down, out_gain):
    bsz, seq, dm = x.shape
    n_tok = bsz * seq
    n_exp = w_router.shape[-1]
    row = lambda a: a.reshape(1, -1)
    wr = w_router[l]
    wr_hi = wr.astype(BF16)
    wr_lo = (wr - wr_hi.astype(F32)).astype(BF16)
    params = dict(
        g1=row(norm1_g[l]), win=w_in[l].astype(BF16), conv_w=conv_w[l], conv_b=row(conv_b[l]),
        ln_g=row(conv_ln_g[l]), ln_b=row(conv_ln_b[l]), lru_conv_w=lru_conv_w[l], lru_conv_b=row(lru_conv_b[l]),
        wg=jnp.concatenate([_block_diag(lru_w_a[l]), _block_diag(lru_w_x[l])], axis=1).astype(BF16),
        bg=jnp.concatenate([lru_b_a[l].reshape(1, -1), lru_b_x[l].reshape(1, -1)], axis=1),
        lam=row(lru_lambda[l]), mng=row(mix_norm_g[l]), wout=w_out[l].astype(BF16), g2=row(norm2_g[l]),
        wrt=jnp.concatenate([wr_hi.T, wr_lo.T], axis=0), br=b_router[l].reshape(n_exp, 1),
    )
    x1, h2p, top_idx, rank, gt, cnt = _mix_call(x, mod, params)

    bm = MOE_BLOCK
    counts = cnt[:, 0].astype(I32)
    padded = (counts + bm - 1) // bm * bm
    pad_ends = jnp.cumsum(padded)
    pad_starts = pad_ends - padded
    n_rows = (n_tok * TOP_K // bm + n_exp) * bm
    n_blocks = n_rows // bm
    dest = pad_starts[top_idx] + rank
    block_expert = jnp.minimum(
        jnp.searchsorted(pad_ends, jnp.arange(n_blocks, dtype=I32) * bm, side="right"), n_exp - 1).astype(I32)
    n_used = (pad_ends[-1:] // bm).astype(I32)

    xs = _dispatch_call(h2p.reshape(n_tok, dm // 2), dest, n_rows)
    ys = _moe_call(xs, block_expert, n_used, w_gate_up[l].astype(BF16),
                   b_gate_up[l][:, None, :], w_down[l].astype(BF16), b_down[l][:, None, :])
    return _combine_call(x1, gt, mod, out_gain, ys, dest)


def kernel(x, c, w_ada, b_ada, norm1_g, w_in, conv_w, conv_b, conv_ln_g, conv_ln_b, lru_conv_w, lru_conv_b,
           lru_w_a, lru_b_a, lru_w_x, lru_b_x, lru_lambda, mix_norm_g, w_out, norm2_g, w_router, b_router,
           w_gate_up, b_gate_up, w_down, b_down, final_norm_g):
    depth = w_ada.shape[0]
    assert depth == 1, "the final norm is fused into the (single) layer's combine kernel"
    bsz, seq, dm = x.shape
    mod = _ada_call(c, w_ada[0], b_ada[0]).reshape(bsz, N_MOD, dm)
    return _layer(x, mod, 0, w_in, norm1_g, conv_w, conv_b, conv_ln_g, conv_ln_b, lru_conv_w, lru_conv_b,
                  lru_w_a, lru_b_a, lru_w_x, lru_b_x, lru_lambda, mix_norm_g, w_out, norm2_g, w_router,
                  b_router, w_gate_up, b_gate_up, w_down, b_down, row_gain(final_norm_g))


def row_gain(g):
    return g.reshape(1, -1)
```

```python
import functools

import jax
import jax.numpy as jnp
from jax import lax
from jax.experimental import pallas as pl
from jax.experimental.pallas import tpu as pltpu

F32 = jnp.float32
BF16 = jnp.bfloat16
I32 = jnp.int32

EPS = 1e-6
N_MOD = 6
LRU_C = 8.0
TOP_K = 4
SWIGLU_ALPHA = 1.702
SWIGLU_LIMIT = 7.0

LANES = 128
SUBLANES = 8
VMEM_LIMIT_BYTES = 56 * 1024 * 1024

SEQ_TILE = 512
CONV_PAD = 32
MOE_BLOCK = 512
HI_MASK = -65536


def _sigmoid(x):
    return jax.nn.sigmoid(x)


def _pack_bf16_pair(lo_f32, hi_f32):
    lo_bits = lax.bitcast_convert_type(lo_f32.astype(BF16).astype(F32), I32)
    hi_bits = lax.bitcast_convert_type(hi_f32.astype(BF16).astype(F32), I32)
    return lax.shift_right_logical(lo_bits, 16) | hi_bits


def _unpack_bf16_pair(p):
    lo = lax.bitcast_convert_type(lax.shift_left(p, 16), F32)
    hi = lax.bitcast_convert_type(p & HI_MASK, F32)
    return lo, hi


def _ada_kernel(c_ref, w_ref, b_ref, o_ref):
    c = c_ref[...]
    ca = c * _sigmoid(c)
    w = w_ref[...]
    c_hi = ca.astype(BF16)
    c_lo = (ca - c_hi.astype(F32)).astype(BF16)
    w_hi = w.astype(BF16)
    w_lo = (w - w_hi.astype(F32)).astype(BF16)
    acc = jnp.dot(c_hi, w_hi, preferred_element_type=F32)
    acc += jnp.dot(c_lo, w_hi, preferred_element_type=F32)
    acc += jnp.dot(c_hi, w_lo, preferred_element_type=F32)
    o_ref[...] = acc + b_ref[...]


def _ada_call(c, w_ada, b_ada):
    bsz, dm = c.shape
    n_out = w_ada.shape[1]
    tn = 1024
    return pl.pallas_call(
        _ada_kernel,
        grid=(n_out // tn,),
        in_specs=[
            pl.BlockSpec((bsz, dm), lambda n: (0, 0)),
            pl.BlockSpec((dm, tn), lambda n: (0, n)),
            pl.BlockSpec((1, tn), lambda n: (0, n)),
        ],
        out_specs=pl.BlockSpec((bsz, tn), lambda n: (0, n)),
        out_shape=jax.ShapeDtypeStruct((bsz, n_out), F32),
        compiler_params=pltpu.CompilerParams(dimension_semantics=("arbitrary",)),
        name="ada_mod",
    )(c, w_ada, b_ada.reshape(1, n_out))


def _rms(x, eps=EPS):
    return x * lax.rsqrt(jnp.mean(x * x, axis=-1, keepdims=True) + eps)


def _gelu_tanh(x):
    return 0.5 * x * (1.0 + jnp.tanh(0.7978845608028654 * (x + 0.044715 * (x * x * x))))


def _mix_kernel(x_ref, mod_ref, g1_ref, win_ref, cw_ref, cb_ref, lng_ref, lnb_ref,
                lcw_ref, lcb_ref, wg_ref, bg_ref, lam_ref, mng_ref, wout_ref, g2_ref,
                wrt_ref, br_ref,
                x1_ref, h2p_ref, idx_ref, rank_ref, gt_ref, cnt_ref,
                vbuf, rbuf, a_s, b_s, hcar, cnt_s,
                *, ts, d_conv, d_lru, conv_w, lru_cw, n_exp):
    b = pl.program_id(0)
    j = pl.program_id(1)

    @pl.when(j == 0)
    def _():
        vbuf[pl.ds(0, CONV_PAD), :] = jnp.zeros((CONV_PAD, d_conv), F32)
        rbuf[pl.ds(0, SUBLANES), :] = jnp.zeros((SUBLANES, d_lru), F32)
        hcar[...] = jnp.zeros_like(hcar)

    @pl.when((b == 0) & (j == 0))
    def _():
        cnt_s[...] = jnp.zeros_like(cnt_s)

    x = x_ref[...]
    mod = mod_ref[...]
    shift1, scale1, gate1 = mod[0:1], mod[1:2], mod[2:3]
    shift2, scale2, gate2 = mod[3:4], mod[4:5], mod[5:6]
    del gate2

    h = _rms(x) * g1_ref[...] * (1.0 + scale1) + shift1
    u = jnp.dot(h.astype(BF16), win_ref[...], preferred_element_type=F32)

    v = u[:, :d_conv] * _sigmoid(u[:, d_conv:2 * d_conv])
    vbuf[pl.ds(CONV_PAD, ts), :] = v
    acc = jnp.broadcast_to(cb_ref[...], (ts, d_conv))
    for k in range(conv_w):
        acc = acc + cw_ref[pl.ds(k, 1), :] * vbuf[pl.ds(CONV_PAD - (conv_w - 1) + k, ts), :]
    vbuf[pl.ds(0, CONV_PAD), :] = vbuf[pl.ds(ts, CONV_PAD), :]
    mu = jnp.mean(acc, axis=-1, keepdims=True)
    cen = acc - mu
    var = jnp.mean(cen * cen, axis=-1, keepdims=True)
    yc = cen * lax.rsqrt(var + EPS) * lng_ref[...] + lnb_ref[...]
    yc = yc * _sigmoid(yc)

    u_gate = u[:, 2 * d_conv:2 * d_conv + d_lru]
    rbuf[pl.ds(SUBLANES, ts), :] = u[:, 2 * d_conv + d_lru:]
    xr = jnp.broadcast_to(lcb_ref[...], (ts, d_lru))
    for k in range(lru_cw):
        xr = xr + lcw_ref[pl.ds(k, 1), :] * rbuf[pl.ds(SUBLANES - (lru_cw - 1) + k, ts), :]
    rbuf[pl.ds(0, SUBLANES), :] = rbuf[pl.ds(ts, SUBLANES), :]
    gates = jnp.dot(xr.astype(BF16), wg_ref[...], preferred_element_type=F32) + bg_ref[...]
    r = _sigmoid(gates[:, :d_lru])
    i_g = _sigmoid(gates[:, d_lru:])
    lam = lam_ref[...]
    softplus_neg_lam = jnp.maximum(-lam, 0.0) + jnp.log1p(jnp.exp(-jnp.abs(lam)))
    log_a = (-LRU_C) * r * softplus_neg_lam
    a = jnp.exp(log_a)
    inp = jnp.sqrt(1.0 - jnp.exp(2.0 * log_a)) * (i_g * xr)

    n_grp = ts // SUBLANES
    a3 = a.reshape(n_grp, SUBLANES, d_lru)
    b3 = inp.reshape(n_grp, SUBLANES, d_lru)
    sub = lax.broadcasted_iota(I32, (n_grp, SUBLANES, d_lru), 1)
    for s in (1, 2, 4):
        a_sh = pltpu.roll(a3, s, axis=1)
        b_sh = pltpu.roll(b3, s, axis=1)
        m = sub >= s
        b3 = jnp.where(m, a3 * b_sh + b3, b3)
        a3 = jnp.where(m, a3 * a_sh, a3)
    a_s[...] = a3.reshape(ts, d_lru)
    b_s[...] = b3.reshape(ts, d_lru)

    def grp_body(g, carry):
        rows = pl.ds(pl.multiple_of(g * SUBLANES, SUBLANES), SUBLANES)
        hg = b_s[rows, :] + a_s[rows, :] * carry
        b_s[rows, :] = hg
        return jnp.broadcast_to(hg[SUBLANES - 1:SUBLANES, :], (SUBLANES, d_lru))

    hcar[...] = lax.fori_loop(0, n_grp, grp_body, hcar[...])
    yl = _gelu_tanh(u_gate) * b_s[...]

    mng = mng_ref[...]
    yc_n = _rms(yc) * mng[:, :d_conv]
    yl_n = _rms(yl) * mng[:, d_conv:]
    mixo = jnp.dot(yc_n.astype(BF16), wout_ref[pl.ds(0, d_conv), :], preferred_element_type=F32)
    mixo += jnp.dot(yl_n.astype(BF16), wout_ref[pl.ds(d_conv, d_lru), :], preferred_element_type=F32)
    x1 = x + gate1 * mixo
    x1_ref[...] = x1

    h2 = _rms(x1) * g2_ref[...] * (1.0 + scale2) + shift2
    dh = h2.shape[1] // 2
    h2_hi = h2.astype(BF16)
    h2p_ref[...] = _pack_bf16_pair(h2[:, :dh], h2[:, dh:])
    h2_lo = (h2 - h2_hi.astype(F32)).astype(BF16)
    nt_dims = (((1,), (1,)), ((), ()))
    wrt = wrt_ref[...]
    lg = lax.dot_general(wrt, h2_hi, nt_dims, preferred_element_type=F32)
    lg2 = lax.dot_general(wrt[:n_exp], h2_lo, nt_dims, preferred_element_type=F32)
    logits = lg[:n_exp] + lg[n_exp:] + lg2 + br_ref[...]

    eidx = lax.broadcasted_iota(I32, (n_exp, ts), 0)
    neg_inf = jnp.float32(-jnp.inf)
    work = logits
    vals, idxs, hots = [], [], []
    for _ in range(TOP_K):
        mval = jnp.max(work, axis=0, keepdims=True)
        midx = jnp.min(jnp.where(work == mval, eidx, n_exp), axis=0, keepdims=True)
        hot = eidx == midx
        vals.append(mval)
        idxs.append(midx)
        hots.append(hot)
        work = jnp.where(hot, neg_inf, work)
    exps = [jnp.exp(vk - vals[0]) for vk in vals]
    denom = exps[0] + exps[1] + exps[2] + exps[3]
    gate_rows = [ek / denom for ek in exps]

    sel = jnp.zeros((n_exp, ts), F32)
    for hot in hots:
        sel = sel + hot.astype(F32)
    tri = (lax.broadcasted_iota(I32, (ts, ts), 0) < lax.broadcasted_iota(I32, (ts, ts), 1)).astype(BF16)
    before = jnp.dot(sel.astype(BF16), tri, preferred_element_type=F32) + cnt_s[...]
    rank_rows = [jnp.sum(jnp.where(hot, before, 0.0), axis=0, keepdims=True) for hot in hots]
    cnt_s[...] = cnt_s[...] + jnp.sum(sel, axis=1, keepdims=True)
    cnt_ref[...] = cnt_s[:, :LANES]

    idx_ref[...] = jnp.concatenate(idxs, axis=0)
    rank_ref[...] = jnp.concatenate(rank_rows, axis=0).astype(I32)
    g4 = jnp.concatenate(gate_rows, axis=0)
    g_pad = jnp.concatenate([g4, jnp.zeros((LANES - TOP_K, ts), F32)], axis=0)
    gt_ref[...] = g_pad.T


def _mix_call(x, mod, p):
    bsz, seq, dm = x.shape
    ts = min(SEQ_TILE, seq)
    nj = seq // ts
    d_conv = p["conv_w"].shape[1]
    d_lru = p["lru_conv_w"].shape[1]
    conv_w = p["conv_w"].shape[0]
    lru_cw = p["lru_conv_w"].shape[0]
    n_exp = p["wrt"].shape[0] // 2
    assert seq % ts == 0 and ts % LANES == 0 and conv_w - 1 <= CONV_PAD and lru_cw - 1 <= SUBLANES

    def full(a):
        return pl.BlockSpec(a.shape, lambda b, j: (0,) * a.ndim)

    weights = [p["g1"], p["win"], p["conv_w"], p["conv_b"], p["ln_g"], p["ln_b"], p["lru_conv_w"],
               p["lru_conv_b"], p["wg"], p["bg"], p["lam"], p["mng"], p["wout"], p["g2"], p["wrt"], p["br"]]
    kern = functools.partial(_mix_kernel, ts=ts, d_conv=d_conv, d_lru=d_lru, conv_w=conv_w,
                             lru_cw=lru_cw, n_exp=n_exp)
    out_shape = (
        jax.ShapeDtypeStruct((bsz, seq, dm), F32),
        jax.ShapeDtypeStruct((bsz, seq, dm // 2), I32),
        jax.ShapeDtypeStruct((bsz * nj, TOP_K, ts), I32),
        jax.ShapeDtypeStruct((bsz * nj, TOP_K, ts), I32),
        jax.ShapeDtypeStruct((bsz, seq, LANES), F32),
        jax.ShapeDtypeStruct((n_exp, LANES), F32),
    )
    return pl.pallas_call(
        kern,
        grid=(bsz, nj),
        in_specs=[pl.BlockSpec((None, ts, dm), lambda b, j: (b, j, 0)),
                  pl.BlockSpec((None, N_MOD, dm), lambda b, j: (b, 0, 0))] + [full(w) for w in weights],
        out_specs=(
            pl.BlockSpec((None, ts, dm), lambda b, j: (b, j, 0)),
            pl.BlockSpec((None, ts, dm // 2), lambda b, j: (b, j, 0)),
            pl.BlockSpec((None, TOP_K, ts), lambda b, j: (b * nj + j, 0, 0)),
            pl.BlockSpec((None, TOP_K, ts), lambda b, j: (b * nj + j, 0, 0)),
            pl.BlockSpec((None, ts, LANES), lambda b, j: (b, j, 0)),
            pl.BlockSpec((n_exp, LANES), lambda b, j: (0, 0)),
        ),
        out_shape=out_shape,
        scratch_shapes=[
            pltpu.VMEM((CONV_PAD + ts, d_conv), F32),
            pltpu.VMEM((SUBLANES + ts, d_lru), F32),
            pltpu.VMEM((ts, d_lru), F32),
            pltpu.VMEM((ts, d_lru), F32),
            pltpu.VMEM((SUBLANES, d_lru), F32),
            pltpu.VMEM((n_exp, ts), F32),
        ],
        compiler_params=pltpu.CompilerParams(dimension_semantics=("arbitrary", "arbitrary"),
                                             vmem_limit_bytes=VMEM_LIMIT_BYTES),
        name="token_mix_route",
    )(x, mod, *weights)


def _dispatch_kernel(dest_ref, h_ref, xs_ref, sem, *, ts):
    def row_copy(t, k):
        return pltpu.make_async_copy(h_ref.at[pl.ds(t, 1), :], xs_ref.at[pl.ds(dest_ref[k, t], 1), :], sem)

    def issue(t, c):
        for k in range(TOP_K):
            row_copy(t, k).start()
        return c

    lax.fori_loop(0, ts, issue, 0, unroll=8)
    for _ in range(TOP_K):
        pltpu.make_async_copy(h_ref, xs_ref.at[pl.ds(0, ts), :], sem).wait()


def _dispatch_call(h2p, dest, n_rows):
    n_tok, dh = h2p.shape
    n_tiles, _, ts = dest.shape
    return pl.pallas_call(
        functools.partial(_dispatch_kernel, ts=ts),
        grid=(n_tiles,),
        in_specs=[pl.BlockSpec((None, TOP_K, ts), lambda i: (i, 0, 0), memory_space=pltpu.SMEM),
                  pl.BlockSpec((ts, dh), lambda i: (i, 0))],
        out_specs=pl.BlockSpec(memory_space=pl.ANY),
        out_shape=jax.ShapeDtypeStruct((n_rows, dh), I32),
        scratch_shapes=[pltpu.SemaphoreType.DMA],
        compiler_params=pltpu.CompilerParams(dimension_semantics=("arbitrary",)),
        name="moe_dispatch",
    )(dest, h2p)


def _moe_kernel(be_ref, nb_ref, xs_ref, wgu_ref, bgu_ref, wd_ref, bd_ref, ys_ref, wgu_b, wd_b, *, d_ff):
    i = pl.program_id(0)
    live = i < nb_ref[0]
    new_expert = (i == 0) | (be_ref[i] != be_ref[jnp.maximum(i - 1, 0)])

    @pl.when(live & new_expert)
    def _():
        wgu_b[...] = wgu_ref[...].astype(BF16)
        wd_b[...] = wd_ref[...].astype(BF16)

    @pl.when(live)
    def _():
        x_lo, x_hi = _unpack_bf16_pair(xs_ref[...])
        dh = x_lo.shape[1]
        gu = jnp.dot(x_lo.astype(BF16), wgu_b[pl.ds(0, dh), :], preferred_element_type=F32)
        gu += jnp.dot(x_hi.astype(BF16), wgu_b[pl.ds(dh, dh), :], preferred_element_type=F32)
        gu += bgu_ref[...]
        g = jnp.minimum(gu[:, :d_ff], SWIGLU_LIMIT)
        u = jnp.clip(gu[:, d_ff:], -SWIGLU_LIMIT, SWIGLU_LIMIT)
        act = (u + 1.0) * (g * _sigmoid(SWIGLU_ALPHA * g))
        y = jnp.dot(act.astype(BF16), wd_b[...], preferred_element_type=F32) + bd_ref[...]
        dm2 = y.shape[1] // 2
        ys_ref[...] = _pack_bf16_pair(y[:, :dm2], y[:, dm2:])


def _moe_call(xs, block_expert, n_used, wgu, bgu, wd, bd):
    n_rows, dh = xs.shape
    n_exp, dm, d_ff2 = wgu.shape
    d_ff = d_ff2 // 2
    bm = MOE_BLOCK
    n_blocks = n_rows // bm

    def row_map(i, be, nb):
        return (jnp.minimum(i, nb[0] - 1), 0)

    def w_map(i, be, nb):
        return (be[jnp.minimum(i, nb[0] - 1)], 0, 0)

    grid_spec = pltpu.PrefetchScalarGridSpec(
        num_scalar_prefetch=2,
        grid=(n_blocks,),
        in_specs=[
            pl.BlockSpec((bm, dh), row_map),
            pl.BlockSpec((None, dm, d_ff2), w_map),
            pl.BlockSpec((None, 1, d_ff2), w_map),
            pl.BlockSpec((None, d_ff, dm), w_map),
            pl.BlockSpec((None, 1, dm), w_map),
        ],
        out_specs=pl.BlockSpec((bm, dm // 2), row_map),
        scratch_shapes=[pltpu.VMEM((dm, d_ff2), BF16), pltpu.VMEM((d_ff, dm), BF16)],
    )
    return pl.pallas_call(
        functools.partial(_moe_kernel, d_ff=d_ff),
        grid_spec=grid_spec,
        out_shape=jax.ShapeDtypeStruct((n_rows, dm // 2), I32),
        compiler_params=pltpu.CompilerParams(dimension_semantics=("arbitrary",),
                                             vmem_limit_bytes=VMEM_LIMIT_BYTES),
        name="moe_experts",
    )(block_expert, n_used, xs, wgu, bgu, wd, bd)


def _combine_kernel(dest_ref, x1_ref, gt_ref, mod_ref, gf_ref, ys_ref, o_ref, ybuf, sem, *, ts):
    def row_copy(t, k):
        return pltpu.make_async_copy(ys_ref.at[pl.ds(dest_ref[k, t], 1), :],
                                     ybuf.at[pl.ds(k * ts + t, 1), :], sem)

    def issue(t, c):
        for k in range(TOP_K):
            row_copy(t, k).start()
        return c

    lax.fori_loop(0, ts, issue, 0, unroll=8)
    for k in range(TOP_K):
        pltpu.make_async_copy(ys_ref.at[pl.ds(0, ts), :], ybuf.at[pl.ds(k * ts, ts), :], sem).wait()

    gt = gt_ref[...]
    dh = ybuf.shape[1]
    acc_lo = jnp.zeros((ts, dh), F32)
    acc_hi = jnp.zeros((ts, dh), F32)
    for k in range(TOP_K):
        lo, hi = _unpack_bf16_pair(ybuf[pl.ds(k * ts, ts), :])
        gk = gt[:, k:k + 1]
        acc_lo += gk * lo
        acc_hi += gk * hi
    y = jnp.concatenate([acc_lo, acc_hi], axis=1)
    gate2 = mod_ref[...][5:6]
    x2 = x1_ref[...] + gate2 * y
    o_ref[...] = _rms(x2) * gf_ref[...]


def _combine_call(x1, gt, mod, gf, ys, dest):
    bsz, seq, dm = x1.shape
    n_tiles, _, ts = dest.shape
    nj = seq // ts
    return pl.pallas_call(
        functools.partial(_combine_kernel, ts=ts),
        grid=(bsz, nj),
        in_specs=[
            pl.BlockSpec((None, TOP_K, ts), lambda b, j: (b * nj + j, 0, 0), memory_space=pltpu.SMEM),
            pl.BlockSpec((None, ts, dm), lambda b, j: (b, j, 0)),
            pl.BlockSpec((None, ts, LANES), lambda b, j: (b, j, 0)),
            pl.BlockSpec((None, N_MOD, dm), lambda b, j: (b, 0, 0)),
            pl.BlockSpec((1, dm), lambda b, j: (0, 0)),
            pl.BlockSpec(memory_space=pl.ANY),
        ],
        out_specs=pl.BlockSpec((None, ts, dm), lambda b, j: (b, j, 0)),
        out_shape=jax.ShapeDtypeStruct((bsz, seq, dm), F32),
        scratch_shapes=[pltpu.VMEM((TOP_K * ts, dm // 2), I32), pltpu.SemaphoreType.DMA],
        compiler_params=pltpu.CompilerParams(dimension_semantics=("arbitrary", "arbitrary"),
                                             vmem_limit_bytes=VMEM_LIMIT_BYTES),
        name="moe_combine",
    )(dest, x1, gt, mod, gf, ys)


def _block_diag(w):
    n_h, d, _ = w.shape
    eye = jnp.eye(n_h, dtype=w.dtype)
    return (eye[:, None, :, None] * w[:, :, None, :]).reshape(n_h * d, n_h * d)


def _layer(x, mod, l, w_in, norm1_g, conv_w, conv_b, conv_ln_g, conv_ln_b, lru_conv_w, lru_conv_b,
           lru_w_a, lru_b_a, lru_w_x, lru_b_x, lru_lambda, mix_norm_g, w_out, norm2_g, w_router,
           b_router, w_gate_up, b_gate_up, w_down, b_down, out_gain):
    bsz, seq, dm = x.shape
    n_tok = bsz * seq
    n_exp = w_router.shape[-1]
    row = lambda a: a.reshape(1, -1)
    wr = w_router[l]
    wr_hi = wr.astype(BF16)
    wr_lo = (wr - wr_hi.astype(F32)).astype(BF16)
    params = dict(
        g1=row(norm1_g[l]), win=w_in[l].astype(BF16), conv_w=conv_w[l], conv_b=row(conv_b[l]),
        ln_g=row(conv_ln_g[l]), ln_b=row(conv_ln_b[l]), lru_conv_w=lru_conv_w[l], lru_conv_b=row(lru_conv_b[l]),
        wg=jnp.concatenate([_block_diag(lru_w_a[l]), _block_diag(lru_w_x[l])], axis=1).astype(BF16),
        bg=jnp.concatenate([lru_b_a[l].reshape(1, -1), lru_b_x[l].reshape(1, -1)], axis=1),
        lam=row(lru_lambda[l]), mng=row(mix_norm_g[l]), wout=w_out[l].astype(BF16), g2=row(norm2_g[l]),
        wrt=jnp.concatenate([wr_hi.T, wr_lo.T], axis=0), br=b_router[l].reshape(n_exp, 1),
    )
    x1, h2p, top_idx, rank, gt, cnt = _mix_call(x, mod, params)

    bm = MOE_BLOCK
    counts = cnt[:, 0].astype(I32)
    padded = (counts + bm - 1) // bm * bm
    e_ids = jnp.arange(n_exp, dtype=I32)
    pad_ends = jnp.sum(jnp.where(e_ids[None, :] <= e_ids[:, None], padded[None, :], 0), axis=1)
    pad_starts = pad_ends - padded
    n_rows = (n_tok * TOP_K // bm + n_exp) * bm
    n_blocks = n_rows // bm
    dest = rank + jnp.sum(jnp.where(top_idx[..., None] == e_ids, pad_starts, 0), axis=-1)
    block_start = jnp.arange(n_blocks, dtype=I32) * bm
    block_expert = jnp.minimum(
        jnp.sum((block_start[:, None] >= pad_ends[None, :]).astype(I32), axis=1), n_exp - 1)
    n_used = pad_ends[-1:] // bm

    xs = _dispatch_call(h2p.reshape(n_tok, dm // 2), dest, n_rows)
    ys = _moe_call(xs, block_expert, n_used, w_gate_up[l], b_gate_up[l][:, None, :],
                   w_down[l], b_down[l][:, None, :])
    return _combine_call(x1, gt, mod, out_gain, ys, dest)


def kernel(x, c, w_ada, b_ada, norm1_g, w_in, conv_w, conv_b, conv_ln_g, conv_ln_b, lru_conv_w, lru_conv_b,
           lru_w_a, lru_b_a, lru_w_x, lru_b_x, lru_lambda, mix_norm_g, w_out, norm2_g, w_router, b_router,
           w_gate_up, b_gate_up, w_down, b_down, final_norm_g):
    depth = w_ada.shape[0]
    assert depth == 1, "the final norm is fused into the (single) layer's combine kernel"
    bsz, seq, dm = x.shape
    mod = _ada_call(c, w_ada[0], b_ada[0]).reshape(bsz, N_MOD, dm)
    return _layer(x, mod, 0, w_in, norm1_g, conv_w, conv_b, conv_ln_g, conv_ln_b, lru_conv_w, lru_conv_b,
                  lru_w_a, lru_b_a, lru_w_x, lru_b_x, lru_lambda, mix_norm_g, w_out, norm2_g, w_router,
                  b_router, w_gate_up, b_gate_up, w_down, b_down, row_gain(final_norm_g))


def row_gain(g):
    return g.reshape(1, -1)
```

```python
import functools

import jax
import jax.numpy as jnp
from jax import lax
from jax.experimental import pallas as pl
from jax.experimental.pallas import tpu as pltpu
from jax.experimental.pallas import tpu_sc as plsc

F32 = jnp.float32
BF16 = jnp.bfloat16
I32 = jnp.int32

EPS = 1e-6
N_MOD = 6
LRU_C = 8.0
TOP_K = 4
SWIGLU_ALPHA = 1.702
SWIGLU_LIMIT = 7.0

LANES = 128
SUBLANES = 8
VMEM_LIMIT_BYTES = 56 * 1024 * 1024

SEQ_TILE = 512
CONV_PAD = 32
MOE_BLOCK = 512
HI_MASK = -65536

SC_CORES = 2
SC_SUBCORES = 16
SC_WINDOW = 128
SC_SPLIT = 2


def _sigmoid(x):
    return jax.nn.sigmoid(x)


def _pack_bf16_pair(lo_f32, hi_f32):
    lo_bits = lax.bitcast_convert_type(lo_f32.astype(BF16).astype(F32), I32)
    hi_bits = lax.bitcast_convert_type(hi_f32.astype(BF16).astype(F32), I32)
    return lax.shift_right_logical(lo_bits, 16) | hi_bits


def _unpack_bf16_pair(p):
    lo = lax.bitcast_convert_type(lax.shift_left(p, 16), F32)
    hi = lax.bitcast_convert_type(p & HI_MASK, F32)
    return lo, hi


def _ada_kernel(c_ref, w_ref, b_ref, o_ref):
    c = c_ref[...]
    ca = c * _sigmoid(c)
    w = w_ref[...]
    c_hi = ca.astype(BF16)
    c_lo = (ca - c_hi.astype(F32)).astype(BF16)
    w_hi = w.astype(BF16)
    w_lo = (w - w_hi.astype(F32)).astype(BF16)
    acc = jnp.dot(c_hi, w_hi, preferred_element_type=F32)
    acc += jnp.dot(c_lo, w_hi, preferred_element_type=F32)
    acc += jnp.dot(c_hi, w_lo, preferred_element_type=F32)
    o_ref[...] = acc + b_ref[...]


def _ada_call(c, w_ada, b_ada):
    bsz, dm = c.shape
    n_out = w_ada.shape[1]
    tn = 1024
    return pl.pallas_call(
        _ada_kernel,
        grid=(n_out // tn,),
        in_specs=[
            pl.BlockSpec((bsz, dm), lambda n: (0, 0)),
            pl.BlockSpec((dm, tn), lambda n: (0, n)),
            pl.BlockSpec((1, tn), lambda n: (0, n)),
        ],
        out_specs=pl.BlockSpec((bsz, tn), lambda n: (0, n)),
        out_shape=jax.ShapeDtypeStruct((bsz, n_out), F32),
        compiler_params=pltpu.CompilerParams(dimension_semantics=("arbitrary",)),
        name="ada_mod",
    )(c, w_ada, b_ada.reshape(1, n_out))


def _rms(x, eps=EPS):
    return x * lax.rsqrt(jnp.mean(x * x, axis=-1, keepdims=True) + eps)


def _gelu_tanh(x):
    return 0.5 * x * (1.0 + jnp.tanh(0.7978845608028654 * (x + 0.044715 * (x * x * x))))


def _mix_kernel(x_ref, mod_ref, g1_ref, win_ref, cw_ref, cb_ref, lng_ref, lnb_ref,
                lcw_ref, lcb_ref, wg_ref, bg_ref, lam_ref, mng_ref, wout_ref, g2_ref,
                wrt_ref, br_ref,
                x1_ref, h2p_ref, idx_ref, rank_ref, gt_ref, cnt_ref,
                vbuf, rbuf, a_s, b_s, hcar, cnt_s,
                *, ts, d_conv, d_lru, conv_w, lru_cw, n_exp):
    b = pl.program_id(0)
    j = pl.program_id(1)

    @pl.when(j == 0)
    def _():
        vbuf[pl.ds(0, CONV_PAD), :] = jnp.zeros((CONV_PAD, d_conv), F32)
        rbuf[pl.ds(0, SUBLANES), :] = jnp.zeros((SUBLANES, d_lru), F32)
        hcar[...] = jnp.zeros_like(hcar)

    @pl.when((b == 0) & (j == 0))
    def _():
        cnt_s[...] = jnp.zeros_like(cnt_s)

    x = x_ref[...]
    mod = mod_ref[...]
    shift1, scale1, gate1 = mod[0:1], mod[1:2], mod[2:3]
    shift2, scale2, gate2 = mod[3:4], mod[4:5], mod[5:6]
    del gate2

    h = _rms(x) * g1_ref[...] * (1.0 + scale1) + shift1
    u = jnp.dot(h.astype(BF16), win_ref[...], preferred_element_type=F32)

    v = u[:, :d_conv] * _sigmoid(u[:, d_conv:2 * d_conv])
    vbuf[pl.ds(CONV_PAD, ts), :] = v
    acc = jnp.broadcast_to(cb_ref[...], (ts, d_conv))
    for k in range(conv_w):
        acc = acc + cw_ref[pl.ds(k, 1), :] * vbuf[pl.ds(CONV_PAD - (conv_w - 1) + k, ts), :]
    vbuf[pl.ds(0, CONV_PAD), :] = vbuf[pl.ds(ts, CONV_PAD), :]
    mu = jnp.mean(acc, axis=-1, keepdims=True)
    cen = acc - mu
    var = jnp.mean(cen * cen, axis=-1, keepdims=True)
    yc = cen * lax.rsqrt(var + EPS) * lng_ref[...] + lnb_ref[...]
    yc = yc * _sigmoid(yc)

    u_gate = u[:, 2 * d_conv:2 * d_conv + d_lru]
    rbuf[pl.ds(SUBLANES, ts), :] = u[:, 2 * d_conv + d_lru:]
    xr = jnp.broadcast_to(lcb_ref[...], (ts, d_lru))
    for k in range(lru_cw):
        xr = xr + lcw_ref[pl.ds(k, 1), :] * rbuf[pl.ds(SUBLANES - (lru_cw - 1) + k, ts), :]
    rbuf[pl.ds(0, SUBLANES), :] = rbuf[pl.ds(ts, SUBLANES), :]
    gates = jnp.dot(xr.astype(BF16), wg_ref[...], preferred_element_type=F32) + bg_ref[...]
    r = _sigmoid(gates[:, :d_lru])
    i_g = _sigmoid(gates[:, d_lru:])
    lam = lam_ref[...]
    softplus_neg_lam = jnp.maximum(-lam, 0.0) + jnp.log1p(jnp.exp(-jnp.abs(lam)))
    log_a = (-LRU_C) * r * softplus_neg_lam
    a = jnp.exp(log_a)
    inp = jnp.sqrt(1.0 - jnp.exp(2.0 * log_a)) * (i_g * xr)

    n_grp = ts // SUBLANES
    a3 = a.reshape(n_grp, SUBLANES, d_lru)
    b3 = inp.reshape(n_grp, SUBLANES, d_lru)
    sub = lax.broadcasted_iota(I32, (n_grp, SUBLANES, d_lru), 1)
    for s in (1, 2, 4):
        a_sh = pltpu.roll(a3, s, axis=1)
        b_sh = pltpu.roll(b3, s, axis=1)
        m = sub >= s
        b3 = jnp.where(m, a3 * b_sh + b3, b3)
        a3 = jnp.where(m, a3 * a_sh, a3)
    a_s[...] = a3.reshape(ts, d_lru)
    b_s[...] = b3.reshape(ts, d_lru)

    def grp_body(g, carry):
        rows = pl.ds(pl.multiple_of(g * SUBLANES, SUBLANES), SUBLANES)
        hg = b_s[rows, :] + a_s[rows, :] * carry
        b_s[rows, :] = hg
        return jnp.broadcast_to(hg[SUBLANES - 1:SUBLANES, :], (SUBLANES, d_lru))

    hcar[...] = lax.fori_loop(0, n_grp, grp_body, hcar[...])
    yl = _gelu_tanh(u_gate) * b_s[...]

    mng = mng_ref[...]
    yc_n = _rms(yc) * mng[:, :d_conv]
    yl_n = _rms(yl) * mng[:, d_conv:]
    mixo = jnp.dot(yc_n.astype(BF16), wout_ref[pl.ds(0, d_conv), :], preferred_element_type=F32)
    mixo += jnp.dot(yl_n.astype(BF16), wout_ref[pl.ds(d_conv, d_lru), :], preferred_element_type=F32)
    x1 = x + gate1 * mixo
    x1_ref[...] = x1

    h2 = _rms(x1) * g2_ref[...] * (1.0 + scale2) + shift2
    dh = h2.shape[1] // 2
    h2_hi = h2.astype(BF16)
    h2p_ref[...] = _pack_bf16_pair(h2[:, :dh], h2[:, dh:])
    h2_lo = (h2 - h2_hi.astype(F32)).astype(BF16)
    nt_dims = (((1,), (1,)), ((), ()))
    wrt = wrt_ref[...]
    lg = lax.dot_general(wrt, h2_hi, nt_dims, preferred_element_type=F32)
    lg2 = lax.dot_general(wrt[:n_exp], h2_lo, nt_dims, preferred_element_type=F32)
    logits = lg[:n_exp] + lg[n_exp:] + lg2 + br_ref[...]

    eidx = lax.broadcasted_iota(I32, (n_exp, ts), 0)
    neg_inf = jnp.float32(-jnp.inf)
    work = logits
    vals, idxs, hots = [], [], []
    for _ in range(TOP_K):
        mval = jnp.max(work, axis=0, keepdims=True)
        midx = jnp.min(jnp.where(work == mval, eidx, n_exp), axis=0, keepdims=True)
        hot = eidx == midx
        vals.append(mval)
        idxs.append(midx)
        hots.append(hot)
        work = jnp.where(hot, neg_inf, work)
    exps = [jnp.exp(vk - vals[0]) for vk in vals]
    denom = exps[0] + exps[1] + exps[2] + exps[3]
    gate_rows = [ek / denom for ek in exps]

    sel = jnp.zeros((n_exp, ts), F32)
    for hot in hots:
        sel = sel + hot.astype(F32)
    tri = (lax.broadcasted_iota(I32, (ts, ts), 0) < lax.broadcasted_iota(I32, (ts, ts), 1)).astype(BF16)
    before = jnp.dot(sel.astype(BF16), tri, preferred_element_type=F32) + cnt_s[...]
    rank_rows = [jnp.sum(jnp.where(hot, before, 0.0), axis=0, keepdims=True) for hot in hots]
    cnt_s[...] = cnt_s[...] + jnp.sum(sel, axis=1, keepdims=True)
    cnt_ref[...] = cnt_s[:, :LANES]

    idx_ref[...] = jnp.concatenate(idxs, axis=0)
    rank_ref[...] = jnp.concatenate(rank_rows, axis=0).astype(I32)
    g4 = jnp.concatenate(gate_rows, axis=0)
    g_pad = jnp.concatenate([g4, jnp.zeros((LANES - TOP_K, ts), F32)], axis=0)
    gt_ref[...] = g_pad.T


def _mix_call(x, mod, p):
    bsz, seq, dm = x.shape
    ts = min(SEQ_TILE, seq)
    nj = seq // ts
    d_conv = p["conv_w"].shape[1]
    d_lru = p["lru_conv_w"].shape[1]
    conv_w = p["conv_w"].shape[0]
    lru_cw = p["lru_conv_w"].shape[0]
    n_exp = p["wrt"].shape[0] // 2
    assert seq % ts == 0 and ts % LANES == 0 and conv_w - 1 <= CONV_PAD and lru_cw - 1 <= SUBLANES

    def full(a):
        return pl.BlockSpec(a.shape, lambda b, j: (0,) * a.ndim)

    weights = [p["g1"], p["win"], p["conv_w"], p["conv_b"], p["ln_g"], p["ln_b"], p["lru_conv_w"],
               p["lru_conv_b"], p["wg"], p["bg"], p["lam"], p["mng"], p["wout"], p["g2"], p["wrt"], p["br"]]
    kern = functools.partial(_mix_kernel, ts=ts, d_conv=d_conv, d_lru=d_lru, conv_w=conv_w,
                             lru_cw=lru_cw, n_exp=n_exp)
    out_shape = (
        jax.ShapeDtypeStruct((bsz, seq, dm), F32),
        jax.ShapeDtypeStruct((bsz, seq, dm // 2), I32),
        jax.ShapeDtypeStruct((TOP_K, bsz * seq), I32),
        jax.ShapeDtypeStruct((TOP_K, bsz * seq), I32),
        jax.ShapeDtypeStruct((bsz, seq, LANES), F32),
        jax.ShapeDtypeStruct((n_exp, LANES), F32),
    )
    return pl.pallas_call(
        kern,
        grid=(bsz, nj),
        in_specs=[pl.BlockSpec((None, ts, dm), lambda b, j: (b, j, 0)),
                  pl.BlockSpec((None, N_MOD, dm), lambda b, j: (b, 0, 0))] + [full(w) for w in weights],
        out_specs=(
            pl.BlockSpec((None, ts, dm), lambda b, j: (b, j, 0)),
            pl.BlockSpec((None, ts, dm // 2), lambda b, j: (b, j, 0)),
            pl.BlockSpec((TOP_K, ts), lambda b, j: (0, b * nj + j)),
            pl.BlockSpec((TOP_K, ts), lambda b, j: (0, b * nj + j)),
            pl.BlockSpec((None, ts, LANES), lambda b, j: (b, j, 0)),
            pl.BlockSpec((n_exp, LANES), lambda b, j: (0, 0)),
        ),
        out_shape=out_shape,
        scratch_shapes=[
            pltpu.VMEM((CONV_PAD + ts, d_conv), F32),
            pltpu.VMEM((SUBLANES + ts, d_lru), F32),
            pltpu.VMEM((ts, d_lru), F32),
            pltpu.VMEM((ts, d_lru), F32),
            pltpu.VMEM((SUBLANES, d_lru), F32),
            pltpu.VMEM((n_exp, ts), F32),
        ],
        compiler_params=pltpu.CompilerParams(dimension_semantics=("arbitrary", "arbitrary"),
                                             vmem_limit_bytes=VMEM_LIMIT_BYTES),
        name="token_mix_route",
    )(x, mod, *weights)


def _sc_mesh():
    return plsc.VectorSubcoreMesh(core_axis_name="core", subcore_axis_name="subcore",
                                  num_cores=SC_CORES, num_subcores=SC_SUBCORES)


def _sc_scatter_rows(rows, idx, n_out):
    n, w = rows.shape
    n_k = idx.shape[0]
    assert n % (SC_WINDOW * SC_CORES * SC_SUBCORES) == 0

    @functools.partial(pl.kernel, out_type=jax.ShapeDtypeStruct((n_out, w), rows.dtype), mesh=_sc_mesh(),
                       scratch_types=[], name="moe_dispatch_sc")
    def scatter(x_hbm, i_hbm, o_hbm):
        def body(x_vmem, *i_vmems):
            for i_vmem in i_vmems:
                pltpu.sync_copy(x_vmem, o_hbm.at[i_vmem.at[0]])

        pltpu.emit_pipeline(
            body,
            grid=(n // SC_WINDOW,),
            in_specs=[pl.BlockSpec((SC_WINDOW, w), lambda i: (i, 0))]
            + [pl.BlockSpec((1, SC_WINDOW), functools.partial(lambda k, i: (k, i), k)) for k in range(n_k)],
            out_specs=[],
            core_axis_name=("core", "subcore"),
            dimension_semantics=(pltpu.PARALLEL,),
        )(x_hbm, *([i_hbm] * n_k))

    return scatter(rows, idx)


def _sc_gather_rows(table, idx):
    m = idx.shape[1]
    w = table.shape[1]
    assert m % (SC_WINDOW * SC_CORES * SC_SUBCORES) == 0

    @functools.partial(pl.kernel, out_type=jax.ShapeDtypeStruct((m, w), table.dtype), mesh=_sc_mesh(),
                       scratch_types=[], name="moe_gather_sc")
    def gather(x_hbm, i_hbm, o_hbm):
        def body(i_vmem, o_vmem):
            pltpu.sync_copy(x_hbm.at[i_vmem.at[0]], o_vmem)

        pltpu.emit_pipeline(
            body,
            grid=(m // SC_WINDOW,),
            in_specs=[pl.BlockSpec((1, SC_WINDOW), lambda i: (0, i))],
            out_specs=[pl.BlockSpec((SC_WINDOW, w), lambda i: (i, 0))],
            core_axis_name=("core", "subcore"),
            dimension_semantics=(pltpu.PARALLEL,),
        )(i_hbm, o_hbm)

    return gather(table, idx)


def _moe_kernel(be_ref, nb_ref, xs_ref, wgu_ref, bgu_ref, wd_ref, bd_ref, ys_ref, wgu_b, wd_b, *, d_ff):
    i = pl.program_id(0)
    live = i < nb_ref[0]
    new_expert = (i == 0) | (be_ref[i] != be_ref[jnp.maximum(i - 1, 0)])

    @pl.when(live & new_expert)
    def _():
        wgu_b[...] = wgu_ref[...].astype(BF16)
        wd_b[...] = wd_ref[...].astype(BF16)

    @pl.when(live)
    def _():
        x_lo, x_hi = _unpack_bf16_pair(xs_ref[...])
        dh = x_lo.shape[1]
        gu = jnp.dot(x_lo.astype(BF16), wgu_b[pl.ds(0, dh), :], preferred_element_type=F32)
        gu += jnp.dot(x_hi.astype(BF16), wgu_b[pl.ds(dh, dh), :], preferred_element_type=F32)
        gu += bgu_ref[...]
        g = jnp.minimum(gu[:, :d_ff], SWIGLU_LIMIT)
        u = jnp.clip(gu[:, d_ff:], -SWIGLU_LIMIT, SWIGLU_LIMIT)
        act = (u + 1.0) * (g * _sigmoid(SWIGLU_ALPHA * g))
        y = jnp.dot(act.astype(BF16), wd_b[...], preferred_element_type=F32) + bd_ref[...]
        dm2 = y.shape[1] // 2
        ys_ref[...] = _pack_bf16_pair(y[:, :dm2], y[:, dm2:])


def _moe_call(xs, block_expert, n_used, wgu, bgu, wd, bd):
    n_rows, dh = xs.shape
    n_exp, dm, d_ff2 = wgu.shape
    d_ff = d_ff2 // 2
    bm = MOE_BLOCK
    n_blocks = n_rows // bm

    def row_map(i, be, nb):
        return (jnp.minimum(i, nb[0] - 1), 0)

    def w_map(i, be, nb):
        return (be[jnp.minimum(i, nb[0] - 1)], 0, 0)

    grid_spec = pltpu.PrefetchScalarGridSpec(
        num_scalar_prefetch=2,
        grid=(n_blocks,),
        in_specs=[
            pl.BlockSpec((bm, dh), row_map),
            pl.BlockSpec((None, dm, d_ff2), w_map),
            pl.BlockSpec((None, 1, d_ff2), w_map),
            pl.BlockSpec((None, d_ff, dm), w_map),
            pl.BlockSpec((None, 1, dm), w_map),
        ],
        out_specs=pl.BlockSpec((bm, dm // 2), row_map),
        scratch_shapes=[pltpu.VMEM((dm, d_ff2), BF16), pltpu.VMEM((d_ff, dm), BF16)],
    )
    return pl.pallas_call(
        functools.partial(_moe_kernel, d_ff=d_ff),
        grid_spec=grid_spec,
        out_shape=jax.ShapeDtypeStruct((n_rows, dm // 2), I32),
        compiler_params=pltpu.CompilerParams(dimension_semantics=("arbitrary",),
                                             vmem_limit_bytes=VMEM_LIMIT_BYTES),
        name="moe_experts",
    )(block_expert, n_used, xs, wgu, bgu, wd, bd)


def _combine_kernel(x1_ref, gt_ref, mod_ref, gf_ref, yg_ref, o_ref):
    gt = gt_ref[...]
    ts, dh = yg_ref.shape[1:]
    acc_lo = jnp.zeros((ts, dh), F32)
    acc_hi = jnp.zeros((ts, dh), F32)
    for k in range(TOP_K):
        lo, hi = _unpack_bf16_pair(yg_ref[k])
        gk = gt[:, k:k + 1]
        acc_lo += gk * lo
        acc_hi += gk * hi
    y = jnp.concatenate([acc_lo, acc_hi], axis=1)
    gate2 = mod_ref[...][5:6]
    x2 = x1_ref[...] + gate2 * y
    o_ref[...] = _rms(x2) * gf_ref[...]


def _combine_call(x1, gt, mod, gf, yg):
    bsz, seq, dm = x1.shape
    ts = min(SEQ_TILE, seq)
    nj = seq // ts
    return pl.pallas_call(
        _combine_kernel,
        grid=(bsz, nj),
        in_specs=[
            pl.BlockSpec((None, ts, dm), lambda b, j: (b, j, 0)),
            pl.BlockSpec((None, ts, LANES), lambda b, j: (b, j, 0)),
            pl.BlockSpec((None, N_MOD, dm), lambda b, j: (b, 0, 0)),
            pl.BlockSpec((1, dm), lambda b, j: (0, 0)),
            pl.BlockSpec((TOP_K, ts, dm // 2), lambda b, j: (0, b * nj + j, 0)),
        ],
        out_specs=pl.BlockSpec((None, ts, dm), lambda b, j: (b, j, 0)),
        out_shape=jax.ShapeDtypeStruct((bsz, seq, dm), F32),
        compiler_params=pltpu.CompilerParams(dimension_semantics=("arbitrary", "arbitrary"),
                                             vmem_limit_bytes=VMEM_LIMIT_BYTES),
        name="moe_combine",
    )(x1, gt, mod, gf, yg)


def _block_diag(w):
    n_h, d, _ = w.shape
    eye = jnp.eye(n_h, dtype=w.dtype)
    return (eye[:, None, :, None] * w[:, :, None, :]).reshape(n_h * d, n_h * d)


def _layer(x, mod, l, w_in, norm1_g, conv_w, conv_b, conv_ln_g, conv_ln_b, lru_conv_w, lru_conv_b,
           lru_w_a, lru_b_a, lru_w_x, lru_b_x, lru_lambda, mix_norm_g, w_out, norm2_g, w_router,
           b_router, w_gate_up, b_gate_up, w_down, b_down, out_gain):
    bsz, seq, dm = x.shape
    n_tok = bsz * seq
    n_exp = w_router.shape[-1]
    row = lambda a: a.reshape(1, -1)
    wr = w_router[l]
    wr_hi = wr.astype(BF16)
    wr_lo = (wr - wr_hi.astype(F32)).astype(BF16)
    params = dict(
        g1=row(norm1_g[l]), win=w_in[l].astype(BF16), conv_w=conv_w[l], conv_b=row(conv_b[l]),
        ln_g=row(conv_ln_g[l]), ln_b=row(conv_ln_b[l]), lru_conv_w=lru_conv_w[l], lru_conv_b=row(lru_conv_b[l]),
        wg=jnp.concatenate([_block_diag(lru_w_a[l]), _block_diag(lru_w_x[l])], axis=1).astype(BF16),
        bg=jnp.concatenate([lru_b_a[l].reshape(1, -1), lru_b_x[l].reshape(1, -1)], axis=1),
        lam=row(lru_lambda[l]), mng=row(mix_norm_g[l]), wout=w_out[l].astype(BF16), g2=row(norm2_g[l]),
        wrt=jnp.concatenate([wr_hi.T, wr_lo.T], axis=0), br=b_router[l].reshape(n_exp, 1),
    )
    x1, h2p, top_idx, rank, gt, cnt = _mix_call(x, mod, params)

    bm = MOE_BLOCK
    counts = cnt[:, 0].astype(I32)
    padded = (counts + bm - 1) // bm * bm
    e_ids = jnp.arange(n_exp, dtype=I32)
    pad_ends = jnp.sum(jnp.where(e_ids[None, :] <= e_ids[:, None], padded[None, :], 0), axis=1)
    pad_starts = pad_ends - padded
    n_rows = (n_tok * TOP_K // bm + n_exp) * bm
    n_blocks = n_rows // bm
    dest = rank + jnp.sum(jnp.where(top_idx[..., None] == e_ids, pad_starts, 0), axis=-1)
    block_start = jnp.arange(n_blocks, dtype=I32) * bm
    block_expert = jnp.minimum(
        jnp.sum((block_start[:, None] >= pad_ends[None, :]).astype(I32), axis=1), n_exp - 1)
    n_used = pad_ends[-1:] // bm

    dh = dm // 2
    sub = jnp.arange(SC_SPLIT, dtype=I32)
    dest_sub = (dest[..., None] * SC_SPLIT + sub).reshape(TOP_K, n_tok * SC_SPLIT)
    xs = _sc_scatter_rows(h2p.reshape(n_tok * SC_SPLIT, dh // SC_SPLIT), dest_sub, n_rows * SC_SPLIT)
    ys = _moe_call(xs.reshape(n_rows, dh), block_expert, n_used, w_gate_up[l], b_gate_up[l][:, None, :],
                   w_down[l], b_down[l][:, None, :])
    yg = _sc_gather_rows(ys.reshape(n_rows * SC_SPLIT, dh // SC_SPLIT),
                         dest_sub.reshape(1, TOP_K * n_tok * SC_SPLIT))
    return _combine_call(x1, gt, mod, out_gain, yg.reshape(TOP_K, n_tok, dh))


def kernel(x, c, w_ada, b_ada, norm1_g, w_in, conv_w, conv_b, conv_ln_g, conv_ln_b, lru_conv_w, lru_conv_b,
           lru_w_a, lru_b_a, lru_w_x, lru_b_x, lru_lambda, mix_norm_g, w_out, norm2_g, w_router, b_router,
           w_gate_up, b_gate_up, w_down, b_down, final_norm_g):
    depth = w_ada.shape[0]
    assert depth == 1, "the final norm is fused into the (single) layer's combine kernel"
    bsz, seq, dm = x.shape
    mod = _ada_call(c, w_ada[0], b_ada[0]).reshape(bsz, N_MOD, dm)
    return _layer(x, mod, 0, w_in, norm1_g, conv_w, conv_b, conv_ln_g, conv_ln_b, lru_conv_w, lru_conv_b,
                  lru_w_a, lru_b_a, lru_w_x, lru_b_x, lru_lambda, mix_norm_g, w_out, norm2_g, w_router,
                  b_router, w_gate_up, b_gate_up, w_down, b_down, row_gain(final_norm_g))


def row_gain(g):
    return g.reshape(1, -1)
```

```python
import functools

import jax
import jax.numpy as jnp
from jax import lax
from jax.experimental import pallas as pl
from jax.experimental.pallas import tpu as pltpu
from jax.experimental.pallas import tpu_sc as plsc

F32 = jnp.float32
BF16 = jnp.bfloat16
I32 = jnp.int32

EPS = 1e-6
N_MOD = 6
LRU_C = 8.0
TOP_K = 4
SWIGLU_ALPHA = 1.702
SWIGLU_LIMIT = 7.0

LANES = 128
SUBLANES = 8
VMEM_LIMIT_BYTES = 56 * 1024 * 1024

SEQ_TILE = 512
CONV_PAD = 32
MOE_BLOCK = 512
HI_MASK = -65536

SC_CORES = 2
SC_SUBCORES = 16
SC_WINDOW = 128
SC_SPLIT = 2


def _sigmoid(x):
    return jax.nn.sigmoid(x)


def _pack_bf16_pair(lo_f32, hi_f32):
    lo_bits = lax.bitcast_convert_type(lo_f32.astype(BF16).astype(F32), I32)
    hi_bits = lax.bitcast_convert_type(hi_f32.astype(BF16).astype(F32), I32)
    return lax.shift_right_logical(lo_bits, 16) | hi_bits


def _unpack_bf16_pair(p):
    lo = lax.bitcast_convert_type(lax.shift_left(p, 16), F32)
    hi = lax.bitcast_convert_type(p & HI_MASK, F32)
    return lo, hi


def _store_planes(ref, packed):
    wp = packed.shape[1] // SC_SPLIT
    for s in range(SC_SPLIT):
        ref[s] = packed[:, s * wp:(s + 1) * wp]


def _load_planes(ref):
    return jnp.concatenate([ref[s] for s in range(SC_SPLIT)], axis=1)


def _ada_kernel(c_ref, w_ref, b_ref, o_ref):
    c = c_ref[...]
    ca = c * _sigmoid(c)
    w = w_ref[...]
    c_hi = ca.astype(BF16)
    c_lo = (ca - c_hi.astype(F32)).astype(BF16)
    w_hi = w.astype(BF16)
    w_lo = (w - w_hi.astype(F32)).astype(BF16)
    acc = jnp.dot(c_hi, w_hi, preferred_element_type=F32)
    acc += jnp.dot(c_lo, w_hi, preferred_element_type=F32)
    acc += jnp.dot(c_hi, w_lo, preferred_element_type=F32)
    o_ref[...] = acc + b_ref[...]


def _ada_call(c, w_ada, b_ada):
    bsz, dm = c.shape
    n_out = w_ada.shape[1]
    tn = 1024
    return pl.pallas_call(
        _ada_kernel,
        grid=(n_out // tn,),
        in_specs=[
            pl.BlockSpec((bsz, dm), lambda n: (0, 0)),
            pl.BlockSpec((dm, tn), lambda n: (0, n)),
            pl.BlockSpec((1, tn), lambda n: (0, n)),
        ],
        out_specs=pl.BlockSpec((bsz, tn), lambda n: (0, n)),
        out_shape=jax.ShapeDtypeStruct((bsz, n_out), F32),
        compiler_params=pltpu.CompilerParams(dimension_semantics=("arbitrary",)),
        name="ada_mod",
    )(c, w_ada, b_ada.reshape(1, n_out))


def _rms(x, eps=EPS):
    return x * lax.rsqrt(jnp.mean(x * x, axis=-1, keepdims=True) + eps)


def _gelu_tanh(x):
    return 0.5 * x * (1.0 + jnp.tanh(0.7978845608028654 * (x + 0.044715 * (x * x * x))))


def _mix_kernel(x_ref, mod_ref, g1_ref, win_ref, cw_ref, cb_ref, lng_ref, lnb_ref,
                lcw_ref, lcb_ref, wg_ref, bg_ref, lam_ref, mng_ref, wout_ref, g2_ref,
                wrt_ref, br_ref,
                x1_ref, h2p_ref, idx_ref, rank_ref, gt_ref, cnt_ref,
                vbuf, rbuf, a_s, b_s, hcar, cnt_s,
                *, ts, d_conv, d_lru, conv_w, lru_cw, n_exp):
    b = pl.program_id(0)
    j = pl.program_id(1)

    @pl.when(j == 0)
    def _():
        vbuf[pl.ds(0, CONV_PAD), :] = jnp.zeros((CONV_PAD, d_conv), F32)
        rbuf[pl.ds(0, SUBLANES), :] = jnp.zeros((SUBLANES, d_lru), F32)
        hcar[...] = jnp.zeros_like(hcar)

    @pl.when((b == 0) & (j == 0))
    def _():
        cnt_s[...] = jnp.zeros_like(cnt_s)

    x = x_ref[...]
    mod = mod_ref[...]
    shift1, scale1, gate1 = mod[0:1], mod[1:2], mod[2:3]
    shift2, scale2, gate2 = mod[3:4], mod[4:5], mod[5:6]
    del gate2

    h = _rms(x) * g1_ref[...] * (1.0 + scale1) + shift1
    u = jnp.dot(h.astype(BF16), win_ref[...], preferred_element_type=F32)

    v = u[:, :d_conv] * _sigmoid(u[:, d_conv:2 * d_conv])
    vbuf[pl.ds(CONV_PAD, ts), :] = v
    acc = jnp.broadcast_to(cb_ref[...], (ts, d_conv))
    for k in range(conv_w):
        acc = acc + cw_ref[pl.ds(k, 1), :] * vbuf[pl.ds(CONV_PAD - (conv_w - 1) + k, ts), :]
    vbuf[pl.ds(0, CONV_PAD), :] = vbuf[pl.ds(ts, CONV_PAD), :]
    mu = jnp.mean(acc, axis=-1, keepdims=True)
    cen = acc - mu
    var = jnp.mean(cen * cen, axis=-1, keepdims=True)
    yc = cen * lax.rsqrt(var + EPS) * lng_ref[...] + lnb_ref[...]
    yc = yc * _sigmoid(yc)

    u_gate = u[:, 2 * d_conv:2 * d_conv + d_lru]
    rbuf[pl.ds(SUBLANES, ts), :] = u[:, 2 * d_conv + d_lru:]
    xr = jnp.broadcast_to(lcb_ref[...], (ts, d_lru))
    for k in range(lru_cw):
        xr = xr + lcw_ref[pl.ds(k, 1), :] * rbuf[pl.ds(SUBLANES - (lru_cw - 1) + k, ts), :]
    rbuf[pl.ds(0, SUBLANES), :] = rbuf[pl.ds(ts, SUBLANES), :]
    gates = jnp.dot(xr.astype(BF16), wg_ref[...], preferred_element_type=F32) + bg_ref[...]
    r = _sigmoid(gates[:, :d_lru])
    i_g = _sigmoid(gates[:, d_lru:])
    lam = lam_ref[...]
    softplus_neg_lam = jnp.maximum(-lam, 0.0) + jnp.log1p(jnp.exp(-jnp.abs(lam)))
    log_a = (-LRU_C) * r * softplus_neg_lam
    a = jnp.exp(log_a)
    inp = jnp.sqrt(1.0 - jnp.exp(2.0 * log_a)) * (i_g * xr)

    n_grp = ts // SUBLANES
    a3 = a.reshape(n_grp, SUBLANES, d_lru)
    b3 = inp.reshape(n_grp, SUBLANES, d_lru)
    sub = lax.broadcasted_iota(I32, (n_grp, SUBLANES, d_lru), 1)
    for s in (1, 2, 4):
        a_sh = pltpu.roll(a3, s, axis=1)
        b_sh = pltpu.roll(b3, s, axis=1)
        m = sub >= s
        b3 = jnp.where(m, a3 * b_sh + b3, b3)
        a3 = jnp.where(m, a3 * a_sh, a3)
    a_s[...] = a3.reshape(ts, d_lru)
    b_s[...] = b3.reshape(ts, d_lru)

    def grp_body(g, carry):
        rows = pl.ds(pl.multiple_of(g * SUBLANES, SUBLANES), SUBLANES)
        hg = b_s[rows, :] + a_s[rows, :] * carry
        b_s[rows, :] = hg
        return jnp.broadcast_to(hg[SUBLANES - 1:SUBLANES, :], (SUBLANES, d_lru))

    hcar[...] = lax.fori_loop(0, n_grp, grp_body, hcar[...])
    yl = _gelu_tanh(u_gate) * b_s[...]

    mng = mng_ref[...]
    yc_n = _rms(yc) * mng[:, :d_conv]
    yl_n = _rms(yl) * mng[:, d_conv:]
    mixo = jnp.dot(yc_n.astype(BF16), wout_ref[pl.ds(0, d_conv), :], preferred_element_type=F32)
    mixo += jnp.dot(yl_n.astype(BF16), wout_ref[pl.ds(d_conv, d_lru), :], preferred_element_type=F32)
    x1 = x + gate1 * mixo
    x1_ref[...] = x1

    h2 = _rms(x1) * g2_ref[...] * (1.0 + scale2) + shift2
    dh = h2.shape[1] // 2
    h2_hi = h2.astype(BF16)
    _store_planes(h2p_ref, _pack_bf16_pair(h2[:, :dh], h2[:, dh:]))
    h2_lo = (h2 - h2_hi.astype(F32)).astype(BF16)
    nt_dims = (((1,), (1,)), ((), ()))
    wrt = wrt_ref[...]
    lg = lax.dot_general(wrt, h2_hi, nt_dims, preferred_element_type=F32)
    lg2 = lax.dot_general(wrt[:n_exp], h2_lo, nt_dims, preferred_element_type=F32)
    logits = lg[:n_exp] + lg[n_exp:] + lg2 + br_ref[...]

    eidx = lax.broadcasted_iota(I32, (n_exp, ts), 0)
    neg_inf = jnp.float32(-jnp.inf)
    work = logits
    vals, idxs, hots = [], [], []
    for _ in range(TOP_K):
        mval = jnp.max(work, axis=0, keepdims=True)
        midx = jnp.min(jnp.where(work == mval, eidx, n_exp), axis=0, keepdims=True)
        hot = eidx == midx
        vals.append(mval)
        idxs.append(midx)
        hots.append(hot)
        work = jnp.where(hot, neg_inf, work)
    exps = [jnp.exp(vk - vals[0]) for vk in vals]
    denom = exps[0] + exps[1] + exps[2] + exps[3]
    gate_rows = [ek / denom for ek in exps]

    sel = jnp.zeros((n_exp, ts), F32)
    for hot in hots:
        sel = sel + hot.astype(F32)
    tri = (lax.broadcasted_iota(I32, (ts, ts), 0) < lax.broadcasted_iota(I32, (ts, ts), 1)).astype(BF16)
    before = jnp.dot(sel.astype(BF16), tri, preferred_element_type=F32) + cnt_s[...]
    rank_rows = [jnp.sum(jnp.where(hot, before, 0.0), axis=0, keepdims=True) for hot in hots]
    cnt_s[...] = cnt_s[...] + jnp.sum(sel, axis=1, keepdims=True)
    cnt_ref[...] = cnt_s[:, :LANES]

    idx_ref[...] = jnp.concatenate(idxs, axis=0)
    rank_ref[...] = jnp.concatenate(rank_rows, axis=0).astype(I32)
    g4 = jnp.concatenate(gate_rows, axis=0)
    g_pad = jnp.concatenate([g4, jnp.zeros((LANES - TOP_K, ts), F32)], axis=0)
    gt_ref[...] = g_pad.T


def _mix_call(x, mod, p):
    bsz, seq, dm = x.shape
    ts = min(SEQ_TILE, seq)
    nj = seq // ts
    d_conv = p["conv_w"].shape[1]
    d_lru = p["lru_conv_w"].shape[1]
    conv_w = p["conv_w"].shape[0]
    lru_cw = p["lru_conv_w"].shape[0]
    n_exp = p["wrt"].shape[0] // 2
    assert seq % ts == 0 and ts % LANES == 0 and conv_w - 1 <= CONV_PAD and lru_cw - 1 <= SUBLANES

    def full(a):
        return pl.BlockSpec(a.shape, lambda b, j: (0,) * a.ndim)

    weights = [p["g1"], p["win"], p["conv_w"], p["conv_b"], p["ln_g"], p["ln_b"], p["lru_conv_w"],
               p["lru_conv_b"], p["wg"], p["bg"], p["lam"], p["mng"], p["wout"], p["g2"], p["wrt"], p["br"]]
    kern = functools.partial(_mix_kernel, ts=ts, d_conv=d_conv, d_lru=d_lru, conv_w=conv_w,
                             lru_cw=lru_cw, n_exp=n_exp)
    out_shape = (
        jax.ShapeDtypeStruct((bsz, seq, dm), F32),
        jax.ShapeDtypeStruct((SC_SPLIT, bsz * seq, dm // 2 // SC_SPLIT), I32),
        jax.ShapeDtypeStruct((TOP_K, bsz * seq), I32),
        jax.ShapeDtypeStruct((TOP_K, bsz * seq), I32),
        jax.ShapeDtypeStruct((bsz, seq, LANES), F32),
        jax.ShapeDtypeStruct((n_exp, LANES), F32),
    )
    return pl.pallas_call(
        kern,
        grid=(bsz, nj),
        in_specs=[pl.BlockSpec((None, ts, dm), lambda b, j: (b, j, 0)),
                  pl.BlockSpec((None, N_MOD, dm), lambda b, j: (b, 0, 0))] + [full(w) for w in weights],
        out_specs=(
            pl.BlockSpec((None, ts, dm), lambda b, j: (b, j, 0)),
            pl.BlockSpec((SC_SPLIT, ts, dm // 2 // SC_SPLIT), lambda b, j: (0, b * nj + j, 0)),
            pl.BlockSpec((TOP_K, ts), lambda b, j: (0, b * nj + j)),
            pl.BlockSpec((TOP_K, ts), lambda b, j: (0, b * nj + j)),
            pl.BlockSpec((None, ts, LANES), lambda b, j: (b, j, 0)),
            pl.BlockSpec((n_exp, LANES), lambda b, j: (0, 0)),
        ),
        out_shape=out_shape,
        scratch_shapes=[
            pltpu.VMEM((CONV_PAD + ts, d_conv), F32),
            pltpu.VMEM((SUBLANES + ts, d_lru), F32),
            pltpu.VMEM((ts, d_lru), F32),
            pltpu.VMEM((ts, d_lru), F32),
            pltpu.VMEM((SUBLANES, d_lru), F32),
            pltpu.VMEM((n_exp, ts), F32),
        ],
        compiler_params=pltpu.CompilerParams(dimension_semantics=("arbitrary", "arbitrary"),
                                             vmem_limit_bytes=VMEM_LIMIT_BYTES),
        name="token_mix_route",
    )(x, mod, *weights)


def _sc_mesh():
    return plsc.VectorSubcoreMesh(core_axis_name="core", subcore_axis_name="subcore",
                                  num_cores=SC_CORES, num_subcores=SC_SUBCORES)


def _sc_scatter_rows(rows, idx, n_out):
    n, w = rows.shape
    n_k = idx.shape[0]
    assert n % (SC_WINDOW * SC_CORES * SC_SUBCORES) == 0

    @functools.partial(pl.kernel, out_type=jax.ShapeDtypeStruct((n_out, w), rows.dtype), mesh=_sc_mesh(),
                       scratch_types=[], name="moe_dispatch_sc")
    def scatter(x_hbm, i_hbm, o_hbm):
        def body(x_vmem, *i_vmems):
            for i_vmem in i_vmems:
                pltpu.sync_copy(x_vmem, o_hbm.at[i_vmem.at[0]])

        pltpu.emit_pipeline(
            body,
            grid=(n // SC_WINDOW,),
            in_specs=[pl.BlockSpec((SC_WINDOW, w), lambda i: (i, 0))]
            + [pl.BlockSpec((1, SC_WINDOW), functools.partial(lambda k, i: (k, i), k)) for k in range(n_k)],
            out_specs=[],
            core_axis_name=("core", "subcore"),
            dimension_semantics=(pltpu.PARALLEL,),
        )(x_hbm, *([i_hbm] * n_k))

    return scatter(rows, idx)


def _sc_gather_rows(table, idx):
    m = idx.shape[1]
    w = table.shape[1]
    assert m % (SC_WINDOW * SC_CORES * SC_SUBCORES) == 0

    @functools.partial(pl.kernel, out_type=jax.ShapeDtypeStruct((m, w), table.dtype), mesh=_sc_mesh(),
                       scratch_types=[], name="moe_gather_sc")
    def gather(x_hbm, i_hbm, o_hbm):
        def body(i_vmem, o_vmem):
            pltpu.sync_copy(x_hbm.at[i_vmem.at[0]], o_vmem)

        pltpu.emit_pipeline(
            body,
            grid=(m // SC_WINDOW,),
            in_specs=[pl.BlockSpec((1, SC_WINDOW), lambda i: (0, i))],
            out_specs=[pl.BlockSpec((SC_WINDOW, w), lambda i: (i, 0))],
            core_axis_name=("core", "subcore"),
            dimension_semantics=(pltpu.PARALLEL,),
        )(i_hbm, o_hbm)

    return gather(table, idx)


def _moe_kernel(be_ref, nb_ref, xs_ref, wgu_ref, bgu_ref, wd_ref, bd_ref, ys_ref, wgu_b, wd_b, *, d_ff):
    i = pl.program_id(0)
    live = i < nb_ref[0]
    new_expert = (i == 0) | (be_ref[i] != be_ref[jnp.maximum(i - 1, 0)])

    @pl.when(live & new_expert)
    def _():
        wgu_b[...] = wgu_ref[...].astype(BF16)
        wd_b[...] = wd_ref[...].astype(BF16)

    @pl.when(live)
    def _():
        x_lo, x_hi = _unpack_bf16_pair(_load_planes(xs_ref))
        dh = x_lo.shape[1]
        gu = jnp.dot(x_lo.astype(BF16), wgu_b[pl.ds(0, dh), :], preferred_element_type=F32)
        gu += jnp.dot(x_hi.astype(BF16), wgu_b[pl.ds(dh, dh), :], preferred_element_type=F32)
        gu += bgu_ref[...]
        g = jnp.minimum(gu[:, :d_ff], SWIGLU_LIMIT)
        u = jnp.clip(gu[:, d_ff:], -SWIGLU_LIMIT, SWIGLU_LIMIT)
        act = (u + 1.0) * (g * _sigmoid(SWIGLU_ALPHA * g))
        y = jnp.dot(act.astype(BF16), wd_b[...], preferred_element_type=F32) + bd_ref[...]
        dm2 = y.shape[1] // 2
        _store_planes(ys_ref, _pack_bf16_pair(y[:, :dm2], y[:, dm2:]))


def _moe_call(xs, block_expert, n_used, wgu, bgu, wd, bd):
    _, n_rows, wp = xs.shape
    n_exp, dm, d_ff2 = wgu.shape
    d_ff = d_ff2 // 2
    bm = MOE_BLOCK
    n_blocks = n_rows // bm

    def row_map(i, be, nb):
        return (0, jnp.minimum(i, nb[0] - 1), 0)

    def w_map(i, be, nb):
        return (be[jnp.minimum(i, nb[0] - 1)], 0, 0)

    grid_spec = pltpu.PrefetchScalarGridSpec(
        num_scalar_prefetch=2,
        grid=(n_blocks,),
        in_specs=[
            pl.BlockSpec((SC_SPLIT, bm, wp), row_map),
            pl.BlockSpec((None, dm, d_ff2), w_map),
            pl.BlockSpec((None, 1, d_ff2), w_map),
            pl.BlockSpec((None, d_ff, dm), w_map),
            pl.BlockSpec((None, 1, dm), w_map),
        ],
        out_specs=pl.BlockSpec((SC_SPLIT, bm, wp), row_map),
        scratch_shapes=[pltpu.VMEM((dm, d_ff2), BF16), pltpu.VMEM((d_ff, dm), BF16)],
    )
    return pl.pallas_call(
        functools.partial(_moe_kernel, d_ff=d_ff),
        grid_spec=grid_spec,
        out_shape=jax.ShapeDtypeStruct(xs.shape, I32),
        compiler_params=pltpu.CompilerParams(dimension_semantics=("arbitrary",),
                                             vmem_limit_bytes=VMEM_LIMIT_BYTES),
        name="moe_experts",
    )(block_expert, n_used, xs, wgu, bgu, wd, bd)


def _combine_kernel(x1_ref, gt_ref, mod_ref, gf_ref, yg_ref, o_ref):
    gt = gt_ref[...]
    ts = yg_ref.shape[2]
    dh = yg_ref.shape[3] * SC_SPLIT
    acc_lo = jnp.zeros((ts, dh), F32)
    acc_hi = jnp.zeros((ts, dh), F32)
    for k in range(TOP_K):
        lo, hi = _unpack_bf16_pair(_load_planes(yg_ref.at[k]))
        gk = gt[:, k:k + 1]
        acc_lo += gk * lo
        acc_hi += gk * hi
    y = jnp.concatenate([acc_lo, acc_hi], axis=1)
    gate2 = mod_ref[...][5:6]
    x2 = x1_ref[...] + gate2 * y
    o_ref[...] = _rms(x2) * gf_ref[...]


def _combine_call(x1, gt, mod, gf, yg):
    bsz, seq, dm = x1.shape
    ts = min(SEQ_TILE, seq)
    nj = seq // ts
    return pl.pallas_call(
        _combine_kernel,
        grid=(bsz, nj),
        in_specs=[
            pl.BlockSpec((None, ts, dm), lambda b, j: (b, j, 0)),
            pl.BlockSpec((None, ts, LANES), lambda b, j: (b, j, 0)),
            pl.BlockSpec((None, N_MOD, dm), lambda b, j: (b, 0, 0)),
            pl.BlockSpec((1, dm), lambda b, j: (0, 0)),
            pl.BlockSpec((TOP_K, SC_SPLIT, ts, yg.shape[3]), lambda b, j: (0, 0, b * nj + j, 0)),
        ],
        out_specs=pl.BlockSpec((None, ts, dm), lambda b, j: (b, j, 0)),
        out_shape=jax.ShapeDtypeStruct((bsz, seq, dm), F32),
        compiler_params=pltpu.CompilerParams(dimension_semantics=("arbitrary", "arbitrary"),
                                             vmem_limit_bytes=VMEM_LIMIT_BYTES),
        name="moe_combine",
    )(x1, gt, mod, gf, yg)


def _block_diag(w):
    n_h, d, _ = w.shape
    eye = jnp.eye(n_h, dtype=w.dtype)
    return (eye[:, None, :, None] * w[:, :, None, :]).reshape(n_h * d, n_h * d)


def _layer(x, mod, l, w_in, norm1_g, conv_w, conv_b, conv_ln_g, conv_ln_b, lru_conv_w, lru_conv_b,
           lru_w_a, lru_b_a, lru_w_x, lru_b_x, lru_lambda, mix_norm_g, w_out, norm2_g, w_router,
           b_router, w_gate_up, b_gate_up, w_down, b_down, out_gain):
    bsz, seq, dm = x.shape
    n_tok = bsz * seq
    n_exp = w_router.shape[-1]
    row = lambda a: a.reshape(1, -1)
    wr = w_router[l]
    wr_hi = wr.astype(BF16)
    wr_lo = (wr - wr_hi.astype(F32)).astype(BF16)
    params = dict(
        g1=row(norm1_g[l]), win=w_in[l].astype(BF16), conv_w=conv_w[l], conv_b=row(conv_b[l]),
        ln_g=row(conv_ln_g[l]), ln_b=row(conv_ln_b[l]), lru_conv_w=lru_conv_w[l], lru_conv_b=row(lru_conv_b[l]),
        wg=jnp.concatenate([_block_diag(lru_w_a[l]), _block_diag(lru_w_x[l])], axis=1).astype(BF16),
        bg=jnp.concatenate([lru_b_a[l].reshape(1, -1), lru_b_x[l].reshape(1, -1)], axis=1),
        lam=row(lru_lambda[l]), mng=row(mix_norm_g[l]), wout=w_out[l].astype(BF16), g2=row(norm2_g[l]),
        wrt=jnp.concatenate([wr_hi.T, wr_lo.T], axis=0), br=b_router[l].reshape(n_exp, 1),
    )
    x1, h2p, top_idx, rank, gt, cnt = _mix_call(x, mod, params)

    bm = MOE_BLOCK
    counts = cnt[:, 0].astype(I32)
    padded = (counts + bm - 1) // bm * bm
    e_ids = jnp.arange(n_exp, dtype=I32)
    pad_ends = jnp.sum(jnp.where(e_ids[None, :] <= e_ids[:, None], padded[None, :], 0), axis=1)
    pad_starts = pad_ends - padded
    n_rows = (n_tok * TOP_K // bm + n_exp) * bm
    n_blocks = n_rows // bm
    dest = rank + jnp.sum(jnp.where(top_idx[..., None] == e_ids, pad_starts, 0), axis=-1)
    block_start = jnp.arange(n_blocks, dtype=I32) * bm
    block_expert = jnp.minimum(
        jnp.sum((block_start[:, None] >= pad_ends[None, :]).astype(I32), axis=1), n_exp - 1)
    n_used = pad_ends[-1:] // bm

    wp = h2p.shape[2]
    plane = jnp.arange(SC_SPLIT, dtype=I32)[None, :, None] * n_rows
    dest_sub = dest[:, None, :] + plane
    xs = _sc_scatter_rows(h2p.reshape(SC_SPLIT * n_tok, wp), dest_sub.reshape(TOP_K, SC_SPLIT * n_tok),
                          SC_SPLIT * n_rows).reshape(SC_SPLIT, n_rows, wp)
    ys = _moe_call(xs, block_expert, n_used, w_gate_up[l], b_gate_up[l][:, None, :],
                   w_down[l], b_down[l][:, None, :])
    yg = _sc_gather_rows(ys.reshape(SC_SPLIT * n_rows, wp), dest_sub.reshape(1, TOP_K * SC_SPLIT * n_tok))
    return _combine_call(x1, gt, mod, out_gain, yg.reshape(TOP_K, SC_SPLIT, n_tok, wp))


def kernel(x, c, w_ada, b_ada, norm1_g, w_in, conv_w, conv_b, conv_ln_g, conv_ln_b, lru_conv_w, lru_conv_b,
           lru_w_a, lru_b_a, lru_w_x, lru_b_x, lru_lambda, mix_norm_g, w_out, norm2_g, w_router, b_router,
           w_gate_up, b_gate_up, w_down, b_down, final_norm_g):
    depth = w_ada.shape[0]
    assert depth == 1, "the final norm is fused into the (single) layer's combine kernel"
    bsz, seq, dm = x.shape
    mod = _ada_call(c, w_ada[0], b_ada[0]).reshape(bsz, N_MOD, dm)
    return _layer(x, mod, 0, w_in, norm1_g, conv_w, conv_b, conv_ln_g, conv_ln_b, lru_conv_w, lru_conv_b,
                  lru_w_a, lru_b_a, lru_w_x, lru_b_x, lru_lambda, mix_norm_g, w_out, norm2_g, w_router,
                  b_router, w_gate_up, b_gate_up, w_down, b_down, row_gain(final_norm_g))


def row_gain(g):
    return g.reshape(1, -1)
```

```python
import functools

import jax
import jax.numpy as jnp
from jax import lax
from jax.experimental import pallas as pl
from jax.experimental.pallas import tpu as pltpu
from jax.experimental.pallas import tpu_sc as plsc

F32 = jnp.float32
BF16 = jnp.bfloat16
I32 = jnp.int32

EPS = 1e-6
N_MOD = 6
LRU_C = 8.0
TOP_K = 4
SWIGLU_ALPHA = 1.702
SWIGLU_LIMIT = 7.0

LANES = 128
SUBLANES = 8
VMEM_LIMIT_BYTES = 56 * 1024 * 1024

SEQ_TILE = 512
CONV_PAD = 32
MOE_BLOCK = 512
N_CHUNKS = 2
HI_MASK = -65536

SC_CORES = 2
SC_SUBCORES = 16
SC_WINDOW = 128
SC_SPLIT = 2


def _sigmoid(x):
    return jax.nn.sigmoid(x)


def _pack_bf16_pair(lo_f32, hi_f32):
    lo_bits = lax.bitcast_convert_type(lo_f32.astype(BF16).astype(F32), I32)
    hi_bits = lax.bitcast_convert_type(hi_f32.astype(BF16).astype(F32), I32)
    return lax.shift_right_logical(lo_bits, 16) | hi_bits


def _unpack_bf16_pair(p):
    lo = lax.bitcast_convert_type(lax.shift_left(p, 16), F32)
    hi = lax.bitcast_convert_type(p & HI_MASK, F32)
    return lo, hi


def _store_planes(ref, packed):
    wp = packed.shape[1] // SC_SPLIT
    for s in range(SC_SPLIT):
        ref[s] = packed[:, s * wp:(s + 1) * wp]


def _load_planes(ref):
    return jnp.concatenate([ref[s] for s in range(SC_SPLIT)], axis=1)


def _ada_kernel(c_ref, w_ref, b_ref, o_ref):
    c = c_ref[...]
    ca = c * _sigmoid(c)
    w = w_ref[...]
    c_hi = ca.astype(BF16)
    c_lo = (ca - c_hi.astype(F32)).astype(BF16)
    w_hi = w.astype(BF16)
    w_lo = (w - w_hi.astype(F32)).astype(BF16)
    acc = jnp.dot(c_hi, w_hi, preferred_element_type=F32)
    acc += jnp.dot(c_lo, w_hi, preferred_element_type=F32)
    acc += jnp.dot(c_hi, w_lo, preferred_element_type=F32)
    o_ref[...] = acc + b_ref[...]


def _ada_call(c, w_ada, b_ada):
    bsz, dm = c.shape
    n_out = w_ada.shape[1]
    tn = 1024
    return pl.pallas_call(
        _ada_kernel,
        grid=(n_out // tn,),
        in_specs=[
            pl.BlockSpec((bsz, dm), lambda n: (0, 0)),
            pl.BlockSpec((dm, tn), lambda n: (0, n)),
            pl.BlockSpec((1, tn), lambda n: (0, n)),
        ],
        out_specs=pl.BlockSpec((bsz, tn), lambda n: (0, n)),
        out_shape=jax.ShapeDtypeStruct((bsz, n_out), F32),
        compiler_params=pltpu.CompilerParams(dimension_semantics=("arbitrary",)),
        name="ada_mod",
    )(c, w_ada, b_ada.reshape(1, n_out))


def _rms(x, eps=EPS):
    return x * lax.rsqrt(jnp.mean(x * x, axis=-1, keepdims=True) + eps)


def _gelu_tanh(x):
    return 0.5 * x * (1.0 + jnp.tanh(0.7978845608028654 * (x + 0.044715 * (x * x * x))))


def _causal_tap_sum(buf, shifted, w_ref, bias_row, first, n_taps, ts):
    acc = jnp.broadcast_to(bias_row, (ts, buf.shape[1]))
    for r in range(SUBLANES):
        taps = [k for k in range(n_taps) if (first + k) % SUBLANES == r]
        if not taps:
            continue
        src = buf
        if r:
            span = max(first + k for k in taps) - r + ts
            shifted[pl.ds(0, span), :] = buf[pl.ds(r, span), :]
            src = shifted
        for k in taps:
            acc = acc + w_ref[pl.ds(k, 1), :] * src[pl.ds(first + k - r, ts), :]
    return acc


def _mix_kernel(x_ref, mod_ref, g1_ref, win_ref, cw_ref, cb_ref, lng_ref, lnb_ref,
                lcw_ref, lcb_ref, wg_ref, bg_ref, lam_ref, mng_ref, wout_ref, g2_ref,
                wrt_ref, br_ref,
                x1_ref, h2p_ref, idx_ref, rank_ref, gt_ref, cnt_ref,
                vbuf, rbuf, shf, a_s, b_s, hcar, cnt_s,
                *, ts, d_conv, d_lru, conv_w, lru_cw, n_exp):
    b = pl.program_id(0)
    j = pl.program_id(1)

    @pl.when(j == 0)
    def _():
        vbuf[pl.ds(0, CONV_PAD), :] = jnp.zeros((CONV_PAD, d_conv), F32)
        rbuf[pl.ds(0, SUBLANES), :] = jnp.zeros((SUBLANES, d_lru), F32)
        hcar[...] = jnp.zeros_like(hcar)

    @pl.when((b == 0) & (j == 0))
    def _():
        cnt_s[...] = jnp.zeros_like(cnt_s)

    x = x_ref[...]
    mod = mod_ref[...]
    shift1, scale1, gate1 = mod[0:1], mod[1:2], mod[2:3]
    shift2, scale2, gate2 = mod[3:4], mod[4:5], mod[5:6]
    del gate2

    h = _rms(x) * g1_ref[...] * (1.0 + scale1) + shift1
    u = jnp.dot(h.astype(BF16), win_ref[...], preferred_element_type=F32)

    v = u[:, :d_conv] * _sigmoid(u[:, d_conv:2 * d_conv])
    vbuf[pl.ds(CONV_PAD, ts), :] = v
    acc = _causal_tap_sum(vbuf, shf, cw_ref, cb_ref[...], CONV_PAD - (conv_w - 1), conv_w, ts)
    vbuf[pl.ds(0, CONV_PAD), :] = vbuf[pl.ds(ts, CONV_PAD), :]
    mu = jnp.mean(acc, axis=-1, keepdims=True)
    cen = acc - mu
    var = jnp.mean(cen * cen, axis=-1, keepdims=True)
    yc = cen * lax.rsqrt(var + EPS) * lng_ref[...] + lnb_ref[...]
    yc = yc * _sigmoid(yc)

    u_gate = u[:, 2 * d_conv:2 * d_conv + d_lru]
    rbuf[pl.ds(SUBLANES, ts), :] = u[:, 2 * d_conv + d_lru:]
    xr = _causal_tap_sum(rbuf, shf, lcw_ref, lcb_ref[...], SUBLANES - (lru_cw - 1), lru_cw, ts)
    rbuf[pl.ds(0, SUBLANES), :] = rbuf[pl.ds(ts, SUBLANES), :]
    gates = jnp.dot(xr.astype(BF16), wg_ref[...], preferred_element_type=F32) + bg_ref[...]
    r = _sigmoid(gates[:, :d_lru])
    i_g = _sigmoid(gates[:, d_lru:])
    lam = lam_ref[...]
    softplus_neg_lam = jnp.maximum(-lam, 0.0) + jnp.log1p(jnp.exp(-jnp.abs(lam)))
    log_a = (-LRU_C) * r * softplus_neg_lam
    a = jnp.exp(log_a)
    inp = jnp.sqrt(1.0 - jnp.exp(2.0 * log_a)) * (i_g * xr)

    n_grp = ts // SUBLANES
    a3 = a.reshape(n_grp, SUBLANES, d_lru)
    b3 = inp.reshape(n_grp, SUBLANES, d_lru)
    sub = lax.broadcasted_iota(I32, (n_grp, SUBLANES, d_lru), 1)
    for s in (1, 2, 4):
        a_sh = pltpu.roll(a3, s, axis=1)
        b_sh = pltpu.roll(b3, s, axis=1)
        m = sub >= s
        b3 = jnp.where(m, a3 * b_sh + b3, b3)
        a3 = jnp.where(m, a3 * a_sh, a3)
    a_s[...] = a3.reshape(ts, d_lru)
    b_s[...] = b3.reshape(ts, d_lru)

    def grp_body(g, carry):
        rows = pl.ds(pl.multiple_of(g * SUBLANES, SUBLANES), SUBLANES)
        hg = b_s[rows, :] + a_s[rows, :] * carry
        b_s[rows, :] = hg
        return jnp.broadcast_to(hg[SUBLANES - 1:SUBLANES, :], (SUBLANES, d_lru))

    hcar[...] = lax.fori_loop(0, n_grp, grp_body, hcar[...])
    yl = _gelu_tanh(u_gate) * b_s[...]

    mng = mng_ref[...]
    yc_n = _rms(yc) * mng[:, :d_conv]
    yl_n = _rms(yl) * mng[:, d_conv:]
    mixo = jnp.dot(yc_n.astype(BF16), wout_ref[pl.ds(0, d_conv), :], preferred_element_type=F32)
    mixo += jnp.dot(yl_n.astype(BF16), wout_ref[pl.ds(d_conv, d_lru), :], preferred_element_type=F32)
    x1 = x + gate1 * mixo
    x1_ref[...] = x1

    h2 = _rms(x1) * g2_ref[...] * (1.0 + scale2) + shift2
    dh = h2.shape[1] // 2
    h2_hi = h2.astype(BF16)
    _store_planes(h2p_ref, _pack_bf16_pair(h2[:, :dh], h2[:, dh:]))
    h2_lo = (h2 - h2_hi.astype(F32)).astype(BF16)
    nt_dims = (((1,), (1,)), ((), ()))
    wrt = wrt_ref[...]
    lg = lax.dot_general(wrt, h2_hi, nt_dims, preferred_element_type=F32)
    lg2 = lax.dot_general(wrt[:n_exp], h2_lo, nt_dims, preferred_element_type=F32)
    logits = lg[:n_exp] + lg[n_exp:] + lg2 + br_ref[...]

    eidx = lax.broadcasted_iota(I32, (n_exp, ts), 0)
    neg_inf = jnp.float32(-jnp.inf)
    work = logits
    vals, idxs, hots = [], [], []
    for _ in range(TOP_K):
        mval = jnp.max(work, axis=0, keepdims=True)
        midx = jnp.min(jnp.where(work == mval, eidx, n_exp), axis=0, keepdims=True)
        hot = eidx == midx
        vals.append(mval)
        idxs.append(midx)
        hots.append(hot)
        work = jnp.where(hot, neg_inf, work)
    exps = [jnp.exp(vk - vals[0]) for vk in vals]
    denom = exps[0] + exps[1] + exps[2] + exps[3]
    gate_rows = [ek / denom for ek in exps]

    sel = jnp.zeros((n_exp, ts), F32)
    for hot in hots:
        sel = sel + hot.astype(F32)
    tri = (lax.broadcasted_iota(I32, (ts, ts), 0) < lax.broadcasted_iota(I32, (ts, ts), 1)).astype(BF16)
    before = jnp.dot(sel.astype(BF16), tri, preferred_element_type=F32) + cnt_s[...]
    rank_rows = [jnp.sum(jnp.where(hot, before, 0.0), axis=0, keepdims=True) for hot in hots]
    cnt_s[...] = cnt_s[...] + jnp.sum(sel, axis=1, keepdims=True)
    cnt_ref[...] = cnt_s[:, :LANES]

    idx_ref[...] = jnp.concatenate(idxs, axis=0)
    rank_ref[...] = jnp.concatenate(rank_rows, axis=0).astype(I32)
    g4 = jnp.concatenate(gate_rows, axis=0)
    g_pad = jnp.concatenate([g4, jnp.zeros((LANES - TOP_K, ts), F32)], axis=0)
    gt_ref[...] = g_pad.T


def _mix_call(x, mod, p, b0, bsz):
    _, seq, dm = x.shape
    ts = min(SEQ_TILE, seq)
    nj = seq // ts
    d_conv = p["conv_w"].shape[1]
    d_lru = p["lru_conv_w"].shape[1]
    conv_w = p["conv_w"].shape[0]
    lru_cw = p["lru_conv_w"].shape[0]
    n_exp = p["wrt"].shape[0] // 2
    assert seq % ts == 0 and ts % LANES == 0 and conv_w - 1 <= CONV_PAD and lru_cw - 1 <= SUBLANES
    assert d_conv == d_lru, "the two convolutions share one row-shift scratch"

    def full(a):
        return pl.BlockSpec(a.shape, lambda b, j: (0,) * a.ndim)

    weights = [p["g1"], p["win"], p["conv_w"], p["conv_b"], p["ln_g"], p["ln_b"], p["lru_conv_w"],
               p["lru_conv_b"], p["wg"], p["bg"], p["lam"], p["mng"], p["wout"], p["g2"], p["wrt"], p["br"]]
    kern = functools.partial(_mix_kernel, ts=ts, d_conv=d_conv, d_lru=d_lru, conv_w=conv_w,
                             lru_cw=lru_cw, n_exp=n_exp)
    out_shape = (
        jax.ShapeDtypeStruct((bsz, seq, dm), F32),
        jax.ShapeDtypeStruct((SC_SPLIT, bsz * seq, dm // 2 // SC_SPLIT), I32),
        jax.ShapeDtypeStruct((TOP_K, bsz * seq), I32),
        jax.ShapeDtypeStruct((TOP_K, bsz * seq), I32),
        jax.ShapeDtypeStruct((bsz, seq, LANES), F32),
        jax.ShapeDtypeStruct((n_exp, LANES), F32),
    )
    return pl.pallas_call(
        kern,
        grid=(bsz, nj),
        in_specs=[pl.BlockSpec((None, ts, dm), lambda b, j: (b + b0, j, 0)),
                  pl.BlockSpec((None, N_MOD, dm), lambda b, j: (b + b0, 0, 0))] + [full(w) for w in weights],
        out_specs=(
            pl.BlockSpec((None, ts, dm), lambda b, j: (b, j, 0)),
            pl.BlockSpec((SC_SPLIT, ts, dm // 2 // SC_SPLIT), lambda b, j: (0, b * nj + j, 0)),
            pl.BlockSpec((TOP_K, ts), lambda b, j: (0, b * nj + j)),
            pl.BlockSpec((TOP_K, ts), lambda b, j: (0, b * nj + j)),
            pl.BlockSpec((None, ts, LANES), lambda b, j: (b, j, 0)),
            pl.BlockSpec((n_exp, LANES), lambda b, j: (0, 0)),
        ),
        out_shape=out_shape,
        scratch_shapes=[
            pltpu.VMEM((CONV_PAD + ts, d_conv), F32),
            pltpu.VMEM((SUBLANES + ts, d_lru), F32),
            pltpu.VMEM((CONV_PAD + ts, max(d_conv, d_lru)), F32),
            pltpu.VMEM((ts, d_lru), F32),
            pltpu.VMEM((ts, d_lru), F32),
            pltpu.VMEM((SUBLANES, d_lru), F32),
            pltpu.VMEM((n_exp, ts), F32),
        ],
        compiler_params=pltpu.CompilerParams(dimension_semantics=("arbitrary", "arbitrary"),
                                             vmem_limit_bytes=VMEM_LIMIT_BYTES),
        name="token_mix_route",
    )(x, mod, *weights)


def _sc_mesh():
    return plsc.VectorSubcoreMesh(core_axis_name="core", subcore_axis_name="subcore",
                                  num_cores=SC_CORES, num_subcores=SC_SUBCORES)


def _sc_scatter_rows(rows, idx, n_out):
    n, w = rows.shape
    n_k = idx.shape[0]
    assert n % (SC_WINDOW * SC_CORES * SC_SUBCORES) == 0

    @functools.partial(pl.kernel, out_type=jax.ShapeDtypeStruct((n_out, w), rows.dtype), mesh=_sc_mesh(),
                       scratch_types=[], name="moe_dispatch_sc")
    def scatter(x_hbm, i_hbm, o_hbm):
        def body(x_vmem, *i_vmems):
            for i_vmem in i_vmems:
                pltpu.sync_copy(x_vmem, o_hbm.at[i_vmem.at[0]])

        pltpu.emit_pipeline(
            body,
            grid=(n // SC_WINDOW,),
            in_specs=[pl.BlockSpec((SC_WINDOW, w), lambda i: (i, 0))]
            + [pl.BlockSpec((1, SC_WINDOW), functools.partial(lambda k, i: (k, i), k)) for k in range(n_k)],
            out_specs=[],
            core_axis_name=("core", "subcore"),
            dimension_semantics=(pltpu.PARALLEL,),
        )(x_hbm, *([i_hbm] * n_k))

    return scatter(rows, idx)


def _sc_gather_rows(table, idx):
    m = idx.shape[1]
    w = table.shape[1]
    assert m % (SC_WINDOW * SC_CORES * SC_SUBCORES) == 0

    @functools.partial(pl.kernel, out_type=jax.ShapeDtypeStruct((m, w), table.dtype), mesh=_sc_mesh(),
                       scratch_types=[], name="moe_gather_sc")
    def gather(x_hbm, i_hbm, o_hbm):
        def body(i_vmem, o_vmem):
            pltpu.sync_copy(x_hbm.at[i_vmem.at[0]], o_vmem)

        pltpu.emit_pipeline(
            body,
            grid=(m // SC_WINDOW,),
            in_specs=[pl.BlockSpec((1, SC_WINDOW), lambda i: (0, i))],
            out_specs=[pl.BlockSpec((SC_WINDOW, w), lambda i: (i, 0))],
            core_axis_name=("core", "subcore"),
            dimension_semantics=(pltpu.PARALLEL,),
        )(i_hbm, o_hbm)

    return gather(table, idx)


def _moe_kernel(be_ref, nb_ref, xs_ref, wgu_ref, bgu_ref, wd_ref, bd_ref, ys_ref, wgu_b, wd_b, *, d_ff):
    i = pl.program_id(0)
    live = i < nb_ref[0]
    new_expert = (i == 0) | (be_ref[i] != be_ref[jnp.maximum(i - 1, 0)])

    @pl.when(live & new_expert)
    def _():
        wgu_b[...] = wgu_ref[...].astype(BF16)
        wd_b[...] = wd_ref[...].astype(BF16)

    @pl.when(live)
    def _():
        x_lo, x_hi = _unpack_bf16_pair(_load_planes(xs_ref))
        dh = x_lo.shape[1]
        gu = jnp.dot(x_lo.astype(BF16), wgu_b[pl.ds(0, dh), :], preferred_element_type=F32)
        gu += jnp.dot(x_hi.astype(BF16), wgu_b[pl.ds(dh, dh), :], preferred_element_type=F32)
        gu += bgu_ref[...]
        g = jnp.minimum(gu[:, :d_ff], SWIGLU_LIMIT)
        u = jnp.clip(gu[:, d_ff:], -SWIGLU_LIMIT, SWIGLU_LIMIT)
        act = (u + 1.0) * (g * _sigmoid(SWIGLU_ALPHA * g))
        y = jnp.dot(act.astype(BF16), wd_b[...], preferred_element_type=F32) + bd_ref[...]
        dm2 = y.shape[1] // 2
        _store_planes(ys_ref, _pack_bf16_pair(y[:, :dm2], y[:, dm2:]))


def _moe_call(xs, block_expert, n_used, wgu, bgu, wd, bd):
    _, n_rows, wp = xs.shape
    n_exp, dm, d_ff2 = wgu.shape
    d_ff = d_ff2 // 2
    bm = MOE_BLOCK
    n_blocks = n_rows // bm

    def row_map(i, be, nb):
        return (0, jnp.minimum(i, nb[0] - 1), 0)

    def w_map(i, be, nb):
        return (be[jnp.minimum(i, nb[0] - 1)], 0, 0)

    grid_spec = pltpu.PrefetchScalarGridSpec(
        num_scalar_prefetch=2,
        grid=(n_blocks,),
        in_specs=[
            pl.BlockSpec((SC_SPLIT, bm, wp), row_map),
            pl.BlockSpec((None, dm, d_ff2), w_map),
            pl.BlockSpec((None, 1, d_ff2), w_map),
            pl.BlockSpec((None, d_ff, dm), w_map),
            pl.BlockSpec((None, 1, dm), w_map),
        ],
        out_specs=pl.BlockSpec((SC_SPLIT, bm, wp), row_map),
        scratch_shapes=[pltpu.VMEM((dm, d_ff2), BF16), pltpu.VMEM((d_ff, dm), BF16)],
    )
    return pl.pallas_call(
        functools.partial(_moe_kernel, d_ff=d_ff),
        grid_spec=grid_spec,
        out_shape=jax.ShapeDtypeStruct(xs.shape, I32),
        compiler_params=pltpu.CompilerParams(dimension_semantics=("arbitrary",),
                                             vmem_limit_bytes=VMEM_LIMIT_BYTES),
        name="moe_experts",
    )(block_expert, n_used, xs, wgu, bgu, wd, bd)


def _combine_kernel(x1_ref, gt_ref, mod_ref, gf_ref, yg_ref, *out_refs):
    o_ref = out_refs[-1]
    gt = gt_ref[...]
    ts = yg_ref.shape[2]
    dh = yg_ref.shape[3] * SC_SPLIT
    acc_lo = jnp.zeros((ts, dh), F32)
    acc_hi = jnp.zeros((ts, dh), F32)
    for k in range(TOP_K):
        lo, hi = _unpack_bf16_pair(_load_planes(yg_ref.at[k]))
        gk = gt[:, k:k + 1]
        acc_lo += gk * lo
        acc_hi += gk * hi
    y = jnp.concatenate([acc_lo, acc_hi], axis=1)
    gate2 = mod_ref[...][5:6]
    x2 = x1_ref[...] + gate2 * y
    o_ref[...] = _rms(x2) * gf_ref[...]


def _combine_call(x1, gt, mod, gf, yg, out_prev, b0):
    bsz, seq, dm = x1.shape
    ts = min(SEQ_TILE, seq)
    nj = seq // ts
    in_specs = [
        pl.BlockSpec((None, ts, dm), lambda b, j: (b, j, 0)),
        pl.BlockSpec((None, ts, LANES), lambda b, j: (b, j, 0)),
        pl.BlockSpec((None, N_MOD, dm), lambda b, j: (b + b0, 0, 0)),
        pl.BlockSpec((1, dm), lambda b, j: (0, 0)),
        pl.BlockSpec((TOP_K, SC_SPLIT, ts, yg.shape[3]), lambda b, j: (0, 0, b * nj + j, 0)),
    ]
    args = [x1, gt, mod, gf, yg]
    aliases = {}
    if out_prev is not None:
        in_specs.append(pl.BlockSpec(memory_space=pl.ANY))
        args.append(out_prev)
        aliases = {len(args) - 1: 0}
    return pl.pallas_call(
        _combine_kernel,
        grid=(bsz, nj),
        in_specs=in_specs,
        out_specs=pl.BlockSpec((None, ts, dm), lambda b, j: (b + b0, j, 0)),
        out_shape=jax.ShapeDtypeStruct((mod.shape[0], seq, dm), F32),
        input_output_aliases=aliases,
        compiler_params=pltpu.CompilerParams(dimension_semantics=("arbitrary", "arbitrary"),
                                             vmem_limit_bytes=VMEM_LIMIT_BYTES),
        name="moe_combine",
    )(*args)


def _block_diag(w):
    n_h, d, _ = w.shape
    eye = jnp.eye(n_h, dtype=w.dtype)
    return (eye[:, None, :, None] * w[:, :, None, :]).reshape(n_h * d, n_h * d)


def _layer(x, mod, l, w_in, norm1_g, conv_w, conv_b, conv_ln_g, conv_ln_b, lru_conv_w, lru_conv_b,
           lru_w_a, lru_b_a, lru_w_x, lru_b_x, lru_lambda, mix_norm_g, w_out, norm2_g, w_router,
           b_router, w_gate_up, b_gate_up, w_down, b_down, out_gain):
    bsz, seq, dm = x.shape
    n_tok = bsz * seq
    n_exp = w_router.shape[-1]
    row = lambda a: a.reshape(1, -1)
    wr = w_router[l]
    wr_hi = wr.astype(BF16)
    wr_lo = (wr - wr_hi.astype(F32)).astype(BF16)
    params = dict(
        g1=row(norm1_g[l]), win=w_in[l].astype(BF16), conv_w=conv_w[l], conv_b=row(conv_b[l]),
        ln_g=row(conv_ln_g[l]), ln_b=row(conv_ln_b[l]), lru_conv_w=lru_conv_w[l], lru_conv_b=row(lru_conv_b[l]),
        wg=jnp.concatenate([_block_diag(lru_w_a[l]), _block_diag(lru_w_x[l])], axis=1).astype(BF16),
        bg=jnp.concatenate([lru_b_a[l].reshape(1, -1), lru_b_x[l].reshape(1, -1)], axis=1),
        lam=row(lru_lambda[l]), mng=row(mix_norm_g[l]), wout=w_out[l].astype(BF16), g2=row(norm2_g[l]),
        wrt=jnp.concatenate([wr_hi.T, wr_lo.T], axis=0), br=b_router[l].reshape(n_exp, 1),
    )
    n_chunks = N_CHUNKS if bsz % N_CHUNKS == 0 else 1
    cb = bsz // n_chunks
    out = None
    for ci in range(n_chunks):
        out = _chunk(x, mod, params, ci * cb, cb, out, w_gate_up[l], b_gate_up[l][:, None, :], w_down[l],
                     b_down[l][:, None, :], out_gain)
    return out


def _chunk(x, mod, params, b0, cb, out_prev, wgu, bgu, wd, bd, out_gain):
    seq, dm = x.shape[1:]
    n_tok = cb * seq
    n_exp = wgu.shape[0]
    x1, h2p, top_idx, rank, gt, cnt = _mix_call(x, mod, params, b0, cb)

    bm = MOE_BLOCK
    counts = cnt[:, 0].astype(I32)
    padded = (counts + bm - 1) // bm * bm
    e_ids = jnp.arange(n_exp, dtype=I32)
    pad_ends = jnp.sum(jnp.where(e_ids[None, :] <= e_ids[:, None], padded[None, :], 0), axis=1)
    pad_starts = pad_ends - padded
    n_rows = (n_tok * TOP_K // bm + n_exp) * bm
    n_blocks = n_rows // bm
    dest = rank + jnp.sum(jnp.where(top_idx[..., None] == e_ids, pad_starts, 0), axis=-1)
    block_start = jnp.arange(n_blocks, dtype=I32) * bm
    block_expert = jnp.minimum(
        jnp.sum((block_start[:, None] >= pad_ends[None, :]).astype(I32), axis=1), n_exp - 1)
    n_used = pad_ends[-1:] // bm

    wp = h2p.shape[2]
    plane = jnp.arange(SC_SPLIT, dtype=I32)[None, :, None] * n_rows
    dest_sub = dest[:, None, :] + plane
    xs = _sc_scatter_rows(h2p.reshape(SC_SPLIT * n_tok, wp), dest_sub.reshape(TOP_K, SC_SPLIT * n_tok),
                          SC_SPLIT * n_rows).reshape(SC_SPLIT, n_rows, wp)
    ys = _moe_call(xs, block_expert, n_used, wgu, bgu, wd, bd)
    yg = _sc_gather_rows(ys.reshape(SC_SPLIT * n_rows, wp), dest_sub.reshape(1, TOP_K * SC_SPLIT * n_tok))
    return _combine_call(x1, gt, mod, out_gain, yg.reshape(TOP_K, SC_SPLIT, n_tok, wp), out_prev, b0)


def kernel(x, c, w_ada, b_ada, norm1_g, w_in, conv_w, conv_b, conv_ln_g, conv_ln_b, lru_conv_w, lru_conv_b,
           lru_w_a, lru_b_a, lru_w_x, lru_b_x, lru_lambda, mix_norm_g, w_out, norm2_g, w_router, b_router,
           w_gate_up, b_gate_up, w_down, b_down, final_norm_g):
    depth = w_ada.shape[0]
    assert depth == 1, "the final norm is fused into the (single) layer's combine kernel"
    bsz, seq, dm = x.shape
    mod = _ada_call(c, w_ada[0], b_ada[0]).reshape(bsz, N_MOD, dm)
    return _layer(x, mod, 0, w_in, norm1_g, conv_w, conv_b, conv_ln_g, conv_ln_b, lru_conv_w, lru_conv_b,
                  lru_w_a, lru_b_a, lru_w_x, lru_b_x, lru_lambda, mix_norm_g, w_out, norm2_g, w_router,
                  b_router, w_gate_up, b_gate_up, w_down, b_down, row_gain(final_norm_g))


def row_gain(g):
    return g.reshape(1, -1)
```

```python
import functools

import jax
import jax.numpy as jnp
from jax import lax
from jax.experimental import pallas as pl
from jax.experimental.pallas import tpu as pltpu
from jax.experimental.pallas import tpu_sc as plsc

F32 = jnp.float32
BF16 = jnp.bfloat16
I32 = jnp.int32

EPS = 1e-6
N_MOD = 6
LRU_C = 8.0
TOP_K = 4
SWIGLU_ALPHA = 1.702
SWIGLU_LIMIT = 7.0

LANES = 128
SUBLANES = 8
VMEM_LIMIT_BYTES = 56 * 1024 * 1024

SEQ_TILE = 512
CONV_PAD = 32
MOE_BLOCK = 512
N_CHUNKS = 2
HI_MASK = -65536

SC_CORES = 2
SC_SUBCORES = 16
SC_WINDOW = 128
SC_SPLIT = 2


def _sigmoid(x):
    return jax.nn.sigmoid(x)


def _pack_bf16_pair(lo_f32, hi_f32):
    lo_bits = lax.bitcast_convert_type(lo_f32.astype(BF16).astype(F32), I32)
    hi_bits = lax.bitcast_convert_type(hi_f32.astype(BF16).astype(F32), I32)
    return lax.shift_right_logical(lo_bits, 16) | hi_bits


def _unpack_bf16_pair(p):
    lo = lax.bitcast_convert_type(lax.shift_left(p, 16), F32)
    hi = lax.bitcast_convert_type(p & HI_MASK, F32)
    return lo, hi


def _store_planes(ref, packed):
    wp = packed.shape[1] // SC_SPLIT
    for s in range(SC_SPLIT):
        ref[s] = packed[:, s * wp:(s + 1) * wp]


def _load_planes(ref):
    return jnp.concatenate([ref[s] for s in range(SC_SPLIT)], axis=1)


def _ada_kernel(c_ref, w_ref, b_ref, o_ref):
    c = c_ref[...]
    ca = c * _sigmoid(c)
    w = w_ref[...]
    c_hi = ca.astype(BF16)
    c_lo = (ca - c_hi.astype(F32)).astype(BF16)
    w_hi = w.astype(BF16)
    w_lo = (w - w_hi.astype(F32)).astype(BF16)
    acc = jnp.dot(c_hi, w_hi, preferred_element_type=F32)
    acc += jnp.dot(c_lo, w_hi, preferred_element_type=F32)
    acc += jnp.dot(c_hi, w_lo, preferred_element_type=F32)
    o_ref[...] = acc + b_ref[...]


def _ada_call(c, w_ada, b_ada):
    bsz, dm = c.shape
    n_out = w_ada.shape[1]
    tn = 1024
    return pl.pallas_call(
        _ada_kernel,
        grid=(n_out // tn,),
        in_specs=[
            pl.BlockSpec((bsz, dm), lambda n: (0, 0)),
            pl.BlockSpec((dm, tn), lambda n: (0, n)),
            pl.BlockSpec((1, tn), lambda n: (0, n)),
        ],
        out_specs=pl.BlockSpec((bsz, tn), lambda n: (0, n)),
        out_shape=jax.ShapeDtypeStruct((bsz, n_out), F32),
        compiler_params=pltpu.CompilerParams(dimension_semantics=("arbitrary",)),
        name="ada_mod",
    )(c, w_ada, b_ada.reshape(1, n_out))


def _rms(x, eps=EPS):
    return x * lax.rsqrt(jnp.mean(x * x, axis=-1, keepdims=True) + eps)


def _gelu_tanh(x):
    return 0.5 * x * (1.0 + jnp.tanh(0.7978845608028654 * (x + 0.044715 * (x * x * x))))


def _causal_tap_sum(buf, shifted, w_ref, bias_row, first, n_taps, ts):
    acc = jnp.broadcast_to(bias_row, (ts, buf.shape[1]))
    for r in range(SUBLANES):
        taps = [k for k in range(n_taps) if (first + k) % SUBLANES == r]
        if not taps:
            continue
        src = buf
        if r:
            span = max(first + k for k in taps) - r + ts
            shifted[pl.ds(0, span), :] = buf[pl.ds(r, span), :]
            src = shifted
        for k in taps:
            acc = acc + w_ref[pl.ds(k, 1), :] * src[pl.ds(first + k - r, ts), :]
    return acc


def _mix_kernel(x_ref, mod_ref, g1_ref, win_ref, cw_ref, cb_ref, lng_ref, lnb_ref,
                lcw_ref, lcb_ref, wg_ref, bg_ref, lam_ref, mng_ref, wout_ref, g2_ref,
                wrt_ref, br_ref,
                x1_ref, h2p_ref, idx_ref, rank_ref, gt_ref, cnt_ref,
                vbuf, rbuf, shf, a_s, b_s, hcar, cnt_s,
                *, ts, d_conv, d_lru, conv_w, lru_cw, n_exp):
    b = pl.program_id(0)
    j = pl.program_id(1)

    @pl.when(j == 0)
    def _():
        vbuf[pl.ds(0, CONV_PAD), :] = jnp.zeros((CONV_PAD, d_conv), F32)
        rbuf[pl.ds(0, SUBLANES), :] = jnp.zeros((SUBLANES, d_lru), F32)
        hcar[...] = jnp.zeros_like(hcar)

    @pl.when((b == 0) & (j == 0))
    def _():
        cnt_s[...] = jnp.zeros_like(cnt_s)

    x = x_ref[...]
    mod = mod_ref[...]
    shift1, scale1, gate1 = mod[0:1], mod[1:2], mod[2:3]
    shift2, scale2, gate2 = mod[3:4], mod[4:5], mod[5:6]
    del gate2

    h = _rms(x) * g1_ref[...] * (1.0 + scale1) + shift1
    u = jnp.dot(h.astype(BF16), win_ref[...], preferred_element_type=F32)

    v = u[:, :d_conv] * _sigmoid(u[:, d_conv:2 * d_conv])
    vbuf[pl.ds(CONV_PAD, ts), :] = v
    acc = _causal_tap_sum(vbuf, shf, cw_ref, cb_ref[...], CONV_PAD - (conv_w - 1), conv_w, ts)
    vbuf[pl.ds(0, CONV_PAD), :] = vbuf[pl.ds(ts, CONV_PAD), :]
    mu = jnp.mean(acc, axis=-1, keepdims=True)
    cen = acc - mu
    var = jnp.mean(cen * cen, axis=-1, keepdims=True)
    yc = cen * lax.rsqrt(var + EPS) * lng_ref[...] + lnb_ref[...]
    yc = yc * _sigmoid(yc)

    u_gate = u[:, 2 * d_conv:2 * d_conv + d_lru]
    rbuf[pl.ds(SUBLANES, ts), :] = u[:, 2 * d_conv + d_lru:]
    xr = _causal_tap_sum(rbuf, shf, lcw_ref, lcb_ref[...], SUBLANES - (lru_cw - 1), lru_cw, ts)
    rbuf[pl.ds(0, SUBLANES), :] = rbuf[pl.ds(ts, SUBLANES), :]
    gates = jnp.dot(xr.astype(BF16), wg_ref[...], preferred_element_type=F32) + bg_ref[...]
    r = _sigmoid(gates[:, :d_lru])
    i_g = _sigmoid(gates[:, d_lru:])
    lam = lam_ref[...]
    softplus_neg_lam = jnp.maximum(-lam, 0.0) + jnp.log1p(jnp.exp(-jnp.abs(lam)))
    log_a = (-LRU_C) * r * softplus_neg_lam
    a = jnp.exp(log_a)
    inp = jnp.sqrt(1.0 - jnp.exp(2.0 * log_a)) * (i_g * xr)

    n_grp = ts // SUBLANES
    a3 = a.reshape(n_grp, SUBLANES, d_lru)
    b3 = inp.reshape(n_grp, SUBLANES, d_lru)
    sub = lax.broadcasted_iota(I32, (n_grp, SUBLANES, d_lru), 1)
    for s in (1, 2, 4):
        a_sh = pltpu.roll(a3, s, axis=1)
        b_sh = pltpu.roll(b3, s, axis=1)
        m = sub >= s
        b3 = jnp.where(m, a3 * b_sh + b3, b3)
        a3 = jnp.where(m, a3 * a_sh, a3)
    a_s[...] = a3.reshape(ts, d_lru)
    b_s[...] = b3.reshape(ts, d_lru)

    def grp_body(g, carry):
        rows = pl.ds(pl.multiple_of(g * SUBLANES, SUBLANES), SUBLANES)
        hg = b_s[rows, :] + a_s[rows, :] * carry
        b_s[rows, :] = hg
        return jnp.broadcast_to(hg[SUBLANES - 1:SUBLANES, :], (SUBLANES, d_lru))

    hcar[...] = lax.fori_loop(0, n_grp, grp_body, hcar[...])
    yl = _gelu_tanh(u_gate) * b_s[...]

    mng = mng_ref[...]
    yc_n = _rms(yc) * mng[:, :d_conv]
    yl_n = _rms(yl) * mng[:, d_conv:]
    mixo = jnp.dot(yc_n.astype(BF16), wout_ref[pl.ds(0, d_conv), :], preferred_element_type=F32)
    mixo += jnp.dot(yl_n.astype(BF16), wout_ref[pl.ds(d_conv, d_lru), :], preferred_element_type=F32)
    x1 = x + gate1 * mixo
    x1_ref[...] = x1

    h2 = _rms(x1) * g2_ref[...] * (1.0 + scale2) + shift2
    dh = h2.shape[1] // 2
    h2_hi = h2.astype(BF16)
    _store_planes(h2p_ref, _pack_bf16_pair(h2[:, :dh], h2[:, dh:]))
    h2_lo = (h2 - h2_hi.astype(F32)).astype(BF16)
    nt_dims = (((1,), (1,)), ((), ()))
    wrt = wrt_ref[...]
    lg = lax.dot_general(wrt, h2_hi, nt_dims, preferred_element_type=F32)
    lg2 = lax.dot_general(wrt[:n_exp], h2_lo, nt_dims, preferred_element_type=F32)
    logits = lg[:n_exp] + lg[n_exp:] + lg2 + br_ref[...]

    eidx = lax.broadcasted_iota(I32, (n_exp, ts), 0)
    neg_inf = jnp.float32(-jnp.inf)
    work = logits
    vals, idxs, hots = [], [], []
    for _ in range(TOP_K):
        mval = jnp.max(work, axis=0, keepdims=True)
        midx = jnp.min(jnp.where(work == mval, eidx, n_exp), axis=0, keepdims=True)
        hot = eidx == midx
        vals.append(mval)
        idxs.append(midx)
        hots.append(hot)
        work = jnp.where(hot, neg_inf, work)
    exps = [jnp.exp(vk - vals[0]) for vk in vals]
    denom = exps[0] + exps[1] + exps[2] + exps[3]
    gate_rows = [ek / denom for ek in exps]

    sel = jnp.zeros((n_exp, ts), F32)
    for hot in hots:
        sel = sel + hot.astype(F32)
    tri = (lax.broadcasted_iota(I32, (ts, ts), 0) < lax.broadcasted_iota(I32, (ts, ts), 1)).astype(BF16)
    before = jnp.dot(sel.astype(BF16), tri, preferred_element_type=F32) + cnt_s[...]
    rank_rows = [jnp.sum(jnp.where(hot, before, 0.0), axis=0, keepdims=True) for hot in hots]
    cnt_s[...] = cnt_s[...] + jnp.sum(sel, axis=1, keepdims=True)
    cnt_ref[...] = cnt_s[:, :LANES]

    idx_ref[...] = jnp.concatenate(idxs, axis=0)
    rank_ref[...] = jnp.concatenate(rank_rows, axis=0).astype(I32)
    g4 = jnp.concatenate(gate_rows, axis=0)
    g_pad = jnp.concatenate([g4, jnp.zeros((LANES - TOP_K, ts), F32)], axis=0)
    gt_ref[...] = g_pad.T


def _mix_call(x, mod, p, b0, bsz):
    _, seq, dm = x.shape
    ts = min(SEQ_TILE, seq)
    nj = seq // ts
    d_conv = p["conv_w"].shape[1]
    d_lru = p["lru_conv_w"].shape[1]
    conv_w = p["conv_w"].shape[0]
    lru_cw = p["lru_conv_w"].shape[0]
    n_exp = p["wrt"].shape[0] // 2
    assert seq % ts == 0 and ts % LANES == 0 and conv_w - 1 <= CONV_PAD and lru_cw - 1 <= SUBLANES
    assert d_conv == d_lru, "the two convolutions share one row-shift scratch"

    def full(a):
        return pl.BlockSpec(a.shape, lambda b, j: (0,) * a.ndim)

    weights = [p["g1"], p["win"], p["conv_w"], p["conv_b"], p["ln_g"], p["ln_b"], p["lru_conv_w"],
               p["lru_conv_b"], p["wg"], p["bg"], p["lam"], p["mng"], p["wout"], p["g2"], p["wrt"], p["br"]]
    kern = functools.partial(_mix_kernel, ts=ts, d_conv=d_conv, d_lru=d_lru, conv_w=conv_w,
                             lru_cw=lru_cw, n_exp=n_exp)
    out_shape = (
        jax.ShapeDtypeStruct((bsz, seq, dm), F32),
        jax.ShapeDtypeStruct((SC_SPLIT, bsz * seq, dm // 2 // SC_SPLIT), I32),
        jax.ShapeDtypeStruct((TOP_K, bsz * seq), I32),
        jax.ShapeDtypeStruct((TOP_K, bsz * seq), I32),
        jax.ShapeDtypeStruct((bsz, seq, LANES), F32),
        jax.ShapeDtypeStruct((n_exp, LANES), F32),
    )
    return pl.pallas_call(
        kern,
        grid=(bsz, nj),
        in_specs=[pl.BlockSpec((None, ts, dm), lambda b, j: (b + b0, j, 0)),
                  pl.BlockSpec((None, N_MOD, dm), lambda b, j: (b + b0, 0, 0))] + [full(w) for w in weights],
        out_specs=(
            pl.BlockSpec((None, ts, dm), lambda b, j: (b, j, 0)),
            pl.BlockSpec((SC_SPLIT, ts, dm // 2 // SC_SPLIT), lambda b, j: (0, b * nj + j, 0)),
            pl.BlockSpec((TOP_K, ts), lambda b, j: (0, b * nj + j)),
            pl.BlockSpec((TOP_K, ts), lambda b, j: (0, b * nj + j)),
            pl.BlockSpec((None, ts, LANES), lambda b, j: (b, j, 0)),
            pl.BlockSpec((n_exp, LANES), lambda b, j: (0, 0)),
        ),
        out_shape=out_shape,
        scratch_shapes=[
            pltpu.VMEM((CONV_PAD + ts, d_conv), F32),
            pltpu.VMEM((SUBLANES + ts, d_lru), F32),
            pltpu.VMEM((CONV_PAD + ts, max(d_conv, d_lru)), F32),
            pltpu.VMEM((ts, d_lru), F32),
            pltpu.VMEM((ts, d_lru), F32),
            pltpu.VMEM((SUBLANES, d_lru), F32),
            pltpu.VMEM((n_exp, ts), F32),
        ],
        compiler_params=pltpu.CompilerParams(dimension_semantics=("arbitrary", "arbitrary"),
                                             vmem_limit_bytes=VMEM_LIMIT_BYTES),
        name="token_mix_route",
    )(x, mod, *weights)


def _sc_mesh():
    return plsc.VectorSubcoreMesh(core_axis_name="core", subcore_axis_name="subcore",
                                  num_cores=SC_CORES, num_subcores=SC_SUBCORES)


def _sc_scatter_rows(rows, idx, n_out):
    n, w = rows.shape
    n_k = idx.shape[0]
    assert n % (SC_WINDOW * SC_CORES * SC_SUBCORES) == 0

    @functools.partial(pl.kernel, out_type=jax.ShapeDtypeStruct((n_out, w), rows.dtype), mesh=_sc_mesh(),
                       scratch_types=[], name="moe_dispatch_sc")
    def scatter(x_hbm, i_hbm, o_hbm):
        def body(x_vmem, *i_vmems):
            for i_vmem in i_vmems:
                pltpu.sync_copy(x_vmem, o_hbm.at[i_vmem.at[0]])

        pltpu.emit_pipeline(
            body,
            grid=(n // SC_WINDOW,),
            in_specs=[pl.BlockSpec((SC_WINDOW, w), lambda i: (i, 0))]
            + [pl.BlockSpec((1, SC_WINDOW), functools.partial(lambda k, i: (k, i), k)) for k in range(n_k)],
            out_specs=[],
            core_axis_name=("core", "subcore"),
            dimension_semantics=(pltpu.PARALLEL,),
        )(x_hbm, *([i_hbm] * n_k))

    return scatter(rows, idx)


def _sc_gather_rows(table, idx):
    m = idx.shape[1]
    w = table.shape[1]
    assert m % (SC_WINDOW * SC_CORES * SC_SUBCORES) == 0

    @functools.partial(pl.kernel, out_type=jax.ShapeDtypeStruct((m, w), table.dtype), mesh=_sc_mesh(),
                       scratch_types=[], name="moe_gather_sc")
    def gather(x_hbm, i_hbm, o_hbm):
        def body(i_vmem, o_vmem):
            pltpu.sync_copy(x_hbm.at[i_vmem.at[0]], o_vmem)

        pltpu.emit_pipeline(
            body,
            grid=(m // SC_WINDOW,),
            in_specs=[pl.BlockSpec((1, SC_WINDOW), lambda i: (0, i))],
            out_specs=[pl.BlockSpec((SC_WINDOW, w), lambda i: (i, 0))],
            core_axis_name=("core", "subcore"),
            dimension_semantics=(pltpu.PARALLEL,),
        )(i_hbm, o_hbm)

    return gather(table, idx)


def _moe_kernel(be_ref, nb_ref, slot_ref, nxt_ref, xs_ref, wgu_hbm, bgu_ref, wd_hbm, bd_ref, ys_ref,
                wgu_f, wd_f, wgu_b, wd_b, sems, *, d_ff):
    i = pl.program_id(0)
    live = i < nb_ref[0]
    new_expert = (i == 0) | (be_ref[i] != be_ref[jnp.maximum(i - 1, 0)])

    def weight_copies(e, s):
        return (pltpu.make_async_copy(wgu_hbm.at[e], wgu_f.at[s], sems.at[0, s]),
                pltpu.make_async_copy(wd_hbm.at[e], wd_f.at[s], sems.at[1, s]))

    @pl.when(live & new_expert)
    def _():
        e, s, nx = be_ref[i], slot_ref[i], nxt_ref[i]

        @pl.when(i == 0)
        def _():
            for cp in weight_copies(e, s):
                cp.start()

        for cp in weight_copies(e, s):
            cp.wait()

        @pl.when(nx >= 0)
        def _():
            for cp in weight_copies(nx, 1 - s):
                cp.start()

        wgu_b[...] = wgu_f[s].astype(BF16)
        wd_b[...] = wd_f[s].astype(BF16)

    @pl.when(live)
    def _():
        x_lo, x_hi = _unpack_bf16_pair(_load_planes(xs_ref))
        dh = x_lo.shape[1]
        gu = jnp.dot(x_lo.astype(BF16), wgu_b[pl.ds(0, dh), :], preferred_element_type=F32)
        gu += jnp.dot(x_hi.astype(BF16), wgu_b[pl.ds(dh, dh), :], preferred_element_type=F32)
        gu += bgu_ref[...]
        g = jnp.minimum(gu[:, :d_ff], SWIGLU_LIMIT)
        u = jnp.clip(gu[:, d_ff:], -SWIGLU_LIMIT, SWIGLU_LIMIT)
        act = (u + 1.0) * (g * _sigmoid(SWIGLU_ALPHA * g))
        y = jnp.dot(act.astype(BF16), wd_b[...], preferred_element_type=F32) + bd_ref[...]
        dm2 = y.shape[1] // 2
        _store_planes(ys_ref, _pack_bf16_pair(y[:, :dm2], y[:, dm2:]))


def _moe_call(xs, block_expert, n_used, seg_slot, next_expert, wgu, bgu, wd, bd):
    _, n_rows, wp = xs.shape
    n_exp, dm, d_ff2 = wgu.shape
    d_ff = d_ff2 // 2
    bm = MOE_BLOCK
    n_blocks = n_rows // bm

    def row_map(i, be, nb, slot, nxt):
        return (0, jnp.minimum(i, nb[0] - 1), 0)

    def b_map(i, be, nb, slot, nxt):
        return (be[jnp.minimum(i, nb[0] - 1)], 0, 0)

    grid_spec = pltpu.PrefetchScalarGridSpec(
        num_scalar_prefetch=4,
        grid=(n_blocks,),
        in_specs=[
            pl.BlockSpec((SC_SPLIT, bm, wp), row_map),
            pl.BlockSpec(memory_space=pl.ANY),
            pl.BlockSpec((None, 1, d_ff2), b_map),
            pl.BlockSpec(memory_space=pl.ANY),
            pl.BlockSpec((None, 1, dm), b_map),
        ],
        out_specs=pl.BlockSpec((SC_SPLIT, bm, wp), row_map),
        scratch_shapes=[
            pltpu.VMEM((2, dm, d_ff2), F32), pltpu.VMEM((2, d_ff, dm), F32),
            pltpu.VMEM((dm, d_ff2), BF16), pltpu.VMEM((d_ff, dm), BF16),
            pltpu.SemaphoreType.DMA((2, 2)),
        ],
    )
    return pl.pallas_call(
        functools.partial(_moe_kernel, d_ff=d_ff),
        grid_spec=grid_spec,
        out_shape=jax.ShapeDtypeStruct(xs.shape, I32),
        compiler_params=pltpu.CompilerParams(dimension_semantics=("arbitrary",),
                                             vmem_limit_bytes=VMEM_LIMIT_BYTES),
        name="moe_experts",
    )(block_expert, n_used, seg_slot, next_expert, xs, wgu, bgu, wd, bd)


def _combine_kernel(x1_ref, gt_ref, mod_ref, gf_ref, yg_ref, *out_refs):
    o_ref = out_refs[-1]
    gt = gt_ref[...]
    ts = yg_ref.shape[2]
    dh = yg_ref.shape[3] * SC_SPLIT
    acc_lo = jnp.zeros((ts, dh), F32)
    acc_hi = jnp.zeros((ts, dh), F32)
    for k in range(TOP_K):
        lo, hi = _unpack_bf16_pair(_load_planes(yg_ref.at[k]))
        gk = gt[:, k:k + 1]
        acc_lo += gk * lo
        acc_hi += gk * hi
    y = jnp.concatenate([acc_lo, acc_hi], axis=1)
    gate2 = mod_ref[...][5:6]
    x2 = x1_ref[...] + gate2 * y
    o_ref[...] = _rms(x2) * gf_ref[...]


def _combine_call(x1, gt, mod, gf, yg, out_prev, b0):
    bsz, seq, dm = x1.shape
    ts = min(SEQ_TILE, seq)
    nj = seq // ts
    in_specs = [
        pl.BlockSpec((None, ts, dm), lambda b, j: (b, j, 0)),
        pl.BlockSpec((None, ts, LANES), lambda b, j: (b, j, 0)),
        pl.BlockSpec((None, N_MOD, dm), lambda b, j: (b + b0, 0, 0)),
        pl.BlockSpec((1, dm), lambda b, j: (0, 0)),
        pl.BlockSpec((TOP_K, SC_SPLIT, ts, yg.shape[3]), lambda b, j: (0, 0, b * nj + j, 0)),
    ]
    args = [x1, gt, mod, gf, yg]
    aliases = {}
    if out_prev is not None:
        in_specs.append(pl.BlockSpec(memory_space=pl.ANY))
        args.append(out_prev)
        aliases = {len(args) - 1: 0}
    return pl.pallas_call(
        _combine_kernel,
        grid=(bsz, nj),
        in_specs=in_specs,
        out_specs=pl.BlockSpec((None, ts, dm), lambda b, j: (b + b0, j, 0)),
        out_shape=jax.ShapeDtypeStruct((mod.shape[0], seq, dm), F32),
        input_output_aliases=aliases,
        compiler_params=pltpu.CompilerParams(dimension_semantics=("arbitrary", "arbitrary"),
                                             vmem_limit_bytes=VMEM_LIMIT_BYTES),
        name="moe_combine",
    )(*args)


def _block_diag(w):
    n_h, d, _ = w.shape
    eye = jnp.eye(n_h, dtype=w.dtype)
    return (eye[:, None, :, None] * w[:, :, None, :]).reshape(n_h * d, n_h * d)


def _layer(x, mod, l, w_in, norm1_g, conv_w, conv_b, conv_ln_g, conv_ln_b, lru_conv_w, lru_conv_b,
           lru_w_a, lru_b_a, lru_w_x, lru_b_x, lru_lambda, mix_norm_g, w_out, norm2_g, w_router,
           b_router, w_gate_up, b_gate_up, w_down, b_down, out_gain):
    bsz, seq, dm = x.shape
    n_tok = bsz * seq
    n_exp = w_router.shape[-1]
    row = lambda a: a.reshape(1, -1)
    wr = w_router[l]
    wr_hi = wr.astype(BF16)
    wr_lo = (wr - wr_hi.astype(F32)).astype(BF16)
    params = dict(
        g1=row(norm1_g[l]), win=w_in[l].astype(BF16), conv_w=conv_w[l], conv_b=row(conv_b[l]),
        ln_g=row(conv_ln_g[l]), ln_b=row(conv_ln_b[l]), lru_conv_w=lru_conv_w[l], lru_conv_b=row(lru_conv_b[l]),
        wg=jnp.concatenate([_block_diag(lru_w_a[l]), _block_diag(lru_w_x[l])], axis=1).astype(BF16),
        bg=jnp.concatenate([lru_b_a[l].reshape(1, -1), lru_b_x[l].reshape(1, -1)], axis=1),
        lam=row(lru_lambda[l]), mng=row(mix_norm_g[l]), wout=w_out[l].astype(BF16), g2=row(norm2_g[l]),
        wrt=jnp.concatenate([wr_hi.T, wr_lo.T], axis=0), br=b_router[l].reshape(n_exp, 1),
    )
    n_chunks = N_CHUNKS if bsz % N_CHUNKS == 0 else 1
    cb = bsz // n_chunks
    out = None
    for ci in range(n_chunks):
        out = _chunk(x, mod, params, ci * cb, cb, out, w_gate_up[l], b_gate_up[l][:, None, :], w_down[l],
                     b_down[l][:, None, :], out_gain)
    return out


def _chunk(x, mod, params, b0, cb, out_prev, wgu, bgu, wd, bd, out_gain):
    seq, dm = x.shape[1:]
    n_tok = cb * seq
    n_exp = wgu.shape[0]
    x1, h2p, top_idx, rank, gt, cnt = _mix_call(x, mod, params, b0, cb)

    bm = MOE_BLOCK
    counts = cnt[:, 0].astype(I32)
    padded = (counts + bm - 1) // bm * bm
    e_ids = jnp.arange(n_exp, dtype=I32)
    pad_ends = jnp.sum(jnp.where(e_ids[None, :] <= e_ids[:, None], padded[None, :], 0), axis=1)
    pad_starts = pad_ends - padded
    n_rows = (n_tok * TOP_K // bm + n_exp) * bm
    n_blocks = n_rows // bm
    dest = rank + jnp.sum(jnp.where(top_idx[..., None] == e_ids, pad_starts, 0), axis=-1)
    block_start = jnp.arange(n_blocks, dtype=I32) * bm
    block_expert = jnp.minimum(
        jnp.sum((block_start[:, None] >= pad_ends[None, :]).astype(I32), axis=1), n_exp - 1)
    n_used = pad_ends[-1:] // bm
    present = counts > 0
    seg_ordinal = jnp.sum(jnp.where((e_ids[None, :] < e_ids[:, None]) & present[None, :], 1, 0), axis=1)
    following = jnp.min(jnp.where((e_ids[None, :] > e_ids[:, None]) & present[None, :], e_ids[None, :], n_exp),
                        axis=1)
    following = jnp.where(following == n_exp, -1, following)
    block_hot = block_expert[:, None] == e_ids[None, :]
    seg_slot = jnp.sum(jnp.where(block_hot, seg_ordinal[None, :] % 2, 0), axis=1)
    next_expert = jnp.sum(jnp.where(block_hot, following[None, :], 0), axis=1)

    wp = h2p.shape[2]
    plane = jnp.arange(SC_SPLIT, dtype=I32)[None, :, None] * n_rows
    dest_sub = dest[:, None, :] + plane
    xs = _sc_scatter_rows(h2p.reshape(SC_SPLIT * n_tok, wp), dest_sub.reshape(TOP_K, SC_SPLIT * n_tok),
                          SC_SPLIT * n_rows).reshape(SC_SPLIT, n_rows, wp)
    ys = _moe_call(xs, block_expert, n_used, seg_slot, next_expert, wgu, bgu, wd, bd)
    yg = _sc_gather_rows(ys.reshape(SC_SPLIT * n_rows, wp), dest_sub.reshape(1, TOP_K * SC_SPLIT * n_tok))
    return _combine_call(x1, gt, mod, out_gain, yg.reshape(TOP_K, SC_SPLIT, n_tok, wp), out_prev, b0)


def kernel(x, c, w_ada, b_ada, norm1_g, w_in, conv_w, conv_b, conv_ln_g, conv_ln_b, lru_conv_w, lru_conv_b,
           lru_w_a, lru_b_a, lru_w_x, lru_b_x, lru_lambda, mix_norm_g, w_out, norm2_g, w_router, b_router,
           w_gate_up, b_gate_up, w_down, b_down, final_norm_g):
    depth = w_ada.shape[0]
    assert depth == 1, "the final norm is fused into the (single) layer's combine kernel"
    bsz, seq, dm = x.shape
    mod = _ada_call(c, w_ada[0], b_ada[0]).reshape(bsz, N_MOD, dm)
    return _layer(x, mod, 0, w_in, norm1_g, conv_w, conv_b, conv_ln_g, conv_ln_b, lru_conv_w, lru_conv_b,
                  lru_w_a, lru_b_a, lru_w_x, lru_b_x, lru_lambda, mix_norm_g, w_out, norm2_g, w_router,
                  b_router, w_gate_up, b_gate_up, w_down, b_down, row_gain(final_norm_g))


def row_gain(g):
    return g.reshape(1, -1)
```

```python
import functools

import jax
import jax.numpy as jnp
from jax import lax
from jax.experimental import pallas as pl
from jax.experimental.pallas import tpu as pltpu
from jax.experimental.pallas import tpu_sc as plsc

F32 = jnp.float32
BF16 = jnp.bfloat16
I32 = jnp.int32

EPS = 1e-6
N_MOD = 6
LRU_C = 8.0
TOP_K = 4
SWIGLU_ALPHA = 1.702
SWIGLU_LIMIT = 7.0

LANES = 128
SUBLANES = 8
VMEM_LIMIT_BYTES = 56 * 1024 * 1024

SEQ_TILE = 512
MIX_SUB = 512
COMBINE_TILE = 512
CONV_PAD = 32
CONV_ROWS = 64
MOE_BLOCK = 512
CHUNK_BATCHES = (4, 4)
HI_MASK = -65536

SC_CORES = 2
SC_SUBCORES = 16
SC_WINDOW = 128
SC_SPLIT = 2


def _sigmoid(x):
    return jax.nn.sigmoid(x)


def _pack_bf16_pair(lo_f32, hi_f32):
    lo_bits = lax.bitcast_convert_type(lo_f32.astype(BF16).astype(F32), I32)
    hi_bits = lax.bitcast_convert_type(hi_f32.astype(BF16).astype(F32), I32)
    return lax.shift_right_logical(lo_bits, 16) | hi_bits


def _unpack_bf16_pair(p):
    lo = lax.bitcast_convert_type(lax.shift_left(p, 16), F32)
    hi = lax.bitcast_convert_type(p & HI_MASK, F32)
    return lo, hi


def _store_planes(ref, packed, rows=slice(None)):
    wp = packed.shape[1] // SC_SPLIT
    for s in range(SC_SPLIT):
        ref[s, rows, :] = packed[:, s * wp:(s + 1) * wp]


def _load_planes(ref):
    return jnp.concatenate([ref[s] for s in range(SC_SPLIT)], axis=1)


def _ada_kernel(c_ref, w_ref, b_ref, o_ref):
    c = c_ref[...]
    ca = c * _sigmoid(c)
    w = w_ref[...]
    c_hi = ca.astype(BF16)
    c_lo = (ca - c_hi.astype(F32)).astype(BF16)
    w_hi = w.astype(BF16)
    w_lo = (w - w_hi.astype(F32)).astype(BF16)
    acc = jnp.dot(c_hi, w_hi, preferred_element_type=F32)
    acc += jnp.dot(c_lo, w_hi, preferred_element_type=F32)
    acc += jnp.dot(c_hi, w_lo, preferred_element_type=F32)
    o_ref[...] = acc + b_ref[...]


def _ada_call(c, w_ada, b_ada):
    bsz, dm = c.shape
    n_out = w_ada.shape[1]
    tn = 1024
    return pl.pallas_call(
        _ada_kernel,
        grid=(n_out // tn,),
        in_specs=[
            pl.BlockSpec((bsz, dm), lambda n: (0, 0)),
            pl.BlockSpec((dm, tn), lambda n: (0, n)),
            pl.BlockSpec((1, tn), lambda n: (0, n)),
        ],
        out_specs=pl.BlockSpec((bsz, tn), lambda n: (0, n)),
        out_shape=jax.ShapeDtypeStruct((bsz, n_out), F32),
        compiler_params=pltpu.CompilerParams(dimension_semantics=("arbitrary",)),
        name="ada_mod",
    )(c, w_ada, b_ada.reshape(1, n_out))


def _rms(x, eps=EPS):
    return x * lax.rsqrt(jnp.mean(x * x, axis=-1, keepdims=True) + eps)


def _gelu_tanh(x):
    return 0.5 * x * (1.0 + jnp.tanh(0.7978845608028654 * (x + 0.044715 * (x * x * x))))


def _slab_store(buf, row0, val):
    for s in range(buf.shape[0]):
        buf[s, pl.ds(row0, val.shape[0]), :] = val[:, s * LANES:(s + 1) * LANES]


def _slab_keep_tail(buf, keep, ts):
    for s in range(buf.shape[0]):
        buf[s, pl.ds(0, keep), :] = buf[s, pl.ds(ts, keep), :]


def _causal_tap_sum(buf, w_ref, bias_row, first, n_taps, ts):
    w = w_ref[...]
    cols = []
    for s in range(buf.shape[0]):
        lanes = slice(s * LANES, (s + 1) * LANES)
        chunks = []
        for c in range(0, ts, CONV_ROWS):
            acc = jnp.broadcast_to(bias_row[:, lanes], (CONV_ROWS, LANES))
            for k in range(n_taps):
                acc = acc + w[k:k + 1, lanes] * buf[s, pl.ds(c + first + k, CONV_ROWS), :]
            chunks.append(acc)
        cols.append(jnp.concatenate(chunks, axis=0))
    return jnp.concatenate(cols, axis=1)


def _mix_kernel(x_ref, mod_ref, g1_ref, win_ref, cw_ref, cb_ref, lng_ref, lnb_ref,
                lcw_ref, lcb_ref, wg_ref, bg_ref, lam_ref, mng_ref, wout_ref, g2_ref,
                wrt_ref, br_ref, anchor_ref,
                x1_ref, h2p_ref, idx_ref, rank_ref, gt_ref, cnt_ref,
                vbuf, rbuf, a_s, b_s, hcar, cnt_s,
                *, ts, sub, d_conv, d_lru, conv_w, lru_cw, n_exp):
    del anchor_ref
    b = pl.program_id(0)
    j = pl.program_id(1)

    @pl.when(j == 0)
    def _():
        _slab_store(vbuf, 0, jnp.zeros((CONV_PAD, d_conv), F32))
        _slab_store(rbuf, 0, jnp.zeros((SUBLANES, d_lru), F32))
        hcar[...] = jnp.zeros_like(hcar)

    @pl.when((b == 0) & (j == 0))
    def _():
        cnt_s[...] = jnp.zeros_like(cnt_s)

    mod = mod_ref[...]
    shift1, gate1, shift2 = mod[0:1], mod[2:3], mod[3:4]
    gain1 = g1_ref[...] * (1.0 + mod[1:2])
    gain2 = g2_ref[...] * (1.0 + mod[4:5])
    lam = lam_ref[...]
    softplus_neg_lam = jnp.maximum(-lam, 0.0) + jnp.log1p(jnp.exp(-jnp.abs(lam)))

    for q in range(ts // sub):
        _mix_subtile(q * sub, sub, x_ref, shift1, gain1, gate1, shift2, gain2, softplus_neg_lam, win_ref,
                     cw_ref, cb_ref, lng_ref, lnb_ref, lcw_ref, lcb_ref, wg_ref, bg_ref, mng_ref, wout_ref,
                     wrt_ref, br_ref, x1_ref, h2p_ref, idx_ref, rank_ref, gt_ref, vbuf, rbuf, a_s, b_s, hcar,
                     cnt_s, d_conv=d_conv, d_lru=d_lru, conv_w=conv_w, lru_cw=lru_cw, n_exp=n_exp)
    _slab_keep_tail(vbuf, CONV_PAD, ts)
    _slab_keep_tail(rbuf, SUBLANES, ts)
    cnt_ref[...] = cnt_s[:, :LANES]


def _mix_subtile(r0, ts, x_ref, shift1, gain1, gate1, shift2, gain2, softplus_neg_lam, win_ref,
                 cw_ref, cb_ref, lng_ref, lnb_ref, lcw_ref, lcb_ref, wg_ref, bg_ref, mng_ref, wout_ref,
                 wrt_ref, br_ref, x1_ref, h2p_ref, idx_ref, rank_ref, gt_ref, vbuf, rbuf, a_s, b_s, hcar,
                 cnt_s, *, d_conv, d_lru, conv_w, lru_cw, n_exp):
    rows = pl.ds(r0, ts)
    x = x_ref[rows, :]
    h = _rms(x) * gain1 + shift1
    u = jnp.dot(h.astype(BF16), win_ref[...], preferred_element_type=F32)

    v = u[:, :d_conv] * _sigmoid(u[:, d_conv:2 * d_conv])
    _slab_store(vbuf, CONV_PAD + r0, v)
    acc = _causal_tap_sum(vbuf, cw_ref, cb_ref[...], r0 + CONV_PAD - (conv_w - 1), conv_w, ts)
    mu = jnp.mean(acc, axis=-1, keepdims=True)
    cen = acc - mu
    var = jnp.mean(cen * cen, axis=-1, keepdims=True)
    yc = cen * lax.rsqrt(var + EPS) * lng_ref[...] + lnb_ref[...]
    yc = yc * _sigmoid(yc)

    u_gate = u[:, 2 * d_conv:2 * d_conv + d_lru]
    _slab_store(rbuf, SUBLANES + r0, u[:, 2 * d_conv + d_lru:])
    xr = _causal_tap_sum(rbuf, lcw_ref, lcb_ref[...], r0 + SUBLANES - (lru_cw - 1), lru_cw, ts)
    gates = jnp.dot(xr.astype(BF16), wg_ref[...], preferred_element_type=F32) + bg_ref[...]
    r = _sigmoid(gates[:, :d_lru])
    i_g = _sigmoid(gates[:, d_lru:])
    log_a = (-LRU_C) * r * softplus_neg_lam
    a = jnp.exp(log_a)
    inp = jnp.sqrt(1.0 - jnp.exp(2.0 * log_a)) * (i_g * xr)

    n_grp = ts // SUBLANES
    a3 = a.reshape(n_grp, SUBLANES, d_lru)
    b3 = inp.reshape(n_grp, SUBLANES, d_lru)
    sub = lax.broadcasted_iota(I32, (n_grp, SUBLANES, d_lru), 1)
    for s in (1, 2, 4):
        a_sh = pltpu.roll(a3, s, axis=1)
        b_sh = pltpu.roll(b3, s, axis=1)
        m = sub >= s
        b3 = jnp.where(m, a3 * b_sh + b3, b3)
        a3 = jnp.where(m, a3 * a_sh, a3)
    a_s[...] = a3.reshape(ts, d_lru)
    b_s[...] = b3.reshape(ts, d_lru)

    def grp_body(g, carry):
        rows = pl.ds(pl.multiple_of(g * SUBLANES, SUBLANES), SUBLANES)
        hg = b_s[rows, :] + a_s[rows, :] * carry
        b_s[rows, :] = hg
        return jnp.broadcast_to(hg[SUBLANES - 1:SUBLANES, :], (SUBLANES, d_lru))

    hcar[...] = lax.fori_loop(0, n_grp, grp_body, hcar[...], unroll=True)
    yl = _gelu_tanh(u_gate) * b_s[...]

    mng = mng_ref[...]
    yc_n = _rms(yc) * mng[:, :d_conv]
    yl_n = _rms(yl) * mng[:, d_conv:]
    mixo = jnp.dot(yc_n.astype(BF16), wout_ref[pl.ds(0, d_conv), :], preferred_element_type=F32)
    mixo += jnp.dot(yl_n.astype(BF16), wout_ref[pl.ds(d_conv, d_lru), :], preferred_element_type=F32)
    x1 = x + gate1 * mixo
    x1_ref[rows, :] = x1

    h2 = _rms(x1) * gain2 + shift2
    dh = h2.shape[1] // 2
    h2_hi = h2.astype(BF16)
    _store_planes(h2p_ref, _pack_bf16_pair(h2[:, :dh], h2[:, dh:]), rows)
    h2_lo = (h2 - h2_hi.astype(F32)).astype(BF16)
    nt_dims = (((1,), (1,)), ((), ()))
    wrt = wrt_ref[...]
    lg = lax.dot_general(wrt, h2_hi, nt_dims, preferred_element_type=F32)
    lg2 = lax.dot_general(wrt[:n_exp], h2_lo, nt_dims, preferred_element_type=F32)
    logits = lg[:n_exp] + lg[n_exp:] + lg2 + br_ref[...]

    eidx = lax.broadcasted_iota(I32, (n_exp, ts), 0)
    neg_inf = jnp.float32(-jnp.inf)
    work = logits
    vals, idxs, hots = [], [], []
    for _ in range(TOP_K):
        mval = jnp.max(work, axis=0, keepdims=True)
        midx = jnp.min(jnp.where(work == mval, eidx, n_exp), axis=0, keepdims=True)
        hot = eidx == midx
        vals.append(mval)
        idxs.append(midx)
        hots.append(hot)
        work = jnp.where(hot, neg_inf, work)
    exps = [jnp.exp(vk - vals[0]) for vk in vals]
    denom = exps[0] + exps[1] + exps[2] + exps[3]
    gate_rows = [ek / denom for ek in exps]

    sel = jnp.zeros((n_exp, ts), F32)
    for hot in hots:
        sel = sel + hot.astype(F32)
    tri = (lax.broadcasted_iota(I32, (ts, ts), 0) < lax.broadcasted_iota(I32, (ts, ts), 1)).astype(BF16)
    before = jnp.dot(sel.astype(BF16), tri, preferred_element_type=F32) + cnt_s[...]
    rank_rows = [jnp.sum(jnp.where(hot, before, 0.0), axis=0, keepdims=True) for hot in hots]
    cnt_s[...] = cnt_s[...] + jnp.sum(sel, axis=1, keepdims=True)

    idx_ref[:, rows] = jnp.concatenate(idxs, axis=0)
    rank_ref[:, rows] = jnp.concatenate(rank_rows, axis=0).astype(I32)
    g4 = jnp.concatenate(gate_rows, axis=0)
    g_pad = jnp.concatenate([g4, jnp.zeros((LANES - TOP_K, ts), F32)], axis=0)
    gt_ref[rows, :] = g_pad.T


def _mix_call(x, mod, p, b0, bsz, anchor):
    _, seq, dm = x.shape
    ts = min(SEQ_TILE, seq)
    nj = seq // ts
    d_conv = p["conv_w"].shape[1]
    d_lru = p["lru_conv_w"].shape[1]
    conv_w = p["conv_w"].shape[0]
    lru_cw = p["lru_conv_w"].shape[0]
    n_exp = p["wrt"].shape[0] // 2
    sub = min(MIX_SUB, ts)
    assert seq % ts == 0 and ts % sub == 0 and sub % LANES == 0 and sub % CONV_ROWS == 0
    assert conv_w - 1 <= CONV_PAD and lru_cw - 1 <= SUBLANES and d_conv % LANES == 0 and d_lru % LANES == 0

    def full(a):
        return pl.BlockSpec(a.shape, lambda b, j: (0,) * a.ndim)

    weights = [p["g1"], p["win"], p["conv_w"], p["conv_b"], p["ln_g"], p["ln_b"], p["lru_conv_w"],
               p["lru_conv_b"], p["wg"], p["bg"], p["lam"], p["mng"], p["wout"], p["g2"], p["wrt"], p["br"]]
    kern = functools.partial(_mix_kernel, ts=ts, sub=sub, d_conv=d_conv, d_lru=d_lru, conv_w=conv_w,
                             lru_cw=lru_cw, n_exp=n_exp)
    out_shape = (
        jax.ShapeDtypeStruct((bsz, seq, dm), F32),
        jax.ShapeDtypeStruct((SC_SPLIT, bsz * seq, dm // 2 // SC_SPLIT), I32),
        jax.ShapeDtypeStruct((TOP_K, bsz * seq), I32),
        jax.ShapeDtypeStruct((TOP_K, bsz * seq), I32),
        jax.ShapeDtypeStruct((bsz, seq, LANES), F32),
        jax.ShapeDtypeStruct((n_exp, LANES), F32),
    )
    return pl.pallas_call(
        kern,
        grid=(bsz, nj),
        in_specs=[pl.BlockSpec((None, ts, dm), lambda b, j: (b + b0, j, 0)),
                  pl.BlockSpec((None, N_MOD, dm), lambda b, j: (b + b0, 0, 0))] + [full(w) for w in weights]
        + [pl.BlockSpec(memory_space=pl.ANY)],
        out_specs=(
            pl.BlockSpec((None, ts, dm), lambda b, j: (b, j, 0)),
            pl.BlockSpec((SC_SPLIT, ts, dm // 2 // SC_SPLIT), lambda b, j: (0, b * nj + j, 0)),
            pl.BlockSpec((TOP_K, ts), lambda b, j: (0, b * nj + j)),
            pl.BlockSpec((TOP_K, ts), lambda b, j: (0, b * nj + j)),
            pl.BlockSpec((None, ts, LANES), lambda b, j: (b, j, 0)),
            pl.BlockSpec((n_exp, LANES), lambda b, j: (0, 0)),
        ),
        out_shape=out_shape,
        scratch_shapes=[
            pltpu.VMEM((d_conv // LANES, CONV_PAD + ts, LANES), F32),
            pltpu.VMEM((d_lru // LANES, SUBLANES + ts, LANES), F32),
            pltpu.VMEM((sub, d_lru), F32),
            pltpu.VMEM((sub, d_lru), F32),
            pltpu.VMEM((SUBLANES, d_lru), F32),
            pltpu.VMEM((n_exp, sub), F32),
        ],
        compiler_params=pltpu.CompilerParams(dimension_semantics=("arbitrary", "arbitrary"),
                                             vmem_limit_bytes=VMEM_LIMIT_BYTES),
        name="token_mix_route",
    )(x, mod, *weights, anchor)


def _sc_mesh():
    return plsc.VectorSubcoreMesh(core_axis_name="core", subcore_axis_name="subcore",
                                  num_cores=SC_CORES, num_subcores=SC_SUBCORES)


def _sc_scatter_rows(rows, idx, n_out):
    n, w = rows.shape
    n_k = idx.shape[0]
    assert n % (SC_WINDOW * SC_CORES * SC_SUBCORES) == 0

    @functools.partial(pl.kernel, out_type=jax.ShapeDtypeStruct((n_out, w), rows.dtype), mesh=_sc_mesh(),
                       scratch_types=[], name="moe_dispatch_sc")
    def scatter(x_hbm, i_hbm, o_hbm):
        def body(x_vmem, *i_vmems):
            for i_vmem in i_vmems:
                pltpu.sync_copy(x_vmem, o_hbm.at[i_vmem.at[0]])

        pltpu.emit_pipeline(
            body,
            grid=(n // SC_WINDOW,),
            in_specs=[pl.BlockSpec((SC_WINDOW, w), lambda i: (i, 0))]
            + [pl.BlockSpec((1, SC_WINDOW), functools.partial(lambda k, i: (k, i), k)) for k in range(n_k)],
            out_specs=[],
            core_axis_name=("core", "subcore"),
            dimension_semantics=(pltpu.PARALLEL,),
        )(x_hbm, *([i_hbm] * n_k))

    return scatter(rows, idx)


def _sc_gather_rows(table, idx):
    m = idx.shape[1]
    w = table.shape[1]
    assert m % (SC_WINDOW * SC_CORES * SC_SUBCORES) == 0

    @functools.partial(pl.kernel, out_type=jax.ShapeDtypeStruct((m, w), table.dtype), mesh=_sc_mesh(),
                       scratch_types=[], name="moe_gather_sc")
    def gather(x_hbm, i_hbm, o_hbm):
        def body(i_vmem, o_vmem):
            pltpu.sync_copy(x_hbm.at[i_vmem.at[0]], o_vmem)

        pltpu.emit_pipeline(
            body,
            grid=(m // SC_WINDOW,),
            in_specs=[pl.BlockSpec((1, SC_WINDOW), lambda i: (0, i))],
            out_specs=[pl.BlockSpec((SC_WINDOW, w), lambda i: (i, 0))],
            core_axis_name=("core", "subcore"),
            dimension_semantics=(pltpu.PARALLEL,),
        )(i_hbm, o_hbm)

    return gather(table, idx)


def _moe_kernel(be_ref, nb_ref, slot_ref, nxt_ref, xs_ref, wgu_hbm, bgu_ref, wd_hbm, bd_ref, ys_ref,
                wgu_f, wd_f, wgu_b, wd_b, sems, *, d_ff):
    i = pl.program_id(0)
    live = i < nb_ref[0]
    new_expert = (i == 0) | (be_ref[i] != be_ref[jnp.maximum(i - 1, 0)])

    def weight_copies(e, s):
        return (pltpu.make_async_copy(wgu_hbm.at[e], wgu_f.at[s], sems.at[0, s]),
                pltpu.make_async_copy(wd_hbm.at[e], wd_f.at[s], sems.at[1, s]))

    @pl.when(live & new_expert)
    def _():
        e, s, nx = be_ref[i], slot_ref[i], nxt_ref[i]

        @pl.when(i == 0)
        def _():
            for cp in weight_copies(e, s):
                cp.start()

        for cp in weight_copies(e, s):
            cp.wait()

        @pl.when(nx >= 0)
        def _():
            for cp in weight_copies(nx, 1 - s):
                cp.start()

        wgu_b[...] = wgu_f[s].astype(BF16)
        wd_b[...] = wd_f[s].astype(BF16)

    @pl.when(live)
    def _():
        x_lo, x_hi = _unpack_bf16_pair(_load_planes(xs_ref))
        dh = x_lo.shape[1]
        gu = jnp.dot(x_lo.astype(BF16), wgu_b[pl.ds(0, dh), :], preferred_element_type=F32)
        gu += jnp.dot(x_hi.astype(BF16), wgu_b[pl.ds(dh, dh), :], preferred_element_type=F32)
        gu += bgu_ref[...]
        g = jnp.minimum(gu[:, :d_ff], SWIGLU_LIMIT)
        u = jnp.clip(gu[:, d_ff:], -SWIGLU_LIMIT, SWIGLU_LIMIT)
        act = (u + 1.0) * (g * _sigmoid(SWIGLU_ALPHA * g))
        y = jnp.dot(act.astype(BF16), wd_b[...], preferred_element_type=F32) + bd_ref[...]
        dm2 = y.shape[1] // 2
        _store_planes(ys_ref, _pack_bf16_pair(y[:, :dm2], y[:, dm2:]))


def _moe_call(xs, block_expert, n_used, seg_slot, next_expert, wgu, bgu, wd, bd):
    _, n_rows, wp = xs.shape
    n_exp, dm, d_ff2 = wgu.shape
    d_ff = d_ff2 // 2
    bm = MOE_BLOCK
    n_blocks = n_rows // bm

    def row_map(i, be, nb, slot, nxt):
        return (0, jnp.minimum(i, nb[0] - 1), 0)

    def b_map(i, be, nb, slot, nxt):
        return (be[jnp.minimum(i, nb[0] - 1)], 0, 0)

    grid_spec = pltpu.PrefetchScalarGridSpec(
        num_scalar_prefetch=4,
        grid=(n_blocks,),
        in_specs=[
            pl.BlockSpec((SC_SPLIT, bm, wp), row_map),
            pl.BlockSpec(memory_space=pl.ANY),
            pl.BlockSpec((None, 1, d_ff2), b_map),
            pl.BlockSpec(memory_space=pl.ANY),
            pl.BlockSpec((None, 1, dm), b_map),
        ],
        out_specs=pl.BlockSpec((SC_SPLIT, bm, wp), row_map),
        scratch_shapes=[
            pltpu.VMEM((2, dm, d_ff2), F32), pltpu.VMEM((2, d_ff, dm), F32),
            pltpu.VMEM((dm, d_ff2), BF16), pltpu.VMEM((d_ff, dm), BF16),
            pltpu.SemaphoreType.DMA((2, 2)),
        ],
    )
    return pl.pallas_call(
        functools.partial(_moe_kernel, d_ff=d_ff),
        grid_spec=grid_spec,
        out_shape=jax.ShapeDtypeStruct(xs.shape, I32),
        compiler_params=pltpu.CompilerParams(dimension_semantics=("arbitrary",),
                                             vmem_limit_bytes=VMEM_LIMIT_BYTES),
        name="moe_experts",
    )(block_expert, n_used, seg_slot, next_expert, xs, wgu, bgu, wd, bd)


def _combine_kernel(x1_ref, gt_ref, mod_ref, gf_ref, yg_ref, *out_refs):
    o_ref = out_refs[-1]
    gt = gt_ref[...]
    ts = yg_ref.shape[2]
    dh = yg_ref.shape[3] * SC_SPLIT
    acc_lo = jnp.zeros((ts, dh), F32)
    acc_hi = jnp.zeros((ts, dh), F32)
    for k in range(TOP_K):
        lo, hi = _unpack_bf16_pair(_load_planes(yg_ref.at[k]))
        gk = gt[:, k:k + 1]
        acc_lo += gk * lo
        acc_hi += gk * hi
    y = jnp.concatenate([acc_lo, acc_hi], axis=1)
    gate2 = mod_ref[...][5:6]
    x2 = x1_ref[...] + gate2 * y
    o_ref[...] = _rms(x2) * gf_ref[...]


def _combine_call(x1, gt, mod, gf, yg, out_prev, b0):
    bsz, seq, dm = x1.shape
    ts = min(COMBINE_TILE, seq)
    assert seq % ts == 0
    nj = seq // ts
    in_specs = [
        pl.BlockSpec((None, ts, dm), lambda b, j: (b, j, 0)),
        pl.BlockSpec((None, ts, LANES), lambda b, j: (b, j, 0)),
        pl.BlockSpec((None, N_MOD, dm), lambda b, j: (b + b0, 0, 0)),
        pl.BlockSpec((1, dm), lambda b, j: (0, 0)),
        pl.BlockSpec((TOP_K, SC_SPLIT, ts, yg.shape[3]), lambda b, j: (0, 0, b * nj + j, 0)),
    ]
    args = [x1, gt, mod, gf, yg]
    aliases = {}
    if out_prev is not None:
        in_specs.append(pl.BlockSpec(memory_space=pl.ANY))
        args.append(out_prev)
        aliases = {len(args) - 1: 0}
    return pl.pallas_call(
        _combine_kernel,
        grid=(bsz, nj),
        in_specs=in_specs,
        out_specs=pl.BlockSpec((None, ts, dm), lambda b, j: (b + b0, j, 0)),
        out_shape=jax.ShapeDtypeStruct((mod.shape[0], seq, dm), F32),
        input_output_aliases=aliases,
        compiler_params=pltpu.CompilerParams(dimension_semantics=("arbitrary", "arbitrary"),
                                             vmem_limit_bytes=VMEM_LIMIT_BYTES),
        name="moe_combine",
    )(*args)


def _block_diag(w):
    n_h, d, _ = w.shape
    eye = jnp.eye(n_h, dtype=w.dtype)
    return (eye[:, None, :, None] * w[:, :, None, :]).reshape(n_h * d, n_h * d)


def _layer(x, mod, l, w_in, norm1_g, conv_w, conv_b, conv_ln_g, conv_ln_b, lru_conv_w, lru_conv_b,
           lru_w_a, lru_b_a, lru_w_x, lru_b_x, lru_lambda, mix_norm_g, w_out, norm2_g, w_router,
           b_router, w_gate_up, b_gate_up, w_down, b_down, out_gain):
    bsz, seq, dm = x.shape
    n_tok = bsz * seq
    n_exp = w_router.shape[-1]
    row = lambda a: a.reshape(1, -1)
    wr = w_router[l]
    wr_hi = wr.astype(BF16)
    wr_lo = (wr - wr_hi.astype(F32)).astype(BF16)
    params = dict(
        g1=row(norm1_g[l]), win=w_in[l].astype(BF16), conv_w=conv_w[l], conv_b=row(conv_b[l]),
        ln_g=row(conv_ln_g[l]), ln_b=row(conv_ln_b[l]), lru_conv_w=lru_conv_w[l], lru_conv_b=row(lru_conv_b[l]),
        wg=jnp.concatenate([_block_diag(lru_w_a[l]), _block_diag(lru_w_x[l])], axis=1).astype(BF16),
        bg=jnp.concatenate([lru_b_a[l].reshape(1, -1), lru_b_x[l].reshape(1, -1)], axis=1),
        lam=row(lru_lambda[l]), mng=row(mix_norm_g[l]), wout=w_out[l].astype(BF16), g2=row(norm2_g[l]),
        wrt=jnp.concatenate([wr_hi.T, wr_lo.T], axis=0), br=b_router[l].reshape(n_exp, 1),
    )
    sizes = [s for s in CHUNK_BATCHES if s > 0] if sum(CHUNK_BATCHES) == bsz else [bsz]
    routed, b0, anchor = [], 0, mod
    for cb in sizes:
        r = _route_chunk(x, mod, params, b0, cb, anchor, n_exp)
        routed.append(r)
        anchor = r["dest_sub"]
        b0 += cb
    out = None
    for r in routed:
        out = _experts_chunk(r, mod, out, w_gate_up[l], b_gate_up[l][:, None, :], w_down[l],
                             b_down[l][:, None, :], out_gain)
    return out


def _route_chunk(x, mod, params, b0, cb, anchor, n_exp):
    seq, dm = x.shape[1:]
    n_tok = cb * seq
    x1, h2p, top_idx, rank, gt, cnt = _mix_call(x, mod, params, b0, cb, anchor)

    bm = MOE_BLOCK
    counts = cnt[:, 0].astype(I32)
    padded = (counts + bm - 1) // bm * bm
    e_ids = jnp.arange(n_exp, dtype=I32)
    pad_ends = jnp.sum(jnp.where(e_ids[None, :] <= e_ids[:, None], padded[None, :], 0), axis=1)
    pad_starts = pad_ends - padded
    n_rows = (n_tok * TOP_K // bm + n_exp) * bm
    n_blocks = n_rows // bm
    dest = rank + jnp.sum(jnp.where(top_idx[..., None] == e_ids, pad_starts, 0), axis=-1)
    block_start = jnp.arange(n_blocks, dtype=I32) * bm
    block_expert = jnp.minimum(
        jnp.sum((block_start[:, None] >= pad_ends[None, :]).astype(I32), axis=1), n_exp - 1)
    n_used = pad_ends[-1:] // bm
    present = counts > 0
    seg_ordinal = jnp.sum(jnp.where((e_ids[None, :] < e_ids[:, None]) & present[None, :], 1, 0), axis=1)
    following = jnp.min(jnp.where((e_ids[None, :] > e_ids[:, None]) & present[None, :], e_ids[None, :], n_exp),
                        axis=1)
    following = jnp.where(following == n_exp, -1, following)
    block_hot = block_expert[:, None] == e_ids[None, :]
    seg_slot = jnp.sum(jnp.where(block_hot, seg_ordinal[None, :] % 2, 0), axis=1)
    next_expert = jnp.sum(jnp.where(block_hot, following[None, :], 0), axis=1)

    wp = h2p.shape[2]
    plane = jnp.arange(SC_SPLIT, dtype=I32)[None, :, None] * n_rows
    dest_sub = dest[:, None, :] + plane
    return dict(b0=b0, n_tok=n_tok, n_rows=n_rows, x1=x1, h2p=h2p, gt=gt, dest_sub=dest_sub,
                tables=(block_expert, n_used, seg_slot, next_expert))


def _experts_chunk(r, mod, out_prev, wgu, bgu, wd, bd, out_gain):
    n_tok, n_rows, dest_sub = r["n_tok"], r["n_rows"], r["dest_sub"]
    wp = r["h2p"].shape[2]
    xs = _sc_scatter_rows(r["h2p"].reshape(SC_SPLIT * n_tok, wp), dest_sub.reshape(TOP_K, SC_SPLIT * n_tok),
                          SC_SPLIT * n_rows).reshape(SC_SPLIT, n_rows, wp)
    ys = _moe_call(xs, *r["tables"], wgu, bgu, wd, bd)
    yg = _sc_gather_rows(ys.reshape(SC_SPLIT * n_rows, wp), dest_sub.reshape(1, TOP_K * SC_SPLIT * n_tok))
    return _combine_call(r["x1"], r["gt"], mod, out_gain, yg.reshape(TOP_K, SC_SPLIT, n_tok, wp), out_prev,
                         r["b0"])


def kernel(x, c, w_ada, b_ada, norm1_g, w_in, conv_w, conv_b, conv_ln_g, conv_ln_b, lru_conv_w, lru_conv_b,
           lru_w_a, lru_b_a, lru_w_x, lru_b_x, lru_lambda, mix_norm_g, w_out, norm2_g, w_router, b_router,
           w_gate_up, b_gate_up, w_down, b_down, final_norm_g):
    depth = w_ada.shape[0]
    assert depth == 1, "the final norm is fused into the (single) layer's combine kernel"
    bsz, seq, dm = x.shape
    mod = _ada_call(c, w_ada[0], b_ada[0]).reshape(bsz, N_MOD, dm)
    return _layer(x, mod, 0, w_in, norm1_g, conv_w, conv_b, conv_ln_g, conv_ln_b, lru_conv_w, lru_conv_b,
                  lru_w_a, lru_b_a, lru_w_x, lru_b_x, lru_lambda, mix_norm_g, w_out, norm2_g, w_router,
                  b_router, w_gate_up, b_gate_up, w_down, b_down, row_gain(final_norm_g))


def row_gain(g):
    return g.reshape(1, -1)
```

```python
import functools

import jax
import jax.numpy as jnp
from jax import lax
from jax.experimental import pallas as pl
from jax.experimental.pallas import tpu as pltpu
from jax.experimental.pallas import tpu_sc as plsc

F32 = jnp.float32
BF16 = jnp.bfloat16
I32 = jnp.int32

EPS = 1e-6
N_MOD = 6
LRU_C = 8.0
TOP_K = 4
SWIGLU_ALPHA = 1.702
SWIGLU_LIMIT = 7.0

LANES = 128
SUBLANES = 8
VMEM_LIMIT_BYTES = 56 * 1024 * 1024

SEQ_TILE = 512
MIX_SUB = 512
COMBINE_TILE = 512
CONV_PAD = 32
CONV_ROWS = 64
MOE_BLOCK = 1024
MOE_SUB = 256
CHUNK_BATCHES = (4, 4)
HI_MASK = -65536

SC_CORES = 2
SC_SUBCORES = 16
SC_WINDOW = 128
SC_SPLIT = 2


def _sigmoid(x):
    return jax.nn.sigmoid(x)


def _pack_bf16_pair(lo_f32, hi_f32):
    lo_bits = lax.bitcast_convert_type(lo_f32.astype(BF16).astype(F32), I32)
    hi_bits = lax.bitcast_convert_type(hi_f32.astype(BF16).astype(F32), I32)
    return lax.shift_right_logical(lo_bits, 16) | hi_bits


def _unpack_bf16_pair(p):
    lo = lax.bitcast_convert_type(lax.shift_left(p, 16), F32)
    hi = lax.bitcast_convert_type(p & HI_MASK, F32)
    return lo, hi


def _store_planes(ref, packed, rows=slice(None)):
    wp = packed.shape[1] // SC_SPLIT
    for s in range(SC_SPLIT):
        ref[s, rows, :] = packed[:, s * wp:(s + 1) * wp]


def _load_planes(ref):
    return jnp.concatenate([ref[s] for s in range(SC_SPLIT)], axis=1)


def _ada_kernel(c_ref, w_ref, b_ref, o_ref):
    c = c_ref[...]
    ca = c * _sigmoid(c)
    w = w_ref[...]
    c_hi = ca.astype(BF16)
    c_lo = (ca - c_hi.astype(F32)).astype(BF16)
    w_hi = w.astype(BF16)
    w_lo = (w - w_hi.astype(F32)).astype(BF16)
    acc = jnp.dot(c_hi, w_hi, preferred_element_type=F32)
    acc += jnp.dot(c_lo, w_hi, preferred_element_type=F32)
    acc += jnp.dot(c_hi, w_lo, preferred_element_type=F32)
    o_ref[...] = acc + b_ref[...]


def _ada_call(c, w_ada, b_ada):
    bsz, dm = c.shape
    n_out = w_ada.shape[1]
    tn = 1024
    return pl.pallas_call(
        _ada_kernel,
        grid=(n_out // tn,),
        in_specs=[
            pl.BlockSpec((bsz, dm), lambda n: (0, 0)),
            pl.BlockSpec((dm, tn), lambda n: (0, n)),
            pl.BlockSpec((1, tn), lambda n: (0, n)),
        ],
        out_specs=pl.BlockSpec((bsz, tn), lambda n: (0, n)),
        out_shape=jax.ShapeDtypeStruct((bsz, n_out), F32),
        compiler_params=pltpu.CompilerParams(dimension_semantics=("arbitrary",)),
        name="ada_mod",
    )(c, w_ada, b_ada.reshape(1, n_out))


def _rms(x, eps=EPS):
    return x * lax.rsqrt(jnp.mean(x * x, axis=-1, keepdims=True) + eps)


def _gelu_tanh(x):
    return 0.5 * x * (1.0 + jnp.tanh(0.7978845608028654 * (x + 0.044715 * (x * x * x))))


def _slab_store(buf, row0, val):
    for s in range(buf.shape[0]):
        buf[s, pl.ds(row0, val.shape[0]), :] = val[:, s * LANES:(s + 1) * LANES]


def _slab_keep_tail(buf, keep, ts):
    for s in range(buf.shape[0]):
        buf[s, pl.ds(0, keep), :] = buf[s, pl.ds(ts, keep), :]


def _causal_tap_sum(buf, w_ref, bias_row, first, n_taps, ts):
    w = w_ref[...]
    cols = []
    for s in range(buf.shape[0]):
        lanes = slice(s * LANES, (s + 1) * LANES)
        chunks = []
        for c in range(0, ts, CONV_ROWS):
            acc = jnp.broadcast_to(bias_row[:, lanes], (CONV_ROWS, LANES))
            for k in range(n_taps):
                acc = acc + w[k:k + 1, lanes] * buf[s, pl.ds(c + first + k, CONV_ROWS), :]
            chunks.append(acc)
        cols.append(jnp.concatenate(chunks, axis=0))
    return jnp.concatenate(cols, axis=1)


def _mix_kernel(x_ref, mod_ref, g1_ref, win_ref, cw_ref, cb_ref, lng_ref, lnb_ref,
                lcw_ref, lcb_ref, wg_ref, bg_ref, lam_ref, mng_ref, wout_ref, g2_ref,
                wrt_ref, br_ref, anchor_ref,
                x1_ref, h2p_ref, idx_ref, rank_ref, gt_ref, cnt_ref,
                vbuf, rbuf, a_s, b_s, hcar, cnt_s,
                *, ts, sub, d_conv, d_lru, conv_w, lru_cw, n_exp):
    del anchor_ref
    b = pl.program_id(0)
    j = pl.program_id(1)

    @pl.when(j == 0)
    def _():
        _slab_store(vbuf, 0, jnp.zeros((CONV_PAD, d_conv), F32))
        _slab_store(rbuf, 0, jnp.zeros((SUBLANES, d_lru), F32))
        hcar[...] = jnp.zeros_like(hcar)

    @pl.when((b == 0) & (j == 0))
    def _():
        cnt_s[...] = jnp.zeros_like(cnt_s)

    mod = mod_ref[...]
    shift1, gate1, shift2 = mod[0:1], mod[2:3], mod[3:4]
    gain1 = g1_ref[...] * (1.0 + mod[1:2])
    gain2 = g2_ref[...] * (1.0 + mod[4:5])
    lam = lam_ref[...]
    softplus_neg_lam = jnp.maximum(-lam, 0.0) + jnp.log1p(jnp.exp(-jnp.abs(lam)))

    for q in range(ts // sub):
        _mix_subtile(q * sub, sub, x_ref, shift1, gain1, gate1, shift2, gain2, softplus_neg_lam, win_ref,
                     cw_ref, cb_ref, lng_ref, lnb_ref, lcw_ref, lcb_ref, wg_ref, bg_ref, mng_ref, wout_ref,
                     wrt_ref, br_ref, x1_ref, h2p_ref, idx_ref, rank_ref, gt_ref, vbuf, rbuf, a_s, b_s, hcar,
                     cnt_s, d_conv=d_conv, d_lru=d_lru, conv_w=conv_w, lru_cw=lru_cw, n_exp=n_exp)
    _slab_keep_tail(vbuf, CONV_PAD, ts)
    _slab_keep_tail(rbuf, SUBLANES, ts)
    cnt_ref[...] = cnt_s[:, :LANES]


def _mix_subtile(r0, ts, x_ref, shift1, gain1, gate1, shift2, gain2, softplus_neg_lam, win_ref,
                 cw_ref, cb_ref, lng_ref, lnb_ref, lcw_ref, lcb_ref, wg_ref, bg_ref, mng_ref, wout_ref,
                 wrt_ref, br_ref, x1_ref, h2p_ref, idx_ref, rank_ref, gt_ref, vbuf, rbuf, a_s, b_s, hcar,
                 cnt_s, *, d_conv, d_lru, conv_w, lru_cw, n_exp):
    rows = pl.ds(r0, ts)
    x = x_ref[rows, :]
    h = _rms(x) * gain1 + shift1
    u = jnp.dot(h.astype(BF16), win_ref[...], preferred_element_type=F32)

    v = u[:, :d_conv] * _sigmoid(u[:, d_conv:2 * d_conv])
    _slab_store(vbuf, CONV_PAD + r0, v)
    acc = _causal_tap_sum(vbuf, cw_ref, cb_ref[...], r0 + CONV_PAD - (conv_w - 1), conv_w, ts)
    mu = jnp.mean(acc, axis=-1, keepdims=True)
    cen = acc - mu
    var = jnp.mean(cen * cen, axis=-1, keepdims=True)
    yc = cen * lax.rsqrt(var + EPS) * lng_ref[...] + lnb_ref[...]
    yc = yc * _sigmoid(yc)

    u_gate = u[:, 2 * d_conv:2 * d_conv + d_lru]
    _slab_store(rbuf, SUBLANES + r0, u[:, 2 * d_conv + d_lru:])
    xr = _causal_tap_sum(rbuf, lcw_ref, lcb_ref[...], r0 + SUBLANES - (lru_cw - 1), lru_cw, ts)
    gates = jnp.dot(xr.astype(BF16), wg_ref[...], preferred_element_type=F32) + bg_ref[...]
    r = _sigmoid(gates[:, :d_lru])
    i_g = _sigmoid(gates[:, d_lru:])
    log_a = (-LRU_C) * r * softplus_neg_lam
    a = jnp.exp(log_a)
    inp = jnp.sqrt(1.0 - jnp.exp(2.0 * log_a)) * (i_g * xr)

    n_grp = ts // SUBLANES
    a3 = a.reshape(n_grp, SUBLANES, d_lru)
    b3 = inp.reshape(n_grp, SUBLANES, d_lru)
    sub = lax.broadcasted_iota(I32, (n_grp, SUBLANES, d_lru), 1)
    for s in (1, 2, 4):
        a_sh = pltpu.roll(a3, s, axis=1)
        b_sh = pltpu.roll(b3, s, axis=1)
        m = sub >= s
        b3 = jnp.where(m, a3 * b_sh + b3, b3)
        a3 = jnp.where(m, a3 * a_sh, a3)
    a_s[...] = a3.reshape(ts, d_lru)
    b_s[...] = b3.reshape(ts, d_lru)

    def grp_body(g, carry):
        rows = pl.ds(pl.multiple_of(g * SUBLANES, SUBLANES), SUBLANES)
        hg = b_s[rows, :] + a_s[rows, :] * carry
        b_s[rows, :] = hg
        return jnp.broadcast_to(hg[SUBLANES - 1:SUBLANES, :], (SUBLANES, d_lru))

    hcar[...] = lax.fori_loop(0, n_grp, grp_body, hcar[...], unroll=True)
    yl = _gelu_tanh(u_gate) * b_s[...]

    mng = mng_ref[...]
    yc_n = _rms(yc) * mng[:, :d_conv]
    yl_n = _rms(yl) * mng[:, d_conv:]
    mixo = jnp.dot(yc_n.astype(BF16), wout_ref[pl.ds(0, d_conv), :], preferred_element_type=F32)
    mixo += jnp.dot(yl_n.astype(BF16), wout_ref[pl.ds(d_conv, d_lru), :], preferred_element_type=F32)
    x1 = x + gate1 * mixo
    x1_ref[rows, :] = x1

    h2 = _rms(x1) * gain2 + shift2
    dh = h2.shape[1] // 2
    h2_hi = h2.astype(BF16)
    _store_planes(h2p_ref, _pack_bf16_pair(h2[:, :dh], h2[:, dh:]), rows)
    h2_lo = (h2 - h2_hi.astype(F32)).astype(BF16)
    nt_dims = (((1,), (1,)), ((), ()))
    wrt = wrt_ref[...]
    lg = lax.dot_general(wrt, h2_hi, nt_dims, preferred_element_type=F32)
    lg2 = lax.dot_general(wrt[:n_exp], h2_lo, nt_dims, preferred_element_type=F32)
    logits = lg[:n_exp] + lg[n_exp:] + lg2 + br_ref[...]

    eidx = lax.broadcasted_iota(I32, (n_exp, ts), 0)
    neg_inf = jnp.float32(-jnp.inf)
    work = logits
    vals, idxs, hots = [], [], []
    for _ in range(TOP_K):
        mval = jnp.max(work, axis=0, keepdims=True)
        midx = jnp.min(jnp.where(work == mval, eidx, n_exp), axis=0, keepdims=True)
        hot = eidx == midx
        vals.append(mval)
        idxs.append(midx)
        hots.append(hot)
        work = jnp.where(hot, neg_inf, work)
    exps = [jnp.exp(vk - vals[0]) for vk in vals]
    denom = exps[0] + exps[1] + exps[2] + exps[3]
    gate_rows = [ek / denom for ek in exps]

    sel = jnp.zeros((n_exp, ts), F32)
    for hot in hots:
        sel = sel + hot.astype(F32)
    tri = (lax.broadcasted_iota(I32, (ts, ts), 0) < lax.broadcasted_iota(I32, (ts, ts), 1)).astype(BF16)
    before = jnp.dot(sel.astype(BF16), tri, preferred_element_type=F32) + cnt_s[...]
    rank_rows = [jnp.sum(jnp.where(hot, before, 0.0), axis=0, keepdims=True) for hot in hots]
    cnt_s[...] = cnt_s[...] + jnp.sum(sel, axis=1, keepdims=True)

    idx_ref[:, rows] = jnp.concatenate(idxs, axis=0)
    rank_ref[:, rows] = jnp.concatenate(rank_rows, axis=0).astype(I32)
    g4 = jnp.concatenate(gate_rows, axis=0)
    g_pad = jnp.concatenate([g4, jnp.zeros((LANES - TOP_K, ts), F32)], axis=0)
    gt_ref[rows, :] = g_pad.T


def _mix_call(x, mod, p, b0, bsz, anchor):
    _, seq, dm = x.shape
    ts = min(SEQ_TILE, seq)
    nj = seq // ts
    d_conv = p["conv_w"].shape[1]
    d_lru = p["lru_conv_w"].shape[1]
    conv_w = p["conv_w"].shape[0]
    lru_cw = p["lru_conv_w"].shape[0]
    n_exp = p["wrt"].shape[0] // 2
    sub = min(MIX_SUB, ts)
    assert seq % ts == 0 and ts % sub == 0 and sub % LANES == 0 and sub % CONV_ROWS == 0
    assert conv_w - 1 <= CONV_PAD and lru_cw - 1 <= SUBLANES and d_conv % LANES == 0 and d_lru % LANES == 0

    def full(a):
        return pl.BlockSpec(a.shape, lambda b, j: (0,) * a.ndim)

    weights = [p["g1"], p["win"], p["conv_w"], p["conv_b"], p["ln_g"], p["ln_b"], p["lru_conv_w"],
               p["lru_conv_b"], p["wg"], p["bg"], p["lam"], p["mng"], p["wout"], p["g2"], p["wrt"], p["br"]]
    kern = functools.partial(_mix_kernel, ts=ts, sub=sub, d_conv=d_conv, d_lru=d_lru, conv_w=conv_w,
                             lru_cw=lru_cw, n_exp=n_exp)
    out_shape = (
        jax.ShapeDtypeStruct((bsz, seq, dm), F32),
        jax.ShapeDtypeStruct((SC_SPLIT, bsz * seq, dm // 2 // SC_SPLIT), I32),
        jax.ShapeDtypeStruct((TOP_K, bsz * seq), I32),
        jax.ShapeDtypeStruct((TOP_K, bsz * seq), I32),
        jax.ShapeDtypeStruct((bsz, seq, LANES), F32),
        jax.ShapeDtypeStruct((n_exp, LANES), F32),
    )
    return pl.pallas_call(
        kern,
        grid=(bsz, nj),
        in_specs=[pl.BlockSpec((None, ts, dm), lambda b, j: (b + b0, j, 0)),
                  pl.BlockSpec((None, N_MOD, dm), lambda b, j: (b + b0, 0, 0))] + [full(w) for w in weights]
        + [pl.BlockSpec(memory_space=pl.ANY)],
        out_specs=(
            pl.BlockSpec((None, ts, dm), lambda b, j: (b, j, 0)),
            pl.BlockSpec((SC_SPLIT, ts, dm // 2 // SC_SPLIT), lambda b, j: (0, b * nj + j, 0)),
            pl.BlockSpec((TOP_K, ts), lambda b, j: (0, b * nj + j)),
            pl.BlockSpec((TOP_K, ts), lambda b, j: (0, b * nj + j)),
            pl.BlockSpec((None, ts, LANES), lambda b, j: (b, j, 0)),
            pl.BlockSpec((n_exp, LANES), lambda b, j: (0, 0)),
        ),
        out_shape=out_shape,
        scratch_shapes=[
            pltpu.VMEM((d_conv // LANES, CONV_PAD + ts, LANES), F32),
            pltpu.VMEM((d_lru // LANES, SUBLANES + ts, LANES), F32),
            pltpu.VMEM((sub, d_lru), F32),
            pltpu.VMEM((sub, d_lru), F32),
            pltpu.VMEM((SUBLANES, d_lru), F32),
            pltpu.VMEM((n_exp, sub), F32),
        ],
        compiler_params=pltpu.CompilerParams(dimension_semantics=("arbitrary", "arbitrary"),
                                             vmem_limit_bytes=VMEM_LIMIT_BYTES),
        name="token_mix_route",
    )(x, mod, *weights, anchor)


def _sc_mesh():
    return plsc.VectorSubcoreMesh(core_axis_name="core", subcore_axis_name="subcore",
                                  num_cores=SC_CORES, num_subcores=SC_SUBCORES)


def _sc_scatter_rows(rows, idx, n_out):
    n, w = rows.shape
    n_k = idx.shape[0]
    assert n % (SC_WINDOW * SC_CORES * SC_SUBCORES) == 0

    @functools.partial(pl.kernel, out_type=jax.ShapeDtypeStruct((n_out, w), rows.dtype), mesh=_sc_mesh(),
                       scratch_types=[], name="moe_dispatch_sc")
    def scatter(x_hbm, i_hbm, o_hbm):
        def body(x_vmem, *i_vmems):
            for i_vmem in i_vmems:
                pltpu.sync_copy(x_vmem, o_hbm.at[i_vmem.at[0]])

        pltpu.emit_pipeline(
            body,
            grid=(n // SC_WINDOW,),
            in_specs=[pl.BlockSpec((SC_WINDOW, w), lambda i: (i, 0))]
            + [pl.BlockSpec((1, SC_WINDOW), functools.partial(lambda k, i: (k, i), k)) for k in range(n_k)],
            out_specs=[],
            core_axis_name=("core", "subcore"),
            dimension_semantics=(pltpu.PARALLEL,),
        )(x_hbm, *([i_hbm] * n_k))

    return scatter(rows, idx)


def _sc_gather_rows(table, idx):
    m = idx.shape[1]
    w = table.shape[1]
    assert m % (SC_WINDOW * SC_CORES * SC_SUBCORES) == 0

    @functools.partial(pl.kernel, out_type=jax.ShapeDtypeStruct((m, w), table.dtype), mesh=_sc_mesh(),
                       scratch_types=[], name="moe_gather_sc")
    def gather(x_hbm, i_hbm, o_hbm):
        def body(i_vmem, o_vmem):
            pltpu.sync_copy(x_hbm.at[i_vmem.at[0]], o_vmem)

        pltpu.emit_pipeline(
            body,
            grid=(m // SC_WINDOW,),
            in_specs=[pl.BlockSpec((1, SC_WINDOW), lambda i: (0, i))],
            out_specs=[pl.BlockSpec((SC_WINDOW, w), lambda i: (i, 0))],
            core_axis_name=("core", "subcore"),
            dimension_semantics=(pltpu.PARALLEL,),
        )(i_hbm, o_hbm)

    return gather(table, idx)


def _moe_kernel(be_ref, nb_ref, slot_ref, nxt_ref, nv_ref, xs_ref, wgu_hbm, bgu_ref, wd_hbm, bd_ref, ys_ref,
                wgu_f, wd_f, wgu_b, wd_b, sems, *, d_ff):
    i = pl.program_id(0)
    live = i < nb_ref[0]
    new_expert = (i == 0) | (be_ref[i] != be_ref[jnp.maximum(i - 1, 0)])

    def weight_copies(e, s):
        return (pltpu.make_async_copy(wgu_hbm.at[e], wgu_f.at[s], sems.at[0, s]),
                pltpu.make_async_copy(wd_hbm.at[e], wd_f.at[s], sems.at[1, s]))

    @pl.when(live & new_expert)
    def _():
        e, s, nx = be_ref[i], slot_ref[i], nxt_ref[i]

        @pl.when(i == 0)
        def _():
            for cp in weight_copies(e, s):
                cp.start()

        for cp in weight_copies(e, s):
            cp.wait()

        @pl.when(nx >= 0)
        def _():
            for cp in weight_copies(nx, 1 - s):
                cp.start()

        wgu_b[...] = wgu_f[s].astype(BF16)
        wd_b[...] = wd_f[s].astype(BF16)

    def sub_block(s):
        rows = pl.ds(s * MOE_SUB, MOE_SUB)
        packed = jnp.concatenate([xs_ref[p, rows, :] for p in range(SC_SPLIT)], axis=1)
        x_lo, x_hi = _unpack_bf16_pair(packed)
        dh = x_lo.shape[1]
        gu = jnp.dot(x_lo.astype(BF16), wgu_b[pl.ds(0, dh), :], preferred_element_type=F32)
        gu += jnp.dot(x_hi.astype(BF16), wgu_b[pl.ds(dh, dh), :], preferred_element_type=F32)
        gu += bgu_ref[...]
        g = jnp.minimum(gu[:, :d_ff], SWIGLU_LIMIT)
        u = jnp.clip(gu[:, d_ff:], -SWIGLU_LIMIT, SWIGLU_LIMIT)
        act = (u + 1.0) * (g * _sigmoid(SWIGLU_ALPHA * g))
        y = jnp.dot(act.astype(BF16), wd_b[...], preferred_element_type=F32) + bd_ref[...]
        dm2 = y.shape[1] // 2
        _store_planes(ys_ref, _pack_bf16_pair(y[:, :dm2], y[:, dm2:]), rows)

    n_sub = xs_ref.shape[1] // MOE_SUB
    n_valid = nv_ref[i]

    @pl.when(live & (n_valid == n_sub * MOE_SUB))
    def _():
        for s in range(n_sub):
            sub_block(s)

    @pl.when(live & (n_valid < n_sub * MOE_SUB))
    def _():
        for s in range(n_sub):
            pl.when(s * MOE_SUB < n_valid)(functools.partial(sub_block, s))


def _moe_call(xs, block_expert, n_used, seg_slot, next_expert, block_valid, wgu, bgu, wd, bd):
    _, n_rows, wp = xs.shape
    n_exp, dm, d_ff2 = wgu.shape
    d_ff = d_ff2 // 2
    bm = MOE_BLOCK
    n_blocks = n_rows // bm

    def row_map(i, be, nb, slot, nxt, nv):
        return (0, jnp.minimum(i, nb[0] - 1), 0)

    def b_map(i, be, nb, slot, nxt, nv):
        return (be[jnp.minimum(i, nb[0] - 1)], 0, 0)

    grid_spec = pltpu.PrefetchScalarGridSpec(
        num_scalar_prefetch=5,
        grid=(n_blocks,),
        in_specs=[
            pl.BlockSpec((SC_SPLIT, bm, wp), row_map),
            pl.BlockSpec(memory_space=pl.ANY),
            pl.BlockSpec((None, 1, d_ff2), b_map),
            pl.BlockSpec(memory_space=pl.ANY),
            pl.BlockSpec((None, 1, dm), b_map),
        ],
        out_specs=pl.BlockSpec((SC_SPLIT, bm, wp), row_map),
        scratch_shapes=[
            pltpu.VMEM((2, dm, d_ff2), F32), pltpu.VMEM((2, d_ff, dm), F32),
            pltpu.VMEM((dm, d_ff2), BF16), pltpu.VMEM((d_ff, dm), BF16),
            pltpu.SemaphoreType.DMA((2, 2)),
        ],
    )
    return pl.pallas_call(
        functools.partial(_moe_kernel, d_ff=d_ff),
        grid_spec=grid_spec,
        out_shape=jax.ShapeDtypeStruct(xs.shape, I32),
        compiler_params=pltpu.CompilerParams(dimension_semantics=("arbitrary",),
                                             vmem_limit_bytes=VMEM_LIMIT_BYTES),
        name="moe_experts",
    )(block_expert, n_used, seg_slot, next_expert, block_valid, xs, wgu, bgu, wd, bd)


def _combine_kernel(x1_ref, gt_ref, mod_ref, gf_ref, yg_ref, *out_refs):
    o_ref = out_refs[-1]
    gt = gt_ref[...]
    ts = yg_ref.shape[2]
    dh = yg_ref.shape[3] * SC_SPLIT
    acc_lo = jnp.zeros((ts, dh), F32)
    acc_hi = jnp.zeros((ts, dh), F32)
    for k in range(TOP_K):
        lo, hi = _unpack_bf16_pair(_load_planes(yg_ref.at[k]))
        gk = gt[:, k:k + 1]
        acc_lo += gk * lo
        acc_hi += gk * hi
    y = jnp.concatenate([acc_lo, acc_hi], axis=1)
    gate2 = mod_ref[...][5:6]
    x2 = x1_ref[...] + gate2 * y
    o_ref[...] = _rms(x2) * gf_ref[...]


def _combine_call(x1, gt, mod, gf, yg, out_prev, b0):
    bsz, seq, dm = x1.shape
    ts = min(COMBINE_TILE, seq)
    assert seq % ts == 0
    nj = seq // ts
    in_specs = [
        pl.BlockSpec((None, ts, dm), lambda b, j: (b, j, 0)),
        pl.BlockSpec((None, ts, LANES), lambda b, j: (b, j, 0)),
        pl.BlockSpec((None, N_MOD, dm), lambda b, j: (b + b0, 0, 0)),
        pl.BlockSpec((1, dm), lambda b, j: (0, 0)),
        pl.BlockSpec((TOP_K, SC_SPLIT, ts, yg.shape[3]), lambda b, j: (0, 0, b * nj + j, 0)),
    ]
    args = [x1, gt, mod, gf, yg]
    aliases = {}
    if out_prev is not None:
        in_specs.append(pl.BlockSpec(memory_space=pl.ANY))
        args.append(out_prev)
        aliases = {len(args) - 1: 0}
    return pl.pallas_call(
        _combine_kernel,
        grid=(bsz, nj),
        in_specs=in_specs,
        out_specs=pl.BlockSpec((None, ts, dm), lambda b, j: (b + b0, j, 0)),
        out_shape=jax.ShapeDtypeStruct((mod.shape[0], seq, dm), F32),
        input_output_aliases=aliases,
        compiler_params=pltpu.CompilerParams(dimension_semantics=("arbitrary", "arbitrary"),
                                             vmem_limit_bytes=VMEM_LIMIT_BYTES),
        name="moe_combine",
    )(*args)


def _block_diag(w):
    n_h, d, _ = w.shape
    eye = jnp.eye(n_h, dtype=w.dtype)
    return (eye[:, None, :, None] * w[:, :, None, :]).reshape(n_h * d, n_h * d)


def _layer(x, mod, l, w_in, norm1_g, conv_w, conv_b, conv_ln_g, conv_ln_b, lru_conv_w, lru_conv_b,
           lru_w_a, lru_b_a, lru_w_x, lru_b_x, lru_lambda, mix_norm_g, w_out, norm2_g, w_router,
           b_router, w_gate_up, b_gate_up, w_down, b_down, out_gain):
    bsz, seq, dm = x.shape
    n_tok = bsz * seq
    n_exp = w_router.shape[-1]
    row = lambda a: a.reshape(1, -1)
    wr = w_router[l]
    wr_hi = wr.astype(BF16)
    wr_lo = (wr - wr_hi.astype(F32)).astype(BF16)
    params = dict(
        g1=row(norm1_g[l]), win=w_in[l].astype(BF16), conv_w=conv_w[l], conv_b=row(conv_b[l]),
        ln_g=row(conv_ln_g[l]), ln_b=row(conv_ln_b[l]), lru_conv_w=lru_conv_w[l], lru_conv_b=row(lru_conv_b[l]),
        wg=jnp.concatenate([_block_diag(lru_w_a[l]), _block_diag(lru_w_x[l])], axis=1).astype(BF16),
        bg=jnp.concatenate([lru_b_a[l].reshape(1, -1), lru_b_x[l].reshape(1, -1)], axis=1),
        lam=row(lru_lambda[l]), mng=row(mix_norm_g[l]), wout=w_out[l].astype(BF16), g2=row(norm2_g[l]),
        wrt=jnp.concatenate([wr_hi.T, wr_lo.T], axis=0), br=b_router[l].reshape(n_exp, 1),
    )
    sizes = [s for s in CHUNK_BATCHES if s > 0] if sum(CHUNK_BATCHES) == bsz else [bsz]
    routed, b0, anchor = [], 0, mod
    for cb in sizes:
        r = _route_chunk(x, mod, params, b0, cb, anchor, n_exp)
        routed.append(r)
        anchor = r["dest_sub"]
        b0 += cb
    out = None
    for r in routed:
        out = _experts_chunk(r, mod, out, w_gate_up[l], b_gate_up[l][:, None, :], w_down[l],
                             b_down[l][:, None, :], out_gain)
    return out


def _route_chunk(x, mod, params, b0, cb, anchor, n_exp):
    seq, dm = x.shape[1:]
    n_tok = cb * seq
    x1, h2p, top_idx, rank, gt, cnt = _mix_call(x, mod, params, b0, cb, anchor)

    bm = MOE_BLOCK
    counts = cnt[:, 0].astype(I32)
    padded = (counts + bm - 1) // bm * bm
    e_ids = jnp.arange(n_exp, dtype=I32)
    pad_ends = jnp.sum(jnp.where(e_ids[None, :] <= e_ids[:, None], padded[None, :], 0), axis=1)
    pad_starts = pad_ends - padded
    n_rows = (n_tok * TOP_K // bm + n_exp) * bm
    n_blocks = n_rows // bm
    dest = rank + jnp.sum(jnp.where(top_idx[..., None] == e_ids, pad_starts, 0), axis=-1)
    block_start = jnp.arange(n_blocks, dtype=I32) * bm
    block_expert = jnp.minimum(
        jnp.sum((block_start[:, None] >= pad_ends[None, :]).astype(I32), axis=1), n_exp - 1)
    n_used = pad_ends[-1:] // bm
    present = counts > 0
    seg_ordinal = jnp.sum(jnp.where((e_ids[None, :] < e_ids[:, None]) & present[None, :], 1, 0), axis=1)
    following = jnp.min(jnp.where((e_ids[None, :] > e_ids[:, None]) & present[None, :], e_ids[None, :], n_exp),
                        axis=1)
    following = jnp.where(following == n_exp, -1, following)
    block_hot = block_expert[:, None] == e_ids[None, :]
    seg_slot = jnp.sum(jnp.where(block_hot, seg_ordinal[None, :] % 2, 0), axis=1)
    next_expert = jnp.sum(jnp.where(block_hot, following[None, :], 0), axis=1)
    block_valid = jnp.clip(
        jnp.sum(jnp.where(block_hot, (pad_starts + counts)[None, :], 0), axis=1) - block_start, 0, bm)

    wp = h2p.shape[2]
    plane = jnp.arange(SC_SPLIT, dtype=I32)[None, :, None] * n_rows
    dest_sub = dest[:, None, :] + plane
    return dict(b0=b0, n_tok=n_tok, n_rows=n_rows, x1=x1, h2p=h2p, gt=gt, dest_sub=dest_sub,
                tables=(block_expert, n_used, seg_slot, next_expert, block_valid))


def _experts_chunk(r, mod, out_prev, wgu, bgu, wd, bd, out_gain):
    n_tok, n_rows, dest_sub = r["n_tok"], r["n_rows"], r["dest_sub"]
    wp = r["h2p"].shape[2]
    xs = _sc_scatter_rows(r["h2p"].reshape(SC_SPLIT * n_tok, wp), dest_sub.reshape(TOP_K, SC_SPLIT * n_tok),
                          SC_SPLIT * n_rows).reshape(SC_SPLIT, n_rows, wp)
    ys = _moe_call(xs, *r["tables"], wgu, bgu, wd, bd)
    yg = _sc_gather_rows(ys.reshape(SC_SPLIT * n_rows, wp), dest_sub.reshape(1, TOP_K * SC_SPLIT * n_tok))
    return _combine_call(r["x1"], r["gt"], mod, out_gain, yg.reshape(TOP_K, SC_SPLIT, n_tok, wp), out_prev,
                         r["b0"])


def kernel(x, c, w_ada, b_ada, norm1_g, w_in, conv_w, conv_b, conv_ln_g, conv_ln_b, lru_conv_w, lru_conv_b,
           lru_w_a, lru_b_a, lru_w_x, lru_b_x, lru_lambda, mix_norm_g, w_out, norm2_g, w_router, b_router,
           w_gate_up, b_gate_up, w_down, b_down, final_norm_g):
    depth = w_ada.shape[0]
    assert depth == 1, "the final norm is fused into the (single) layer's combine kernel"
    bsz, seq, dm = x.shape
    mod = _ada_call(c, w_ada[0], b_ada[0]).reshape(bsz, N_MOD, dm)
    return _layer(x, mod, 0, w_in, norm1_g, conv_w, conv_b, conv_ln_g, conv_ln_b, lru_conv_w, lru_conv_b,
                  lru_w_a, lru_b_a, lru_w_x, lru_b_x, lru_lambda, mix_norm_g, w_out, norm2_g, w_router,
                  b_router, w_gate_up, b_gate_up, w_down, b_down, row_gain(final_norm_g))


def row_gain(g):
    return g.reshape(1, -1)
```

```python
import functools

import jax
import jax.numpy as jnp
from jax import lax
from jax.experimental import pallas as pl
from jax.experimental.pallas import tpu as pltpu
from jax.experimental.pallas import tpu_sc as plsc

F32 = jnp.float32
BF16 = jnp.bfloat16
I32 = jnp.int32

EPS = 1e-6
N_MOD = 6
LRU_C = 8.0
TOP_K = 4
SWIGLU_ALPHA = 1.702
SWIGLU_LIMIT = 7.0

LANES = 128
SUBLANES = 8
VMEM_LIMIT_BYTES = 56 * 1024 * 1024

SEQ_TILE = 512
COMBINE_TILE = 512
CONV_PAD = 32
CONV_ROWS = 64
MOE_BLOCK = 1024
MOE_SUB = 256
CHUNK_BATCHES = (4, 4)
HI_MASK = -65536

SC_CORES = 2
SC_SUBCORES = 16
SC_WINDOW = 128
SC_SPLIT = 2


def _sigmoid(x):
    return jax.nn.sigmoid(x)


def _pack_bf16_pair(lo_f32, hi_f32):
    lo_bits = lax.bitcast_convert_type(lo_f32.astype(BF16).astype(F32), I32)
    hi_bits = lax.bitcast_convert_type(hi_f32.astype(BF16).astype(F32), I32)
    return lax.shift_right_logical(lo_bits, 16) | hi_bits


def _unpack_bf16_pair(p):
    lo = lax.bitcast_convert_type(lax.shift_left(p, 16), F32)
    hi = lax.bitcast_convert_type(p & HI_MASK, F32)
    return lo, hi


def _store_planes(ref, packed, rows=slice(None)):
    wp = packed.shape[1] // SC_SPLIT
    for s in range(SC_SPLIT):
        ref[s, rows, :] = packed[:, s * wp:(s + 1) * wp]


def _load_planes(ref):
    return jnp.concatenate([ref[s] for s in range(SC_SPLIT)], axis=1)


def _ada_kernel(c_ref, w_ref, b_ref, o_ref):
    c = c_ref[...]
    ca = c * _sigmoid(c)
    w = w_ref[...]
    c_hi = ca.astype(BF16)
    c_lo = (ca - c_hi.astype(F32)).astype(BF16)
    w_hi = w.astype(BF16)
    w_lo = (w - w_hi.astype(F32)).astype(BF16)
    acc = jnp.dot(c_hi, w_hi, preferred_element_type=F32)
    acc += jnp.dot(c_lo, w_hi, preferred_element_type=F32)
    acc += jnp.dot(c_hi, w_lo, preferred_element_type=F32)
    o_ref[...] = acc + b_ref[...]


def _ada_call(c, w_ada, b_ada):
    bsz, dm = c.shape
    n_out = w_ada.shape[1]
    tn = 1024
    return pl.pallas_call(
        _ada_kernel,
        grid=(n_out // tn,),
        in_specs=[
            pl.BlockSpec((bsz, dm), lambda n: (0, 0)),
            pl.BlockSpec((dm, tn), lambda n: (0, n)),
            pl.BlockSpec((1, tn), lambda n: (0, n)),
        ],
        out_specs=pl.BlockSpec((bsz, tn), lambda n: (0, n)),
        out_shape=jax.ShapeDtypeStruct((bsz, n_out), F32),
        compiler_params=pltpu.CompilerParams(dimension_semantics=("arbitrary",)),
        name="ada_mod",
    )(c, w_ada, b_ada.reshape(1, n_out))


def _rms(x, eps=EPS):
    return x * lax.rsqrt(jnp.mean(x * x, axis=-1, keepdims=True) + eps)


def _gelu_tanh(x):
    return 0.5 * x * (1.0 + jnp.tanh(0.7978845608028654 * (x + 0.044715 * (x * x * x))))


def _slab_store(buf, row0, val):
    for s in range(buf.shape[0]):
        buf[s, pl.ds(row0, val.shape[0]), :] = val[:, s * LANES:(s + 1) * LANES]


def _slab_keep_tail(buf, keep, ts):
    for s in range(buf.shape[0]):
        buf[s, pl.ds(0, keep), :] = buf[s, pl.ds(ts, keep), :]


def _causal_tap_sum(buf, w_ref, bias_row, first, n_taps, ts):
    w = w_ref[...]
    cols = []
    for s in range(buf.shape[0]):
        lanes = slice(s * LANES, (s + 1) * LANES)
        chunks = []
        for c in range(0, ts, CONV_ROWS):
            acc = jnp.broadcast_to(bias_row[:, lanes], (CONV_ROWS, LANES))
            for k in range(n_taps):
                acc = acc + w[k:k + 1, lanes] * buf[s, pl.ds(c + first + k, CONV_ROWS), :]
            chunks.append(acc)
        cols.append(jnp.concatenate(chunks, axis=0))
    return jnp.concatenate(cols, axis=1)


def _mix_kernel(x_ref, xn_ref, mod_ref, modn_ref, g1_ref, win_ref, cw_ref, cb_ref, lng_ref, lnb_ref,
                lcw_ref, lcb_ref, wg_ref, bg_ref, lam_ref, mng_ref, wout_ref, g2_ref,
                wrt_ref, br_ref, anchor_ref,
                x1_ref, h2p_ref, idx_ref, rank_ref, gt_ref, cnt_ref,
                u_a, u_b, vbuf, rbuf, a_s, b_s, hcar, cnt_s,
                *, tile, steps_per_seq, d_conv, d_lru, conv_w, lru_cw, n_exp):
    del anchor_ref
    s = pl.program_id(0)

    @pl.when(s % steps_per_seq == 0)
    def _():
        _slab_store(vbuf, 0, jnp.zeros((CONV_PAD, d_conv), F32))
        _slab_store(rbuf, 0, jnp.zeros((SUBLANES, d_lru), F32))
        hcar[...] = jnp.zeros_like(hcar)

    mod = mod_ref[...]
    modn = modn_ref[...]
    shift1, gate1, shift2 = mod[0:1], mod[2:3], mod[3:4]
    gain1 = g1_ref[...] * (1.0 + mod[1:2])
    gain2 = g2_ref[...] * (1.0 + mod[4:5])
    lam = lam_ref[...]
    softplus_neg_lam = jnp.maximum(-lam, 0.0) + jnp.log1p(jnp.exp(-jnp.abs(lam)))

    def project(src_ref, r0, gain, shift, u_ref):
        h = _rms(src_ref[pl.ds(r0, tile), :]) * gain + shift
        u_ref[...] = jnp.dot(h.astype(BF16), win_ref[...], preferred_element_type=F32)

    def conv_group(r0, u_ref):
        return _mix_conv_group(r0, tile, u_ref, cw_ref, cb_ref, lng_ref, lnb_ref, vbuf, d_conv=d_conv,
                               conv_w=conv_w)

    def finish(r0, u_ref, yc):
        _mix_finish(r0, tile, x_ref, u_ref, yc, gate1, shift2, gain2, softplus_neg_lam,
                    lcw_ref, lcb_ref, wg_ref, bg_ref, mng_ref, wout_ref,
                    wrt_ref, br_ref, x1_ref, h2p_ref, idx_ref, rank_ref, gt_ref, rbuf, a_s, b_s, hcar,
                    cnt_s, d_conv=d_conv, d_lru=d_lru, lru_cw=lru_cw, n_exp=n_exp)

    @pl.when(s == 0)
    def _():
        cnt_s[...] = jnp.zeros_like(cnt_s)
        project(x_ref, 0, gain1, shift1, u_a)

    yc0 = conv_group(0, u_a)
    project(x_ref, tile, gain1, shift1, u_b)
    finish(0, u_a, yc0)
    yc1 = conv_group(tile, u_b)
    project(xn_ref, 0, g1_ref[...] * (1.0 + modn[1:2]), modn[0:1], u_a)
    finish(tile, u_b, yc1)
    _slab_keep_tail(vbuf, CONV_PAD, 2 * tile)
    _slab_keep_tail(rbuf, SUBLANES, 2 * tile)
    cnt_ref[...] = cnt_s[:, :LANES]


def _mix_conv_group(r0, ts, u_ref, cw_ref, cb_ref, lng_ref, lnb_ref, vbuf, *, d_conv, conv_w):
    v = u_ref[:, :d_conv] * _sigmoid(u_ref[:, d_conv:2 * d_conv])
    _slab_store(vbuf, CONV_PAD + r0, v)
    acc = _causal_tap_sum(vbuf, cw_ref, cb_ref[...], r0 + CONV_PAD - (conv_w - 1), conv_w, ts)
    mu = jnp.mean(acc, axis=-1, keepdims=True)
    cen = acc - mu
    var = jnp.mean(cen * cen, axis=-1, keepdims=True)
    yc = cen * lax.rsqrt(var + EPS) * lng_ref[...] + lnb_ref[...]
    return yc * _sigmoid(yc)


def _mix_finish(r0, ts, x_ref, u_ref, yc, gate1, shift2, gain2, softplus_neg_lam,
                lcw_ref, lcb_ref, wg_ref, bg_ref, mng_ref, wout_ref,
                wrt_ref, br_ref, x1_ref, h2p_ref, idx_ref, rank_ref, gt_ref, rbuf, a_s, b_s, hcar,
                cnt_s, *, d_conv, d_lru, lru_cw, n_exp):
    rows = pl.ds(r0, ts)
    x = x_ref[rows, :]
    u = u_ref

    u_gate = u[:, 2 * d_conv:2 * d_conv + d_lru]
    _slab_store(rbuf, SUBLANES + r0, u[:, 2 * d_conv + d_lru:])
    xr = _causal_tap_sum(rbuf, lcw_ref, lcb_ref[...], r0 + SUBLANES - (lru_cw - 1), lru_cw, ts)
    gates = jnp.dot(xr.astype(BF16), wg_ref[...], preferred_element_type=F32) + bg_ref[...]
    r = _sigmoid(gates[:, :d_lru])
    i_g = _sigmoid(gates[:, d_lru:])
    log_a = (-LRU_C) * r * softplus_neg_lam
    a = jnp.exp(log_a)
    inp = jnp.sqrt(1.0 - jnp.exp(2.0 * log_a)) * (i_g * xr)

    n_grp = ts // SUBLANES
    a3 = a.reshape(n_grp, SUBLANES, d_lru)
    b3 = inp.reshape(n_grp, SUBLANES, d_lru)
    sub = lax.broadcasted_iota(I32, (n_grp, SUBLANES, d_lru), 1)
    for s in (1, 2, 4):
        a_sh = pltpu.roll(a3, s, axis=1)
        b_sh = pltpu.roll(b3, s, axis=1)
        m = sub >= s
        b3 = jnp.where(m, a3 * b_sh + b3, b3)
        a3 = jnp.where(m, a3 * a_sh, a3)
    a_s[...] = a3.reshape(ts, d_lru)
    b_s[...] = b3.reshape(ts, d_lru)

    def grp_body(g, carry):
        rows = pl.ds(pl.multiple_of(g * SUBLANES, SUBLANES), SUBLANES)
        hg = b_s[rows, :] + a_s[rows, :] * carry
        b_s[rows, :] = hg
        return jnp.broadcast_to(hg[SUBLANES - 1:SUBLANES, :], (SUBLANES, d_lru))

    hcar[...] = lax.fori_loop(0, n_grp, grp_body, hcar[...], unroll=True)
    yl = _gelu_tanh(u_gate) * b_s[...]

    mng = mng_ref[...]
    yc_n = _rms(yc) * mng[:, :d_conv]
    yl_n = _rms(yl) * mng[:, d_conv:]
    mixo = jnp.dot(yc_n.astype(BF16), wout_ref[pl.ds(0, d_conv), :], preferred_element_type=F32)
    mixo += jnp.dot(yl_n.astype(BF16), wout_ref[pl.ds(d_conv, d_lru), :], preferred_element_type=F32)
    x1 = x + gate1 * mixo
    x1_ref[rows, :] = x1

    h2 = _rms(x1) * gain2 + shift2
    dh = h2.shape[1] // 2
    h2_hi = h2.astype(BF16)
    _store_planes(h2p_ref, _pack_bf16_pair(h2[:, :dh], h2[:, dh:]), rows)
    h2_lo = (h2 - h2_hi.astype(F32)).astype(BF16)
    nt_dims = (((1,), (1,)), ((), ()))
    wrt = wrt_ref[...]
    lg = lax.dot_general(wrt, h2_hi, nt_dims, preferred_element_type=F32)
    lg2 = lax.dot_general(wrt[:n_exp], h2_lo, nt_dims, preferred_element_type=F32)
    logits = lg[:n_exp] + lg[n_exp:] + lg2 + br_ref[...]

    eidx = lax.broadcasted_iota(I32, (n_exp, ts), 0)
    neg_inf = jnp.float32(-jnp.inf)
    work = logits
    vals, idxs, hots = [], [], []
    for _ in range(TOP_K):
        mval = jnp.max(work, axis=0, keepdims=True)
        midx = jnp.min(jnp.where(work == mval, eidx, n_exp), axis=0, keepdims=True)
        hot = eidx == midx
        vals.append(mval)
        idxs.append(midx)
        hots.append(hot)
        work = jnp.where(hot, neg_inf, work)
    exps = [jnp.exp(vk - vals[0]) for vk in vals]
    denom = exps[0] + exps[1] + exps[2] + exps[3]
    gate_rows = [ek / denom for ek in exps]

    sel = jnp.zeros((n_exp, ts), F32)
    for hot in hots:
        sel = sel + hot.astype(F32)
    tri = (lax.broadcasted_iota(I32, (ts, ts), 0) < lax.broadcasted_iota(I32, (ts, ts), 1)).astype(BF16)
    before = jnp.dot(sel.astype(BF16), tri, preferred_element_type=F32) + cnt_s[...]
    rank_rows = [jnp.sum(jnp.where(hot, before, 0.0), axis=0, keepdims=True) for hot in hots]
    cnt_s[...] = cnt_s[...] + jnp.sum(sel, axis=1, keepdims=True)

    idx_ref[:, rows] = jnp.concatenate(idxs, axis=0)
    rank_ref[:, rows] = jnp.concatenate(rank_rows, axis=0).astype(I32)
    g4 = jnp.concatenate(gate_rows, axis=0)
    g_pad = jnp.concatenate([g4, jnp.zeros((LANES - TOP_K, ts), F32)], axis=0)
    gt_ref[rows, :] = g_pad.T


def _mix_call(x, mod, p, b0, bsz, anchor):
    _, seq, dm = x.shape
    tile = min(SEQ_TILE, seq // 2)
    ts = 2 * tile
    nj = seq // ts
    n_steps = bsz * nj
    tiles_per_seq = seq // tile
    d_conv = p["conv_w"].shape[1]
    d_lru = p["lru_conv_w"].shape[1]
    conv_w = p["conv_w"].shape[0]
    lru_cw = p["lru_conv_w"].shape[0]
    n_exp = p["wrt"].shape[0] // 2
    assert seq % ts == 0 and tile % LANES == 0 and tile % CONV_ROWS == 0
    assert conv_w - 1 <= CONV_PAD and lru_cw - 1 <= SUBLANES and d_conv % LANES == 0 and d_lru % LANES == 0

    def full(a):
        return pl.BlockSpec(a.shape, lambda s: (0,) * a.ndim)

    def next_tile(s):
        return jnp.minimum(2 * s + 2, bsz * tiles_per_seq - 1)

    weights = [p["g1"], p["win"], p["conv_w"], p["conv_b"], p["ln_g"], p["ln_b"], p["lru_conv_w"],
               p["lru_conv_b"], p["wg"], p["bg"], p["lam"], p["mng"], p["wout"], p["g2"], p["wrt"], p["br"]]
    kern = functools.partial(_mix_kernel, tile=tile, steps_per_seq=nj, d_conv=d_conv, d_lru=d_lru,
                             conv_w=conv_w, lru_cw=lru_cw, n_exp=n_exp)
    out_shape = (
        jax.ShapeDtypeStruct((bsz, seq, dm), F32),
        jax.ShapeDtypeStruct((SC_SPLIT, bsz * seq, dm // 2 // SC_SPLIT), I32),
        jax.ShapeDtypeStruct((TOP_K, bsz * seq), I32),
        jax.ShapeDtypeStruct((TOP_K, bsz * seq), I32),
        jax.ShapeDtypeStruct((bsz, seq, LANES), F32),
        jax.ShapeDtypeStruct((n_exp, LANES), F32),
    )
    return pl.pallas_call(
        kern,
        grid=(n_steps,),
        in_specs=[pl.BlockSpec((None, ts, dm), lambda s: (s // nj + b0, s % nj, 0)),
                  pl.BlockSpec((None, tile, dm),
                               lambda s: (next_tile(s) // tiles_per_seq + b0, next_tile(s) % tiles_per_seq, 0)),
                  pl.BlockSpec((None, N_MOD, dm), lambda s: (s // nj + b0, 0, 0)),
                  pl.BlockSpec((None, N_MOD, dm), lambda s: (next_tile(s) // tiles_per_seq + b0, 0, 0))]
        + [full(w) for w in weights] + [pl.BlockSpec(memory_space=pl.ANY)],
        out_specs=(
            pl.BlockSpec((None, ts, dm), lambda s: (s // nj, s % nj, 0)),
            pl.BlockSpec((SC_SPLIT, ts, dm // 2 // SC_SPLIT), lambda s: (0, s, 0)),
            pl.BlockSpec((TOP_K, ts), lambda s: (0, s)),
            pl.BlockSpec((TOP_K, ts), lambda s: (0, s)),
            pl.BlockSpec((None, ts, LANES), lambda s: (s // nj, s % nj, 0)),
            pl.BlockSpec((n_exp, LANES), lambda s: (0, 0)),
        ),
        out_shape=out_shape,
        scratch_shapes=[
            pltpu.VMEM((tile, 2 * d_conv + 2 * d_lru), F32),
            pltpu.VMEM((tile, 2 * d_conv + 2 * d_lru), F32),
            pltpu.VMEM((d_conv // LANES, CONV_PAD + ts, LANES), F32),
            pltpu.VMEM((d_lru // LANES, SUBLANES + ts, LANES), F32),
            pltpu.VMEM((tile, d_lru), F32),
            pltpu.VMEM((tile, d_lru), F32),
            pltpu.VMEM((SUBLANES, d_lru), F32),
            pltpu.VMEM((n_exp, tile), F32),
        ],
        compiler_params=pltpu.CompilerParams(dimension_semantics=("arbitrary",),
                                             vmem_limit_bytes=VMEM_LIMIT_BYTES),
        name="token_mix_route",
    )(x, x, mod, mod, *weights, anchor)


def _sc_mesh():
    return plsc.VectorSubcoreMesh(core_axis_name="core", subcore_axis_name="subcore",
                                  num_cores=SC_CORES, num_subcores=SC_SUBCORES)


def _sc_scatter_rows(rows, idx, n_out):
    n, w = rows.shape
    n_k = idx.shape[0]
    assert n % (SC_WINDOW * SC_CORES * SC_SUBCORES) == 0

    @functools.partial(pl.kernel, out_type=jax.ShapeDtypeStruct((n_out, w), rows.dtype), mesh=_sc_mesh(),
                       scratch_types=[], name="moe_dispatch_sc")
    def scatter(x_hbm, i_hbm, o_hbm):
        def body(x_vmem, *i_vmems):
            for i_vmem in i_vmems:
                pltpu.sync_copy(x_vmem, o_hbm.at[i_vmem.at[0]])

        pltpu.emit_pipeline(
            body,
            grid=(n // SC_WINDOW,),
            in_specs=[pl.BlockSpec((SC_WINDOW, w), lambda i: (i, 0))]
            + [pl.BlockSpec((1, SC_WINDOW), functools.partial(lambda k, i: (k, i), k)) for k in range(n_k)],
            out_specs=[],
            core_axis_name=("core", "subcore"),
            dimension_semantics=(pltpu.PARALLEL,),
        )(x_hbm, *([i_hbm] * n_k))

    return scatter(rows, idx)


def _sc_gather_rows(table, idx):
    m = idx.shape[1]
    w = table.shape[1]
    assert m % (SC_WINDOW * SC_CORES * SC_SUBCORES) == 0

    @functools.partial(pl.kernel, out_type=jax.ShapeDtypeStruct((m, w), table.dtype), mesh=_sc_mesh(),
                       scratch_types=[], name="moe_gather_sc")
    def gather(x_hbm, i_hbm, o_hbm):
        def body(i_vmem, o_vmem):
            pltpu.sync_copy(x_hbm.at[i_vmem.at[0]], o_vmem)

        pltpu.emit_pipeline(
            body,
            grid=(m // SC_WINDOW,),
            in_specs=[pl.BlockSpec((1, SC_WINDOW), lambda i: (0, i))],
            out_specs=[pl.BlockSpec((SC_WINDOW, w), lambda i: (i, 0))],
            core_axis_name=("core", "subcore"),
            dimension_semantics=(pltpu.PARALLEL,),
        )(i_hbm, o_hbm)

    return gather(table, idx)


def _moe_kernel(be_ref, nb_ref, slot_ref, nxt_ref, nv_ref, xs_ref, wgu_hbm, bgu_ref, wd_hbm, bd_ref, ys_ref,
                wgu_f, wd_f, wgu_b, wd_b, sems, *, d_ff):
    i = pl.program_id(0)
    live = i < nb_ref[0]
    new_expert = (i == 0) | (be_ref[i] != be_ref[jnp.maximum(i - 1, 0)])

    def weight_copies(e, s):
        return (pltpu.make_async_copy(wgu_hbm.at[e], wgu_f.at[s], sems.at[0, s]),
                pltpu.make_async_copy(wd_hbm.at[e], wd_f.at[s], sems.at[1, s]))

    @pl.when(live & new_expert)
    def _():
        e, s, nx = be_ref[i], slot_ref[i], nxt_ref[i]

        @pl.when(i == 0)
        def _():
            for cp in weight_copies(e, s):
                cp.start()

        for cp in weight_copies(e, s):
            cp.wait()

        @pl.when(nx >= 0)
        def _():
            for cp in weight_copies(nx, 1 - s):
                cp.start()

        wgu_b[...] = wgu_f[s].astype(BF16)
        wd_b[...] = wd_f[s].astype(BF16)

    def sub_block(s):
        rows = pl.ds(s * MOE_SUB, MOE_SUB)
        packed = jnp.concatenate([xs_ref[p, rows, :] for p in range(SC_SPLIT)], axis=1)
        x_lo, x_hi = _unpack_bf16_pair(packed)
        dh = x_lo.shape[1]
        gu = jnp.dot(x_lo.astype(BF16), wgu_b[pl.ds(0, dh), :], preferred_element_type=F32)
        gu += jnp.dot(x_hi.astype(BF16), wgu_b[pl.ds(dh, dh), :], preferred_element_type=F32)
        gu += bgu_ref[...]
        g = jnp.minimum(gu[:, :d_ff], SWIGLU_LIMIT)
        u = jnp.clip(gu[:, d_ff:], -SWIGLU_LIMIT, SWIGLU_LIMIT)
        act = (u + 1.0) * (g * _sigmoid(SWIGLU_ALPHA * g))
        y = jnp.dot(act.astype(BF16), wd_b[...], preferred_element_type=F32) + bd_ref[...]
        dm2 = y.shape[1] // 2
        _store_planes(ys_ref, _pack_bf16_pair(y[:, :dm2], y[:, dm2:]), rows)

    n_sub = xs_ref.shape[1] // MOE_SUB
    n_valid = nv_ref[i]

    @pl.when(live & (n_valid == n_sub * MOE_SUB))
    def _():
        for s in range(n_sub):
            sub_block(s)

    @pl.when(live & (n_valid < n_sub * MOE_SUB))
    def _():
        for s in range(n_sub):
            pl.when(s * MOE_SUB < n_valid)(functools.partial(sub_block, s))


def _moe_call(xs, block_expert, n_used, seg_slot, next_expert, block_valid, wgu, bgu, wd, bd):
    _, n_rows, wp = xs.shape
    n_exp, dm, d_ff2 = wgu.shape
    d_ff = d_ff2 // 2
    bm = MOE_BLOCK
    n_blocks = n_rows // bm

    def row_map(i, be, nb, slot, nxt, nv):
        return (0, jnp.minimum(i, nb[0] - 1), 0)

    def b_map(i, be, nb, slot, nxt, nv):
        return (be[jnp.minimum(i, nb[0] - 1)], 0, 0)

    grid_spec = pltpu.PrefetchScalarGridSpec(
        num_scalar_prefetch=5,
        grid=(n_blocks,),
        in_specs=[
            pl.BlockSpec((SC_SPLIT, bm, wp), row_map),
            pl.BlockSpec(memory_space=pl.ANY),
            pl.BlockSpec((None, 1, d_ff2), b_map),
            pl.BlockSpec(memory_space=pl.ANY),
            pl.BlockSpec((None, 1, dm), b_map),
        ],
        out_specs=pl.BlockSpec((SC_SPLIT, bm, wp), row_map),
        scratch_shapes=[
            pltpu.VMEM((2, dm, d_ff2), F32), pltpu.VMEM((2, d_ff, dm), F32),
            pltpu.VMEM((dm, d_ff2), BF16), pltpu.VMEM((d_ff, dm), BF16),
            pltpu.SemaphoreType.DMA((2, 2)),
        ],
    )
    return pl.pallas_call(
        functools.partial(_moe_kernel, d_ff=d_ff),
        grid_spec=grid_spec,
        out_shape=jax.ShapeDtypeStruct(xs.shape, I32),
        compiler_params=pltpu.CompilerParams(dimension_semantics=("arbitrary",),
                                             vmem_limit_bytes=VMEM_LIMIT_BYTES),
        name="moe_experts",
    )(block_expert, n_used, seg_slot, next_expert, block_valid, xs, wgu, bgu, wd, bd)


def _combine_kernel(x1_ref, gt_ref, mod_ref, gf_ref, yg_ref, *out_refs):
    o_ref = out_refs[-1]
    gt = gt_ref[...]
    ts = yg_ref.shape[2]
    dh = yg_ref.shape[3] * SC_SPLIT
    acc_lo = jnp.zeros((ts, dh), F32)
    acc_hi = jnp.zeros((ts, dh), F32)
    for k in range(TOP_K):
        lo, hi = _unpack_bf16_pair(_load_planes(yg_ref.at[k]))
        gk = gt[:, k:k + 1]
        acc_lo += gk * lo
        acc_hi += gk * hi
    y = jnp.concatenate([acc_lo, acc_hi], axis=1)
    gate2 = mod_ref[...][5:6]
    x2 = x1_ref[...] + gate2 * y
    o_ref[...] = _rms(x2) * gf_ref[...]


def _combine_call(x1, gt, mod, gf, yg, out_prev, b0):
    bsz, seq, dm = x1.shape
    ts = min(COMBINE_TILE, seq)
    assert seq % ts == 0
    nj = seq // ts
    in_specs = [
        pl.BlockSpec((None, ts, dm), lambda b, j: (b, j, 0)),
        pl.BlockSpec((None, ts, LANES), lambda b, j: (b, j, 0)),
        pl.BlockSpec((None, N_MOD, dm), lambda b, j: (b + b0, 0, 0)),
        pl.BlockSpec((1, dm), lambda b, j: (0, 0)),
        pl.BlockSpec((TOP_K, SC_SPLIT, ts, yg.shape[3]), lambda b, j: (0, 0, b * nj + j, 0)),
    ]
    args = [x1, gt, mod, gf, yg]
    aliases = {}
    if out_prev is not None:
        in_specs.append(pl.BlockSpec(memory_space=pl.ANY))
        args.append(out_prev)
        aliases = {len(args) - 1: 0}
    return pl.pallas_call(
        _combine_kernel,
        grid=(bsz, nj),
        in_specs=in_specs,
        out_specs=pl.BlockSpec((None, ts, dm), lambda b, j: (b + b0, j, 0)),
        out_shape=jax.ShapeDtypeStruct((mod.shape[0], seq, dm), F32),
        input_output_aliases=aliases,
        compiler_params=pltpu.CompilerParams(dimension_semantics=("arbitrary", "arbitrary"),
                                             vmem_limit_bytes=VMEM_LIMIT_BYTES),
        name="moe_combine",
    )(*args)


def _block_diag(w):
    n_h, d, _ = w.shape
    eye = jnp.eye(n_h, dtype=w.dtype)
    return (eye[:, None, :, None] * w[:, :, None, :]).reshape(n_h * d, n_h * d)


def _layer(x, mod, l, w_in, norm1_g, conv_w, conv_b, conv_ln_g, conv_ln_b, lru_conv_w, lru_conv_b,
           lru_w_a, lru_b_a, lru_w_x, lru_b_x, lru_lambda, mix_norm_g, w_out, norm2_g, w_router,
           b_router, w_gate_up, b_gate_up, w_down, b_down, out_gain):
    bsz, seq, dm = x.shape
    n_tok = bsz * seq
    n_exp = w_router.shape[-1]
    row = lambda a: a.reshape(1, -1)
    wr = w_router[l]
    wr_hi = wr.astype(BF16)
    wr_lo = (wr - wr_hi.astype(F32)).astype(BF16)
    params = dict(
        g1=row(norm1_g[l]), win=w_in[l].astype(BF16), conv_w=conv_w[l], conv_b=row(conv_b[l]),
        ln_g=row(conv_ln_g[l]), ln_b=row(conv_ln_b[l]), lru_conv_w=lru_conv_w[l], lru_conv_b=row(lru_conv_b[l]),
        wg=jnp.concatenate([_block_diag(lru_w_a[l]), _block_diag(lru_w_x[l])], axis=1).astype(BF16),
        bg=jnp.concatenate([lru_b_a[l].reshape(1, -1), lru_b_x[l].reshape(1, -1)], axis=1),
        lam=row(lru_lambda[l]), mng=row(mix_norm_g[l]), wout=w_out[l].astype(BF16), g2=row(norm2_g[l]),
        wrt=jnp.concatenate([wr_hi.T, wr_lo.T], axis=0), br=b_router[l].reshape(n_exp, 1),
    )
    sizes = [s for s in CHUNK_BATCHES if s > 0] if sum(CHUNK_BATCHES) == bsz else [bsz]
    routed, b0, anchor = [], 0, mod
    for cb in sizes:
        r = _route_chunk(x, mod, params, b0, cb, anchor, n_exp)
        routed.append(r)
        anchor = r["dest_sub"]
        b0 += cb
    out = None
    for r in routed:
        out = _experts_chunk(r, mod, out, w_gate_up[l], b_gate_up[l][:, None, :], w_down[l],
                             b_down[l][:, None, :], out_gain)
    return out


def _route_chunk(x, mod, params, b0, cb, anchor, n_exp):
    seq, dm = x.shape[1:]
    n_tok = cb * seq
    x1, h2p, top_idx, rank, gt, cnt = _mix_call(x, mod, params, b0, cb, anchor)

    bm = MOE_BLOCK
    counts = cnt[:, 0].astype(I32)
    padded = (counts + bm - 1) // bm * bm
    e_ids = jnp.arange(n_exp, dtype=I32)
    pad_ends = jnp.sum(jnp.where(e_ids[None, :] <= e_ids[:, None], padded[None, :], 0), axis=1)
    pad_starts = pad_ends - padded
    n_rows = (n_tok * TOP_K // bm + n_exp) * bm
    n_blocks = n_rows // bm
    dest = rank + jnp.sum(jnp.where(top_idx[..., None] == e_ids, pad_starts, 0), axis=-1)
    block_start = jnp.arange(n_blocks, dtype=I32) * bm
    block_expert = jnp.minimum(
        jnp.sum((block_start[:, None] >= pad_ends[None, :]).astype(I32), axis=1), n_exp - 1)
    n_used = pad_ends[-1:] // bm
    present = counts > 0
    seg_ordinal = jnp.sum(jnp.where((e_ids[None, :] < e_ids[:, None]) & present[None, :], 1, 0), axis=1)
    following = jnp.min(jnp.where((e_ids[None, :] > e_ids[:, None]) & present[None, :], e_ids[None, :], n_exp),
                        axis=1)
    following = jnp.where(following == n_exp, -1, following)
    block_hot = block_expert[:, None] == e_ids[None, :]
    seg_slot = jnp.sum(jnp.where(block_hot, seg_ordinal[None, :] % 2, 0), axis=1)
    next_expert = jnp.sum(jnp.where(block_hot, following[None, :], 0), axis=1)
    block_valid = jnp.clip(
        jnp.sum(jnp.where(block_hot, (pad_starts + counts)[None, :], 0), axis=1) - block_start, 0, bm)

    wp = h2p.shape[2]
    plane = jnp.arange(SC_SPLIT, dtype=I32)[None, :, None] * n_rows
    dest_sub = dest[:, None, :] + plane
    return dict(b0=b0, n_tok=n_tok, n_rows=n_rows, x1=x1, h2p=h2p, gt=gt, dest_sub=dest_sub,
                tables=(block_expert, n_used, seg_slot, next_expert, block_valid))


def _experts_chunk(r, mod, out_prev, wgu, bgu, wd, bd, out_gain):
    n_tok, n_rows, dest_sub = r["n_tok"], r["n_rows"], r["dest_sub"]
    wp = r["h2p"].shape[2]
    xs = _sc_scatter_rows(r["h2p"].reshape(SC_SPLIT * n_tok, wp), dest_sub.reshape(TOP_K, SC_SPLIT * n_tok),
                          SC_SPLIT * n_rows).reshape(SC_SPLIT, n_rows, wp)
    ys = _moe_call(xs, *r["tables"], wgu, bgu, wd, bd)
    yg = _sc_gather_rows(ys.reshape(SC_SPLIT * n_rows, wp), dest_sub.reshape(1, TOP_K * SC_SPLIT * n_tok))
    return _combine_call(r["x1"], r["gt"], mod, out_gain, yg.reshape(TOP_K, SC_SPLIT, n_tok, wp), out_prev,
                         r["b0"])


def kernel(x, c, w_ada, b_ada, norm1_g, w_in, conv_w, conv_b, conv_ln_g, conv_ln_b, lru_conv_w, lru_conv_b,
           lru_w_a, lru_b_a, lru_w_x, lru_b_x, lru_lambda, mix_norm_g, w_out, norm2_g, w_router, b_router,
           w_gate_up, b_gate_up, w_down, b_down, final_norm_g):
    depth = w_ada.shape[0]
    assert depth == 1, "the final norm is fused into the (single) layer's combine kernel"
    bsz, seq, dm = x.shape
    mod = _ada_call(c, w_ada[0], b_ada[0]).reshape(bsz, N_MOD, dm)
    return _layer(x, mod, 0, w_in, norm1_g, conv_w, conv_b, conv_ln_g, conv_ln_b, lru_conv_w, lru_conv_b,
                  lru_w_a, lru_b_a, lru_w_x, lru_b_x, lru_lambda, mix_norm_g, w_out, norm2_g, w_router,
                  b_router, w_gate_up, b_gate_up, w_down, b_down, row_gain(final_norm_g))


def row_gain(g):
    return g.reshape(1, -1)
```

```python
import functools

import jax
import jax.numpy as jnp
from jax import lax
from jax.experimental import pallas as pl
from jax.experimental.pallas import tpu as pltpu
from jax.experimental.pallas import tpu_sc as plsc

F32 = jnp.float32
BF16 = jnp.bfloat16
I32 = jnp.int32

EPS = 1e-6
N_MOD = 6
LRU_C = 8.0
TOP_K = 4
SWIGLU_ALPHA = 1.702
SWIGLU_LIMIT = 7.0

LANES = 128
SUBLANES = 8
VMEM_LIMIT_BYTES = 56 * 1024 * 1024

SEQ_TILE = 512
COMBINE_TILE = 512
CONV_PAD = 32
CONV_ROWS = 64
MOE_BLOCK = 1024
MOE_SUB = 256
CHUNK_BATCHES = (6, 2)
HI_MASK = -65536

SC_CORES = 2
SC_SUBCORES = 16
SC_WINDOW = 128
SC_SPLIT = 2


def _sigmoid(x):
    return 0.5 * jnp.tanh(0.5 * x) + 0.5


def _pack_bf16_pair(lo_f32, hi_f32):
    lo_bits = lax.bitcast_convert_type(lo_f32.astype(BF16).astype(F32), I32)
    hi_bits = lax.bitcast_convert_type(hi_f32.astype(BF16).astype(F32), I32)
    return lax.shift_right_logical(lo_bits, 16) | hi_bits


def _unpack_bf16_pair(p):
    lo = lax.bitcast_convert_type(lax.shift_left(p, 16), F32)
    hi = lax.bitcast_convert_type(p & HI_MASK, F32)
    return lo, hi


def _store_planes(ref, packed, rows=slice(None)):
    wp = packed.shape[1] // SC_SPLIT
    for s in range(SC_SPLIT):
        ref[s, rows, :] = packed[:, s * wp:(s + 1) * wp]


def _load_planes(ref):
    return jnp.concatenate([ref[s] for s in range(SC_SPLIT)], axis=1)


def _ada_kernel(c_ref, w_ref, b_ref, o_ref):
    c = c_ref[...]
    ca = c * _sigmoid(c)
    w = w_ref[...]
    c_hi = ca.astype(BF16)
    c_lo = (ca - c_hi.astype(F32)).astype(BF16)
    w_hi = w.astype(BF16)
    w_lo = (w - w_hi.astype(F32)).astype(BF16)
    acc = jnp.dot(c_hi, w_hi, preferred_element_type=F32)
    acc += jnp.dot(c_lo, w_hi, preferred_element_type=F32)
    acc += jnp.dot(c_hi, w_lo, preferred_element_type=F32)
    o_ref[...] = acc + b_ref[...]


def _ada_call(c, w_ada, b_ada):
    bsz, dm = c.shape
    n_out = w_ada.shape[1]
    tn = 1024
    return pl.pallas_call(
        _ada_kernel,
        grid=(n_out // tn,),
        in_specs=[
            pl.BlockSpec((bsz, dm), lambda n: (0, 0)),
            pl.BlockSpec((dm, tn), lambda n: (0, n)),
            pl.BlockSpec((1, tn), lambda n: (0, n)),
        ],
        out_specs=pl.BlockSpec((bsz, tn), lambda n: (0, n)),
        out_shape=jax.ShapeDtypeStruct((bsz, n_out), F32),
        compiler_params=pltpu.CompilerParams(dimension_semantics=("arbitrary",)),
        name="ada_mod",
    )(c, w_ada, b_ada.reshape(1, n_out))


def _rms(x, eps=EPS):
    return x * lax.rsqrt(jnp.mean(x * x, axis=-1, keepdims=True) + eps)


def _gelu_tanh(x):
    return 0.5 * x * (1.0 + jnp.tanh(0.7978845608028654 * (x + 0.044715 * (x * x * x))))


def _slab_store(buf, row0, val):
    for s in range(buf.shape[0]):
        buf[s, pl.ds(row0, val.shape[0]), :] = val[:, s * LANES:(s + 1) * LANES]


def _slab_keep_tail(buf, keep, ts):
    for s in range(buf.shape[0]):
        buf[s, pl.ds(0, keep), :] = buf[s, pl.ds(ts, keep), :]


def _causal_tap_sum(buf, w_ref, bias_row, first, n_taps, ts):
    w = w_ref[...]
    cols = []
    for s in range(buf.shape[0]):
        lanes = slice(s * LANES, (s + 1) * LANES)
        chunks = []
        for c in range(0, ts, CONV_ROWS):
            acc = jnp.broadcast_to(bias_row[:, lanes], (CONV_ROWS, LANES))
            for k in range(n_taps):
                acc = acc + w[k:k + 1, lanes] * buf[s, pl.ds(c + first + k, CONV_ROWS), :]
            chunks.append(acc)
        cols.append(jnp.concatenate(chunks, axis=0))
    return jnp.concatenate(cols, axis=1)


def _mix_kernel(x_ref, xn_ref, mod_ref, modn_ref, g1_ref, win_ref, cw_ref, cb_ref, lng_ref, lnb_ref,
                lcw_ref, lcb_ref, wg_ref, bg_ref, lam_ref, mng_ref, wout_ref, g2_ref,
                wrt_ref, br_ref, anchor_ref,
                x1_ref, h2p_ref, idx_ref, rank_ref, gt_ref, cnt_ref,
                u_a, u_b, vbuf, rbuf, a_s, b_s, hcar, cnt_s,
                *, tile, steps_per_seq, d_conv, d_lru, conv_w, lru_cw, n_exp):
    del anchor_ref
    s = pl.program_id(0)

    @pl.when(s % steps_per_seq == 0)
    def _():
        _slab_store(vbuf, 0, jnp.zeros((CONV_PAD, d_conv), F32))
        _slab_store(rbuf, 0, jnp.zeros((SUBLANES, d_lru), F32))
        hcar[...] = jnp.zeros_like(hcar)

    mod = mod_ref[...]
    modn = modn_ref[...]
    shift1, gate1, shift2 = mod[0:1], mod[2:3], mod[3:4]
    gain1 = g1_ref[...] * (1.0 + mod[1:2])
    gain2 = g2_ref[...] * (1.0 + mod[4:5])
    lam = lam_ref[...]
    softplus_neg_lam = jnp.maximum(-lam, 0.0) + jnp.log1p(jnp.exp(-jnp.abs(lam)))

    def project(src_ref, r0, gain, shift, u_ref):
        h = _rms(src_ref[pl.ds(r0, tile), :]) * gain + shift
        u_ref[...] = jnp.dot(h.astype(BF16), win_ref[...], preferred_element_type=F32)

    def conv_group(r0, u_ref):
        return _mix_conv_group(r0, tile, u_ref, cw_ref, cb_ref, lng_ref, lnb_ref, vbuf, d_conv=d_conv,
                               conv_w=conv_w)

    def finish(r0, u_ref, yc):
        _mix_finish(r0, tile, x_ref, u_ref, yc, gate1, shift2, gain2, softplus_neg_lam,
                    lcw_ref, lcb_ref, wg_ref, bg_ref, mng_ref, wout_ref,
                    wrt_ref, br_ref, x1_ref, h2p_ref, idx_ref, rank_ref, gt_ref, rbuf, a_s, b_s, hcar,
                    cnt_s, d_conv=d_conv, d_lru=d_lru, lru_cw=lru_cw, n_exp=n_exp)

    @pl.when(s == 0)
    def _():
        cnt_s[...] = jnp.zeros_like(cnt_s)
        project(x_ref, 0, gain1, shift1, u_a)

    yc0 = conv_group(0, u_a)
    project(x_ref, tile, gain1, shift1, u_b)
    finish(0, u_a, yc0)
    yc1 = conv_group(tile, u_b)
    project(xn_ref, 0, g1_ref[...] * (1.0 + modn[1:2]), modn[0:1], u_a)
    finish(tile, u_b, yc1)
    _slab_keep_tail(vbuf, CONV_PAD, 2 * tile)
    _slab_keep_tail(rbuf, SUBLANES, 2 * tile)
    cnt_ref[...] = cnt_s[:, :LANES]


def _mix_conv_group(r0, ts, u_ref, cw_ref, cb_ref, lng_ref, lnb_ref, vbuf, *, d_conv, conv_w):
    v = u_ref[:, :d_conv] * _sigmoid(u_ref[:, d_conv:2 * d_conv])
    _slab_store(vbuf, CONV_PAD + r0, v)
    acc = _causal_tap_sum(vbuf, cw_ref, cb_ref[...], r0 + CONV_PAD - (conv_w - 1), conv_w, ts)
    mu = jnp.mean(acc, axis=-1, keepdims=True)
    cen = acc - mu
    var = jnp.mean(cen * cen, axis=-1, keepdims=True)
    yc = cen * lax.rsqrt(var + EPS) * lng_ref[...] + lnb_ref[...]
    return yc * _sigmoid(yc)


def _mix_finish(r0, ts, x_ref, u_ref, yc, gate1, shift2, gain2, softplus_neg_lam,
                lcw_ref, lcb_ref, wg_ref, bg_ref, mng_ref, wout_ref,
                wrt_ref, br_ref, x1_ref, h2p_ref, idx_ref, rank_ref, gt_ref, rbuf, a_s, b_s, hcar,
                cnt_s, *, d_conv, d_lru, lru_cw, n_exp):
    rows = pl.ds(r0, ts)
    x = x_ref[rows, :]
    u = u_ref

    u_gate = u[:, 2 * d_conv:2 * d_conv + d_lru]
    _slab_store(rbuf, SUBLANES + r0, u[:, 2 * d_conv + d_lru:])
    xr = _causal_tap_sum(rbuf, lcw_ref, lcb_ref[...], r0 + SUBLANES - (lru_cw - 1), lru_cw, ts)
    gates = jnp.dot(xr.astype(BF16), wg_ref[...], preferred_element_type=F32) + bg_ref[...]
    r = _sigmoid(gates[:, :d_lru])
    i_g = _sigmoid(gates[:, d_lru:])
    log_a = (-LRU_C) * r * softplus_neg_lam
    a = jnp.exp(log_a)
    inp = jnp.sqrt(1.0 - jnp.exp(2.0 * log_a)) * (i_g * xr)

    n_grp = ts // SUBLANES
    a3 = a.reshape(n_grp, SUBLANES, d_lru)
    b3 = inp.reshape(n_grp, SUBLANES, d_lru)
    sub = lax.broadcasted_iota(I32, (n_grp, SUBLANES, d_lru), 1)
    for s in (1, 2, 4):
        a_sh = pltpu.roll(a3, s, axis=1)
        b_sh = pltpu.roll(b3, s, axis=1)
        m = sub >= s
        b3 = jnp.where(m, a3 * b_sh + b3, b3)
        a3 = jnp.where(m, a3 * a_sh, a3)
    a_s[...] = a3.reshape(ts, d_lru)
    b_s[...] = b3.reshape(ts, d_lru)

    def grp_body(g, carry):
        rows = pl.ds(pl.multiple_of(g * SUBLANES, SUBLANES), SUBLANES)
        hg = b_s[rows, :] + a_s[rows, :] * carry
        b_s[rows, :] = hg
        return jnp.broadcast_to(hg[SUBLANES - 1:SUBLANES, :], (SUBLANES, d_lru))

    hcar[...] = lax.fori_loop(0, n_grp, grp_body, hcar[...], unroll=True)
    yl = _gelu_tanh(u_gate) * b_s[...]

    mng = mng_ref[...]
    yc_n = _rms(yc) * mng[:, :d_conv]
    yl_n = _rms(yl) * mng[:, d_conv:]
    mixo = jnp.dot(yc_n.astype(BF16), wout_ref[pl.ds(0, d_conv), :], preferred_element_type=F32)
    mixo += jnp.dot(yl_n.astype(BF16), wout_ref[pl.ds(d_conv, d_lru), :], preferred_element_type=F32)
    x1 = x + gate1 * mixo
    x1_ref[rows, :] = x1

    h2 = _rms(x1) * gain2 + shift2
    dh = h2.shape[1] // 2
    h2_hi = h2.astype(BF16)
    _store_planes(h2p_ref, _pack_bf16_pair(h2[:, :dh], h2[:, dh:]), rows)
    h2_lo = (h2 - h2_hi.astype(F32)).astype(BF16)
    nt_dims = (((1,), (1,)), ((), ()))
    wrt = wrt_ref[...]
    lg = lax.dot_general(wrt, h2_hi, nt_dims, preferred_element_type=F32)
    lg2 = lax.dot_general(wrt[:n_exp], h2_lo, nt_dims, preferred_element_type=F32)
    logits = lg[:n_exp] + lg[n_exp:] + lg2 + br_ref[...]

    eidx = lax.broadcasted_iota(I32, (n_exp, ts), 0)
    neg_inf = jnp.float32(-jnp.inf)
    work = logits
    vals, idxs, hots = [], [], []
    for _ in range(TOP_K):
        mval = jnp.max(work, axis=0, keepdims=True)
        midx = jnp.min(jnp.where(work == mval, eidx, n_exp), axis=0, keepdims=True)
        hot = eidx == midx
        vals.append(mval)
        idxs.append(midx)
        hots.append(hot)
        work = jnp.where(hot, neg_inf, work)
    exps = [jnp.exp(vk - vals[0]) for vk in vals]
    denom = exps[0] + exps[1] + exps[2] + exps[3]
    gate_rows = [ek / denom for ek in exps]

    sel = jnp.zeros((n_exp, ts), F32)
    for hot in hots:
        sel = sel + hot.astype(F32)
    tri = (lax.broadcasted_iota(I32, (ts, ts), 0) < lax.broadcasted_iota(I32, (ts, ts), 1)).astype(BF16)
    before = jnp.dot(sel.astype(BF16), tri, preferred_element_type=F32) + cnt_s[...]
    rank_rows = [jnp.sum(jnp.where(hot, before, 0.0), axis=0, keepdims=True) for hot in hots]
    cnt_s[...] = cnt_s[...] + jnp.sum(sel, axis=1, keepdims=True)

    idx_ref[:, rows] = jnp.concatenate(idxs, axis=0)
    rank_ref[:, rows] = jnp.concatenate(rank_rows, axis=0).astype(I32)
    g4 = jnp.concatenate(gate_rows, axis=0)
    g_pad = jnp.concatenate([g4, jnp.zeros((LANES - TOP_K, ts), F32)], axis=0)
    gt_ref[rows, :] = g_pad.T


def _mix_call(x, mod, p, b0, bsz, anchor):
    _, seq, dm = x.shape
    tile = min(SEQ_TILE, seq // 2)
    ts = 2 * tile
    nj = seq // ts
    n_steps = bsz * nj
    tiles_per_seq = seq // tile
    d_conv = p["conv_w"].shape[1]
    d_lru = p["lru_conv_w"].shape[1]
    conv_w = p["conv_w"].shape[0]
    lru_cw = p["lru_conv_w"].shape[0]
    n_exp = p["wrt"].shape[0] // 2
    assert seq % ts == 0 and tile % LANES == 0 and tile % CONV_ROWS == 0
    assert conv_w - 1 <= CONV_PAD and lru_cw - 1 <= SUBLANES and d_conv % LANES == 0 and d_lru % LANES == 0

    def full(a):
        return pl.BlockSpec(a.shape, lambda s: (0,) * a.ndim)

    def next_tile(s):
        return jnp.minimum(2 * s + 2, bsz * tiles_per_seq - 1)

    weights = [p["g1"], p["win"], p["conv_w"], p["conv_b"], p["ln_g"], p["ln_b"], p["lru_conv_w"],
               p["lru_conv_b"], p["wg"], p["bg"], p["lam"], p["mng"], p["wout"], p["g2"], p["wrt"], p["br"]]
    kern = functools.partial(_mix_kernel, tile=tile, steps_per_seq=nj, d_conv=d_conv, d_lru=d_lru,
                             conv_w=conv_w, lru_cw=lru_cw, n_exp=n_exp)
    out_shape = (
        jax.ShapeDtypeStruct((bsz, seq, dm), F32),
        jax.ShapeDtypeStruct((SC_SPLIT, bsz * seq, dm // 2 // SC_SPLIT), I32),
        jax.ShapeDtypeStruct((TOP_K, bsz * seq), I32),
        jax.ShapeDtypeStruct((TOP_K, bsz * seq), I32),
        jax.ShapeDtypeStruct((bsz, seq, LANES), F32),
        jax.ShapeDtypeStruct((n_exp, LANES), F32),
    )
    return pl.pallas_call(
        kern,
        grid=(n_steps,),
        in_specs=[pl.BlockSpec((None, ts, dm), lambda s: (s // nj + b0, s % nj, 0)),
                  pl.BlockSpec((None, tile, dm),
                               lambda s: (next_tile(s) // tiles_per_seq + b0, next_tile(s) % tiles_per_seq, 0)),
                  pl.BlockSpec((None, N_MOD, dm), lambda s: (s // nj + b0, 0, 0)),
                  pl.BlockSpec((None, N_MOD, dm), lambda s: (next_tile(s) // tiles_per_seq + b0, 0, 0))]
        + [full(w) for w in weights] + [pl.BlockSpec(memory_space=pl.ANY)],
        out_specs=(
            pl.BlockSpec((None, ts, dm), lambda s: (s // nj, s % nj, 0)),
            pl.BlockSpec((SC_SPLIT, ts, dm // 2 // SC_SPLIT), lambda s: (0, s, 0)),
            pl.BlockSpec((TOP_K, ts), lambda s: (0, s)),
            pl.BlockSpec((TOP_K, ts), lambda s: (0, s)),
            pl.BlockSpec((None, ts, LANES), lambda s: (s // nj, s % nj, 0)),
            pl.BlockSpec((n_exp, LANES), lambda s: (0, 0)),
        ),
        out_shape=out_shape,
        scratch_shapes=[
            pltpu.VMEM((tile, 2 * d_conv + 2 * d_lru), F32),
            pltpu.VMEM((tile, 2 * d_conv + 2 * d_lru), F32),
            pltpu.VMEM((d_conv // LANES, CONV_PAD + ts, LANES), F32),
            pltpu.VMEM((d_lru // LANES, SUBLANES + ts, LANES), F32),
            pltpu.VMEM((tile, d_lru), F32),
            pltpu.VMEM((tile, d_lru), F32),
            pltpu.VMEM((SUBLANES, d_lru), F32),
            pltpu.VMEM((n_exp, tile), F32),
        ],
        compiler_params=pltpu.CompilerParams(dimension_semantics=("arbitrary",),
                                             vmem_limit_bytes=VMEM_LIMIT_BYTES),
        name="token_mix_route",
    )(x, x, mod, mod, *weights, anchor)


def _sc_mesh():
    return plsc.VectorSubcoreMesh(core_axis_name="core", subcore_axis_name="subcore",
                                  num_cores=SC_CORES, num_subcores=SC_SUBCORES)


def _sc_scatter_rows(rows, idx, n_out):
    n, w = rows.shape
    n_k = idx.shape[0]
    assert n % (SC_WINDOW * SC_CORES * SC_SUBCORES) == 0

    @functools.partial(pl.kernel, out_type=jax.ShapeDtypeStruct((n_out, w), rows.dtype), mesh=_sc_mesh(),
                       scratch_types=[], name="moe_dispatch_sc")
    def scatter(x_hbm, i_hbm, o_hbm):
        def body(x_vmem, *i_vmems):
            for i_vmem in i_vmems:
                pltpu.sync_copy(x_vmem, o_hbm.at[i_vmem.at[0]])

        pltpu.emit_pipeline(
            body,
            grid=(n // SC_WINDOW,),
            in_specs=[pl.BlockSpec((SC_WINDOW, w), lambda i: (i, 0))]
            + [pl.BlockSpec((1, SC_WINDOW), functools.partial(lambda k, i: (k, i), k)) for k in range(n_k)],
            out_specs=[],
            core_axis_name=("core", "subcore"),
            dimension_semantics=(pltpu.PARALLEL,),
        )(x_hbm, *([i_hbm] * n_k))

    return scatter(rows, idx)


def _sc_gather_rows(table, idx):
    m = idx.shape[1]
    w = table.shape[1]
    assert m % (SC_WINDOW * SC_CORES * SC_SUBCORES) == 0

    @functools.partial(pl.kernel, out_type=jax.ShapeDtypeStruct((m, w), table.dtype), mesh=_sc_mesh(),
                       scratch_types=[], name="moe_gather_sc")
    def gather(x_hbm, i_hbm, o_hbm):
        def body(i_vmem, o_vmem):
            pltpu.sync_copy(x_hbm.at[i_vmem.at[0]], o_vmem)

        pltpu.emit_pipeline(
            body,
            grid=(m // SC_WINDOW,),
            in_specs=[pl.BlockSpec((1, SC_WINDOW), lambda i: (0, i))],
            out_specs=[pl.BlockSpec((SC_WINDOW, w), lambda i: (i, 0))],
            core_axis_name=("core", "subcore"),
            dimension_semantics=(pltpu.PARALLEL,),
        )(i_hbm, o_hbm)

    return gather(table, idx)


def _moe_kernel(be_ref, nb_ref, slot_ref, nxt_ref, nv_ref, xs_ref, wgu_hbm, bgu_ref, wd_hbm, bd_ref, ys_ref,
                wgu_f, wd_f, wgu_b, wd_b, sems, *, d_ff):
    i = pl.program_id(0)
    live = i < nb_ref[0]
    new_expert = (i == 0) | (be_ref[i] != be_ref[jnp.maximum(i - 1, 0)])

    def weight_copies(e, s):
        return (pltpu.make_async_copy(wgu_hbm.at[e], wgu_f.at[s], sems.at[0, s]),
                pltpu.make_async_copy(wd_hbm.at[e], wd_f.at[s], sems.at[1, s]))

    @pl.when(live & new_expert)
    def _():
        e, s, nx = be_ref[i], slot_ref[i], nxt_ref[i]

        @pl.when(i == 0)
        def _():
            for cp in weight_copies(e, s):
                cp.start()

        for cp in weight_copies(e, s):
            cp.wait()

        @pl.when(nx >= 0)
        def _():
            for cp in weight_copies(nx, 1 - s):
                cp.start()

        wgu_b[...] = wgu_f[s].astype(BF16)
        wd_b[...] = wd_f[s].astype(BF16)

    def sub_block(s):
        rows = pl.ds(s * MOE_SUB, MOE_SUB)
        packed = jnp.concatenate([xs_ref[p, rows, :] for p in range(SC_SPLIT)], axis=1)
        x_lo, x_hi = _unpack_bf16_pair(packed)
        dh = x_lo.shape[1]
        gu = jnp.dot(x_lo.astype(BF16), wgu_b[pl.ds(0, dh), :], preferred_element_type=F32)
        gu += jnp.dot(x_hi.astype(BF16), wgu_b[pl.ds(dh, dh), :], preferred_element_type=F32)
        gu += bgu_ref[...]
        g = jnp.minimum(gu[:, :d_ff], SWIGLU_LIMIT)
        u = jnp.clip(gu[:, d_ff:], -SWIGLU_LIMIT, SWIGLU_LIMIT)
        act = (u + 1.0) * (g * _sigmoid(SWIGLU_ALPHA * g))
        y = jnp.dot(act.astype(BF16), wd_b[...], preferred_element_type=F32) + bd_ref[...]
        dm2 = y.shape[1] // 2
        _store_planes(ys_ref, _pack_bf16_pair(y[:, :dm2], y[:, dm2:]), rows)

    n_sub = xs_ref.shape[1] // MOE_SUB
    n_valid = nv_ref[i]

    @pl.when(live & (n_valid == n_sub * MOE_SUB))
    def _():
        for s in range(n_sub):
            sub_block(s)

    @pl.when(live & (n_valid < n_sub * MOE_SUB))
    def _():
        for s in range(n_sub):
            pl.when(s * MOE_SUB < n_valid)(functools.partial(sub_block, s))


def _moe_call(xs, block_expert, n_used, seg_slot, next_expert, block_valid, wgu, bgu, wd, bd):
    _, n_rows, wp = xs.shape
    n_exp, dm, d_ff2 = wgu.shape
    d_ff = d_ff2 // 2
    bm = MOE_BLOCK
    n_blocks = n_rows // bm

    def row_map(i, be, nb, slot, nxt, nv):
        return (0, jnp.minimum(i, nb[0] - 1), 0)

    def b_map(i, be, nb, slot, nxt, nv):
        return (be[jnp.minimum(i, nb[0] - 1)], 0, 0)

    grid_spec = pltpu.PrefetchScalarGridSpec(
        num_scalar_prefetch=5,
        grid=(n_blocks,),
        in_specs=[
            pl.BlockSpec((SC_SPLIT, bm, wp), row_map),
            pl.BlockSpec(memory_space=pl.ANY),
            pl.BlockSpec((None, 1, d_ff2), b_map),
            pl.BlockSpec(memory_space=pl.ANY),
            pl.BlockSpec((None, 1, dm), b_map),
        ],
        out_specs=pl.BlockSpec((SC_SPLIT, bm, wp), row_map),
        scratch_shapes=[
            pltpu.VMEM((2, dm, d_ff2), F32), pltpu.VMEM((2, d_ff, dm), F32),
            pltpu.VMEM((dm, d_ff2), BF16), pltpu.VMEM((d_ff, dm), BF16),
            pltpu.SemaphoreType.DMA((2, 2)),
        ],
    )
    return pl.pallas_call(
        functools.partial(_moe_kernel, d_ff=d_ff),
        grid_spec=grid_spec,
        out_shape=jax.ShapeDtypeStruct(xs.shape, I32),
        compiler_params=pltpu.CompilerParams(dimension_semantics=("arbitrary",),
                                             vmem_limit_bytes=VMEM_LIMIT_BYTES),
        name="moe_experts",
    )(block_expert, n_used, seg_slot, next_expert, block_valid, xs, wgu, bgu, wd, bd)


def _combine_kernel(x1_ref, gt_ref, mod_ref, gf_ref, yg_ref, *out_refs):
    o_ref = out_refs[-1]
    gt = gt_ref[...]
    ts = yg_ref.shape[2]
    dh = yg_ref.shape[3] * SC_SPLIT
    acc_lo = jnp.zeros((ts, dh), F32)
    acc_hi = jnp.zeros((ts, dh), F32)
    for k in range(TOP_K):
        lo, hi = _unpack_bf16_pair(_load_planes(yg_ref.at[k]))
        gk = gt[:, k:k + 1]
        acc_lo += gk * lo
        acc_hi += gk * hi
    y = jnp.concatenate([acc_lo, acc_hi], axis=1)
    gate2 = mod_ref[...][5:6]
    x2 = x1_ref[...] + gate2 * y
    o_ref[...] = _rms(x2) * gf_ref[...]


def _combine_call(x1, gt, mod, gf, yg, out_prev, b0):
    bsz, seq, dm = x1.shape
    ts = min(COMBINE_TILE, seq)
    assert seq % ts == 0
    nj = seq // ts
    in_specs = [
        pl.BlockSpec((None, ts, dm), lambda b, j: (b, j, 0)),
        pl.BlockSpec((None, ts, LANES), lambda b, j: (b, j, 0)),
        pl.BlockSpec((None, N_MOD, dm), lambda b, j: (b + b0, 0, 0)),
        pl.BlockSpec((1, dm), lambda b, j: (0, 0)),
        pl.BlockSpec((TOP_K, SC_SPLIT, ts, yg.shape[3]), lambda b, j: (0, 0, b * nj + j, 0)),
    ]
    args = [x1, gt, mod, gf, yg]
    aliases = {}
    if out_prev is not None:
        in_specs.append(pl.BlockSpec(memory_space=pl.ANY))
        args.append(out_prev)
        aliases = {len(args) - 1: 0}
    return pl.pallas_call(
        _combine_kernel,
        grid=(bsz, nj),
        in_specs=in_specs,
        out_specs=pl.BlockSpec((None, ts, dm), lambda b, j: (b + b0, j, 0)),
        out_shape=jax.ShapeDtypeStruct((mod.shape[0], seq, dm), F32),
        input_output_aliases=aliases,
        compiler_params=pltpu.CompilerParams(dimension_semantics=("arbitrary", "arbitrary"),
                                             vmem_limit_bytes=VMEM_LIMIT_BYTES),
        name="moe_combine",
    )(*args)


def _block_diag(w):
    n_h, d, _ = w.shape
    eye = jnp.eye(n_h, dtype=w.dtype)
    return (eye[:, None, :, None] * w[:, :, None, :]).reshape(n_h * d, n_h * d)


def _layer(x, mod, l, w_in, norm1_g, conv_w, conv_b, conv_ln_g, conv_ln_b, lru_conv_w, lru_conv_b,
           lru_w_a, lru_b_a, lru_w_x, lru_b_x, lru_lambda, mix_norm_g, w_out, norm2_g, w_router,
           b_router, w_gate_up, b_gate_up, w_down, b_down, out_gain):
    bsz, seq, dm = x.shape
    n_tok = bsz * seq
    n_exp = w_router.shape[-1]
    row = lambda a: a.reshape(1, -1)
    wr = w_router[l]
    wr_hi = wr.astype(BF16)
    wr_lo = (wr - wr_hi.astype(F32)).astype(BF16)
    params = dict(
        g1=row(norm1_g[l]), win=w_in[l].astype(BF16), conv_w=conv_w[l], conv_b=row(conv_b[l]),
        ln_g=row(conv_ln_g[l]), ln_b=row(conv_ln_b[l]), lru_conv_w=lru_conv_w[l], lru_conv_b=row(lru_conv_b[l]),
        wg=jnp.concatenate([_block_diag(lru_w_a[l]), _block_diag(lru_w_x[l])], axis=1).astype(BF16),
        bg=jnp.concatenate([lru_b_a[l].reshape(1, -1), lru_b_x[l].reshape(1, -1)], axis=1),
        lam=row(lru_lambda[l]), mng=row(mix_norm_g[l]), wout=w_out[l].astype(BF16), g2=row(norm2_g[l]),
        wrt=jnp.concatenate([wr_hi.T, wr_lo.T], axis=0), br=b_router[l].reshape(n_exp, 1),
    )
    sizes = [s for s in CHUNK_BATCHES if s > 0] if sum(CHUNK_BATCHES) == bsz else [bsz]
    routed, b0, anchor = [], 0, mod
    for cb in sizes:
        r = _route_chunk(x, mod, params, b0, cb, anchor, n_exp)
        routed.append(r)
        anchor = r["dest_sub"]
        b0 += cb
    out = None
    for r in routed:
        out = _experts_chunk(r, mod, out, w_gate_up[l], b_gate_up[l][:, None, :], w_down[l],
                             b_down[l][:, None, :], out_gain)
    return out


def _route_chunk(x, mod, params, b0, cb, anchor, n_exp):
    seq, dm = x.shape[1:]
    n_tok = cb * seq
    x1, h2p, top_idx, rank, gt, cnt = _mix_call(x, mod, params, b0, cb, anchor)

    bm = MOE_BLOCK
    counts = cnt[:, 0].astype(I32)
    padded = (counts + bm - 1) // bm * bm
    e_ids = jnp.arange(n_exp, dtype=I32)
    pad_ends = jnp.sum(jnp.where(e_ids[None, :] <= e_ids[:, None], padded[None, :], 0), axis=1)
    pad_starts = pad_ends - padded
    n_rows = (n_tok * TOP_K // bm + n_exp) * bm
    n_blocks = n_rows // bm
    dest = rank + jnp.sum(jnp.where(top_idx[..., None] == e_ids, pad_starts, 0), axis=-1)
    block_start = jnp.arange(n_blocks, dtype=I32) * bm
    block_expert = jnp.minimum(
        jnp.sum((block_start[:, None] >= pad_ends[None, :]).astype(I32), axis=1), n_exp - 1)
    n_used = pad_ends[-1:] // bm
    present = counts > 0
    seg_ordinal = jnp.sum(jnp.where((e_ids[None, :] < e_ids[:, None]) & present[None, :], 1, 0), axis=1)
    following = jnp.min(jnp.where((e_ids[None, :] > e_ids[:, None]) & present[None, :], e_ids[None, :], n_exp),
                        axis=1)
    following = jnp.where(following == n_exp, -1, following)
    block_hot = block_expert[:, None] == e_ids[None, :]
    seg_slot = jnp.sum(jnp.where(block_hot, seg_ordinal[None, :] % 2, 0), axis=1)
    next_expert = jnp.sum(jnp.where(block_hot, following[None, :], 0), axis=1)
    block_valid = jnp.clip(
        jnp.sum(jnp.where(block_hot, (pad_starts + counts)[None, :], 0), axis=1) - block_start, 0, bm)

    wp = h2p.shape[2]
    plane = jnp.arange(SC_SPLIT, dtype=I32)[None, :, None] * n_rows
    dest_sub = dest[:, None, :] + plane
    return dict(b0=b0, n_tok=n_tok, n_rows=n_rows, x1=x1, h2p=h2p, gt=gt, dest_sub=dest_sub,
                tables=(block_expert, n_used, seg_slot, next_expert, block_valid))


def _experts_chunk(r, mod, out_prev, wgu, bgu, wd, bd, out_gain):
    n_tok, n_rows, dest_sub = r["n_tok"], r["n_rows"], r["dest_sub"]
    wp = r["h2p"].shape[2]
    xs = _sc_scatter_rows(r["h2p"].reshape(SC_SPLIT * n_tok, wp), dest_sub.reshape(TOP_K, SC_SPLIT * n_tok),
                          SC_SPLIT * n_rows).reshape(SC_SPLIT, n_rows, wp)
    ys = _moe_call(xs, *r["tables"], wgu, bgu, wd, bd)
    yg = _sc_gather_rows(ys.reshape(SC_SPLIT * n_rows, wp), dest_sub.reshape(1, TOP_K * SC_SPLIT * n_tok))
    return _combine_call(r["x1"], r["gt"], mod, out_gain, yg.reshape(TOP_K, SC_SPLIT, n_tok, wp), out_prev,
                         r["b0"])


def kernel(x, c, w_ada, b_ada, norm1_g, w_in, conv_w, conv_b, conv_ln_g, conv_ln_b, lru_conv_w, lru_conv_b,
           lru_w_a, lru_b_a, lru_w_x, lru_b_x, lru_lambda, mix_norm_g, w_out, norm2_g, w_router, b_router,
           w_gate_up, b_gate_up, w_down, b_down, final_norm_g):
    depth = w_ada.shape[0]
    assert depth == 1, "the final norm is fused into the (single) layer's combine kernel"
    bsz, seq, dm = x.shape
    mod = _ada_call(c, w_ada[0], b_ada[0]).reshape(bsz, N_MOD, dm)
    return _layer(x, mod, 0, w_in, norm1_g, conv_w, conv_b, conv_ln_g, conv_ln_b, lru_conv_w, lru_conv_b,
                  lru_w_a, lru_b_a, lru_w_x, lru_b_x, lru_lambda, mix_norm_g, w_out, norm2_g, w_router,
                  b_router, w_gate_up, b_gate_up, w_down, b_down, row_gain(final_norm_g))


def row_gain(g):
    return g.reshape(1, -1)
```

```python
import functools

import jax
import jax.numpy as jnp
from jax import lax
from jax.experimental import pallas as pl
from jax.experimental.pallas import tpu as pltpu
from jax.experimental.pallas import tpu_sc as plsc

F32 = jnp.float32
BF16 = jnp.bfloat16
I32 = jnp.int32

EPS = 1e-6
N_MOD = 6
LRU_C = 8.0
TOP_K = 4
SWIGLU_ALPHA = 1.702
SWIGLU_LIMIT = 7.0

LANES = 128
SUBLANES = 8
VMEM_LIMIT_BYTES = 56 * 1024 * 1024

SEQ_TILE = 512
TILES_PER_STEP = 2
COMBINE_TILE = 512
CONV_PAD = 32
CONV_ROWS = 128
MOE_BLOCK = 1024
MOE_SUB = 256
CHUNK_BATCHES = (5, 3)
HI_MASK = -65536

SC_CORES = 2
SC_SUBCORES = 16
SC_WINDOW = 128
SC_SPLIT = 2


def _sigmoid(x):
    return 0.5 * jnp.tanh(0.5 * x) + 0.5


def _pack_bf16_pair(lo_f32, hi_f32):
    lo_bits = lax.bitcast_convert_type(lo_f32.astype(BF16).astype(F32), I32)
    hi_bits = lax.bitcast_convert_type(hi_f32.astype(BF16).astype(F32), I32)
    return lax.shift_right_logical(lo_bits, 16) | hi_bits


def _unpack_bf16_pair(p):
    lo = lax.bitcast_convert_type(lax.shift_left(p, 16), F32)
    hi = lax.bitcast_convert_type(p & HI_MASK, F32)
    return lo, hi


def _store_planes(ref, packed, rows=slice(None)):
    wp = packed.shape[1] // SC_SPLIT
    for s in range(SC_SPLIT):
        ref[s, rows, :] = packed[:, s * wp:(s + 1) * wp]


def _load_planes(ref):
    return jnp.concatenate([ref[s] for s in range(SC_SPLIT)], axis=1)


def _ada_kernel(c_ref, w_ref, b_ref, o_ref):
    c = c_ref[...]
    ca = c * _sigmoid(c)
    w = w_ref[...]
    c_hi = ca.astype(BF16)
    c_lo = (ca - c_hi.astype(F32)).astype(BF16)
    w_hi = w.astype(BF16)
    w_lo = (w - w_hi.astype(F32)).astype(BF16)
    acc = jnp.dot(c_hi, w_hi, preferred_element_type=F32)
    acc += jnp.dot(c_lo, w_hi, preferred_element_type=F32)
    acc += jnp.dot(c_hi, w_lo, preferred_element_type=F32)
    o_ref[...] = acc + b_ref[...]


def _ada_call(c, w_ada, b_ada):
    bsz, dm = c.shape
    n_out = w_ada.shape[1]
    tn = 1024
    return pl.pallas_call(
        _ada_kernel,
        grid=(n_out // tn,),
        in_specs=[
            pl.BlockSpec((bsz, dm), lambda n: (0, 0)),
            pl.BlockSpec((dm, tn), lambda n: (0, n)),
            pl.BlockSpec((1, tn), lambda n: (0, n)),
        ],
        out_specs=pl.BlockSpec((bsz, tn), lambda n: (0, n)),
        out_shape=jax.ShapeDtypeStruct((bsz, n_out), F32),
        compiler_params=pltpu.CompilerParams(dimension_semantics=("arbitrary",)),
        name="ada_mod",
    )(c, w_ada, b_ada.reshape(1, n_out))


def _rms(x, eps=EPS):
    return x * lax.rsqrt(jnp.mean(x * x, axis=-1, keepdims=True) + eps)


def _gelu_tanh(x):
    return 0.5 * x * (1.0 + jnp.tanh(0.7978845608028654 * (x + 0.044715 * (x * x * x))))


def _slab_store(buf, row0, val):
    for s in range(buf.shape[0]):
        buf[s, pl.ds(row0, val.shape[0]), :] = val[:, s * LANES:(s + 1) * LANES]


def _slab_keep_tail(buf, keep, ts):
    for s in range(buf.shape[0]):
        buf[s, pl.ds(0, keep), :] = buf[s, pl.ds(ts, keep), :]


def _causal_tap_sum(buf, w_ref, bias_row, first, n_taps, ts):
    w = w_ref[...]
    cols = []
    for s in range(buf.shape[0]):
        lanes = slice(s * LANES, (s + 1) * LANES)
        chunks = []
        for c in range(0, ts, CONV_ROWS):
            acc = jnp.broadcast_to(bias_row[:, lanes], (CONV_ROWS, LANES))
            for k in range(n_taps):
                acc = acc + w[k:k + 1, lanes] * buf[s, pl.ds(c + first + k, CONV_ROWS), :]
            chunks.append(acc)
        cols.append(jnp.concatenate(chunks, axis=0))
    return jnp.concatenate(cols, axis=1)


def _mix_kernel(x_ref, xn_ref, mod_ref, modn_ref, g1_ref, win_ref, cw_ref, cb_ref, lng_ref, lnb_ref,
                lcw_ref, lcb_ref, wg_ref, bg_ref, lam_ref, mng_ref, wout_ref, g2_ref,
                wrt_ref, br_ref, anchor_ref,
                x1_ref, h2p_ref, idx_ref, rank_ref, gt_ref, cnt_ref,
                u_a, u_b, vbuf, rbuf, a_s, b_s, hcar, cnt_s,
                *, tile, n_tiles, steps_per_seq, d_conv, d_lru, conv_w, lru_cw, n_exp):
    del anchor_ref
    s = pl.program_id(0)

    @pl.when(s % steps_per_seq == 0)
    def _():
        _slab_store(vbuf, 0, jnp.zeros((CONV_PAD, d_conv), F32))
        _slab_store(rbuf, 0, jnp.zeros((SUBLANES, d_lru), F32))
        hcar[...] = jnp.zeros_like(hcar)

    mod = mod_ref[...]
    modn = modn_ref[...]
    shift1, gate1, shift2 = mod[0:1], mod[2:3], mod[3:4]
    gain1 = g1_ref[...] * (1.0 + mod[1:2])
    gain2 = g2_ref[...] * (1.0 + mod[4:5])
    lam = lam_ref[...]
    softplus_neg_lam = jnp.maximum(-lam, 0.0) + jnp.log1p(jnp.exp(-jnp.abs(lam)))

    def project(src_ref, r0, gain, shift, u_ref):
        h = _rms(src_ref[pl.ds(r0, tile), :]) * gain + shift
        u_ref[...] = jnp.dot(h.astype(BF16), win_ref[...], preferred_element_type=F32)

    def conv_group(r0, u_ref):
        return _mix_conv_group(r0, tile, u_ref, cw_ref, cb_ref, lng_ref, lnb_ref, vbuf, d_conv=d_conv,
                               conv_w=conv_w)

    def finish(r0, u_ref, yc):
        _mix_finish(r0, tile, x_ref, u_ref, yc, gate1, shift2, gain2, softplus_neg_lam,
                    lcw_ref, lcb_ref, wg_ref, bg_ref, mng_ref, wout_ref,
                    wrt_ref, br_ref, x1_ref, h2p_ref, idx_ref, rank_ref, gt_ref, rbuf, a_s, b_s, hcar,
                    cnt_s, d_conv=d_conv, d_lru=d_lru, lru_cw=lru_cw, n_exp=n_exp)

    @pl.when(s == 0)
    def _():
        cnt_s[...] = jnp.zeros_like(cnt_s)
        project(x_ref, 0, gain1, shift1, u_a)

    bufs = (u_a, u_b)
    for q in range(n_tiles):
        cur, nxt = bufs[q % 2], bufs[(q + 1) % 2]
        yc = conv_group(q * tile, cur)
        if q + 1 < n_tiles:
            project(x_ref, (q + 1) * tile, gain1, shift1, nxt)
        else:
            project(xn_ref, 0, g1_ref[...] * (1.0 + modn[1:2]), modn[0:1], nxt)
        finish(q * tile, cur, yc)
    _slab_keep_tail(vbuf, CONV_PAD, n_tiles * tile)
    _slab_keep_tail(rbuf, SUBLANES, n_tiles * tile)
    cnt_ref[...] = cnt_s[:, :LANES]


def _mix_conv_group(r0, ts, u_ref, cw_ref, cb_ref, lng_ref, lnb_ref, vbuf, *, d_conv, conv_w):
    v = u_ref[:, :d_conv] * _sigmoid(u_ref[:, d_conv:2 * d_conv])
    _slab_store(vbuf, CONV_PAD + r0, v)
    acc = _causal_tap_sum(vbuf, cw_ref, cb_ref[...], r0 + CONV_PAD - (conv_w - 1), conv_w, ts)
    mu = jnp.mean(acc, axis=-1, keepdims=True)
    cen = acc - mu
    var = jnp.mean(cen * cen, axis=-1, keepdims=True)
    yc = cen * lax.rsqrt(var + EPS) * lng_ref[...] + lnb_ref[...]
    return yc * _sigmoid(yc)


def _mix_finish(r0, ts, x_ref, u_ref, yc, gate1, shift2, gain2, softplus_neg_lam,
                lcw_ref, lcb_ref, wg_ref, bg_ref, mng_ref, wout_ref,
                wrt_ref, br_ref, x1_ref, h2p_ref, idx_ref, rank_ref, gt_ref, rbuf, a_s, b_s, hcar,
                cnt_s, *, d_conv, d_lru, lru_cw, n_exp):
    rows = pl.ds(r0, ts)
    x = x_ref[rows, :]
    u = u_ref

    u_gate = u[:, 2 * d_conv:2 * d_conv + d_lru]
    _slab_store(rbuf, SUBLANES + r0, u[:, 2 * d_conv + d_lru:])
    xr = _causal_tap_sum(rbuf, lcw_ref, lcb_ref[...], r0 + SUBLANES - (lru_cw - 1), lru_cw, ts)
    gates = jnp.dot(xr.astype(BF16), wg_ref[...], preferred_element_type=F32) + bg_ref[...]
    r = _sigmoid(gates[:, :d_lru])
    i_g = _sigmoid(gates[:, d_lru:])
    log_a = (-LRU_C) * r * softplus_neg_lam
    a = jnp.exp(log_a)
    inp = jnp.sqrt(1.0 - jnp.exp(2.0 * log_a)) * (i_g * xr)

    n_grp = ts // SUBLANES
    a3 = a.reshape(n_grp, SUBLANES, d_lru)
    b3 = inp.reshape(n_grp, SUBLANES, d_lru)
    sub = lax.broadcasted_iota(I32, (n_grp, SUBLANES, d_lru), 1)
    for s in (1, 2, 4):
        a_sh = pltpu.roll(a3, s, axis=1)
        b_sh = pltpu.roll(b3, s, axis=1)
        m = sub >= s
        b3 = jnp.where(m, a3 * b_sh + b3, b3)
        a3 = jnp.where(m, a3 * a_sh, a3)
    a_s[...] = a3.reshape(ts, d_lru)
    b_s[...] = b3.reshape(ts, d_lru)

    def grp_body(g, carry):
        rows = pl.ds(pl.multiple_of(g * SUBLANES, SUBLANES), SUBLANES)
        hg = b_s[rows, :] + a_s[rows, :] * carry
        b_s[rows, :] = hg
        return jnp.broadcast_to(hg[SUBLANES - 1:SUBLANES, :], (SUBLANES, d_lru))

    hcar[...] = lax.fori_loop(0, n_grp, grp_body, hcar[...], unroll=True)
    yl = _gelu_tanh(u_gate) * b_s[...]

    mng = mng_ref[...]
    yc_n = _rms(yc) * mng[:, :d_conv]
    yl_n = _rms(yl) * mng[:, d_conv:]
    mixo = jnp.dot(yc_n.astype(BF16), wout_ref[pl.ds(0, d_conv), :], preferred_element_type=F32)
    mixo += jnp.dot(yl_n.astype(BF16), wout_ref[pl.ds(d_conv, d_lru), :], preferred_element_type=F32)
    x1 = x + gate1 * mixo
    x1_ref[rows, :] = x1

    h2 = _rms(x1) * gain2 + shift2
    dh = h2.shape[1] // 2
    h2_hi = h2.astype(BF16)
    _store_planes(h2p_ref, _pack_bf16_pair(h2[:, :dh], h2[:, dh:]), rows)
    h2_lo = (h2 - h2_hi.astype(F32)).astype(BF16)
    nt_dims = (((1,), (1,)), ((), ()))
    wrt = wrt_ref[...]
    lg = lax.dot_general(wrt, h2_hi, nt_dims, preferred_element_type=F32)
    lg2 = lax.dot_general(wrt[:n_exp], h2_lo, nt_dims, preferred_element_type=F32)
    logits = lg[:n_exp] + lg[n_exp:] + lg2 + br_ref[...]

    eidx = lax.broadcasted_iota(I32, (n_exp, ts), 0)
    neg_inf = jnp.float32(-jnp.inf)
    work = logits
    vals, idxs, hots = [], [], []
    for _ in range(TOP_K):
        mval = jnp.max(work, axis=0, keepdims=True)
        midx = jnp.min(jnp.where(work == mval, eidx, n_exp), axis=0, keepdims=True)
        hot = eidx == midx
        vals.append(mval)
        idxs.append(midx)
        hots.append(hot)
        work = jnp.where(hot, neg_inf, work)
    exps = [jnp.exp(vk - vals[0]) for vk in vals]
    denom = exps[0] + exps[1] + exps[2] + exps[3]
    gate_rows = [ek / denom for ek in exps]

    sel = jnp.zeros((n_exp, ts), F32)
    for hot in hots:
        sel = sel + hot.astype(F32)
    tri = (lax.broadcasted_iota(I32, (ts, ts), 0) < lax.broadcasted_iota(I32, (ts, ts), 1)).astype(BF16)
    before = jnp.dot(sel.astype(BF16), tri, preferred_element_type=F32) + cnt_s[...]
    rank_rows = [jnp.sum(jnp.where(hot, before, 0.0), axis=0, keepdims=True) for hot in hots]
    cnt_s[...] = cnt_s[...] + jnp.sum(sel, axis=1, keepdims=True)

    idx_ref[:, rows] = jnp.concatenate(idxs, axis=0)
    rank_ref[:, rows] = jnp.concatenate(rank_rows, axis=0).astype(I32)
    g4 = jnp.concatenate(gate_rows, axis=0)
    g_pad = jnp.concatenate([g4, jnp.zeros((LANES - TOP_K, ts), F32)], axis=0)
    gt_ref[rows, :] = g_pad.T


def _mix_call(x, mod, p, b0, bsz, anchor):
    _, seq, dm = x.shape
    n_tiles = TILES_PER_STEP
    tile = min(SEQ_TILE, seq // n_tiles)
    ts = n_tiles * tile
    nj = seq // ts
    n_steps = bsz * nj
    tiles_per_seq = seq // tile
    d_conv = p["conv_w"].shape[1]
    d_lru = p["lru_conv_w"].shape[1]
    conv_w = p["conv_w"].shape[0]
    lru_cw = p["lru_conv_w"].shape[0]
    n_exp = p["wrt"].shape[0] // 2
    assert seq % ts == 0 and tile % LANES == 0 and tile % CONV_ROWS == 0 and n_tiles % 2 == 0
    assert conv_w - 1 <= CONV_PAD and lru_cw - 1 <= SUBLANES and d_conv % LANES == 0 and d_lru % LANES == 0

    def full(a):
        return pl.BlockSpec(a.shape, lambda s: (0,) * a.ndim)

    def next_tile(s):
        return jnp.minimum(n_tiles * (s + 1), bsz * tiles_per_seq - 1)

    weights = [p["g1"], p["win"], p["conv_w"], p["conv_b"], p["ln_g"], p["ln_b"], p["lru_conv_w"],
               p["lru_conv_b"], p["wg"], p["bg"], p["lam"], p["mng"], p["wout"], p["g2"], p["wrt"], p["br"]]
    kern = functools.partial(_mix_kernel, tile=tile, n_tiles=n_tiles, steps_per_seq=nj, d_conv=d_conv, d_lru=d_lru,
                             conv_w=conv_w, lru_cw=lru_cw, n_exp=n_exp)
    out_shape = (
        jax.ShapeDtypeStruct((bsz, seq, dm), F32),
        jax.ShapeDtypeStruct((SC_SPLIT, bsz * seq, dm // 2 // SC_SPLIT), I32),
        jax.ShapeDtypeStruct((TOP_K, bsz * seq), I32),
        jax.ShapeDtypeStruct((TOP_K, bsz * seq), I32),
        jax.ShapeDtypeStruct((bsz, seq, LANES), F32),
        jax.ShapeDtypeStruct((n_exp, LANES), F32),
    )
    return pl.pallas_call(
        kern,
        grid=(n_steps,),
        in_specs=[pl.BlockSpec((None, ts, dm), lambda s: (s // nj + b0, s % nj, 0)),
                  pl.BlockSpec((None, tile, dm),
                               lambda s: (next_tile(s) // tiles_per_seq + b0, next_tile(s) % tiles_per_seq, 0)),
                  pl.BlockSpec((None, N_MOD, dm), lambda s: (s // nj + b0, 0, 0)),
                  pl.BlockSpec((None, N_MOD, dm), lambda s: (next_tile(s) // tiles_per_seq + b0, 0, 0))]
        + [full(w) for w in weights] + [pl.BlockSpec(memory_space=pl.ANY)],
        out_specs=(
            pl.BlockSpec((None, ts, dm), lambda s: (s // nj, s % nj, 0)),
            pl.BlockSpec((SC_SPLIT, ts, dm // 2 // SC_SPLIT), lambda s: (0, s, 0)),
            pl.BlockSpec((TOP_K, ts), lambda s: (0, s)),
            pl.BlockSpec((TOP_K, ts), lambda s: (0, s)),
            pl.BlockSpec((None, ts, LANES), lambda s: (s // nj, s % nj, 0)),
            pl.BlockSpec((n_exp, LANES), lambda s: (0, 0)),
        ),
        out_shape=out_shape,
        scratch_shapes=[
            pltpu.VMEM((tile, 2 * d_conv + 2 * d_lru), F32),
            pltpu.VMEM((tile, 2 * d_conv + 2 * d_lru), F32),
            pltpu.VMEM((d_conv // LANES, CONV_PAD + ts, LANES), F32),
            pltpu.VMEM((d_lru // LANES, SUBLANES + ts, LANES), F32),
            pltpu.VMEM((tile, d_lru), F32),
            pltpu.VMEM((tile, d_lru), F32),
            pltpu.VMEM((SUBLANES, d_lru), F32),
            pltpu.VMEM((n_exp, tile), F32),
        ],
        compiler_params=pltpu.CompilerParams(dimension_semantics=("arbitrary",),
                                             vmem_limit_bytes=VMEM_LIMIT_BYTES),
        name="token_mix_route",
    )(x, x, mod, mod, *weights, anchor)


def _sc_mesh():
    return plsc.VectorSubcoreMesh(core_axis_name="core", subcore_axis_name="subcore",
                                  num_cores=SC_CORES, num_subcores=SC_SUBCORES)


def _sc_scatter_rows(rows, idx, n_out):
    n, w = rows.shape
    n_k = idx.shape[0]
    assert n % (SC_WINDOW * SC_CORES * SC_SUBCORES) == 0

    @functools.partial(pl.kernel, out_type=jax.ShapeDtypeStruct((n_out, w), rows.dtype), mesh=_sc_mesh(),
                       scratch_types=[], name="moe_dispatch_sc")
    def scatter(x_hbm, i_hbm, o_hbm):
        def body(x_vmem, *i_vmems):
            for i_vmem in i_vmems:
                pltpu.sync_copy(x_vmem, o_hbm.at[i_vmem.at[0]])

        pltpu.emit_pipeline(
            body,
            grid=(n // SC_WINDOW,),
            in_specs=[pl.BlockSpec((SC_WINDOW, w), lambda i: (i, 0))]
            + [pl.BlockSpec((1, SC_WINDOW), functools.partial(lambda k, i: (k, i), k)) for k in range(n_k)],
            out_specs=[],
            core_axis_name=("core", "subcore"),
            dimension_semantics=(pltpu.PARALLEL,),
        )(x_hbm, *([i_hbm] * n_k))

    return scatter(rows, idx)


def _sc_gather_rows(table, idx):
    m = idx.shape[1]
    w = table.shape[1]
    assert m % (SC_WINDOW * SC_CORES * SC_SUBCORES) == 0

    @functools.partial(pl.kernel, out_type=jax.ShapeDtypeStruct((m, w), table.dtype), mesh=_sc_mesh(),
                       scratch_types=[], name="moe_gather_sc")
    def gather(x_hbm, i_hbm, o_hbm):
        def body(i_vmem, o_vmem):
            pltpu.sync_copy(x_hbm.at[i_vmem.at[0]], o_vmem)

        pltpu.emit_pipeline(
            body,
            grid=(m // SC_WINDOW,),
            in_specs=[pl.BlockSpec((1, SC_WINDOW), lambda i: (0, i))],
            out_specs=[pl.BlockSpec((SC_WINDOW, w), lambda i: (i, 0))],
            core_axis_name=("core", "subcore"),
            dimension_semantics=(pltpu.PARALLEL,),
        )(i_hbm, o_hbm)

    return gather(table, idx)


def _moe_kernel(be_ref, nb_ref, slot_ref, nxt_ref, nv_ref, xs_ref, wgu_hbm, bgu_ref, wd_hbm, bd_ref, ys_ref,
                wgu_f, wd_f, wgu_b, wd_b, sems, *, d_ff):
    i = pl.program_id(0)
    live = i < nb_ref[0]
    new_expert = (i == 0) | (be_ref[i] != be_ref[jnp.maximum(i - 1, 0)])

    def weight_copies(e, s):
        return (pltpu.make_async_copy(wgu_hbm.at[e], wgu_f.at[s], sems.at[0, s]),
                pltpu.make_async_copy(wd_hbm.at[e], wd_f.at[s], sems.at[1, s]))

    @pl.when(live & new_expert)
    def _():
        e, s, nx = be_ref[i], slot_ref[i], nxt_ref[i]

        @pl.when(i == 0)
        def _():
            for cp in weight_copies(e, s):
                cp.start()

        for cp in weight_copies(e, s):
            cp.wait()

        @pl.when(nx >= 0)
        def _():
            for cp in weight_copies(nx, 1 - s):
                cp.start()

        wgu_b[...] = wgu_f[s].astype(BF16)
        wd_b[...] = wd_f[s].astype(BF16)

    def sub_block(s):
        rows = pl.ds(s * MOE_SUB, MOE_SUB)
        packed = jnp.concatenate([xs_ref[p, rows, :] for p in range(SC_SPLIT)], axis=1)
        x_lo, x_hi = _unpack_bf16_pair(packed)
        dh = x_lo.shape[1]
        gu = jnp.dot(x_lo.astype(BF16), wgu_b[pl.ds(0, dh), :], preferred_element_type=F32)
        gu += jnp.dot(x_hi.astype(BF16), wgu_b[pl.ds(dh, dh), :], preferred_element_type=F32)
        gu += bgu_ref[...]
        g = jnp.minimum(gu[:, :d_ff], SWIGLU_LIMIT)
        u = jnp.clip(gu[:, d_ff:], -SWIGLU_LIMIT, SWIGLU_LIMIT)
        act = (u + 1.0) * (g * _sigmoid(SWIGLU_ALPHA * g))
        y = jnp.dot(act.astype(BF16), wd_b[...], preferred_element_type=F32) + bd_ref[...]
        dm2 = y.shape[1] // 2
        _store_planes(ys_ref, _pack_bf16_pair(y[:, :dm2], y[:, dm2:]), rows)

    n_sub = xs_ref.shape[1] // MOE_SUB
    n_valid = nv_ref[i]

    @pl.when(live & (n_valid == n_sub * MOE_SUB))
    def _():
        for s in range(n_sub):
            sub_block(s)

    @pl.when(live & (n_valid < n_sub * MOE_SUB))
    def _():
        for s in range(n_sub):
            pl.when(s * MOE_SUB < n_valid)(functools.partial(sub_block, s))


def _moe_call(xs, block_expert, n_used, seg_slot, next_expert, block_valid, wgu, bgu, wd, bd):
    _, n_rows, wp = xs.shape
    n_exp, dm, d_ff2 = wgu.shape
    d_ff = d_ff2 // 2
    bm = MOE_BLOCK
    n_blocks = n_rows // bm

    def row_map(i, be, nb, slot, nxt, nv):
        return (0, jnp.minimum(i, nb[0] - 1), 0)

    def b_map(i, be, nb, slot, nxt, nv):
        return (be[jnp.minimum(i, nb[0] - 1)], 0, 0)

    grid_spec = pltpu.PrefetchScalarGridSpec(
        num_scalar_prefetch=5,
        grid=(n_blocks,),
        in_specs=[
            pl.BlockSpec((SC_SPLIT, bm, wp), row_map),
            pl.BlockSpec(memory_space=pl.ANY),
            pl.BlockSpec((None, 1, d_ff2), b_map),
            pl.BlockSpec(memory_space=pl.ANY),
            pl.BlockSpec((None, 1, dm), b_map),
        ],
        out_specs=pl.BlockSpec((SC_SPLIT, bm, wp), row_map),
        scratch_shapes=[
            pltpu.VMEM((2, dm, d_ff2), F32), pltpu.VMEM((2, d_ff, dm), F32),
            pltpu.VMEM((dm, d_ff2), BF16), pltpu.VMEM((d_ff, dm), BF16),
            pltpu.SemaphoreType.DMA((2, 2)),
        ],
    )
    return pl.pallas_call(
        functools.partial(_moe_kernel, d_ff=d_ff),
        grid_spec=grid_spec,
        out_shape=jax.ShapeDtypeStruct(xs.shape, I32),
        compiler_params=pltpu.CompilerParams(dimension_semantics=("arbitrary",),
                                             vmem_limit_bytes=VMEM_LIMIT_BYTES),
        name="moe_experts",
    )(block_expert, n_used, seg_slot, next_expert, block_valid, xs, wgu, bgu, wd, bd)


def _combine_kernel(x1_ref, gt_ref, mod_ref, gf_ref, yg_ref, *out_refs):
    o_ref = out_refs[-1]
    gt = gt_ref[...]
    ts = yg_ref.shape[2]
    dh = yg_ref.shape[3] * SC_SPLIT
    acc_lo = jnp.zeros((ts, dh), F32)
    acc_hi = jnp.zeros((ts, dh), F32)
    for k in range(TOP_K):
        lo, hi = _unpack_bf16_pair(_load_planes(yg_ref.at[k]))
        gk = gt[:, k:k + 1]
        acc_lo += gk * lo
        acc_hi += gk * hi
    y = jnp.concatenate([acc_lo, acc_hi], axis=1)
    gate2 = mod_ref[...][5:6]
    x2 = x1_ref[...] + gate2 * y
    o_ref[...] = _rms(x2) * gf_ref[...]


def _combine_call(x1, gt, mod, gf, yg, out_prev, b0):
    bsz, seq, dm = x1.shape
    ts = min(COMBINE_TILE, seq)
    assert seq % ts == 0
    nj = seq // ts
    in_specs = [
        pl.BlockSpec((None, ts, dm), lambda b, j: (b, j, 0)),
        pl.BlockSpec((None, ts, LANES), lambda b, j: (b, j, 0)),
        pl.BlockSpec((None, N_MOD, dm), lambda b, j: (b + b0, 0, 0)),
        pl.BlockSpec((1, dm), lambda b, j: (0, 0)),
        pl.BlockSpec((TOP_K, SC_SPLIT, ts, yg.shape[3]), lambda b, j: (0, 0, b * nj + j, 0)),
    ]
    args = [x1, gt, mod, gf, yg]
    aliases = {}
    if out_prev is not None:
        in_specs.append(pl.BlockSpec(memory_space=pl.ANY))
        args.append(out_prev)
        aliases = {len(args) - 1: 0}
    return pl.pallas_call(
        _combine_kernel,
        grid=(bsz, nj),
        in_specs=in_specs,
        out_specs=pl.BlockSpec((None, ts, dm), lambda b, j: (b + b0, j, 0)),
        out_shape=jax.ShapeDtypeStruct((mod.shape[0], seq, dm), F32),
        input_output_aliases=aliases,
        compiler_params=pltpu.CompilerParams(dimension_semantics=("arbitrary", "arbitrary"),
                                             vmem_limit_bytes=VMEM_LIMIT_BYTES),
        name="moe_combine",
    )(*args)


def _block_diag(w):
    n_h, d, _ = w.shape
    eye = jnp.eye(n_h, dtype=w.dtype)
    return (eye[:, None, :, None] * w[:, :, None, :]).reshape(n_h * d, n_h * d)


def _layer(x, mod, l, w_in, norm1_g, conv_w, conv_b, conv_ln_g, conv_ln_b, lru_conv_w, lru_conv_b,
           lru_w_a, lru_b_a, lru_w_x, lru_b_x, lru_lambda, mix_norm_g, w_out, norm2_g, w_router,
           b_router, w_gate_up, b_gate_up, w_down, b_down, out_gain):
    bsz, seq, dm = x.shape
    n_tok = bsz * seq
    n_exp = w_router.shape[-1]
    row = lambda a: a.reshape(1, -1)
    wr = w_router[l]
    wr_hi = wr.astype(BF16)
    wr_lo = (wr - wr_hi.astype(F32)).astype(BF16)
    params = dict(
        g1=row(norm1_g[l]), win=w_in[l].astype(BF16), conv_w=conv_w[l], conv_b=row(conv_b[l]),
        ln_g=row(conv_ln_g[l]), ln_b=row(conv_ln_b[l]), lru_conv_w=lru_conv_w[l], lru_conv_b=row(lru_conv_b[l]),
        wg=jnp.concatenate([_block_diag(lru_w_a[l]), _block_diag(lru_w_x[l])], axis=1).astype(BF16),
        bg=jnp.concatenate([lru_b_a[l].reshape(1, -1), lru_b_x[l].reshape(1, -1)], axis=1),
        lam=row(lru_lambda[l]), mng=row(mix_norm_g[l]), wout=w_out[l].astype(BF16), g2=row(norm2_g[l]),
        wrt=jnp.concatenate([wr_hi.T, wr_lo.T], axis=0), br=b_router[l].reshape(n_exp, 1),
    )
    sizes = [s for s in CHUNK_BATCHES if s > 0] if sum(CHUNK_BATCHES) == bsz else [bsz]
    routed, b0, anchor = [], 0, mod
    for cb in sizes:
        r = _route_chunk(x, mod, params, b0, cb, anchor, n_exp)
        routed.append(r)
        anchor = r["dest_sub"]
        b0 += cb
    out = None
    for r in routed:
        out = _experts_chunk(r, mod, out, w_gate_up[l], b_gate_up[l][:, None, :], w_down[l],
                             b_down[l][:, None, :], out_gain)
    return out


def _route_chunk(x, mod, params, b0, cb, anchor, n_exp):
    seq, dm = x.shape[1:]
    n_tok = cb * seq
    x1, h2p, top_idx, rank, gt, cnt = _mix_call(x, mod, params, b0, cb, anchor)

    bm = MOE_BLOCK
    counts = cnt[:, 0].astype(I32)
    padded = (counts + bm - 1) // bm * bm
    e_ids = jnp.arange(n_exp, dtype=I32)
    pad_ends = jnp.sum(jnp.where(e_ids[None, :] <= e_ids[:, None], padded[None, :], 0), axis=1)
    pad_starts = pad_ends - padded
    n_rows = (n_tok * TOP_K // bm + n_exp) * bm
    n_blocks = n_rows // bm
    dest = rank + jnp.sum(jnp.where(top_idx[..., None] == e_ids, pad_starts, 0), axis=-1)
    block_start = jnp.arange(n_blocks, dtype=I32) * bm
    block_expert = jnp.minimum(
        jnp.sum((block_start[:, None] >= pad_ends[None, :]).astype(I32), axis=1), n_exp - 1)
    n_used = pad_ends[-1:] // bm
    present = counts > 0
    seg_ordinal = jnp.sum(jnp.where((e_ids[None, :] < e_ids[:, None]) & present[None, :], 1, 0), axis=1)
    following = jnp.min(jnp.where((e_ids[None, :] > e_ids[:, None]) & present[None, :], e_ids[None, :], n_exp),
                        axis=1)
    following = jnp.where(following == n_exp, -1, following)
    block_hot = block_expert[:, None] == e_ids[None, :]
    seg_slot = jnp.sum(jnp.where(block_hot, seg_ordinal[None, :] % 2, 0), axis=1)
    next_expert = jnp.sum(jnp.where(block_hot, following[None, :], 0), axis=1)
    block_valid = jnp.clip(
        jnp.sum(jnp.where(block_hot, (pad_starts + counts)[None, :], 0), axis=1) - block_start, 0, bm)

    wp = h2p.shape[2]
    plane = jnp.arange(SC_SPLIT, dtype=I32)[None, :, None] * n_rows
    dest_sub = dest[:, None, :] + plane
    return dict(b0=b0, n_tok=n_tok, n_rows=n_rows, x1=x1, h2p=h2p, gt=gt, dest_sub=dest_sub,
                tables=(block_expert, n_used, seg_slot, next_expert, block_valid))


def _experts_chunk(r, mod, out_prev, wgu, bgu, wd, bd, out_gain):
    n_tok, n_rows, dest_sub = r["n_tok"], r["n_rows"], r["dest_sub"]
    wp = r["h2p"].shape[2]
    xs = _sc_scatter_rows(r["h2p"].reshape(SC_SPLIT * n_tok, wp), dest_sub.reshape(TOP_K, SC_SPLIT * n_tok),
                          SC_SPLIT * n_rows).reshape(SC_SPLIT, n_rows, wp)
    ys = _moe_call(xs, *r["tables"], wgu, bgu, wd, bd)
    yg = _sc_gather_rows(ys.reshape(SC_SPLIT * n_rows, wp), dest_sub.reshape(1, TOP_K * SC_SPLIT * n_tok))
    return _combine_call(r["x1"], r["gt"], mod, out_gain, yg.reshape(TOP_K, SC_SPLIT, n_tok, wp), out_prev,
                         r["b0"])


def kernel(x, c, w_ada, b_ada, norm1_g, w_in, conv_w, conv_b, conv_ln_g, conv_ln_b, lru_conv_w, lru_conv_b,
           lru_w_a, lru_b_a, lru_w_x, lru_b_x, lru_lambda, mix_norm_g, w_out, norm2_g, w_router, b_router,
           w_gate_up, b_gate_up, w_down, b_down, final_norm_g):
    depth = w_ada.shape[0]
    assert depth == 1, "the final norm is fused into the (single) layer's combine kernel"
    bsz, seq, dm = x.shape
    mod = _ada_call(c, w_ada[0], b_ada[0]).reshape(bsz, N_MOD, dm)
    return _layer(x, mod, 0, w_in, norm1_g, conv_w, conv_b, conv_ln_g, conv_ln_b, lru_conv_w, lru_conv_b,
                  lru_w_a, lru_b_a, lru_w_x, lru_b_x, lru_lambda, mix_norm_g, w_out, norm2_g, w_router,
                  b_router, w_gate_up, b_gate_up, w_down, b_down, row_gain(final_norm_g))


def row_gain(g):
    return g.reshape(1, -1)
```

```python
import functools

import jax
import jax.numpy as jnp
from jax import lax
from jax.experimental import pallas as pl
from jax.experimental.pallas import tpu as pltpu
from jax.experimental.pallas import tpu_sc as plsc

F32 = jnp.float32
BF16 = jnp.bfloat16
I32 = jnp.int32

EPS = 1e-6
N_MOD = 6
LRU_C = 8.0
TOP_K = 4
SWIGLU_ALPHA = 1.702
SWIGLU_LIMIT = 7.0

LANES = 128
SUBLANES = 8
VMEM_LIMIT_BYTES = 56 * 1024 * 1024

SEQ_TILE = 512
TILES_PER_STEP = 2
COMBINE_TILE = 512
CONV_PAD = 32
CONV_ROWS = 128
MOE_BLOCK = 1024
MOE_SUB = 256
CHUNK_BATCHES = (5, 3)
HI_MASK = -65536

SC_CORES = 2
SC_SUBCORES = 16
SC_WINDOW = 128
SC_SPLIT = 2


def _sigmoid(x):
    return 0.5 * jnp.tanh(0.5 * x) + 0.5


def _pack_bf16_pair(lo_f32, hi_f32):
    lo_bits = lax.bitcast_convert_type(lo_f32.astype(BF16).astype(F32), I32)
    hi_bits = lax.bitcast_convert_type(hi_f32.astype(BF16).astype(F32), I32)
    return lax.shift_right_logical(lo_bits, 16) | hi_bits


def _unpack_bf16_pair(p):
    lo = lax.bitcast_convert_type(lax.shift_left(p, 16), F32)
    hi = lax.bitcast_convert_type(p & HI_MASK, F32)
    return lo, hi


def _store_planes(ref, packed, rows=slice(None)):
    wp = packed.shape[1] // SC_SPLIT
    for s in range(SC_SPLIT):
        ref[s, rows, :] = packed[:, s * wp:(s + 1) * wp]


def _load_planes(ref):
    return jnp.concatenate([ref[s] for s in range(SC_SPLIT)], axis=1)


def _ada_kernel(c_ref, w_ref, b_ref, o_ref):
    c = c_ref[...]
    ca = c * _sigmoid(c)
    w = w_ref[...]
    c_hi = ca.astype(BF16)
    c_lo = (ca - c_hi.astype(F32)).astype(BF16)
    w_hi = w.astype(BF16)
    w_lo = (w - w_hi.astype(F32)).astype(BF16)
    acc = jnp.dot(c_hi, w_hi, preferred_element_type=F32)
    acc += jnp.dot(c_lo, w_hi, preferred_element_type=F32)
    acc += jnp.dot(c_hi, w_lo, preferred_element_type=F32)
    o_ref[...] = acc + b_ref[...]


def _ada_call(c, w_ada, b_ada):
    bsz, dm = c.shape
    n_out = w_ada.shape[1]
    tn = 1024
    return pl.pallas_call(
        _ada_kernel,
        grid=(n_out // tn,),
        in_specs=[
            pl.BlockSpec((bsz, dm), lambda n: (0, 0)),
            pl.BlockSpec((dm, tn), lambda n: (0, n)),
            pl.BlockSpec((1, tn), lambda n: (0, n)),
        ],
        out_specs=pl.BlockSpec((bsz, tn), lambda n: (0, n)),
        out_shape=jax.ShapeDtypeStruct((bsz, n_out), F32),
        compiler_params=pltpu.CompilerParams(dimension_semantics=("arbitrary",)),
        name="ada_mod",
    )(c, w_ada, b_ada.reshape(1, n_out))


def _rms(x, eps=EPS):
    return x * lax.rsqrt(jnp.mean(x * x, axis=-1, keepdims=True) + eps)


def _gelu_tanh(x):
    return 0.5 * x * (1.0 + jnp.tanh(0.7978845608028654 * (x + 0.044715 * (x * x * x))))


def _slab_store(buf, row0, val):
    for s in range(buf.shape[0]):
        buf[s, pl.ds(row0, val.shape[0]), :] = val[:, s * LANES:(s + 1) * LANES]


def _slab_keep_tail(buf, keep, ts):
    for s in range(buf.shape[0]):
        buf[s, pl.ds(0, keep), :] = buf[s, pl.ds(ts, keep), :]


def _causal_tap_sum(buf, w_ref, bias_row, first, n_taps, ts):
    w = w_ref[...]
    cols = []
    for s in range(buf.shape[0]):
        lanes = slice(s * LANES, (s + 1) * LANES)
        chunks = []
        for c in range(0, ts, CONV_ROWS):
            acc = jnp.broadcast_to(bias_row[:, lanes], (CONV_ROWS, LANES))
            for k in range(n_taps):
                acc = acc + w[k:k + 1, lanes] * buf[s, pl.ds(c + first + k, CONV_ROWS), :]
            chunks.append(acc)
        cols.append(jnp.concatenate(chunks, axis=0))
    return jnp.concatenate(cols, axis=1)


def _linear_recurrence(a, b, a_s, b_s, hcar):
    ts = a.shape[0]
    blk = ts // SUBLANES
    pitch = blk + 1
    cols = []
    for j in range(a_s.shape[0]):
        lanes = slice(j * LANES, (j + 1) * LANES)
        for k in range(SUBLANES):
            a_s[j, pl.ds(k * pitch, blk), :] = a[k * blk:(k + 1) * blk, lanes]
            b_s[j, pl.ds(k * pitch, blk), :] = b[k * blk:(k + 1) * blk, lanes]
        h = jnp.zeros((SUBLANES, LANES), F32)
        p = jnp.ones((SUBLANES, LANES), F32)
        for i in range(blk):
            row_i = pl.ds(i, SUBLANES, stride=pitch)
            a_i = a_s[j, row_i, :]
            h = a_i * h + b_s[j, row_i, :]
            p = a_i * p
            b_s[j, row_i, :] = h
            a_s[j, row_i, :] = p
        state = hcar[:, lanes]
        entering = []
        for k in range(SUBLANES):
            entering.append(state)
            state = p[k:k + 1, :] * state + h[k:k + 1, :]
        hcar[:, lanes] = state
        h_in = jnp.concatenate(entering, axis=0)
        for i in range(blk):
            row_i = pl.ds(i, SUBLANES, stride=pitch)
            b_s[j, row_i, :] = b_s[j, row_i, :] + a_s[j, row_i, :] * h_in
        cols.append(jnp.concatenate([b_s[j, pl.ds(k * pitch, blk), :] for k in range(SUBLANES)], axis=0))
    return jnp.concatenate(cols, axis=1)


def _mix_kernel(x_ref, xn_ref, mod_ref, modn_ref, g1_ref, win_ref, cw_ref, cb_ref, lng_ref, lnb_ref,
                lcw_ref, lcb_ref, wg_ref, bg_ref, lam_ref, mng_ref, wout_ref, g2_ref,
                wrt_ref, br_ref, anchor_ref,
                x1_ref, h2p_ref, idx_ref, rank_ref, gt_ref, cnt_ref,
                u_a, u_b, vbuf, rbuf, a_s, b_s, hcar, cnt_s,
                *, tile, n_tiles, steps_per_seq, d_conv, d_lru, conv_w, lru_cw, n_exp):
    del anchor_ref
    s = pl.program_id(0)

    @pl.when(s % steps_per_seq == 0)
    def _():
        _slab_store(vbuf, 0, jnp.zeros((CONV_PAD, d_conv), F32))
        _slab_store(rbuf, 0, jnp.zeros((SUBLANES, d_lru), F32))
        hcar[...] = jnp.zeros_like(hcar)

    mod = mod_ref[...]
    modn = modn_ref[...]
    shift1, gate1, shift2 = mod[0:1], mod[2:3], mod[3:4]
    gain1 = g1_ref[...] * (1.0 + mod[1:2])
    gain2 = g2_ref[...] * (1.0 + mod[4:5])
    lam = lam_ref[...]
    softplus_neg_lam = jnp.maximum(-lam, 0.0) + jnp.log1p(jnp.exp(-jnp.abs(lam)))

    def project(src_ref, r0, gain, shift, u_ref):
        h = _rms(src_ref[pl.ds(r0, tile), :]) * gain + shift
        u_ref[...] = jnp.dot(h.astype(BF16), win_ref[...], preferred_element_type=F32)

    def conv_group(r0, u_ref):
        return _mix_conv_group(r0, tile, u_ref, cw_ref, cb_ref, lng_ref, lnb_ref, vbuf, d_conv=d_conv,
                               conv_w=conv_w)

    def finish(r0, u_ref, yc):
        _mix_finish(r0, tile, x_ref, u_ref, yc, gate1, shift2, gain2, softplus_neg_lam,
                    lcw_ref, lcb_ref, wg_ref, bg_ref, mng_ref, wout_ref,
                    wrt_ref, br_ref, x1_ref, h2p_ref, idx_ref, rank_ref, gt_ref, rbuf, a_s, b_s, hcar,
                    cnt_s, d_conv=d_conv, d_lru=d_lru, lru_cw=lru_cw, n_exp=n_exp)

    @pl.when(s == 0)
    def _():
        cnt_s[...] = jnp.zeros_like(cnt_s)
        project(x_ref, 0, gain1, shift1, u_a)

    bufs = (u_a, u_b)
    for q in range(n_tiles):
        cur, nxt = bufs[q % 2], bufs[(q + 1) % 2]
        yc = conv_group(q * tile, cur)
        if q + 1 < n_tiles:
            project(x_ref, (q + 1) * tile, gain1, shift1, nxt)
        else:
            project(xn_ref, 0, g1_ref[...] * (1.0 + modn[1:2]), modn[0:1], nxt)
        finish(q * tile, cur, yc)
    _slab_keep_tail(vbuf, CONV_PAD, n_tiles * tile)
    _slab_keep_tail(rbuf, SUBLANES, n_tiles * tile)
    cnt_ref[...] = cnt_s[:, :LANES]


def _mix_conv_group(r0, ts, u_ref, cw_ref, cb_ref, lng_ref, lnb_ref, vbuf, *, d_conv, conv_w):
    v = u_ref[:, :d_conv] * _sigmoid(u_ref[:, d_conv:2 * d_conv])
    _slab_store(vbuf, CONV_PAD + r0, v)
    acc = _causal_tap_sum(vbuf, cw_ref, cb_ref[...], r0 + CONV_PAD - (conv_w - 1), conv_w, ts)
    mu = jnp.mean(acc, axis=-1, keepdims=True)
    cen = acc - mu
    var = jnp.mean(cen * cen, axis=-1, keepdims=True)
    yc = cen * lax.rsqrt(var + EPS) * lng_ref[...] + lnb_ref[...]
    return yc * _sigmoid(yc)


def _mix_finish(r0, ts, x_ref, u_ref, yc, gate1, shift2, gain2, softplus_neg_lam,
                lcw_ref, lcb_ref, wg_ref, bg_ref, mng_ref, wout_ref,
                wrt_ref, br_ref, x1_ref, h2p_ref, idx_ref, rank_ref, gt_ref, rbuf, a_s, b_s, hcar,
                cnt_s, *, d_conv, d_lru, lru_cw, n_exp):
    rows = pl.ds(r0, ts)
    x = x_ref[rows, :]
    u = u_ref

    u_gate = u[:, 2 * d_conv:2 * d_conv + d_lru]
    _slab_store(rbuf, SUBLANES + r0, u[:, 2 * d_conv + d_lru:])
    xr = _causal_tap_sum(rbuf, lcw_ref, lcb_ref[...], r0 + SUBLANES - (lru_cw - 1), lru_cw, ts)
    gates = jnp.dot(xr.astype(BF16), wg_ref[...], preferred_element_type=F32) + bg_ref[...]
    r = _sigmoid(gates[:, :d_lru])
    i_g = _sigmoid(gates[:, d_lru:])
    log_a = (-LRU_C) * r * softplus_neg_lam
    a = jnp.exp(log_a)
    inp = jnp.sqrt(1.0 - jnp.exp(2.0 * log_a)) * (i_g * xr)

    yl = _gelu_tanh(u_gate) * _linear_recurrence(a, inp, a_s, b_s, hcar)

    mng = mng_ref[...]
    yc_n = _rms(yc) * mng[:, :d_conv]
    yl_n = _rms(yl) * mng[:, d_conv:]
    mixo = jnp.dot(yc_n.astype(BF16), wout_ref[pl.ds(0, d_conv), :], preferred_element_type=F32)
    mixo += jnp.dot(yl_n.astype(BF16), wout_ref[pl.ds(d_conv, d_lru), :], preferred_element_type=F32)
    x1 = x + gate1 * mixo
    x1_ref[rows, :] = x1

    h2 = _rms(x1) * gain2 + shift2
    dh = h2.shape[1] // 2
    h2_hi = h2.astype(BF16)
    _store_planes(h2p_ref, _pack_bf16_pair(h2[:, :dh], h2[:, dh:]), rows)
    h2_lo = (h2 - h2_hi.astype(F32)).astype(BF16)
    nt_dims = (((1,), (1,)), ((), ()))
    wrt = wrt_ref[...]
    lg = lax.dot_general(wrt, h2_hi, nt_dims, preferred_element_type=F32)
    lg2 = lax.dot_general(wrt[:n_exp], h2_lo, nt_dims, preferred_element_type=F32)
    logits = lg[:n_exp] + lg[n_exp:] + lg2 + br_ref[...]

    eidx = lax.broadcasted_iota(I32, (n_exp, ts), 0)
    neg_inf = jnp.float32(-jnp.inf)
    work = logits
    vals, idxs, hots = [], [], []
    for _ in range(TOP_K):
        mval = jnp.max(work, axis=0, keepdims=True)
        midx = jnp.min(jnp.where(work == mval, eidx, n_exp), axis=0, keepdims=True)
        hot = eidx == midx
        vals.append(mval)
        idxs.append(midx)
        hots.append(hot)
        work = jnp.where(hot, neg_inf, work)
    exps = [jnp.exp(vk - vals[0]) for vk in vals]
    denom = exps[0] + exps[1] + exps[2] + exps[3]
    gate_rows = [ek / denom for ek in exps]

    sel = jnp.zeros((n_exp, ts), F32)
    for hot in hots:
        sel = sel + hot.astype(F32)
    tri = (lax.broadcasted_iota(I32, (ts, ts), 0) < lax.broadcasted_iota(I32, (ts, ts), 1)).astype(BF16)
    before = jnp.dot(sel.astype(BF16), tri, preferred_element_type=F32) + cnt_s[...]
    rank_rows = [jnp.sum(jnp.where(hot, before, 0.0), axis=0, keepdims=True) for hot in hots]
    cnt_s[...] = cnt_s[...] + jnp.sum(sel, axis=1, keepdims=True)

    idx_ref[:, rows] = jnp.concatenate(idxs, axis=0)
    rank_ref[:, rows] = jnp.concatenate(rank_rows, axis=0).astype(I32)
    g4 = jnp.concatenate(gate_rows, axis=0)
    g_pad = jnp.concatenate([g4, jnp.zeros((LANES - TOP_K, ts), F32)], axis=0)
    gt_ref[rows, :] = g_pad.T


def _mix_call(x, mod, p, b0, bsz, anchor):
    _, seq, dm = x.shape
    n_tiles = TILES_PER_STEP
    tile = min(SEQ_TILE, seq // n_tiles)
    ts = n_tiles * tile
    nj = seq // ts
    n_steps = bsz * nj
    tiles_per_seq = seq // tile
    d_conv = p["conv_w"].shape[1]
    d_lru = p["lru_conv_w"].shape[1]
    conv_w = p["conv_w"].shape[0]
    lru_cw = p["lru_conv_w"].shape[0]
    n_exp = p["wrt"].shape[0] // 2
    assert seq % ts == 0 and tile % LANES == 0 and tile % CONV_ROWS == 0 and n_tiles % 2 == 0
    assert conv_w - 1 <= CONV_PAD and lru_cw - 1 <= SUBLANES and d_conv % LANES == 0 and d_lru % LANES == 0

    def full(a):
        return pl.BlockSpec(a.shape, lambda s: (0,) * a.ndim)

    def next_tile(s):
        return jnp.minimum(n_tiles * (s + 1), bsz * tiles_per_seq - 1)

    weights = [p["g1"], p["win"], p["conv_w"], p["conv_b"], p["ln_g"], p["ln_b"], p["lru_conv_w"],
               p["lru_conv_b"], p["wg"], p["bg"], p["lam"], p["mng"], p["wout"], p["g2"], p["wrt"], p["br"]]
    kern = functools.partial(_mix_kernel, tile=tile, n_tiles=n_tiles, steps_per_seq=nj, d_conv=d_conv, d_lru=d_lru,
                             conv_w=conv_w, lru_cw=lru_cw, n_exp=n_exp)
    out_shape = (
        jax.ShapeDtypeStruct((bsz, seq, dm), F32),
        jax.ShapeDtypeStruct((SC_SPLIT, bsz * seq, dm // 2 // SC_SPLIT), I32),
        jax.ShapeDtypeStruct((TOP_K, bsz * seq), I32),
        jax.ShapeDtypeStruct((TOP_K, bsz * seq), I32),
        jax.ShapeDtypeStruct((bsz, seq, LANES), F32),
        jax.ShapeDtypeStruct((n_exp, LANES), F32),
    )
    return pl.pallas_call(
        kern,
        grid=(n_steps,),
        in_specs=[pl.BlockSpec((None, ts, dm), lambda s: (s // nj + b0, s % nj, 0)),
                  pl.BlockSpec((None, tile, dm),
                               lambda s: (next_tile(s) // tiles_per_seq + b0, next_tile(s) % tiles_per_seq, 0)),
                  pl.BlockSpec((None, N_MOD, dm), lambda s: (s // nj + b0, 0, 0)),
                  pl.BlockSpec((None, N_MOD, dm), lambda s: (next_tile(s) // tiles_per_seq + b0, 0, 0))]
        + [full(w) for w in weights] + [pl.BlockSpec(memory_space=pl.ANY)],
        out_specs=(
            pl.BlockSpec((None, ts, dm), lambda s: (s // nj, s % nj, 0)),
            pl.BlockSpec((SC_SPLIT, ts, dm // 2 // SC_SPLIT), lambda s: (0, s, 0)),
            pl.BlockSpec((TOP_K, ts), lambda s: (0, s)),
            pl.BlockSpec((TOP_K, ts), lambda s: (0, s)),
            pl.BlockSpec((None, ts, LANES), lambda s: (s // nj, s % nj, 0)),
            pl.BlockSpec((n_exp, LANES), lambda s: (0, 0)),
        ),
        out_shape=out_shape,
        scratch_shapes=[
            pltpu.VMEM((tile, 2 * d_conv + 2 * d_lru), F32),
            pltpu.VMEM((tile, 2 * d_conv + 2 * d_lru), F32),
            pltpu.VMEM((d_conv // LANES, CONV_PAD + ts, LANES), F32),
            pltpu.VMEM((d_lru // LANES, SUBLANES + ts, LANES), F32),
            pltpu.VMEM((d_lru // LANES, tile + SUBLANES, LANES), F32),
            pltpu.VMEM((d_lru // LANES, tile + SUBLANES, LANES), F32),
            pltpu.VMEM((1, d_lru), F32),
            pltpu.VMEM((n_exp, tile), F32),
        ],
        compiler_params=pltpu.CompilerParams(dimension_semantics=("arbitrary",),
                                             vmem_limit_bytes=VMEM_LIMIT_BYTES),
        name="token_mix_route",
    )(x, x, mod, mod, *weights, anchor)


def _sc_mesh():
    return plsc.VectorSubcoreMesh(core_axis_name="core", subcore_axis_name="subcore",
                                  num_cores=SC_CORES, num_subcores=SC_SUBCORES)


def _sc_scatter_rows(rows, idx, n_out):
    n, w = rows.shape
    n_k = idx.shape[0]
    assert n % (SC_WINDOW * SC_CORES * SC_SUBCORES) == 0

    @functools.partial(pl.kernel, out_type=jax.ShapeDtypeStruct((n_out, w), rows.dtype), mesh=_sc_mesh(),
                       scratch_types=[], name="moe_dispatch_sc")
    def scatter(x_hbm, i_hbm, o_hbm):
        def body(x_vmem, *i_vmems):
            for i_vmem in i_vmems:
                pltpu.sync_copy(x_vmem, o_hbm.at[i_vmem.at[0]])

        pltpu.emit_pipeline(
            body,
            grid=(n // SC_WINDOW,),
            in_specs=[pl.BlockSpec((SC_WINDOW, w), lambda i: (i, 0))]
            + [pl.BlockSpec((1, SC_WINDOW), functools.partial(lambda k, i: (k, i), k)) for k in range(n_k)],
            out_specs=[],
            core_axis_name=("core", "subcore"),
            dimension_semantics=(pltpu.PARALLEL,),
        )(x_hbm, *([i_hbm] * n_k))

    return scatter(rows, idx)


def _sc_gather_rows(table, idx):
    m = idx.shape[1]
    w = table.shape[1]
    assert m % (SC_WINDOW * SC_CORES * SC_SUBCORES) == 0

    @functools.partial(pl.kernel, out_type=jax.ShapeDtypeStruct((m, w), table.dtype), mesh=_sc_mesh(),
                       scratch_types=[], name="moe_gather_sc")
    def gather(x_hbm, i_hbm, o_hbm):
        def body(i_vmem, o_vmem):
            pltpu.sync_copy(x_hbm.at[i_vmem.at[0]], o_vmem)

        pltpu.emit_pipeline(
            body,
            grid=(m // SC_WINDOW,),
            in_specs=[pl.BlockSpec((1, SC_WINDOW), lambda i: (0, i))],
            out_specs=[pl.BlockSpec((SC_WINDOW, w), lambda i: (i, 0))],
            core_axis_name=("core", "subcore"),
            dimension_semantics=(pltpu.PARALLEL,),
        )(i_hbm, o_hbm)

    return gather(table, idx)


def _moe_kernel(be_ref, nb_ref, slot_ref, nxt_ref, nv_ref, xs_ref, wgu_hbm, bgu_ref, wd_hbm, bd_ref, ys_ref,
                wgu_f, wd_f, wgu_b, wd_b, sems, *, d_ff):
    i = pl.program_id(0)
    live = i < nb_ref[0]
    new_expert = (i == 0) | (be_ref[i] != be_ref[jnp.maximum(i - 1, 0)])

    def weight_copies(e, s):
        return (pltpu.make_async_copy(wgu_hbm.at[e], wgu_f.at[s], sems.at[0, s]),
                pltpu.make_async_copy(wd_hbm.at[e], wd_f.at[s], sems.at[1, s]))

    @pl.when(live & new_expert)
    def _():
        e, s, nx = be_ref[i], slot_ref[i], nxt_ref[i]

        @pl.when(i == 0)
        def _():
            for cp in weight_copies(e, s):
                cp.start()

        for cp in weight_copies(e, s):
            cp.wait()

        @pl.when(nx >= 0)
        def _():
            for cp in weight_copies(nx, 1 - s):
                cp.start()

        wgu_b[...] = wgu_f[s].astype(BF16)
        wd_b[...] = wd_f[s].astype(BF16)

    def sub_block(s):
        rows = pl.ds(s * MOE_SUB, MOE_SUB)
        packed = jnp.concatenate([xs_ref[p, rows, :] for p in range(SC_SPLIT)], axis=1)
        x_lo, x_hi = _unpack_bf16_pair(packed)
        dh = x_lo.shape[1]
        gu = jnp.dot(x_lo.astype(BF16), wgu_b[pl.ds(0, dh), :], preferred_element_type=F32)
        gu += jnp.dot(x_hi.astype(BF16), wgu_b[pl.ds(dh, dh), :], preferred_element_type=F32)
        gu += bgu_ref[...]
        g = jnp.minimum(gu[:, :d_ff], SWIGLU_LIMIT)
        u = jnp.clip(gu[:, d_ff:], -SWIGLU_LIMIT, SWIGLU_LIMIT)
        act = (u + 1.0) * (g * _sigmoid(SWIGLU_ALPHA * g))
        y = jnp.dot(act.astype(BF16), wd_b[...], preferred_element_type=F32) + bd_ref[...]
        dm2 = y.shape[1] // 2
        _store_planes(ys_ref, _pack_bf16_pair(y[:, :dm2], y[:, dm2:]), rows)

    n_sub = xs_ref.shape[1] // MOE_SUB
    n_valid = nv_ref[i]

    @pl.when(live & (n_valid == n_sub * MOE_SUB))
    def _():
        for s in range(n_sub):
            sub_block(s)

    @pl.when(live & (n_valid < n_sub * MOE_SUB))
    def _():
        for s in range(n_sub):
            pl.when(s * MOE_SUB < n_valid)(functools.partial(sub_block, s))


def _moe_call(xs, block_expert, n_used, seg_slot, next_expert, block_valid, wgu, bgu, wd, bd):
    _, n_rows, wp = xs.shape
    n_exp, dm, d_ff2 = wgu.shape
    d_ff = d_ff2 // 2
    bm = MOE_BLOCK
    n_blocks = n_rows // bm

    def row_map(i, be, nb, slot, nxt, nv):
        return (0, jnp.minimum(i, nb[0] - 1), 0)

    def b_map(i, be, nb, slot, nxt, nv):
        return (be[jnp.minimum(i, nb[0] - 1)], 0, 0)

    grid_spec = pltpu.PrefetchScalarGridSpec(
        num_scalar_prefetch=5,
        grid=(n_blocks,),
        in_specs=[
            pl.BlockSpec((SC_SPLIT, bm, wp), row_map),
            pl.BlockSpec(memory_space=pl.ANY),
            pl.BlockSpec((None, 1, d_ff2), b_map),
            pl.BlockSpec(memory_space=pl.ANY),
            pl.BlockSpec((None, 1, dm), b_map),
        ],
        out_specs=pl.BlockSpec((SC_SPLIT, bm, wp), row_map),
        scratch_shapes=[
            pltpu.VMEM((2, dm, d_ff2), F32), pltpu.VMEM((2, d_ff, dm), F32),
            pltpu.VMEM((dm, d_ff2), BF16), pltpu.VMEM((d_ff, dm), BF16),
            pltpu.SemaphoreType.DMA((2, 2)),
        ],
    )
    return pl.pallas_call(
        functools.partial(_moe_kernel, d_ff=d_ff),
        grid_spec=grid_spec,
        out_shape=jax.ShapeDtypeStruct(xs.shape, I32),
        compiler_params=pltpu.CompilerParams(dimension_semantics=("arbitrary",),
                                             vmem_limit_bytes=VMEM_LIMIT_BYTES),
        name="moe_experts",
    )(block_expert, n_used, seg_slot, next_expert, block_valid, xs, wgu, bgu, wd, bd)


def _combine_kernel(x1_ref, gt_ref, mod_ref, gf_ref, yg_ref, *out_refs):
    o_ref = out_refs[-1]
    gt = gt_ref[...]
    ts = yg_ref.shape[2]
    dh = yg_ref.shape[3] * SC_SPLIT
    acc_lo = jnp.zeros((ts, dh), F32)
    acc_hi = jnp.zeros((ts, dh), F32)
    for k in range(TOP_K):
        lo, hi = _unpack_bf16_pair(_load_planes(yg_ref.at[k]))
        gk = gt[:, k:k + 1]
        acc_lo += gk * lo
        acc_hi += gk * hi
    y = jnp.concatenate([acc_lo, acc_hi], axis=1)
    gate2 = mod_ref[...][5:6]
    x2 = x1_ref[...] + gate2 * y
    o_ref[...] = _rms(x2) * gf_ref[...]


def _combine_call(x1, gt, mod, gf, yg, out_prev, b0):
    bsz, seq, dm = x1.shape
    ts = min(COMBINE_TILE, seq)
    assert seq % ts == 0
    nj = seq // ts
    in_specs = [
        pl.BlockSpec((None, ts, dm), lambda b, j: (b, j, 0)),
        pl.BlockSpec((None, ts, LANES), lambda b, j: (b, j, 0)),
        pl.BlockSpec((None, N_MOD, dm), lambda b, j: (b + b0, 0, 0)),
        pl.BlockSpec((1, dm), lambda b, j: (0, 0)),
        pl.BlockSpec((TOP_K, SC_SPLIT, ts, yg.shape[3]), lambda b, j: (0, 0, b * nj + j, 0)),
    ]
    args = [x1, gt, mod, gf, yg]
    aliases = {}
    if out_prev is not None:
        in_specs.append(pl.BlockSpec(memory_space=pl.ANY))
        args.append(out_prev)
        aliases = {len(args) - 1: 0}
    return pl.pallas_call(
        _combine_kernel,
        grid=(bsz, nj),
        in_specs=in_specs,
        out_specs=pl.BlockSpec((None, ts, dm), lambda b, j: (b + b0, j, 0)),
        out_shape=jax.ShapeDtypeStruct((mod.shape[0], seq, dm), F32),
        input_output_aliases=aliases,
        compiler_params=pltpu.CompilerParams(dimension_semantics=("arbitrary", "arbitrary"),
                                             vmem_limit_bytes=VMEM_LIMIT_BYTES),
        name="moe_combine",
    )(*args)


def _block_diag(w):
    n_h, d, _ = w.shape
    eye = jnp.eye(n_h, dtype=w.dtype)
    return (eye[:, None, :, None] * w[:, :, None, :]).reshape(n_h * d, n_h * d)


def _layer(x, mod, l, w_in, norm1_g, conv_w, conv_b, conv_ln_g, conv_ln_b, lru_conv_w, lru_conv_b,
           lru_w_a, lru_b_a, lru_w_x, lru_b_x, lru_lambda, mix_norm_g, w_out, norm2_g, w_router,
           b_router, w_gate_up, b_gate_up, w_down, b_down, out_gain):
    bsz, seq, dm = x.shape
    n_tok = bsz * seq
    n_exp = w_router.shape[-1]
    row = lambda a: a.reshape(1, -1)
    wr = w_router[l]
    wr_hi = wr.astype(BF16)
    wr_lo = (wr - wr_hi.astype(F32)).astype(BF16)
    params = dict(
        g1=row(norm1_g[l]), win=w_in[l].astype(BF16), conv_w=conv_w[l], conv_b=row(conv_b[l]),
        ln_g=row(conv_ln_g[l]), ln_b=row(conv_ln_b[l]), lru_conv_w=lru_conv_w[l], lru_conv_b=row(lru_conv_b[l]),
        wg=jnp.concatenate([_block_diag(lru_w_a[l]), _block_diag(lru_w_x[l])], axis=1).astype(BF16),
        bg=jnp.concatenate([lru_b_a[l].reshape(1, -1), lru_b_x[l].reshape(1, -1)], axis=1),
        lam=row(lru_lambda[l]), mng=row(mix_norm_g[l]), wout=w_out[l].astype(BF16), g2=row(norm2_g[l]),
        wrt=jnp.concatenate([wr_hi.T, wr_lo.T], axis=0), br=b_router[l].reshape(n_exp, 1),
    )
    sizes = [s for s in CHUNK_BATCHES if s > 0] if sum(CHUNK_BATCHES) == bsz else [bsz]
    routed, b0, anchor = [], 0, mod
    for cb in sizes:
        r = _route_chunk(x, mod, params, b0, cb, anchor, n_exp)
        routed.append(r)
        anchor = r["dest_sub"]
        b0 += cb
    out = None
    for r in routed:
        out = _experts_chunk(r, mod, out, w_gate_up[l], b_gate_up[l][:, None, :], w_down[l],
                             b_down[l][:, None, :], out_gain)
    return out


def _route_chunk(x, mod, params, b0, cb, anchor, n_exp):
    seq, dm = x.shape[1:]
    n_tok = cb * seq
    x1, h2p, top_idx, rank, gt, cnt = _mix_call(x, mod, params, b0, cb, anchor)

    bm = MOE_BLOCK
    counts = cnt[:, 0].astype(I32)
    padded = (counts + bm - 1) // bm * bm
    e_ids = jnp.arange(n_exp, dtype=I32)
    pad_ends = jnp.sum(jnp.where(e_ids[None, :] <= e_ids[:, None], padded[None, :], 0), axis=1)
    pad_starts = pad_ends - padded
    n_rows = (n_tok * TOP_K // bm + n_exp) * bm
    n_blocks = n_rows // bm
    dest = rank + jnp.sum(jnp.where(top_idx[..., None] == e_ids, pad_starts, 0), axis=-1)
    block_start = jnp.arange(n_blocks, dtype=I32) * bm
    block_expert = jnp.minimum(
        jnp.sum((block_start[:, None] >= pad_ends[None, :]).astype(I32), axis=1), n_exp - 1)
    n_used = pad_ends[-1:] // bm
    present = counts > 0
    seg_ordinal = jnp.sum(jnp.where((e_ids[None, :] < e_ids[:, None]) & present[None, :], 1, 0), axis=1)
    following = jnp.min(jnp.where((e_ids[None, :] > e_ids[:, None]) & present[None, :], e_ids[None, :], n_exp),
                        axis=1)
    following = jnp.where(following == n_exp, -1, following)
    block_hot = block_expert[:, None] == e_ids[None, :]
    seg_slot = jnp.sum(jnp.where(block_hot, seg_ordinal[None, :] % 2, 0), axis=1)
    next_expert = jnp.sum(jnp.where(block_hot, following[None, :], 0), axis=1)
    block_valid = jnp.clip(
        jnp.sum(jnp.where(block_hot, (pad_starts + counts)[None, :], 0), axis=1) - block_start, 0, bm)

    wp = h2p.shape[2]
    plane = jnp.arange(SC_SPLIT, dtype=I32)[None, :, None] * n_rows
    dest_sub = dest[:, None, :] + plane
    return dict(b0=b0, n_tok=n_tok, n_rows=n_rows, x1=x1, h2p=h2p, gt=gt, dest_sub=dest_sub,
                tables=(block_expert, n_used, seg_slot, next_expert, block_valid))


def _experts_chunk(r, mod, out_prev, wgu, bgu, wd, bd, out_gain):
    n_tok, n_rows, dest_sub = r["n_tok"], r["n_rows"], r["dest_sub"]
    wp = r["h2p"].shape[2]
    xs = _sc_scatter_rows(r["h2p"].reshape(SC_SPLIT * n_tok, wp), dest_sub.reshape(TOP_K, SC_SPLIT * n_tok),
                          SC_SPLIT * n_rows).reshape(SC_SPLIT, n_rows, wp)
    ys = _moe_call(xs, *r["tables"], wgu, bgu, wd, bd)
    yg = _sc_gather_rows(ys.reshape(SC_SPLIT * n_rows, wp), dest_sub.reshape(1, TOP_K * SC_SPLIT * n_tok))
    return _combine_call(r["x1"], r["gt"], mod, out_gain, yg.reshape(TOP_K, SC_SPLIT, n_tok, wp), out_prev,
                         r["b0"])


def kernel(x, c, w_ada, b_ada, norm1_g, w_in, conv_w, conv_b, conv_ln_g, conv_ln_b, lru_conv_w, lru_conv_b,
           lru_w_a, lru_b_a, lru_w_x, lru_b_x, lru_lambda, mix_norm_g, w_out, norm2_g, w_router, b_router,
           w_gate_up, b_gate_up, w_down, b_down, final_norm_g):
    depth = w_ada.shape[0]
    assert depth == 1, "the final norm is fused into the (single) layer's combine kernel"
    bsz, seq, dm = x.shape
    mod = _ada_call(c, w_ada[0], b_ada[0]).reshape(bsz, N_MOD, dm)
    return _layer(x, mod, 0, w_in, norm1_g, conv_w, conv_b, conv_ln_g, conv_ln_b, lru_conv_w, lru_conv_b,
                  lru_w_a, lru_b_a, lru_w_x, lru_b_x, lru_lambda, mix_norm_g, w_out, norm2_g, w_router,
                  b_router, w_gate_up, b_gate_up, w_down, b_down, row_gain(final_norm_g))


def row_gain(g):
    return g.reshape(1, -1)
```

```python
import functools

import jax
import jax.numpy as jnp
from jax import lax
from jax.experimental import pallas as pl
from jax.experimental.pallas import tpu as pltpu
from jax.experimental.pallas import tpu_sc as plsc

F32 = jnp.float32
BF16 = jnp.bfloat16
I32 = jnp.int32

EPS = 1e-6
N_MOD = 6
LRU_C = 8.0
TOP_K = 4
SWIGLU_ALPHA = 1.702
SWIGLU_LIMIT = 7.0

LANES = 128
SUBLANES = 8
VMEM_LIMIT_BYTES = 56 * 1024 * 1024

SEQ_TILE = 512
TILES_PER_STEP = 2
COMBINE_TILE = 512
CONV_PAD = 32
CONV_ROWS = 128
MOE_BLOCKS = (1024, 2048)
MOE_SUB = 256
CHUNK_BATCHES = (5, 3)
HI_MASK = -65536

SC_CORES = 2
SC_SUBCORES = 16
SC_WINDOW = 128
SC_SPLIT = 2


def _sigmoid(x):
    return 0.5 * jnp.tanh(0.5 * x) + 0.5


def _pack_bf16_pair(lo_f32, hi_f32):
    lo_bits = lax.bitcast_convert_type(lo_f32.astype(BF16).astype(F32), I32)
    hi_bits = lax.bitcast_convert_type(hi_f32.astype(BF16).astype(F32), I32)
    return lax.shift_right_logical(lo_bits, 16) | hi_bits


def _unpack_bf16_pair(p):
    lo = lax.bitcast_convert_type(lax.shift_left(p, 16), F32)
    hi = lax.bitcast_convert_type(p & HI_MASK, F32)
    return lo, hi


def _store_planes(ref, packed, rows=slice(None)):
    wp = packed.shape[1] // SC_SPLIT
    for s in range(SC_SPLIT):
        ref[s, rows, :] = packed[:, s * wp:(s + 1) * wp]


def _load_planes(ref):
    return jnp.concatenate([ref[s] for s in range(SC_SPLIT)], axis=1)


def _ada_kernel(c_ref, w_ref, b_ref, o_ref):
    c = c_ref[...]
    ca = c * _sigmoid(c)
    w = w_ref[...]
    c_hi = ca.astype(BF16)
    c_lo = (ca - c_hi.astype(F32)).astype(BF16)
    w_hi = w.astype(BF16)
    w_lo = (w - w_hi.astype(F32)).astype(BF16)
    acc = jnp.dot(c_hi, w_hi, preferred_element_type=F32)
    acc += jnp.dot(c_lo, w_hi, preferred_element_type=F32)
    acc += jnp.dot(c_hi, w_lo, preferred_element_type=F32)
    o_ref[...] = acc + b_ref[...]


def _ada_call(c, w_ada, b_ada):
    bsz, dm = c.shape
    n_out = w_ada.shape[1]
    tn = 1024
    return pl.pallas_call(
        _ada_kernel,
        grid=(n_out // tn,),
        in_specs=[
            pl.BlockSpec((bsz, dm), lambda n: (0, 0)),
            pl.BlockSpec((dm, tn), lambda n: (0, n)),
            pl.BlockSpec((1, tn), lambda n: (0, n)),
        ],
        out_specs=pl.BlockSpec((bsz, tn), lambda n: (0, n)),
        out_shape=jax.ShapeDtypeStruct((bsz, n_out), F32),
        compiler_params=pltpu.CompilerParams(dimension_semantics=("arbitrary",)),
        name="ada_mod",
    )(c, w_ada, b_ada.reshape(1, n_out))


def _rms(x, eps=EPS):
    return x * lax.rsqrt(jnp.mean(x * x, axis=-1, keepdims=True) + eps)


def _gelu_tanh(x):
    return 0.5 * x * (1.0 + jnp.tanh(0.7978845608028654 * (x + 0.044715 * (x * x * x))))


def _slab_store(buf, row0, val):
    for s in range(buf.shape[0]):
        buf[s, pl.ds(row0, val.shape[0]), :] = val[:, s * LANES:(s + 1) * LANES]


def _slab_keep_tail(buf, keep, ts):
    for s in range(buf.shape[0]):
        buf[s, pl.ds(0, keep), :] = buf[s, pl.ds(ts, keep), :]


def _causal_tap_sum(buf, w_ref, bias_row, first, n_taps, ts):
    w = w_ref[...]
    cols = []
    for s in range(buf.shape[0]):
        lanes = slice(s * LANES, (s + 1) * LANES)
        chunks = []
        for c in range(0, ts, CONV_ROWS):
            acc = jnp.broadcast_to(bias_row[:, lanes], (CONV_ROWS, LANES))
            for k in range(n_taps):
                acc = acc + w[k:k + 1, lanes] * buf[s, pl.ds(c + first + k, CONV_ROWS), :]
            chunks.append(acc)
        cols.append(jnp.concatenate(chunks, axis=0))
    return jnp.concatenate(cols, axis=1)


def _linear_recurrence(a, b, a_s, b_s, hcar):
    ts = a.shape[0]
    blk = ts // SUBLANES
    pitch = blk + 1
    cols = []
    for j in range(a_s.shape[0]):
        lanes = slice(j * LANES, (j + 1) * LANES)
        for k in range(SUBLANES):
            a_s[j, pl.ds(k * pitch, blk), :] = a[k * blk:(k + 1) * blk, lanes]
            b_s[j, pl.ds(k * pitch, blk), :] = b[k * blk:(k + 1) * blk, lanes]
        h = jnp.zeros((SUBLANES, LANES), F32)
        p = jnp.ones((SUBLANES, LANES), F32)
        for i in range(blk):
            row_i = pl.ds(i, SUBLANES, stride=pitch)
            a_i = a_s[j, row_i, :]
            h = a_i * h + b_s[j, row_i, :]
            p = a_i * p
            b_s[j, row_i, :] = h
            a_s[j, row_i, :] = p
        state = hcar[:, lanes]
        entering = []
        for k in range(SUBLANES):
            entering.append(state)
            state = p[k:k + 1, :] * state + h[k:k + 1, :]
        hcar[:, lanes] = state
        h_in = jnp.concatenate(entering, axis=0)
        for i in range(blk):
            row_i = pl.ds(i, SUBLANES, stride=pitch)
            b_s[j, row_i, :] = b_s[j, row_i, :] + a_s[j, row_i, :] * h_in
        cols.append(jnp.concatenate([b_s[j, pl.ds(k * pitch, blk), :] for k in range(SUBLANES)], axis=0))
    return jnp.concatenate(cols, axis=1)


def _mix_kernel(x_ref, xn_ref, mod_ref, modn_ref, g1_ref, win_ref, cw_ref, cb_ref, lng_ref, lnb_ref,
                lcw_ref, lcb_ref, wg_ref, bg_ref, lam_ref, mng_ref, wout_ref, g2_ref,
                wrt_ref, br_ref, anchor_ref,
                x1_ref, h2p_ref, idx_ref, rank_ref, gt_ref, cnt_ref,
                u_a, u_b, vbuf, rbuf, a_s, b_s, hcar, cnt_s,
                *, tile, n_tiles, steps_per_seq, d_conv, d_lru, conv_w, lru_cw, n_exp):
    del anchor_ref
    s = pl.program_id(0)

    @pl.when(s % steps_per_seq == 0)
    def _():
        _slab_store(vbuf, 0, jnp.zeros((CONV_PAD, d_conv), F32))
        _slab_store(rbuf, 0, jnp.zeros((SUBLANES, d_lru), F32))
        hcar[...] = jnp.zeros_like(hcar)

    mod = mod_ref[...]
    modn = modn_ref[...]
    shift1, gate1, shift2 = mod[0:1], mod[2:3], mod[3:4]
    gain1 = g1_ref[...] * (1.0 + mod[1:2])
    gain2 = g2_ref[...] * (1.0 + mod[4:5])
    lam = lam_ref[...]
    softplus_neg_lam = jnp.maximum(-lam, 0.0) + jnp.log1p(jnp.exp(-jnp.abs(lam)))

    def project(src_ref, r0, gain, shift, u_ref):
        h = _rms(src_ref[pl.ds(r0, tile), :]) * gain + shift
        u_ref[...] = jnp.dot(h.astype(BF16), win_ref[...], preferred_element_type=F32)

    def conv_group(r0, u_ref):
        return _mix_conv_group(r0, tile, u_ref, cw_ref, cb_ref, lng_ref, lnb_ref, vbuf, d_conv=d_conv,
                               conv_w=conv_w)

    def finish(r0, u_ref, yc):
        _mix_finish(r0, tile, x_ref, u_ref, yc, gate1, shift2, gain2, softplus_neg_lam,
                    lcw_ref, lcb_ref, wg_ref, bg_ref, mng_ref, wout_ref,
                    wrt_ref, br_ref, x1_ref, h2p_ref, idx_ref, rank_ref, gt_ref, rbuf, a_s, b_s, hcar,
                    cnt_s, d_conv=d_conv, d_lru=d_lru, lru_cw=lru_cw, n_exp=n_exp)

    @pl.when(s == 0)
    def _():
        cnt_s[...] = jnp.zeros_like(cnt_s)
        project(x_ref, 0, gain1, shift1, u_a)

    bufs = (u_a, u_b)
    for q in range(n_tiles):
        cur, nxt = bufs[q % 2], bufs[(q + 1) % 2]
        yc = conv_group(q * tile, cur)
        if q + 1 < n_tiles:
            project(x_ref, (q + 1) * tile, gain1, shift1, nxt)
        else:
            project(xn_ref, 0, g1_ref[...] * (1.0 + modn[1:2]), modn[0:1], nxt)
        finish(q * tile, cur, yc)
    _slab_keep_tail(vbuf, CONV_PAD, n_tiles * tile)
    _slab_keep_tail(rbuf, SUBLANES, n_tiles * tile)
    cnt_ref[...] = cnt_s[:, :LANES]


def _mix_conv_group(r0, ts, u_ref, cw_ref, cb_ref, lng_ref, lnb_ref, vbuf, *, d_conv, conv_w):
    v = u_ref[:, :d_conv] * _sigmoid(u_ref[:, d_conv:2 * d_conv])
    _slab_store(vbuf, CONV_PAD + r0, v)
    acc = _causal_tap_sum(vbuf, cw_ref, cb_ref[...], r0 + CONV_PAD - (conv_w - 1), conv_w, ts)
    mu = jnp.mean(acc, axis=-1, keepdims=True)
    cen = acc - mu
    var = jnp.mean(cen * cen, axis=-1, keepdims=True)
    yc = cen * lax.rsqrt(var + EPS) * lng_ref[...] + lnb_ref[...]
    return yc * _sigmoid(yc)


def _mix_finish(r0, ts, x_ref, u_ref, yc, gate1, shift2, gain2, softplus_neg_lam,
                lcw_ref, lcb_ref, wg_ref, bg_ref, mng_ref, wout_ref,
                wrt_ref, br_ref, x1_ref, h2p_ref, idx_ref, rank_ref, gt_ref, rbuf, a_s, b_s, hcar,
                cnt_s, *, d_conv, d_lru, lru_cw, n_exp):
    rows = pl.ds(r0, ts)
    x = x_ref[rows, :]
    u = u_ref

    u_gate = u[:, 2 * d_conv:2 * d_conv + d_lru]
    _slab_store(rbuf, SUBLANES + r0, u[:, 2 * d_conv + d_lru:])
    xr = _causal_tap_sum(rbuf, lcw_ref, lcb_ref[...], r0 + SUBLANES - (lru_cw - 1), lru_cw, ts)
    gates = jnp.dot(xr.astype(BF16), wg_ref[...], preferred_element_type=F32) + bg_ref[...]
    r = _sigmoid(gates[:, :d_lru])
    i_g = _sigmoid(gates[:, d_lru:])
    log_a = (-LRU_C) * r * softplus_neg_lam
    a = jnp.exp(log_a)
    inp = jnp.sqrt(1.0 - jnp.exp(2.0 * log_a)) * (i_g * xr)

    yl = _gelu_tanh(u_gate) * _linear_recurrence(a, inp, a_s, b_s, hcar)

    mng = mng_ref[...]
    yc_n = _rms(yc) * mng[:, :d_conv]
    yl_n = _rms(yl) * mng[:, d_conv:]
    mixo = jnp.dot(yc_n.astype(BF16), wout_ref[pl.ds(0, d_conv), :], preferred_element_type=F32)
    mixo += jnp.dot(yl_n.astype(BF16), wout_ref[pl.ds(d_conv, d_lru), :], preferred_element_type=F32)
    x1 = x + gate1 * mixo
    x1_ref[rows, :] = x1

    h2 = _rms(x1) * gain2 + shift2
    dh = h2.shape[1] // 2
    h2_hi = h2.astype(BF16)
    _store_planes(h2p_ref, _pack_bf16_pair(h2[:, :dh], h2[:, dh:]), rows)
    h2_lo = (h2 - h2_hi.astype(F32)).astype(BF16)
    nt_dims = (((1,), (1,)), ((), ()))
    wrt = wrt_ref[...]
    lg = lax.dot_general(wrt, h2_hi, nt_dims, preferred_element_type=F32)
    lg2 = lax.dot_general(wrt[:n_exp], h2_lo, nt_dims, preferred_element_type=F32)
    logits = lg[:n_exp] + lg[n_exp:] + lg2 + br_ref[...]

    eidx = lax.broadcasted_iota(I32, (n_exp, ts), 0)
    neg_inf = jnp.float32(-jnp.inf)
    work = logits
    vals, idxs, hots = [], [], []
    for _ in range(TOP_K):
        mval = jnp.max(work, axis=0, keepdims=True)
        midx = jnp.min(jnp.where(work == mval, eidx, n_exp), axis=0, keepdims=True)
        hot = eidx == midx
        vals.append(mval)
        idxs.append(midx)
        hots.append(hot)
        work = jnp.where(hot, neg_inf, work)
    exps = [jnp.exp(vk - vals[0]) for vk in vals]
    denom = exps[0] + exps[1] + exps[2] + exps[3]
    gate_rows = [ek / denom for ek in exps]

    sel = jnp.zeros((n_exp, ts), F32)
    for hot in hots:
        sel = sel + hot.astype(F32)
    tri = (lax.broadcasted_iota(I32, (ts, ts), 0) < lax.broadcasted_iota(I32, (ts, ts), 1)).astype(BF16)
    before = jnp.dot(sel.astype(BF16), tri, preferred_element_type=F32) + cnt_s[...]
    rank_rows = [jnp.sum(jnp.where(hot, before, 0.0), axis=0, keepdims=True) for hot in hots]
    cnt_s[...] = cnt_s[...] + jnp.sum(sel, axis=1, keepdims=True)

    idx_ref[:, rows] = jnp.concatenate(idxs, axis=0)
    rank_ref[:, rows] = jnp.concatenate(rank_rows, axis=0).astype(I32)
    g4 = jnp.concatenate(gate_rows, axis=0)
    g_pad = jnp.concatenate([g4, jnp.zeros((LANES - TOP_K, ts), F32)], axis=0)
    gt_ref[rows, :] = g_pad.T


def _mix_call(x, mod, p, b0, bsz, anchor):
    _, seq, dm = x.shape
    n_tiles = TILES_PER_STEP
    tile = min(SEQ_TILE, seq // n_tiles)
    ts = n_tiles * tile
    nj = seq // ts
    n_steps = bsz * nj
    tiles_per_seq = seq // tile
    d_conv = p["conv_w"].shape[1]
    d_lru = p["lru_conv_w"].shape[1]
    conv_w = p["conv_w"].shape[0]
    lru_cw = p["lru_conv_w"].shape[0]
    n_exp = p["wrt"].shape[0] // 2
    assert seq % ts == 0 and tile % LANES == 0 and tile % CONV_ROWS == 0 and n_tiles % 2 == 0
    assert conv_w - 1 <= CONV_PAD and lru_cw - 1 <= SUBLANES and d_conv % LANES == 0 and d_lru % LANES == 0

    def full(a):
        return pl.BlockSpec(a.shape, lambda s: (0,) * a.ndim)

    def next_tile(s):
        return jnp.minimum(n_tiles * (s + 1), bsz * tiles_per_seq - 1)

    weights = [p["g1"], p["win"], p["conv_w"], p["conv_b"], p["ln_g"], p["ln_b"], p["lru_conv_w"],
               p["lru_conv_b"], p["wg"], p["bg"], p["lam"], p["mng"], p["wout"], p["g2"], p["wrt"], p["br"]]
    kern = functools.partial(_mix_kernel, tile=tile, n_tiles=n_tiles, steps_per_seq=nj, d_conv=d_conv, d_lru=d_lru,
                             conv_w=conv_w, lru_cw=lru_cw, n_exp=n_exp)
    out_shape = (
        jax.ShapeDtypeStruct((bsz, seq, dm), F32),
        jax.ShapeDtypeStruct((SC_SPLIT, bsz * seq, dm // 2 // SC_SPLIT), I32),
        jax.ShapeDtypeStruct((TOP_K, bsz * seq), I32),
        jax.ShapeDtypeStruct((TOP_K, bsz * seq), I32),
        jax.ShapeDtypeStruct((bsz, seq, LANES), F32),
        jax.ShapeDtypeStruct((n_exp, LANES), F32),
    )
    return pl.pallas_call(
        kern,
        grid=(n_steps,),
        in_specs=[pl.BlockSpec((None, ts, dm), lambda s: (s // nj + b0, s % nj, 0)),
                  pl.BlockSpec((None, tile, dm),
                               lambda s: (next_tile(s) // tiles_per_seq + b0, next_tile(s) % tiles_per_seq, 0)),
                  pl.BlockSpec((None, N_MOD, dm), lambda s: (s // nj + b0, 0, 0)),
                  pl.BlockSpec((None, N_MOD, dm), lambda s: (next_tile(s) // tiles_per_seq + b0, 0, 0))]
        + [full(w) for w in weights] + [pl.BlockSpec(memory_space=pl.ANY)],
        out_specs=(
            pl.BlockSpec((None, ts, dm), lambda s: (s // nj, s % nj, 0)),
            pl.BlockSpec((SC_SPLIT, ts, dm // 2 // SC_SPLIT), lambda s: (0, s, 0)),
            pl.BlockSpec((TOP_K, ts), lambda s: (0, s)),
            pl.BlockSpec((TOP_K, ts), lambda s: (0, s)),
            pl.BlockSpec((None, ts, LANES), lambda s: (s // nj, s % nj, 0)),
            pl.BlockSpec((n_exp, LANES), lambda s: (0, 0)),
        ),
        out_shape=out_shape,
        scratch_shapes=[
            pltpu.VMEM((tile, 2 * d_conv + 2 * d_lru), F32),
            pltpu.VMEM((tile, 2 * d_conv + 2 * d_lru), F32),
            pltpu.VMEM((d_conv // LANES, CONV_PAD + ts, LANES), F32),
            pltpu.VMEM((d_lru // LANES, SUBLANES + ts, LANES), F32),
            pltpu.VMEM((d_lru // LANES, tile + SUBLANES, LANES), F32),
            pltpu.VMEM((d_lru // LANES, tile + SUBLANES, LANES), F32),
            pltpu.VMEM((1, d_lru), F32),
            pltpu.VMEM((n_exp, tile), F32),
        ],
        compiler_params=pltpu.CompilerParams(dimension_semantics=("arbitrary",),
                                             vmem_limit_bytes=VMEM_LIMIT_BYTES),
        name="token_mix_route",
    )(x, x, mod, mod, *weights, anchor)


def _sc_mesh():
    return plsc.VectorSubcoreMesh(core_axis_name="core", subcore_axis_name="subcore",
                                  num_cores=SC_CORES, num_subcores=SC_SUBCORES)


def _sc_scatter_rows(rows, idx, n_out):
    n, w = rows.shape
    n_k = idx.shape[0]
    assert n % (SC_WINDOW * SC_CORES * SC_SUBCORES) == 0

    @functools.partial(pl.kernel, out_type=jax.ShapeDtypeStruct((n_out, w), rows.dtype), mesh=_sc_mesh(),
                       scratch_types=[], name="moe_dispatch_sc")
    def scatter(x_hbm, i_hbm, o_hbm):
        def body(x_vmem, *i_vmems):
            for i_vmem in i_vmems:
                pltpu.sync_copy(x_vmem, o_hbm.at[i_vmem.at[0]])

        pltpu.emit_pipeline(
            body,
            grid=(n // SC_WINDOW,),
            in_specs=[pl.BlockSpec((SC_WINDOW, w), lambda i: (i, 0))]
            + [pl.BlockSpec((1, SC_WINDOW), functools.partial(lambda k, i: (k, i), k)) for k in range(n_k)],
            out_specs=[],
            core_axis_name=("core", "subcore"),
            dimension_semantics=(pltpu.PARALLEL,),
        )(x_hbm, *([i_hbm] * n_k))

    return scatter(rows, idx)


def _sc_gather_rows(table, idx):
    m = idx.shape[1]
    w = table.shape[1]
    assert m % (SC_WINDOW * SC_CORES * SC_SUBCORES) == 0

    @functools.partial(pl.kernel, out_type=jax.ShapeDtypeStruct((m, w), table.dtype), mesh=_sc_mesh(),
                       scratch_types=[], name="moe_gather_sc")
    def gather(x_hbm, i_hbm, o_hbm):
        def body(i_vmem, o_vmem):
            pltpu.sync_copy(x_hbm.at[i_vmem.at[0]], o_vmem)

        pltpu.emit_pipeline(
            body,
            grid=(m // SC_WINDOW,),
            in_specs=[pl.BlockSpec((1, SC_WINDOW), lambda i: (0, i))],
            out_specs=[pl.BlockSpec((SC_WINDOW, w), lambda i: (i, 0))],
            core_axis_name=("core", "subcore"),
            dimension_semantics=(pltpu.PARALLEL,),
        )(i_hbm, o_hbm)

    return gather(table, idx)


def _moe_kernel(be_ref, nb_ref, slot_ref, nxt_ref, nv_ref, xs_ref, wgu_hbm, bgu_ref, wd_hbm, bd_ref, ys_ref,
                wgu_f, wd_f, wgu_b, wd_b, sems, *, d_ff):
    i = pl.program_id(0)
    live = i < nb_ref[0]
    new_expert = (i == 0) | (be_ref[i] != be_ref[jnp.maximum(i - 1, 0)])

    def weight_copies(e, s):
        return (pltpu.make_async_copy(wgu_hbm.at[e], wgu_f.at[s], sems.at[0, s]),
                pltpu.make_async_copy(wd_hbm.at[e], wd_f.at[s], sems.at[1, s]))

    @pl.when(live & new_expert)
    def _():
        e, s, nx = be_ref[i], slot_ref[i], nxt_ref[i]

        @pl.when(i == 0)
        def _():
            for cp in weight_copies(e, s):
                cp.start()

        for cp in weight_copies(e, s):
            cp.wait()

        @pl.when(nx >= 0)
        def _():
            for cp in weight_copies(nx, 1 - s):
                cp.start()

        wgu_b[...] = wgu_f[s].astype(BF16)
        wd_b[...] = wd_f[s].astype(BF16)

    def sub_block(s):
        rows = pl.ds(s * MOE_SUB, MOE_SUB)
        packed = jnp.concatenate([xs_ref[p, rows, :] for p in range(SC_SPLIT)], axis=1)
        x_lo, x_hi = _unpack_bf16_pair(packed)
        dh = x_lo.shape[1]
        gu = jnp.dot(x_lo.astype(BF16), wgu_b[pl.ds(0, dh), :], preferred_element_type=F32)
        gu += jnp.dot(x_hi.astype(BF16), wgu_b[pl.ds(dh, dh), :], preferred_element_type=F32)
        gu += bgu_ref[...]
        g = jnp.minimum(gu[:, :d_ff], SWIGLU_LIMIT)
        u = jnp.clip(gu[:, d_ff:], -SWIGLU_LIMIT, SWIGLU_LIMIT)
        act = (u + 1.0) * (g * _sigmoid(SWIGLU_ALPHA * g))
        y = jnp.dot(act.astype(BF16), wd_b[...], preferred_element_type=F32) + bd_ref[...]
        dm2 = y.shape[1] // 2
        _store_planes(ys_ref, _pack_bf16_pair(y[:, :dm2], y[:, dm2:]), rows)

    n_sub = xs_ref.shape[1] // MOE_SUB
    n_valid = nv_ref[i]

    @pl.when(live & (n_valid == n_sub * MOE_SUB))
    def _():
        for s in range(n_sub):
            sub_block(s)

    @pl.when(live & (n_valid < n_sub * MOE_SUB))
    def _():
        for s in range(n_sub):
            pl.when(s * MOE_SUB < n_valid)(functools.partial(sub_block, s))


def _moe_block_rows(n_pairs, n_exp):
    return max([b for b in MOE_BLOCKS if b * n_exp <= n_pairs] or [min(MOE_BLOCKS)])


def _moe_call(xs, bm, block_expert, n_used, seg_slot, next_expert, block_valid, wgu, bgu, wd, bd):
    _, n_rows, wp = xs.shape
    n_exp, dm, d_ff2 = wgu.shape
    d_ff = d_ff2 // 2
    n_blocks = n_rows // bm

    def row_map(i, be, nb, slot, nxt, nv):
        return (0, jnp.minimum(i, nb[0] - 1), 0)

    def b_map(i, be, nb, slot, nxt, nv):
        return (be[jnp.minimum(i, nb[0] - 1)], 0, 0)

    grid_spec = pltpu.PrefetchScalarGridSpec(
        num_scalar_prefetch=5,
        grid=(n_blocks,),
        in_specs=[
            pl.BlockSpec((SC_SPLIT, bm, wp), row_map),
            pl.BlockSpec(memory_space=pl.ANY),
            pl.BlockSpec((None, 1, d_ff2), b_map),
            pl.BlockSpec(memory_space=pl.ANY),
            pl.BlockSpec((None, 1, dm), b_map),
        ],
        out_specs=pl.BlockSpec((SC_SPLIT, bm, wp), row_map),
        scratch_shapes=[
            pltpu.VMEM((2, dm, d_ff2), F32), pltpu.VMEM((2, d_ff, dm), F32),
            pltpu.VMEM((dm, d_ff2), BF16), pltpu.VMEM((d_ff, dm), BF16),
            pltpu.SemaphoreType.DMA((2, 2)),
        ],
    )
    return pl.pallas_call(
        functools.partial(_moe_kernel, d_ff=d_ff),
        grid_spec=grid_spec,
        out_shape=jax.ShapeDtypeStruct(xs.shape, I32),
        compiler_params=pltpu.CompilerParams(dimension_semantics=("arbitrary",),
                                             vmem_limit_bytes=VMEM_LIMIT_BYTES),
        name="moe_experts",
    )(block_expert, n_used, seg_slot, next_expert, block_valid, xs, wgu, bgu, wd, bd)


def _combine_kernel(x1_ref, gt_ref, mod_ref, gf_ref, yg_ref, *out_refs):
    o_ref = out_refs[-1]
    gt = gt_ref[...]
    ts = yg_ref.shape[2]
    dh = yg_ref.shape[3] * SC_SPLIT
    acc_lo = jnp.zeros((ts, dh), F32)
    acc_hi = jnp.zeros((ts, dh), F32)
    for k in range(TOP_K):
        lo, hi = _unpack_bf16_pair(_load_planes(yg_ref.at[k]))
        gk = gt[:, k:k + 1]
        acc_lo += gk * lo
        acc_hi += gk * hi
    y = jnp.concatenate([acc_lo, acc_hi], axis=1)
    gate2 = mod_ref[...][5:6]
    x2 = x1_ref[...] + gate2 * y
    o_ref[...] = _rms(x2) * gf_ref[...]


def _combine_call(x1, gt, mod, gf, yg, out_prev, b0):
    bsz, seq, dm = x1.shape
    ts = min(COMBINE_TILE, seq)
    assert seq % ts == 0
    nj = seq // ts
    in_specs = [
        pl.BlockSpec((None, ts, dm), lambda b, j: (b, j, 0)),
        pl.BlockSpec((None, ts, LANES), lambda b, j: (b, j, 0)),
        pl.BlockSpec((None, N_MOD, dm), lambda b, j: (b + b0, 0, 0)),
        pl.BlockSpec((1, dm), lambda b, j: (0, 0)),
        pl.BlockSpec((TOP_K, SC_SPLIT, ts, yg.shape[3]), lambda b, j: (0, 0, b * nj + j, 0)),
    ]
    args = [x1, gt, mod, gf, yg]
    aliases = {}
    if out_prev is not None:
        in_specs.append(pl.BlockSpec(memory_space=pl.ANY))
        args.append(out_prev)
        aliases = {len(args) - 1: 0}
    return pl.pallas_call(
        _combine_kernel,
        grid=(bsz, nj),
        in_specs=in_specs,
        out_specs=pl.BlockSpec((None, ts, dm), lambda b, j: (b + b0, j, 0)),
        out_shape=jax.ShapeDtypeStruct((mod.shape[0], seq, dm), F32),
        input_output_aliases=aliases,
        compiler_params=pltpu.CompilerParams(dimension_semantics=("arbitrary", "arbitrary"),
                                             vmem_limit_bytes=VMEM_LIMIT_BYTES),
        name="moe_combine",
    )(*args)


def _block_diag(w):
    n_h, d, _ = w.shape
    eye = jnp.eye(n_h, dtype=w.dtype)
    return (eye[:, None, :, None] * w[:, :, None, :]).reshape(n_h * d, n_h * d)


def _layer(x, mod, l, w_in, norm1_g, conv_w, conv_b, conv_ln_g, conv_ln_b, lru_conv_w, lru_conv_b,
           lru_w_a, lru_b_a, lru_w_x, lru_b_x, lru_lambda, mix_norm_g, w_out, norm2_g, w_router,
           b_router, w_gate_up, b_gate_up, w_down, b_down, out_gain):
    bsz, seq, dm = x.shape
    n_tok = bsz * seq
    n_exp = w_router.shape[-1]
    row = lambda a: a.reshape(1, -1)
    wr = w_router[l]
    wr_hi = wr.astype(BF16)
    wr_lo = (wr - wr_hi.astype(F32)).astype(BF16)
    params = dict(
        g1=row(norm1_g[l]), win=w_in[l].astype(BF16), conv_w=conv_w[l], conv_b=row(conv_b[l]),
        ln_g=row(conv_ln_g[l]), ln_b=row(conv_ln_b[l]), lru_conv_w=lru_conv_w[l], lru_conv_b=row(lru_conv_b[l]),
        wg=jnp.concatenate([_block_diag(lru_w_a[l]), _block_diag(lru_w_x[l])], axis=1).astype(BF16),
        bg=jnp.concatenate([lru_b_a[l].reshape(1, -1), lru_b_x[l].reshape(1, -1)], axis=1),
        lam=row(lru_lambda[l]), mng=row(mix_norm_g[l]), wout=w_out[l].astype(BF16), g2=row(norm2_g[l]),
        wrt=jnp.concatenate([wr_hi.T, wr_lo.T], axis=0), br=b_router[l].reshape(n_exp, 1),
    )
    sizes = [s for s in CHUNK_BATCHES if s > 0] if sum(CHUNK_BATCHES) == bsz else [bsz]
    routed, b0, anchor = [], 0, mod
    for cb in sizes:
        r = _route_chunk(x, mod, params, b0, cb, anchor, n_exp)
        routed.append(r)
        anchor = r["dest_sub"]
        b0 += cb
    out = None
    for r in routed:
        out = _experts_chunk(r, mod, out, w_gate_up[l], b_gate_up[l][:, None, :], w_down[l],
                             b_down[l][:, None, :], out_gain)
    return out


def _route_chunk(x, mod, params, b0, cb, anchor, n_exp):
    seq, dm = x.shape[1:]
    n_tok = cb * seq
    x1, h2p, top_idx, rank, gt, cnt = _mix_call(x, mod, params, b0, cb, anchor)

    bm = _moe_block_rows(n_tok * TOP_K, n_exp)
    counts = cnt[:, 0].astype(I32)
    padded = (counts + bm - 1) // bm * bm
    e_ids = jnp.arange(n_exp, dtype=I32)
    pad_ends = jnp.sum(jnp.where(e_ids[None, :] <= e_ids[:, None], padded[None, :], 0), axis=1)
    pad_starts = pad_ends - padded
    n_rows = (n_tok * TOP_K // bm + n_exp) * bm
    n_blocks = n_rows // bm
    dest = rank + jnp.sum(jnp.where(top_idx[..., None] == e_ids, pad_starts, 0), axis=-1)
    block_start = jnp.arange(n_blocks, dtype=I32) * bm
    block_expert = jnp.minimum(
        jnp.sum((block_start[:, None] >= pad_ends[None, :]).astype(I32), axis=1), n_exp - 1)
    n_used = pad_ends[-1:] // bm
    present = counts > 0
    seg_ordinal = jnp.sum(jnp.where((e_ids[None, :] < e_ids[:, None]) & present[None, :], 1, 0), axis=1)
    following = jnp.min(jnp.where((e_ids[None, :] > e_ids[:, None]) & present[None, :], e_ids[None, :], n_exp),
                        axis=1)
    following = jnp.where(following == n_exp, -1, following)
    block_hot = block_expert[:, None] == e_ids[None, :]
    seg_slot = jnp.sum(jnp.where(block_hot, seg_ordinal[None, :] % 2, 0), axis=1)
    next_expert = jnp.sum(jnp.where(block_hot, following[None, :], 0), axis=1)
    block_valid = jnp.clip(
        jnp.sum(jnp.where(block_hot, (pad_starts + counts)[None, :], 0), axis=1) - block_start, 0, bm)

    wp = h2p.shape[2]
    plane = jnp.arange(SC_SPLIT, dtype=I32)[None, :, None] * n_rows
    dest_sub = dest[:, None, :] + plane
    return dict(b0=b0, n_tok=n_tok, n_rows=n_rows, x1=x1, h2p=h2p, gt=gt, dest_sub=dest_sub,
                tables=(bm, block_expert, n_used, seg_slot, next_expert, block_valid))


def _experts_chunk(r, mod, out_prev, wgu, bgu, wd, bd, out_gain):
    n_tok, n_rows, dest_sub = r["n_tok"], r["n_rows"], r["dest_sub"]
    wp = r["h2p"].shape[2]
    xs = _sc_scatter_rows(r["h2p"].reshape(SC_SPLIT * n_tok, wp), dest_sub.reshape(TOP_K, SC_SPLIT * n_tok),
                          SC_SPLIT * n_rows).reshape(SC_SPLIT, n_rows, wp)
    ys = _moe_call(xs, *r["tables"], wgu, bgu, wd, bd)
    yg = _sc_gather_rows(ys.reshape(SC_SPLIT * n_rows, wp), dest_sub.reshape(1, TOP_K * SC_SPLIT * n_tok))
    return _combine_call(r["x1"], r["gt"], mod, out_gain, yg.reshape(TOP_K, SC_SPLIT, n_tok, wp), out_prev,
                         r["b0"])


def kernel(x, c, w_ada, b_ada, norm1_g, w_in, conv_w, conv_b, conv_ln_g, conv_ln_b, lru_conv_w, lru_conv_b,
           lru_w_a, lru_b_a, lru_w_x, lru_b_x, lru_lambda, mix_norm_g, w_out, norm2_g, w_router, b_router,
           w_gate_up, b_gate_up, w_down, b_down, final_norm_g):
    depth = w_ada.shape[0]
    assert depth == 1, "the final norm is fused into the (single) layer's combine kernel"
    bsz, seq, dm = x.shape
    mod = _ada_call(c, w_ada[0], b_ada[0]).reshape(bsz, N_MOD, dm)
    return _layer(x, mod, 0, w_in, norm1_g, conv_w, conv_b, conv_ln_g, conv_ln_b, lru_conv_w, lru_conv_b,
                  lru_w_a, lru_b_a, lru_w_x, lru_b_x, lru_lambda, mix_norm_g, w_out, norm2_g, w_router,
                  b_router, w_gate_up, b_gate_up, w_down, b_down, row_gain(final_norm_g))


def row_gain(g):
    return g.reshape(1, -1)
```

```python
import functools

import jax
import jax.numpy as jnp
from jax import lax
from jax.experimental import pallas as pl
from jax.experimental.pallas import tpu as pltpu
from jax.experimental.pallas import tpu_sc as plsc

F32 = jnp.float32
BF16 = jnp.bfloat16
I32 = jnp.int32

EPS = 1e-6
N_MOD = 6
LRU_C = 8.0
TOP_K = 4
SWIGLU_ALPHA = 1.702
SWIGLU_LIMIT = 7.0

LANES = 128
SUBLANES = 8
VMEM_LIMIT_BYTES = 56 * 1024 * 1024

SEQ_TILE = 512
TILES_PER_STEP = 2
COMBINE_TILE = 1024
CONV_PAD = 32
CONV_ROWS = 128
MOE_BLOCK = 1024
MOE_SUB = 256
CHUNK_BATCHES = (5, 3)
HI_MASK = -65536

SC_CORES = 2
SC_SUBCORES = 16
SC_WINDOW = 128
SC_SPLIT = 2


def _sigmoid(x):
    return 0.5 * jnp.tanh(0.5 * x) + 0.5


def _pack_bf16_pair(lo_f32, hi_f32):
    lo_bits = lax.bitcast_convert_type(lo_f32.astype(BF16).astype(F32), I32)
    hi_bits = lax.bitcast_convert_type(hi_f32.astype(BF16).astype(F32), I32)
    return lax.shift_right_logical(lo_bits, 16) | hi_bits


def _unpack_bf16_pair(p):
    lo = lax.bitcast_convert_type(lax.shift_left(p, 16), F32)
    hi = lax.bitcast_convert_type(p & HI_MASK, F32)
    return lo, hi


def _store_planes(ref, packed, rows=slice(None)):
    wp = packed.shape[1] // SC_SPLIT
    for s in range(SC_SPLIT):
        ref[s, rows, :] = packed[:, s * wp:(s + 1) * wp]


def _load_planes(ref):
    return jnp.concatenate([ref[s] for s in range(SC_SPLIT)], axis=1)


def _ada_kernel(c_ref, w_ref, b_ref, o_ref):
    c = c_ref[...]
    ca = c * _sigmoid(c)
    w = w_ref[...]
    c_hi = ca.astype(BF16)
    c_lo = (ca - c_hi.astype(F32)).astype(BF16)
    w_hi = w.astype(BF16)
    w_lo = (w - w_hi.astype(F32)).astype(BF16)
    acc = jnp.dot(c_hi, w_hi, preferred_element_type=F32)
    acc += jnp.dot(c_lo, w_hi, preferred_element_type=F32)
    acc += jnp.dot(c_hi, w_lo, preferred_element_type=F32)
    o_ref[...] = acc + b_ref[...]


def _ada_call(c, w_ada, b_ada):
    bsz, dm = c.shape
    n_out = w_ada.shape[1]
    tn = 1024
    return pl.pallas_call(
        _ada_kernel,
        grid=(n_out // tn,),
        in_specs=[
            pl.BlockSpec((bsz, dm), lambda n: (0, 0)),
            pl.BlockSpec((dm, tn), lambda n: (0, n)),
            pl.BlockSpec((1, tn), lambda n: (0, n)),
        ],
        out_specs=pl.BlockSpec((bsz, tn), lambda n: (0, n)),
        out_shape=jax.ShapeDtypeStruct((bsz, n_out), F32),
        compiler_params=pltpu.CompilerParams(dimension_semantics=("arbitrary",)),
        name="ada_mod",
    )(c, w_ada, b_ada.reshape(1, n_out))


def _rms(x, eps=EPS):
    return x * lax.rsqrt(jnp.mean(x * x, axis=-1, keepdims=True) + eps)


def _gelu_tanh(x):
    return 0.5 * x * (1.0 + jnp.tanh(0.7978845608028654 * (x + 0.044715 * (x * x * x))))


def _slab_store(buf, row0, val):
    for s in range(buf.shape[0]):
        buf[s, pl.ds(row0, val.shape[0]), :] = val[:, s * LANES:(s + 1) * LANES]


def _slab_keep_tail(buf, keep, ts):
    for s in range(buf.shape[0]):
        buf[s, pl.ds(0, keep), :] = buf[s, pl.ds(ts, keep), :]


def _causal_tap_sum(buf, w_ref, bias_row, first, n_taps, ts):
    w = w_ref[...]
    cols = []
    for s in range(buf.shape[0]):
        lanes = slice(s * LANES, (s + 1) * LANES)
        chunks = []
        for c in range(0, ts, CONV_ROWS):
            acc = jnp.broadcast_to(bias_row[:, lanes], (CONV_ROWS, LANES))
            for k in range(n_taps):
                acc = acc + w[k:k + 1, lanes] * buf[s, pl.ds(c + first + k, CONV_ROWS), :]
            chunks.append(acc)
        cols.append(jnp.concatenate(chunks, axis=0))
    return jnp.concatenate(cols, axis=1)


def _linear_recurrence(a, b, a_s, b_s, hcar):
    ts = a.shape[0]
    blk = ts // SUBLANES
    pitch = blk + 1
    cols = []
    for j in range(a_s.shape[0]):
        lanes = slice(j * LANES, (j + 1) * LANES)
        for k in range(SUBLANES):
            a_s[j, pl.ds(k * pitch, blk), :] = a[k * blk:(k + 1) * blk, lanes]
            b_s[j, pl.ds(k * pitch, blk), :] = b[k * blk:(k + 1) * blk, lanes]
        h = jnp.zeros((SUBLANES, LANES), F32)
        p = jnp.ones((SUBLANES, LANES), F32)
        for i in range(blk):
            row_i = pl.ds(i, SUBLANES, stride=pitch)
            a_i = a_s[j, row_i, :]
            h = a_i * h + b_s[j, row_i, :]
            p = a_i * p
            b_s[j, row_i, :] = h
            a_s[j, row_i, :] = p
        state = hcar[:, lanes]
        entering = []
        for k in range(SUBLANES):
            entering.append(state)
            state = p[k:k + 1, :] * state + h[k:k + 1, :]
        hcar[:, lanes] = state
        h_in = jnp.concatenate(entering, axis=0)
        for i in range(blk):
            row_i = pl.ds(i, SUBLANES, stride=pitch)
            b_s[j, row_i, :] = b_s[j, row_i, :] + a_s[j, row_i, :] * h_in
        cols.append(jnp.concatenate([b_s[j, pl.ds(k * pitch, blk), :] for k in range(SUBLANES)], axis=0))
    return jnp.concatenate(cols, axis=1)


def _mix_kernel(x_ref, xn_ref, mod_ref, modn_ref, g1_ref, win_ref, cw_ref, cb_ref, lng_ref, lnb_ref,
                lcw_ref, lcb_ref, wg_ref, bg_ref, lam_ref, mng_ref, wout_ref, g2_ref,
                wrt_ref, br_ref, anchor_ref,
                x1_ref, h2p_ref, idx_ref, rank_ref, gt_ref, cnt_ref,
                u_a, u_b, vbuf, rbuf, a_s, b_s, hcar, cnt_s,
                *, tile, n_tiles, steps_per_seq, d_conv, d_lru, conv_w, lru_cw, n_exp):
    del anchor_ref
    s = pl.program_id(0)

    @pl.when(s % steps_per_seq == 0)
    def _():
        _slab_store(vbuf, 0, jnp.zeros((CONV_PAD, d_conv), F32))
        _slab_store(rbuf, 0, jnp.zeros((SUBLANES, d_lru), F32))
        hcar[...] = jnp.zeros_like(hcar)

    mod = mod_ref[...]
    modn = modn_ref[...]
    shift1, gate1, shift2 = mod[0:1], mod[2:3], mod[3:4]
    gain1 = g1_ref[...] * (1.0 + mod[1:2])
    gain2 = g2_ref[...] * (1.0 + mod[4:5])
    lam = lam_ref[...]
    softplus_neg_lam = jnp.maximum(-lam, 0.0) + jnp.log1p(jnp.exp(-jnp.abs(lam)))

    def project(src_ref, r0, gain, shift, u_ref):
        h = _rms(src_ref[pl.ds(r0, tile), :]) * gain + shift
        u_ref[...] = jnp.dot(h.astype(BF16), win_ref[...], preferred_element_type=F32)

    def conv_group(r0, u_ref):
        return _mix_conv_group(r0, tile, u_ref, cw_ref, cb_ref, lng_ref, lnb_ref, vbuf, d_conv=d_conv,
                               conv_w=conv_w)

    def finish(r0, u_ref, yc):
        _mix_finish(r0, tile, x_ref, u_ref, yc, gate1, shift2, gain2, softplus_neg_lam,
                    lcw_ref, lcb_ref, wg_ref, bg_ref, mng_ref, wout_ref,
                    wrt_ref, br_ref, x1_ref, h2p_ref, idx_ref, rank_ref, gt_ref, rbuf, a_s, b_s, hcar,
                    cnt_s, d_conv=d_conv, d_lru=d_lru, lru_cw=lru_cw, n_exp=n_exp)

    @pl.when(s == 0)
    def _():
        cnt_s[...] = jnp.zeros_like(cnt_s)
        project(x_ref, 0, gain1, shift1, u_a)

    bufs = (u_a, u_b)
    for q in range(n_tiles):
        cur, nxt = bufs[q % 2], bufs[(q + 1) % 2]
        yc = conv_group(q * tile, cur)
        if q + 1 < n_tiles:
            project(x_ref, (q + 1) * tile, gain1, shift1, nxt)
        else:
            project(xn_ref, 0, g1_ref[...] * (1.0 + modn[1:2]), modn[0:1], nxt)
        finish(q * tile, cur, yc)
    _slab_keep_tail(vbuf, CONV_PAD, n_tiles * tile)
    _slab_keep_tail(rbuf, SUBLANES, n_tiles * tile)
    cnt_ref[...] = cnt_s[:, :LANES]


def _mix_conv_group(r0, ts, u_ref, cw_ref, cb_ref, lng_ref, lnb_ref, vbuf, *, d_conv, conv_w):
    v = u_ref[:, :d_conv] * _sigmoid(u_ref[:, d_conv:2 * d_conv])
    _slab_store(vbuf, CONV_PAD + r0, v)
    acc = _causal_tap_sum(vbuf, cw_ref, cb_ref[...], r0 + CONV_PAD - (conv_w - 1), conv_w, ts)
    mu = jnp.mean(acc, axis=-1, keepdims=True)
    cen = acc - mu
    var = jnp.mean(cen * cen, axis=-1, keepdims=True)
    yc = cen * lax.rsqrt(var + EPS) * lng_ref[...] + lnb_ref[...]
    return yc * _sigmoid(yc)


def _mix_finish(r0, ts, x_ref, u_ref, yc, gate1, shift2, gain2, softplus_neg_lam,
                lcw_ref, lcb_ref, wg_ref, bg_ref, mng_ref, wout_ref,
                wrt_ref, br_ref, x1_ref, h2p_ref, idx_ref, rank_ref, gt_ref, rbuf, a_s, b_s, hcar,
                cnt_s, *, d_conv, d_lru, lru_cw, n_exp):
    rows = pl.ds(r0, ts)
    x = x_ref[rows, :]
    u = u_ref

    u_gate = u[:, 2 * d_conv:2 * d_conv + d_lru]
    _slab_store(rbuf, SUBLANES + r0, u[:, 2 * d_conv + d_lru:])
    xr = _causal_tap_sum(rbuf, lcw_ref, lcb_ref[...], r0 + SUBLANES - (lru_cw - 1), lru_cw, ts)
    gates = jnp.dot(xr.astype(BF16), wg_ref[...], preferred_element_type=F32) + bg_ref[...]
    r = _sigmoid(gates[:, :d_lru])
    i_g = _sigmoid(gates[:, d_lru:])
    log_a = (-LRU_C) * r * softplus_neg_lam
    a = jnp.exp(log_a)
    inp = jnp.sqrt(1.0 - jnp.exp(2.0 * log_a)) * (i_g * xr)

    yl = _gelu_tanh(u_gate) * _linear_recurrence(a, inp, a_s, b_s, hcar)

    mng = mng_ref[...]
    yc_n = _rms(yc) * mng[:, :d_conv]
    yl_n = _rms(yl) * mng[:, d_conv:]
    mixo = jnp.dot(yc_n.astype(BF16), wout_ref[pl.ds(0, d_conv), :], preferred_element_type=F32)
    mixo += jnp.dot(yl_n.astype(BF16), wout_ref[pl.ds(d_conv, d_lru), :], preferred_element_type=F32)
    x1 = x + gate1 * mixo
    x1_ref[rows, :] = x1

    h2 = _rms(x1) * gain2 + shift2
    dh = h2.shape[1] // 2
    h2_hi = h2.astype(BF16)
    _store_planes(h2p_ref, _pack_bf16_pair(h2[:, :dh], h2[:, dh:]), rows)
    h2_lo = (h2 - h2_hi.astype(F32)).astype(BF16)
    nt_dims = (((1,), (1,)), ((), ()))
    wrt = wrt_ref[...]
    lg = lax.dot_general(wrt, h2_hi, nt_dims, preferred_element_type=F32)
    lg2 = lax.dot_general(wrt[:n_exp], h2_lo, nt_dims, preferred_element_type=F32)
    logits = lg[:n_exp] + lg[n_exp:] + lg2 + br_ref[...]

    eidx = lax.broadcasted_iota(I32, (n_exp, ts), 0)
    neg_inf = jnp.float32(-jnp.inf)
    work = logits
    vals, idxs, hots = [], [], []
    for _ in range(TOP_K):
        mval = jnp.max(work, axis=0, keepdims=True)
        midx = jnp.min(jnp.where(work == mval, eidx, n_exp), axis=0, keepdims=True)
        hot = eidx == midx
        vals.append(mval)
        idxs.append(midx)
        hots.append(hot)
        work = jnp.where(hot, neg_inf, work)
    exps = [jnp.exp(vk - vals[0]) for vk in vals]
    denom = exps[0] + exps[1] + exps[2] + exps[3]
    gate_rows = [ek / denom for ek in exps]

    sel = jnp.zeros((n_exp, ts), F32)
    for hot in hots:
        sel = sel + hot.astype(F32)
    tri = (lax.broadcasted_iota(I32, (ts, ts), 0) < lax.broadcasted_iota(I32, (ts, ts), 1)).astype(BF16)
    before = jnp.dot(sel.astype(BF16), tri, preferred_element_type=F32) + cnt_s[...]
    rank_rows = [jnp.sum(jnp.where(hot, before, 0.0), axis=0, keepdims=True) for hot in hots]
    cnt_s[...] = cnt_s[...] + jnp.sum(sel, axis=1, keepdims=True)

    idx_ref[:, rows] = jnp.concatenate(idxs, axis=0)
    rank_ref[:, rows] = jnp.concatenate(rank_rows, axis=0).astype(I32)
    g4 = jnp.concatenate(gate_rows, axis=0)
    g_pad = jnp.concatenate([g4, jnp.zeros((LANES - TOP_K, ts), F32)], axis=0)
    gt_ref[rows, :] = g_pad.T


def _mix_call(x, mod, p, b0, bsz, anchor):
    _, seq, dm = x.shape
    n_tiles = TILES_PER_STEP
    tile = min(SEQ_TILE, seq // n_tiles)
    ts = n_tiles * tile
    nj = seq // ts
    n_steps = bsz * nj
    tiles_per_seq = seq // tile
    d_conv = p["conv_w"].shape[1]
    d_lru = p["lru_conv_w"].shape[1]
    conv_w = p["conv_w"].shape[0]
    lru_cw = p["lru_conv_w"].shape[0]
    n_exp = p["wrt"].shape[0] // 2
    assert seq % ts == 0 and tile % LANES == 0 and tile % CONV_ROWS == 0 and n_tiles % 2 == 0
    assert conv_w - 1 <= CONV_PAD and lru_cw - 1 <= SUBLANES and d_conv % LANES == 0 and d_lru % LANES == 0

    def full(a):
        return pl.BlockSpec(a.shape, lambda s: (0,) * a.ndim)

    def next_tile(s):
        return jnp.minimum(n_tiles * (s + 1), bsz * tiles_per_seq - 1)

    weights = [p["g1"], p["win"], p["conv_w"], p["conv_b"], p["ln_g"], p["ln_b"], p["lru_conv_w"],
               p["lru_conv_b"], p["wg"], p["bg"], p["lam"], p["mng"], p["wout"], p["g2"], p["wrt"], p["br"]]
    kern = functools.partial(_mix_kernel, tile=tile, n_tiles=n_tiles, steps_per_seq=nj, d_conv=d_conv, d_lru=d_lru,
                             conv_w=conv_w, lru_cw=lru_cw, n_exp=n_exp)
    out_shape = (
        jax.ShapeDtypeStruct((bsz, seq, dm), F32),
        jax.ShapeDtypeStruct((SC_SPLIT, bsz * seq, dm // 2 // SC_SPLIT), I32),
        jax.ShapeDtypeStruct((TOP_K, bsz * seq), I32),
        jax.ShapeDtypeStruct((TOP_K, bsz * seq), I32),
        jax.ShapeDtypeStruct((bsz, seq, LANES), F32),
        jax.ShapeDtypeStruct((n_exp, LANES), F32),
    )
    return pl.pallas_call(
        kern,
        grid=(n_steps,),
        in_specs=[pl.BlockSpec((None, ts, dm), lambda s: (s // nj + b0, s % nj, 0)),
                  pl.BlockSpec((None, tile, dm),
                               lambda s: (next_tile(s) // tiles_per_seq + b0, next_tile(s) % tiles_per_seq, 0)),
                  pl.BlockSpec((None, N_MOD, dm), lambda s: (s // nj + b0, 0, 0)),
                  pl.BlockSpec((None, N_MOD, dm), lambda s: (next_tile(s) // tiles_per_seq + b0, 0, 0))]
        + [full(w) for w in weights] + [pl.BlockSpec(memory_space=pl.ANY)],
        out_specs=(
            pl.BlockSpec((None, ts, dm), lambda s: (s // nj, s % nj, 0)),
            pl.BlockSpec((SC_SPLIT, ts, dm // 2 // SC_SPLIT), lambda s: (0, s, 0)),
            pl.BlockSpec((TOP_K, ts), lambda s: (0, s)),
            pl.BlockSpec((TOP_K, ts), lambda s: (0, s)),
            pl.BlockSpec((None, ts, LANES), lambda s: (s // nj, s % nj, 0)),
            pl.BlockSpec((n_exp, LANES), lambda s: (0, 0)),
        ),
        out_shape=out_shape,
        scratch_shapes=[
            pltpu.VMEM((tile, 2 * d_conv + 2 * d_lru), F32),
            pltpu.VMEM((tile, 2 * d_conv + 2 * d_lru), F32),
            pltpu.VMEM((d_conv // LANES, CONV_PAD + ts, LANES), F32),
            pltpu.VMEM((d_lru // LANES, SUBLANES + ts, LANES), F32),
            pltpu.VMEM((d_lru // LANES, tile + SUBLANES, LANES), F32),
            pltpu.VMEM((d_lru // LANES, tile + SUBLANES, LANES), F32),
            pltpu.VMEM((1, d_lru), F32),
            pltpu.VMEM((n_exp, tile), F32),
        ],
        compiler_params=pltpu.CompilerParams(dimension_semantics=("arbitrary",),
                                             vmem_limit_bytes=VMEM_LIMIT_BYTES),
        name="token_mix_route",
    )(x, x, mod, mod, *weights, anchor)


def _sc_mesh():
    return plsc.VectorSubcoreMesh(core_axis_name="core", subcore_axis_name="subcore",
                                  num_cores=SC_CORES, num_subcores=SC_SUBCORES)


def _sc_scatter_rows(rows, idx, n_out):
    n, w = rows.shape
    n_k = idx.shape[0]
    assert n % (SC_WINDOW * SC_CORES * SC_SUBCORES) == 0

    @functools.partial(pl.kernel, out_type=jax.ShapeDtypeStruct((n_out, w), rows.dtype), mesh=_sc_mesh(),
                       scratch_types=[], name="moe_dispatch_sc")
    def scatter(x_hbm, i_hbm, o_hbm):
        def body(x_vmem, *i_vmems):
            for i_vmem in i_vmems:
                pltpu.sync_copy(x_vmem, o_hbm.at[i_vmem.at[0]])

        pltpu.emit_pipeline(
            body,
            grid=(n // SC_WINDOW,),
            in_specs=[pl.BlockSpec((SC_WINDOW, w), lambda i: (i, 0))]
            + [pl.BlockSpec((1, SC_WINDOW), functools.partial(lambda k, i: (k, i), k)) for k in range(n_k)],
            out_specs=[],
            core_axis_name=("core", "subcore"),
            dimension_semantics=(pltpu.PARALLEL,),
        )(x_hbm, *([i_hbm] * n_k))

    return scatter(rows, idx)


def _sc_gather_rows(table, idx):
    m = idx.shape[1]
    w = table.shape[1]
    assert m % (SC_WINDOW * SC_CORES * SC_SUBCORES) == 0

    @functools.partial(pl.kernel, out_type=jax.ShapeDtypeStruct((m, w), table.dtype), mesh=_sc_mesh(),
                       scratch_types=[], name="moe_gather_sc")
    def gather(x_hbm, i_hbm, o_hbm):
        def body(i_vmem, o_vmem):
            pltpu.sync_copy(x_hbm.at[i_vmem.at[0]], o_vmem)

        pltpu.emit_pipeline(
            body,
            grid=(m // SC_WINDOW,),
            in_specs=[pl.BlockSpec((1, SC_WINDOW), lambda i: (0, i))],
            out_specs=[pl.BlockSpec((SC_WINDOW, w), lambda i: (i, 0))],
            core_axis_name=("core", "subcore"),
            dimension_semantics=(pltpu.PARALLEL,),
        )(i_hbm, o_hbm)

    return gather(table, idx)


def _moe_kernel(be_ref, nb_ref, slot_ref, nxt_ref, nv_ref, xs_ref, wgu_hbm, bgu_ref, wd_hbm, bd_ref, ys_ref,
                wgu_f, wd_f, wgu_b, wd_b, sems, *, d_ff):
    i = pl.program_id(0)
    live = i < nb_ref[0]
    new_expert = (i == 0) | (be_ref[i] != be_ref[jnp.maximum(i - 1, 0)])

    def weight_copies(e, s):
        return (pltpu.make_async_copy(wgu_hbm.at[e], wgu_f.at[s], sems.at[0, s]),
                pltpu.make_async_copy(wd_hbm.at[e], wd_f.at[s], sems.at[1, s]))

    @pl.when(live & new_expert)
    def _():
        e, s, nx = be_ref[i], slot_ref[i], nxt_ref[i]

        @pl.when(i == 0)
        def _():
            for cp in weight_copies(e, s):
                cp.start()

        for cp in weight_copies(e, s):
            cp.wait()

        @pl.when(nx >= 0)
        def _():
            for cp in weight_copies(nx, 1 - s):
                cp.start()

        wgu_b[...] = wgu_f[s].astype(BF16)
        wd_b[...] = wd_f[s].astype(BF16)

    def sub_block(s):
        rows = pl.ds(s * MOE_SUB, MOE_SUB)
        packed = jnp.concatenate([xs_ref[p, rows, :] for p in range(SC_SPLIT)], axis=1)
        x_lo, x_hi = _unpack_bf16_pair(packed)
        dh = x_lo.shape[1]
        gu = jnp.dot(x_lo.astype(BF16), wgu_b[pl.ds(0, dh), :], preferred_element_type=F32)
        gu += jnp.dot(x_hi.astype(BF16), wgu_b[pl.ds(dh, dh), :], preferred_element_type=F32)
        gu += bgu_ref[...]
        g = jnp.minimum(gu[:, :d_ff], SWIGLU_LIMIT)
        u = jnp.clip(gu[:, d_ff:], -SWIGLU_LIMIT, SWIGLU_LIMIT)
        act = (u + 1.0) * (g * _sigmoid(SWIGLU_ALPHA * g))
        y = jnp.dot(act.astype(BF16), wd_b[...], preferred_element_type=F32) + bd_ref[...]
        dm2 = y.shape[1] // 2
        _store_planes(ys_ref, _pack_bf16_pair(y[:, :dm2], y[:, dm2:]), rows)

    n_sub = xs_ref.shape[1] // MOE_SUB
    n_valid = nv_ref[i]

    @pl.when(live & (n_valid == n_sub * MOE_SUB))
    def _():
        for s in range(n_sub):
            sub_block(s)

    @pl.when(live & (n_valid < n_sub * MOE_SUB))
    def _():
        for s in range(n_sub):
            pl.when(s * MOE_SUB < n_valid)(functools.partial(sub_block, s))


def _moe_call(xs, block_expert, n_used, seg_slot, next_expert, block_valid, wgu, bgu, wd, bd):
    _, n_rows, wp = xs.shape
    n_exp, dm, d_ff2 = wgu.shape
    d_ff = d_ff2 // 2
    bm = MOE_BLOCK
    n_blocks = n_rows // bm

    def row_map(i, be, nb, slot, nxt, nv):
        return (0, jnp.minimum(i, nb[0] - 1), 0)

    def b_map(i, be, nb, slot, nxt, nv):
        return (be[jnp.minimum(i, nb[0] - 1)], 0, 0)

    grid_spec = pltpu.PrefetchScalarGridSpec(
        num_scalar_prefetch=5,
        grid=(n_blocks,),
        in_specs=[
            pl.BlockSpec((SC_SPLIT, bm, wp), row_map),
            pl.BlockSpec(memory_space=pl.ANY),
            pl.BlockSpec((None, 1, d_ff2), b_map),
            pl.BlockSpec(memory_space=pl.ANY),
            pl.BlockSpec((None, 1, dm), b_map),
        ],
        out_specs=pl.BlockSpec((SC_SPLIT, bm, wp), row_map),
        scratch_shapes=[
            pltpu.VMEM((2, dm, d_ff2), F32), pltpu.VMEM((2, d_ff, dm), F32),
            pltpu.VMEM((dm, d_ff2), BF16), pltpu.VMEM((d_ff, dm), BF16),
            pltpu.SemaphoreType.DMA((2, 2)),
        ],
    )
    return pl.pallas_call(
        functools.partial(_moe_kernel, d_ff=d_ff),
        grid_spec=grid_spec,
        out_shape=jax.ShapeDtypeStruct(xs.shape, I32),
        compiler_params=pltpu.CompilerParams(dimension_semantics=("arbitrary",),
                                             vmem_limit_bytes=VMEM_LIMIT_BYTES),
        name="moe_experts",
    )(block_expert, n_used, seg_slot, next_expert, block_valid, xs, wgu, bgu, wd, bd)


def _combine_kernel(x1_ref, gt_ref, mod_ref, gf_ref, yg_ref, *out_refs):
    o_ref = out_refs[-1]
    gt = gt_ref[...]
    ts = yg_ref.shape[2]
    dh = yg_ref.shape[3] * SC_SPLIT
    acc_lo = jnp.zeros((ts, dh), F32)
    acc_hi = jnp.zeros((ts, dh), F32)
    for k in range(TOP_K):
        lo, hi = _unpack_bf16_pair(_load_planes(yg_ref.at[k]))
        gk = gt[:, k:k + 1]
        acc_lo += gk * lo
        acc_hi += gk * hi
    y = jnp.concatenate([acc_lo, acc_hi], axis=1)
    gate2 = mod_ref[...][5:6]
    x2 = x1_ref[...] + gate2 * y
    o_ref[...] = _rms(x2) * gf_ref[...]


def _combine_call(x1, gt, mod, gf, yg, out_prev, b0):
    bsz, seq, dm = x1.shape
    ts = min(COMBINE_TILE, seq)
    assert seq % ts == 0
    nj = seq // ts
    in_specs = [
        pl.BlockSpec((None, ts, dm), lambda b, j: (b, j, 0)),
        pl.BlockSpec((None, ts, LANES), lambda b, j: (b, j, 0)),
        pl.BlockSpec((None, N_MOD, dm), lambda b, j: (b + b0, 0, 0)),
        pl.BlockSpec((1, dm), lambda b, j: (0, 0)),
        pl.BlockSpec((TOP_K, SC_SPLIT, ts, yg.shape[3]), lambda b, j: (0, 0, b * nj + j, 0)),
    ]
    args = [x1, gt, mod, gf, yg]
    aliases = {}
    if out_prev is not None:
        in_specs.append(pl.BlockSpec(memory_space=pl.ANY))
        args.append(out_prev)
        aliases = {len(args) - 1: 0}
    return pl.pallas_call(
        _combine_kernel,
        grid=(bsz, nj),
        in_specs=in_specs,
        out_specs=pl.BlockSpec((None, ts, dm), lambda b, j: (b + b0, j, 0)),
        out_shape=jax.ShapeDtypeStruct((mod.shape[0], seq, dm), F32),
        input_output_aliases=aliases,
        compiler_params=pltpu.CompilerParams(dimension_semantics=("arbitrary", "arbitrary"),
                                             vmem_limit_bytes=VMEM_LIMIT_BYTES),
        name="moe_combine",
    )(*args)


def _block_diag(w):
    n_h, d, _ = w.shape
    eye = jnp.eye(n_h, dtype=w.dtype)
    return (eye[:, None, :, None] * w[:, :, None, :]).reshape(n_h * d, n_h * d)


def _layer(x, mod, l, w_in, norm1_g, conv_w, conv_b, conv_ln_g, conv_ln_b, lru_conv_w, lru_conv_b,
           lru_w_a, lru_b_a, lru_w_x, lru_b_x, lru_lambda, mix_norm_g, w_out, norm2_g, w_router,
           b_router, w_gate_up, b_gate_up, w_down, b_down, out_gain):
    bsz, seq, dm = x.shape
    n_tok = bsz * seq
    n_exp = w_router.shape[-1]
    row = lambda a: a.reshape(1, -1)
    wr = w_router[l]
    wr_hi = wr.astype(BF16)
    wr_lo = (wr - wr_hi.astype(F32)).astype(BF16)
    params = dict(
        g1=row(norm1_g[l]), win=w_in[l].astype(BF16), conv_w=conv_w[l], conv_b=row(conv_b[l]),
        ln_g=row(conv_ln_g[l]), ln_b=row(conv_ln_b[l]), lru_conv_w=lru_conv_w[l], lru_conv_b=row(lru_conv_b[l]),
        wg=jnp.concatenate([_block_diag(lru_w_a[l]), _block_diag(lru_w_x[l])], axis=1).astype(BF16),
        bg=jnp.concatenate([lru_b_a[l].reshape(1, -1), lru_b_x[l].reshape(1, -1)], axis=1),
        lam=row(lru_lambda[l]), mng=row(mix_norm_g[l]), wout=w_out[l].astype(BF16), g2=row(norm2_g[l]),
        wrt=jnp.concatenate([wr_hi.T, wr_lo.T], axis=0), br=b_router[l].reshape(n_exp, 1),
    )
    sizes = [s for s in CHUNK_BATCHES if s > 0] if sum(CHUNK_BATCHES) == bsz else [bsz]
    routed, b0, anchor = [], 0, mod
    for cb in sizes:
        r = _route_chunk(x, mod, params, b0, cb, anchor, n_exp)
        routed.append(r)
        anchor = r["dest_sub"]
        b0 += cb
    out = None
    for r in routed:
        out = _experts_chunk(r, mod, out, w_gate_up[l], b_gate_up[l][:, None, :], w_down[l],
                             b_down[l][:, None, :], out_gain)
    return out


def _route_chunk(x, mod, params, b0, cb, anchor, n_exp):
    seq, dm = x.shape[1:]
    n_tok = cb * seq
    x1, h2p, top_idx, rank, gt, cnt = _mix_call(x, mod, params, b0, cb, anchor)

    bm = MOE_BLOCK
    counts = cnt[:, 0].astype(I32)
    padded = (counts + bm - 1) // bm * bm
    e_ids = jnp.arange(n_exp, dtype=I32)
    pad_ends = jnp.sum(jnp.where(e_ids[None, :] <= e_ids[:, None], padded[None, :], 0), axis=1)
    pad_starts = pad_ends - padded
    n_rows = (n_tok * TOP_K // bm + n_exp) * bm
    n_blocks = n_rows // bm
    dest = rank + jnp.sum(jnp.where(top_idx[..., None] == e_ids, pad_starts, 0), axis=-1)
    block_start = jnp.arange(n_blocks, dtype=I32) * bm
    block_expert = jnp.minimum(
        jnp.sum((block_start[:, None] >= pad_ends[None, :]).astype(I32), axis=1), n_exp - 1)
    n_used = pad_ends[-1:] // bm
    present = counts > 0
    seg_ordinal = jnp.sum(jnp.where((e_ids[None, :] < e_ids[:, None]) & present[None, :], 1, 0), axis=1)
    following = jnp.min(jnp.where((e_ids[None, :] > e_ids[:, None]) & present[None, :], e_ids[None, :], n_exp),
                        axis=1)
    following = jnp.where(following == n_exp, -1, following)
    block_hot = block_expert[:, None] == e_ids[None, :]
    seg_slot = jnp.sum(jnp.where(block_hot, seg_ordinal[None, :] % 2, 0), axis=1)
    next_expert = jnp.sum(jnp.where(block_hot, following[None, :], 0), axis=1)
    block_valid = jnp.clip(
        jnp.sum(jnp.where(block_hot, (pad_starts + counts)[None, :], 0), axis=1) - block_start, 0, bm)

    wp = h2p.shape[2]
    plane = jnp.arange(SC_SPLIT, dtype=I32)[None, :, None] * n_rows
    dest_sub = dest[:, None, :] + plane
    return dict(b0=b0, n_tok=n_tok, n_rows=n_rows, x1=x1, h2p=h2p, gt=gt, dest_sub=dest_sub,
                tables=(block_expert, n_used, seg_slot, next_expert, block_valid))


def _experts_chunk(r, mod, out_prev, wgu, bgu, wd, bd, out_gain):
    n_tok, n_rows, dest_sub = r["n_tok"], r["n_rows"], r["dest_sub"]
    wp = r["h2p"].shape[2]
    xs = _sc_scatter_rows(r["h2p"].reshape(SC_SPLIT * n_tok, wp), dest_sub.reshape(TOP_K, SC_SPLIT * n_tok),
                          SC_SPLIT * n_rows).reshape(SC_SPLIT, n_rows, wp)
    ys = _moe_call(xs, *r["tables"], wgu, bgu, wd, bd)
    yg = _sc_gather_rows(ys.reshape(SC_SPLIT * n_rows, wp), dest_sub.reshape(1, TOP_K * SC_SPLIT * n_tok))
    return _combine_call(r["x1"], r["gt"], mod, out_gain, yg.reshape(TOP_K, SC_SPLIT, n_tok, wp), out_prev,
                         r["b0"])


def kernel(x, c, w_ada, b_ada, norm1_g, w_in, conv_w, conv_b, conv_ln_g, conv_ln_b, lru_conv_w, lru_conv_b,
           lru_w_a, lru_b_a, lru_w_x, lru_b_x, lru_lambda, mix_norm_g, w_out, norm2_g, w_router, b_router,
           w_gate_up, b_gate_up, w_down, b_down, final_norm_g):
    depth = w_ada.shape[0]
    assert depth == 1, "the final norm is fused into the (single) layer's combine kernel"
    bsz, seq, dm = x.shape
    mod = _ada_call(c, w_ada[0], b_ada[0]).reshape(bsz, N_MOD, dm)
    return _layer(x, mod, 0, w_in, norm1_g, conv_w, conv_b, conv_ln_g, conv_ln_b, lru_conv_w, lru_conv_b,
                  lru_w_a, lru_b_a, lru_w_x, lru_b_x, lru_lambda, mix_norm_g, w_out, norm2_g, w_router,
                  b_router, w_gate_up, b_gate_up, w_down, b_down, row_gain(final_norm_g))


def row_gain(g):
    return g.reshape(1, -1)
```

```python
import functools

import jax
import jax.numpy as jnp
from jax import lax
from jax.experimental import pallas as pl
from jax.experimental.pallas import tpu as pltpu
from jax.experimental.pallas import tpu_sc as plsc

F32 = jnp.float32
BF16 = jnp.bfloat16
I32 = jnp.int32

EPS = 1e-6
N_MOD = 6
LRU_C = 8.0
TOP_K = 4
SWIGLU_ALPHA = 1.702
SWIGLU_LIMIT = 7.0

LANES = 128
SUBLANES = 8
VMEM_LIMIT_BYTES = 56 * 1024 * 1024

SEQ_TILE = 512
TILES_PER_STEP = 2
COMBINE_TILE = 512
CONV_PAD = 32
CONV_ROWS = 128
MOE_BLOCK = 1024
MOE_SUB = 256
CHUNK_BATCHES = (5, 3)
HI_MASK = -65536

SC_CORES = 2
SC_SUBCORES = 16
SC_LANES = 16
SC_PACK_ROWS = 16
SC_PACK_COLS = 512
SC_WINDOW = 128
SC_SPLIT = 2


def _sigmoid(x):
    return 0.5 * jnp.tanh(0.5 * x) + 0.5


def _pack_bf16_pair(lo_f32, hi_f32):
    lo_bits = lax.bitcast_convert_type(lo_f32.astype(BF16).astype(F32), I32)
    hi_bits = lax.bitcast_convert_type(hi_f32.astype(BF16).astype(F32), I32)
    return lax.shift_right_logical(lo_bits, 16) | hi_bits


def _unpack_bf16_pair(p):
    lo = lax.bitcast_convert_type(lax.shift_left(p, 16), F32)
    hi = lax.bitcast_convert_type(p & HI_MASK, F32)
    return lo, hi


def _store_planes(ref, packed, rows=slice(None)):
    wp = packed.shape[1] // SC_SPLIT
    for s in range(SC_SPLIT):
        ref[s, rows, :] = packed[:, s * wp:(s + 1) * wp]


def _load_planes(ref):
    return jnp.concatenate([ref[s] for s in range(SC_SPLIT)], axis=1)


def _ada_kernel(c_ref, w_ref, b_ref, o_ref):
    c = c_ref[...]
    ca = c * _sigmoid(c)
    w = w_ref[...]
    c_hi = ca.astype(BF16)
    c_lo = (ca - c_hi.astype(F32)).astype(BF16)
    w_hi = w.astype(BF16)
    w_lo = (w - w_hi.astype(F32)).astype(BF16)
    acc = jnp.dot(c_hi, w_hi, preferred_element_type=F32)
    acc += jnp.dot(c_lo, w_hi, preferred_element_type=F32)
    acc += jnp.dot(c_hi, w_lo, preferred_element_type=F32)
    o_ref[...] = acc + b_ref[...]


def _ada_call(c, w_ada, b_ada):
    bsz, dm = c.shape
    n_out = w_ada.shape[1]
    tn = 1024
    return pl.pallas_call(
        _ada_kernel,
        grid=(n_out // tn,),
        in_specs=[
            pl.BlockSpec((bsz, dm), lambda n: (0, 0)),
            pl.BlockSpec((dm, tn), lambda n: (0, n)),
            pl.BlockSpec((1, tn), lambda n: (0, n)),
        ],
        out_specs=pl.BlockSpec((bsz, tn), lambda n: (0, n)),
        out_shape=jax.ShapeDtypeStruct((bsz, n_out), F32),
        compiler_params=pltpu.CompilerParams(dimension_semantics=("arbitrary",)),
        name="ada_mod",
    )(c, w_ada, b_ada.reshape(1, n_out))


def _rms(x, eps=EPS):
    return x * lax.rsqrt(jnp.mean(x * x, axis=-1, keepdims=True) + eps)


def _gelu_tanh(x):
    return 0.5 * x * (1.0 + jnp.tanh(0.7978845608028654 * (x + 0.044715 * (x * x * x))))


def _slab_store(buf, row0, val):
    for s in range(buf.shape[0]):
        buf[s, pl.ds(row0, val.shape[0]), :] = val[:, s * LANES:(s + 1) * LANES]


def _slab_keep_tail(buf, keep, ts):
    for s in range(buf.shape[0]):
        buf[s, pl.ds(0, keep), :] = buf[s, pl.ds(ts, keep), :]


def _causal_tap_sum(buf, w_ref, bias_row, first, n_taps, ts):
    w = w_ref[...]
    cols = []
    for s in range(buf.shape[0]):
        lanes = slice(s * LANES, (s + 1) * LANES)
        chunks = []
        for c in range(0, ts, CONV_ROWS):
            acc = jnp.broadcast_to(bias_row[:, lanes], (CONV_ROWS, LANES))
            for k in range(n_taps):
                acc = acc + w[k:k + 1, lanes] * buf[s, pl.ds(c + first + k, CONV_ROWS), :]
            chunks.append(acc)
        cols.append(jnp.concatenate(chunks, axis=0))
    return jnp.concatenate(cols, axis=1)


def _linear_recurrence(a, b, a_s, b_s, hcar):
    ts = a.shape[0]
    blk = ts // SUBLANES
    pitch = blk + 1
    cols = []
    for j in range(a_s.shape[0]):
        lanes = slice(j * LANES, (j + 1) * LANES)
        for k in range(SUBLANES):
            a_s[j, pl.ds(k * pitch, blk), :] = a[k * blk:(k + 1) * blk, lanes]
            b_s[j, pl.ds(k * pitch, blk), :] = b[k * blk:(k + 1) * blk, lanes]
        h = jnp.zeros((SUBLANES, LANES), F32)
        p = jnp.ones((SUBLANES, LANES), F32)
        for i in range(blk):
            row_i = pl.ds(i, SUBLANES, stride=pitch)
            a_i = a_s[j, row_i, :]
            h = a_i * h + b_s[j, row_i, :]
            p = a_i * p
            b_s[j, row_i, :] = h
            a_s[j, row_i, :] = p
        state = hcar[:, lanes]
        entering = []
        for k in range(SUBLANES):
            entering.append(state)
            state = p[k:k + 1, :] * state + h[k:k + 1, :]
        hcar[:, lanes] = state
        h_in = jnp.concatenate(entering, axis=0)
        for i in range(blk):
            row_i = pl.ds(i, SUBLANES, stride=pitch)
            b_s[j, row_i, :] = b_s[j, row_i, :] + a_s[j, row_i, :] * h_in
        cols.append(jnp.concatenate([b_s[j, pl.ds(k * pitch, blk), :] for k in range(SUBLANES)], axis=0))
    return jnp.concatenate(cols, axis=1)


def _mix_kernel(x_ref, xn_ref, mod_ref, modn_ref, g1_ref, win_ref, cw_ref, cb_ref, lng_ref, lnb_ref,
                lcw_ref, lcb_ref, wg_ref, bg_ref, lam_ref, mng_ref, wout_ref, g2_ref,
                wrt_ref, br_ref, anchor_ref,
                x1_ref, h2p_ref, idx_ref, rank_ref, gt_ref, cnt_ref,
                u_a, u_b, vbuf, rbuf, a_s, b_s, hcar, cnt_s,
                *, tile, n_tiles, steps_per_seq, d_conv, d_lru, conv_w, lru_cw, n_exp):
    del anchor_ref
    s = pl.program_id(0)

    @pl.when(s % steps_per_seq == 0)
    def _():
        _slab_store(vbuf, 0, jnp.zeros((CONV_PAD, d_conv), F32))
        _slab_store(rbuf, 0, jnp.zeros((SUBLANES, d_lru), F32))
        hcar[...] = jnp.zeros_like(hcar)

    mod = mod_ref[...]
    modn = modn_ref[...]
    shift1, gate1, shift2 = mod[0:1], mod[2:3], mod[3:4]
    gain1 = g1_ref[...] * (1.0 + mod[1:2])
    gain2 = g2_ref[...] * (1.0 + mod[4:5])
    lam = lam_ref[...]
    softplus_neg_lam = jnp.maximum(-lam, 0.0) + jnp.log1p(jnp.exp(-jnp.abs(lam)))

    def project(src_ref, r0, gain, shift, u_ref):
        h = _rms(src_ref[pl.ds(r0, tile), :]) * gain + shift
        u_ref[...] = jnp.dot(h.astype(BF16), win_ref[...], preferred_element_type=F32)

    def conv_group(r0, u_ref):
        return _mix_conv_group(r0, tile, u_ref, cw_ref, cb_ref, lng_ref, lnb_ref, vbuf, d_conv=d_conv,
                               conv_w=conv_w)

    def finish(r0, u_ref, yc):
        _mix_finish(r0, tile, x_ref, u_ref, yc, gate1, shift2, gain2, softplus_neg_lam,
                    lcw_ref, lcb_ref, wg_ref, bg_ref, mng_ref, wout_ref,
                    wrt_ref, br_ref, x1_ref, h2p_ref, idx_ref, rank_ref, gt_ref, rbuf, a_s, b_s, hcar,
                    cnt_s, d_conv=d_conv, d_lru=d_lru, lru_cw=lru_cw, n_exp=n_exp)

    @pl.when(s == 0)
    def _():
        cnt_s[...] = jnp.zeros_like(cnt_s)
        project(x_ref, 0, gain1, shift1, u_a)

    bufs = (u_a, u_b)
    for q in range(n_tiles):
        cur, nxt = bufs[q % 2], bufs[(q + 1) % 2]
        yc = conv_group(q * tile, cur)
        if q + 1 < n_tiles:
            project(x_ref, (q + 1) * tile, gain1, shift1, nxt)
        else:
            project(xn_ref, 0, g1_ref[...] * (1.0 + modn[1:2]), modn[0:1], nxt)
        finish(q * tile, cur, yc)
    _slab_keep_tail(vbuf, CONV_PAD, n_tiles * tile)
    _slab_keep_tail(rbuf, SUBLANES, n_tiles * tile)
    cnt_ref[...] = cnt_s[:, :LANES]


def _mix_conv_group(r0, ts, u_ref, cw_ref, cb_ref, lng_ref, lnb_ref, vbuf, *, d_conv, conv_w):
    v = u_ref[:, :d_conv] * _sigmoid(u_ref[:, d_conv:2 * d_conv])
    _slab_store(vbuf, CONV_PAD + r0, v)
    acc = _causal_tap_sum(vbuf, cw_ref, cb_ref[...], r0 + CONV_PAD - (conv_w - 1), conv_w, ts)
    mu = jnp.mean(acc, axis=-1, keepdims=True)
    cen = acc - mu
    var = jnp.mean(cen * cen, axis=-1, keepdims=True)
    yc = cen * lax.rsqrt(var + EPS) * lng_ref[...] + lnb_ref[...]
    return yc * _sigmoid(yc)


def _mix_finish(r0, ts, x_ref, u_ref, yc, gate1, shift2, gain2, softplus_neg_lam,
                lcw_ref, lcb_ref, wg_ref, bg_ref, mng_ref, wout_ref,
                wrt_ref, br_ref, x1_ref, h2p_ref, idx_ref, rank_ref, gt_ref, rbuf, a_s, b_s, hcar,
                cnt_s, *, d_conv, d_lru, lru_cw, n_exp):
    rows = pl.ds(r0, ts)
    x = x_ref[rows, :]
    u = u_ref

    u_gate = u[:, 2 * d_conv:2 * d_conv + d_lru]
    _slab_store(rbuf, SUBLANES + r0, u[:, 2 * d_conv + d_lru:])
    xr = _causal_tap_sum(rbuf, lcw_ref, lcb_ref[...], r0 + SUBLANES - (lru_cw - 1), lru_cw, ts)
    gates = jnp.dot(xr.astype(BF16), wg_ref[...], preferred_element_type=F32) + bg_ref[...]
    r = _sigmoid(gates[:, :d_lru])
    i_g = _sigmoid(gates[:, d_lru:])
    log_a = (-LRU_C) * r * softplus_neg_lam
    a = jnp.exp(log_a)
    inp = jnp.sqrt(1.0 - jnp.exp(2.0 * log_a)) * (i_g * xr)

    yl = _gelu_tanh(u_gate) * _linear_recurrence(a, inp, a_s, b_s, hcar)

    mng = mng_ref[...]
    yc_n = _rms(yc) * mng[:, :d_conv]
    yl_n = _rms(yl) * mng[:, d_conv:]
    mixo = jnp.dot(yc_n.astype(BF16), wout_ref[pl.ds(0, d_conv), :], preferred_element_type=F32)
    mixo += jnp.dot(yl_n.astype(BF16), wout_ref[pl.ds(d_conv, d_lru), :], preferred_element_type=F32)
    x1 = x + gate1 * mixo
    x1_ref[rows, :] = x1

    h2 = _rms(x1) * gain2 + shift2
    dh = h2.shape[1] // 2
    h2_hi = h2.astype(BF16)
    _store_planes(h2p_ref, _pack_bf16_pair(h2[:, :dh], h2[:, dh:]), rows)
    h2_lo = (h2 - h2_hi.astype(F32)).astype(BF16)
    nt_dims = (((1,), (1,)), ((), ()))
    wrt = wrt_ref[...]
    lg = lax.dot_general(wrt, h2_hi, nt_dims, preferred_element_type=F32)
    lg2 = lax.dot_general(wrt[:n_exp], h2_lo, nt_dims, preferred_element_type=F32)
    logits = lg[:n_exp] + lg[n_exp:] + lg2 + br_ref[...]

    eidx = lax.broadcasted_iota(I32, (n_exp, ts), 0)
    neg_inf = jnp.float32(-jnp.inf)
    work = logits
    vals, idxs, hots = [], [], []
    for _ in range(TOP_K):
        mval = jnp.max(work, axis=0, keepdims=True)
        midx = jnp.min(jnp.where(work == mval, eidx, n_exp), axis=0, keepdims=True)
        hot = eidx == midx
        vals.append(mval)
        idxs.append(midx)
        hots.append(hot)
        work = jnp.where(hot, neg_inf, work)
    exps = [jnp.exp(vk - vals[0]) for vk in vals]
    denom = exps[0] + exps[1] + exps[2] + exps[3]
    gate_rows = [ek / denom for ek in exps]

    sel = jnp.zeros((n_exp, ts), F32)
    for hot in hots:
        sel = sel + hot.astype(F32)
    tri = (lax.broadcasted_iota(I32, (ts, ts), 0) < lax.broadcasted_iota(I32, (ts, ts), 1)).astype(BF16)
    before = jnp.dot(sel.astype(BF16), tri, preferred_element_type=F32) + cnt_s[...]
    rank_rows = [jnp.sum(jnp.where(hot, before, 0.0), axis=0, keepdims=True) for hot in hots]
    cnt_s[...] = cnt_s[...] + jnp.sum(sel, axis=1, keepdims=True)

    idx_ref[:, rows] = jnp.concatenate(idxs, axis=0)
    rank_ref[:, rows] = jnp.concatenate(rank_rows, axis=0).astype(I32)
    g4 = jnp.concatenate(gate_rows, axis=0)
    g_pad = jnp.concatenate([g4, jnp.zeros((LANES - TOP_K, ts), F32)], axis=0)
    gt_ref[rows, :] = g_pad.T


def _mix_call(x, mod, p, b0, bsz, anchor):
    _, seq, dm = x.shape
    n_tiles = TILES_PER_STEP
    tile = min(SEQ_TILE, seq // n_tiles)
    ts = n_tiles * tile
    nj = seq // ts
    n_steps = bsz * nj
    tiles_per_seq = seq // tile
    d_conv = p["conv_w"].shape[1]
    d_lru = p["lru_conv_w"].shape[1]
    conv_w = p["conv_w"].shape[0]
    lru_cw = p["lru_conv_w"].shape[0]
    n_exp = p["wrt"].shape[0] // 2
    assert seq % ts == 0 and tile % LANES == 0 and tile % CONV_ROWS == 0 and n_tiles % 2 == 0
    assert conv_w - 1 <= CONV_PAD and lru_cw - 1 <= SUBLANES and d_conv % LANES == 0 and d_lru % LANES == 0

    def full(a):
        return pl.BlockSpec(a.shape, lambda s: (0,) * a.ndim)

    def next_tile(s):
        return jnp.minimum(n_tiles * (s + 1), bsz * tiles_per_seq - 1)

    weights = [p["g1"], p["win"], p["conv_w"], p["conv_b"], p["ln_g"], p["ln_b"], p["lru_conv_w"],
               p["lru_conv_b"], p["wg"], p["bg"], p["lam"], p["mng"], p["wout"], p["g2"], p["wrt"], p["br"]]
    kern = functools.partial(_mix_kernel, tile=tile, n_tiles=n_tiles, steps_per_seq=nj, d_conv=d_conv, d_lru=d_lru,
                             conv_w=conv_w, lru_cw=lru_cw, n_exp=n_exp)
    out_shape = (
        jax.ShapeDtypeStruct((bsz, seq, dm), F32),
        jax.ShapeDtypeStruct((SC_SPLIT, bsz * seq, dm // 2 // SC_SPLIT), I32),
        jax.ShapeDtypeStruct((TOP_K, bsz * seq), I32),
        jax.ShapeDtypeStruct((TOP_K, bsz * seq), I32),
        jax.ShapeDtypeStruct((bsz, seq, LANES), F32),
        jax.ShapeDtypeStruct((n_exp, LANES), F32),
    )
    return pl.pallas_call(
        kern,
        grid=(n_steps,),
        in_specs=[pl.BlockSpec((None, ts, dm), lambda s: (s // nj + b0, s % nj, 0)),
                  pl.BlockSpec((None, tile, dm),
                               lambda s: (next_tile(s) // tiles_per_seq + b0, next_tile(s) % tiles_per_seq, 0)),
                  pl.BlockSpec((None, N_MOD, dm), lambda s: (s // nj + b0, 0, 0)),
                  pl.BlockSpec((None, N_MOD, dm), lambda s: (next_tile(s) // tiles_per_seq + b0, 0, 0))]
        + [full(w) for w in weights] + [pl.BlockSpec(memory_space=pl.ANY)],
        out_specs=(
            pl.BlockSpec((None, ts, dm), lambda s: (s // nj, s % nj, 0)),
            pl.BlockSpec((SC_SPLIT, ts, dm // 2 // SC_SPLIT), lambda s: (0, s, 0)),
            pl.BlockSpec((TOP_K, ts), lambda s: (0, s)),
            pl.BlockSpec((TOP_K, ts), lambda s: (0, s)),
            pl.BlockSpec((None, ts, LANES), lambda s: (s // nj, s % nj, 0)),
            pl.BlockSpec((n_exp, LANES), lambda s: (0, 0)),
        ),
        out_shape=out_shape,
        scratch_shapes=[
            pltpu.VMEM((tile, 2 * d_conv + 2 * d_lru), F32),
            pltpu.VMEM((tile, 2 * d_conv + 2 * d_lru), F32),
            pltpu.VMEM((d_conv // LANES, CONV_PAD + ts, LANES), F32),
            pltpu.VMEM((d_lru // LANES, SUBLANES + ts, LANES), F32),
            pltpu.VMEM((d_lru // LANES, tile + SUBLANES, LANES), F32),
            pltpu.VMEM((d_lru // LANES, tile + SUBLANES, LANES), F32),
            pltpu.VMEM((1, d_lru), F32),
            pltpu.VMEM((n_exp, tile), F32),
        ],
        compiler_params=pltpu.CompilerParams(dimension_semantics=("arbitrary",),
                                             vmem_limit_bytes=VMEM_LIMIT_BYTES),
        name="token_mix_route",
    )(x, x, mod, mod, *weights, anchor)


def _sc_mesh():
    return plsc.VectorSubcoreMesh(core_axis_name="core", subcore_axis_name="subcore",
                                  num_cores=SC_CORES, num_subcores=SC_SUBCORES)


def _sc_scatter_rows(rows, idx, n_out):
    n, w = rows.shape
    n_k = idx.shape[0]
    assert n % (SC_WINDOW * SC_CORES * SC_SUBCORES) == 0

    @functools.partial(pl.kernel, out_type=jax.ShapeDtypeStruct((n_out, w), rows.dtype), mesh=_sc_mesh(),
                       scratch_types=[], name="moe_dispatch_sc")
    def scatter(x_hbm, i_hbm, o_hbm):
        def body(x_vmem, *i_vmems):
            for i_vmem in i_vmems:
                pltpu.sync_copy(x_vmem, o_hbm.at[i_vmem.at[0]])

        pltpu.emit_pipeline(
            body,
            grid=(n // SC_WINDOW,),
            in_specs=[pl.BlockSpec((SC_WINDOW, w), lambda i: (i, 0))]
            + [pl.BlockSpec((1, SC_WINDOW), functools.partial(lambda k, i: (k, i), k)) for k in range(n_k)],
            out_specs=[],
            core_axis_name=("core", "subcore"),
            dimension_semantics=(pltpu.PARALLEL,),
        )(x_hbm, *([i_hbm] * n_k))

    return scatter(rows, idx)


def _sc_gather_rows(table, idx):
    m = idx.shape[1]
    w = table.shape[1]
    assert m % (SC_WINDOW * SC_CORES * SC_SUBCORES) == 0

    @functools.partial(pl.kernel, out_type=jax.ShapeDtypeStruct((m, w), table.dtype), mesh=_sc_mesh(),
                       scratch_types=[], name="moe_gather_sc")
    def gather(x_hbm, i_hbm, o_hbm):
        def body(i_vmem, o_vmem):
            pltpu.sync_copy(x_hbm.at[i_vmem.at[0]], o_vmem)

        pltpu.emit_pipeline(
            body,
            grid=(m // SC_WINDOW,),
            in_specs=[pl.BlockSpec((1, SC_WINDOW), lambda i: (0, i))],
            out_specs=[pl.BlockSpec((SC_WINDOW, w), lambda i: (i, 0))],
            core_axis_name=("core", "subcore"),
            dimension_semantics=(pltpu.PARALLEL,),
        )(i_hbm, o_hbm)

    return gather(table, idx)


def _sc_pack_rows_bf16(w):
    r, c = w.shape
    assert r % SC_PACK_ROWS == 0 and c % SC_PACK_COLS == 0 and (r // SC_PACK_ROWS) % (SC_CORES * SC_SUBCORES) == 0
    lanes = SC_LANES

    @functools.partial(pl.kernel, out_type=jax.ShapeDtypeStruct((r // 2, c), I32), mesh=_sc_mesh(),
                       scratch_types=[], compiler_params=pltpu.CompilerParams(needs_layout_passes=False),
                       name="expert_weight_pack_sc")
    def pack(x_hbm, o_hbm):
        def body(x_vmem, o_vmem):
            @pl.loop(0, SC_PACK_ROWS // 2)
            def _(i):
                for j in range(0, SC_PACK_COLS, lanes):
                    pair = plsc.pack(x_vmem[2 * i, pl.ds(j, lanes)], x_vmem[2 * i + 1, pl.ds(j, lanes)],
                                     format=plsc.PackFormat.INTERLEAVED)
                    o_vmem[i, pl.ds(j, lanes)] = plsc.bitcast(pair, I32)

        pltpu.emit_pipeline(
            body,
            grid=(r // SC_PACK_ROWS, c // SC_PACK_COLS),
            in_specs=[pl.BlockSpec((SC_PACK_ROWS, SC_PACK_COLS), lambda i, j: (i, j))],
            out_specs=[pl.BlockSpec((SC_PACK_ROWS // 2, SC_PACK_COLS), lambda i, j: (i, j))],
            core_axis_name=("core", "subcore"),
            dimension_semantics=(pltpu.PARALLEL, pltpu.PARALLEL),
        )(x_hbm, o_hbm)

    return pack(w)


def _moe_kernel(be_ref, nb_ref, slot_ref, nxt_ref, nv_ref, xs_ref, wgu_hbm, bgu_ref, wd_hbm, bd_ref, ys_ref,
                wgu_s, wd_s, sems, *, d_ff):
    i = pl.program_id(0)
    live = i < nb_ref[0]
    new_expert = (i == 0) | (be_ref[i] != be_ref[jnp.maximum(i - 1, 0)])
    slot = slot_ref[i]

    def weight_copies(e, s):
        return (pltpu.make_async_copy(wgu_hbm.at[e], wgu_s.at[s], sems.at[0, s]),
                pltpu.make_async_copy(wd_hbm.at[e], wd_s.at[s], sems.at[1, s]))

    @pl.when(live & new_expert)
    def _():
        e, nx = be_ref[i], nxt_ref[i]

        @pl.when(i == 0)
        def _():
            for cp in weight_copies(e, slot):
                cp.start()

        for cp in weight_copies(e, slot):
            cp.wait()

        @pl.when(nx >= 0)
        def _():
            for cp in weight_copies(nx, 1 - slot):
                cp.start()

    def sub_block(s):
        rows = pl.ds(s * MOE_SUB, MOE_SUB)
        packed = jnp.concatenate([xs_ref[p, rows, :] for p in range(SC_SPLIT)], axis=1)
        x_lo, x_hi = _unpack_bf16_pair(packed)
        dh = x_lo.shape[1]
        w_top = pltpu.bitcast(wgu_s[slot, pl.ds(0, dh // 2), :], BF16)
        w_bot = pltpu.bitcast(wgu_s[slot, pl.ds(dh // 2, dh // 2), :], BF16)
        gu = jnp.dot(x_lo.astype(BF16), w_top, preferred_element_type=F32)
        gu += jnp.dot(x_hi.astype(BF16), w_bot, preferred_element_type=F32)
        gu += bgu_ref[...]
        g = jnp.minimum(gu[:, :d_ff], SWIGLU_LIMIT)
        u = jnp.clip(gu[:, d_ff:], -SWIGLU_LIMIT, SWIGLU_LIMIT)
        act = (u + 1.0) * (g * _sigmoid(SWIGLU_ALPHA * g))
        y = jnp.dot(act.astype(BF16), pltpu.bitcast(wd_s[slot], BF16), preferred_element_type=F32) + bd_ref[...]
        dm2 = y.shape[1] // 2
        _store_planes(ys_ref, _pack_bf16_pair(y[:, :dm2], y[:, dm2:]), rows)

    n_sub = xs_ref.shape[1] // MOE_SUB
    n_valid = nv_ref[i]

    @pl.when(live & (n_valid == n_sub * MOE_SUB))
    def _():
        for s in range(n_sub):
            sub_block(s)

    @pl.when(live & (n_valid < n_sub * MOE_SUB))
    def _():
        for s in range(n_sub):
            pl.when(s * MOE_SUB < n_valid)(functools.partial(sub_block, s))


def _moe_call(xs, block_expert, n_used, seg_slot, next_expert, block_valid, wgu, bgu, wd, bd):
    _, n_rows, wp = xs.shape
    n_exp, dm_half, d_ff2 = wgu.shape
    dm = 2 * dm_half
    d_ff = d_ff2 // 2
    bm = MOE_BLOCK
    n_blocks = n_rows // bm

    def row_map(i, be, nb, slot, nxt, nv):
        return (0, jnp.minimum(i, nb[0] - 1), 0)

    def b_map(i, be, nb, slot, nxt, nv):
        return (be[jnp.minimum(i, nb[0] - 1)], 0, 0)

    grid_spec = pltpu.PrefetchScalarGridSpec(
        num_scalar_prefetch=5,
        grid=(n_blocks,),
        in_specs=[
            pl.BlockSpec((SC_SPLIT, bm, wp), row_map),
            pl.BlockSpec(memory_space=pl.ANY),
            pl.BlockSpec((None, 1, d_ff2), b_map),
            pl.BlockSpec(memory_space=pl.ANY),
            pl.BlockSpec((None, 1, dm), b_map),
        ],
        out_specs=pl.BlockSpec((SC_SPLIT, bm, wp), row_map),
        scratch_shapes=[
            pltpu.VMEM((2,) + wgu.shape[1:], I32), pltpu.VMEM((2,) + wd.shape[1:], I32),
            pltpu.SemaphoreType.DMA((2, 2)),
        ],
    )
    return pl.pallas_call(
        functools.partial(_moe_kernel, d_ff=d_ff),
        grid_spec=grid_spec,
        out_shape=jax.ShapeDtypeStruct(xs.shape, I32),
        compiler_params=pltpu.CompilerParams(dimension_semantics=("arbitrary",),
                                             vmem_limit_bytes=VMEM_LIMIT_BYTES),
        name="moe_experts",
    )(block_expert, n_used, seg_slot, next_expert, block_valid, xs, wgu, bgu, wd, bd)


def _combine_kernel(x1_ref, gt_ref, mod_ref, gf_ref, yg_ref, *out_refs):
    o_ref = out_refs[-1]
    gt = gt_ref[...]
    ts = yg_ref.shape[2]
    dh = yg_ref.shape[3] * SC_SPLIT
    acc_lo = jnp.zeros((ts, dh), F32)
    acc_hi = jnp.zeros((ts, dh), F32)
    for k in range(TOP_K):
        lo, hi = _unpack_bf16_pair(_load_planes(yg_ref.at[k]))
        gk = gt[:, k:k + 1]
        acc_lo += gk * lo
        acc_hi += gk * hi
    y = jnp.concatenate([acc_lo, acc_hi], axis=1)
    gate2 = mod_ref[...][5:6]
    x2 = x1_ref[...] + gate2 * y
    o_ref[...] = _rms(x2) * gf_ref[...]


def _combine_call(x1, gt, mod, gf, yg, out_prev, b0):
    bsz, seq, dm = x1.shape
    ts = min(COMBINE_TILE, seq)
    assert seq % ts == 0
    nj = seq // ts
    in_specs = [
        pl.BlockSpec((None, ts, dm), lambda b, j: (b, j, 0)),
        pl.BlockSpec((None, ts, LANES), lambda b, j: (b, j, 0)),
        pl.BlockSpec((None, N_MOD, dm), lambda b, j: (b + b0, 0, 0)),
        pl.BlockSpec((1, dm), lambda b, j: (0, 0)),
        pl.BlockSpec((TOP_K, SC_SPLIT, ts, yg.shape[3]), lambda b, j: (0, 0, b * nj + j, 0)),
    ]
    args = [x1, gt, mod, gf, yg]
    aliases = {}
    if out_prev is not None:
        in_specs.append(pl.BlockSpec(memory_space=pl.ANY))
        args.append(out_prev)
        aliases = {len(args) - 1: 0}
    return pl.pallas_call(
        _combine_kernel,
        grid=(bsz, nj),
        in_specs=in_specs,
        out_specs=pl.BlockSpec((None, ts, dm), lambda b, j: (b + b0, j, 0)),
        out_shape=jax.ShapeDtypeStruct((mod.shape[0], seq, dm), F32),
        input_output_aliases=aliases,
        compiler_params=pltpu.CompilerParams(dimension_semantics=("arbitrary", "arbitrary"),
                                             vmem_limit_bytes=VMEM_LIMIT_BYTES),
        name="moe_combine",
    )(*args)


def _block_diag(w):
    n_h, d, _ = w.shape
    eye = jnp.eye(n_h, dtype=w.dtype)
    return (eye[:, None, :, None] * w[:, :, None, :]).reshape(n_h * d, n_h * d)


def _layer(x, mod, l, w_in, norm1_g, conv_w, conv_b, conv_ln_g, conv_ln_b, lru_conv_w, lru_conv_b,
           lru_w_a, lru_b_a, lru_w_x, lru_b_x, lru_lambda, mix_norm_g, w_out, norm2_g, w_router,
           b_router, w_gate_up, b_gate_up, w_down, b_down, out_gain):
    bsz, seq, dm = x.shape
    n_tok = bsz * seq
    n_exp = w_router.shape[-1]
    row = lambda a: a.reshape(1, -1)
    wr = w_router[l]
    wr_hi = wr.astype(BF16)
    wr_lo = (wr - wr_hi.astype(F32)).astype(BF16)
    params = dict(
        g1=row(norm1_g[l]), win=w_in[l].astype(BF16), conv_w=conv_w[l], conv_b=row(conv_b[l]),
        ln_g=row(conv_ln_g[l]), ln_b=row(conv_ln_b[l]), lru_conv_w=lru_conv_w[l], lru_conv_b=row(lru_conv_b[l]),
        wg=jnp.concatenate([_block_diag(lru_w_a[l]), _block_diag(lru_w_x[l])], axis=1).astype(BF16),
        bg=jnp.concatenate([lru_b_a[l].reshape(1, -1), lru_b_x[l].reshape(1, -1)], axis=1),
        lam=row(lru_lambda[l]), mng=row(mix_norm_g[l]), wout=w_out[l].astype(BF16), g2=row(norm2_g[l]),
        wrt=jnp.concatenate([wr_hi.T, wr_lo.T], axis=0), br=b_router[l].reshape(n_exp, 1),
    )
    sizes = [s for s in CHUNK_BATCHES if s > 0] if sum(CHUNK_BATCHES) == bsz else [bsz]
    routed, b0, anchor = [], 0, mod
    for cb in sizes:
        r = _route_chunk(x, mod, params, b0, cb, anchor, n_exp)
        routed.append(r)
        anchor = r["dest_sub"]
        b0 += cb
    _, d_in, d_ff2 = w_gate_up[l].shape
    _, d_ff, d_out = w_down[l].shape
    wgu_p = _sc_pack_rows_bf16(w_gate_up[l].reshape(n_exp * d_in, d_ff2)).reshape(n_exp, d_in // 2, d_ff2)
    wd_p = _sc_pack_rows_bf16(w_down[l].reshape(n_exp * d_ff, d_out)).reshape(n_exp, d_ff // 2, d_out)
    out = None
    for r in routed:
        out = _experts_chunk(r, mod, out, wgu_p, b_gate_up[l][:, None, :], wd_p, b_down[l][:, None, :], out_gain)
    return out


def _route_chunk(x, mod, params, b0, cb, anchor, n_exp):
    seq, dm = x.shape[1:]
    n_tok = cb * seq
    x1, h2p, top_idx, rank, gt, cnt = _mix_call(x, mod, params, b0, cb, anchor)

    bm = MOE_BLOCK
    counts = cnt[:, 0].astype(I32)
    padded = (counts + bm - 1) // bm * bm
    e_ids = jnp.arange(n_exp, dtype=I32)
    pad_ends = jnp.sum(jnp.where(e_ids[None, :] <= e_ids[:, None], padded[None, :], 0), axis=1)
    pad_starts = pad_ends - padded
    n_rows = (n_tok * TOP_K // bm + n_exp) * bm
    n_blocks = n_rows // bm
    dest = rank + jnp.sum(jnp.where(top_idx[..., None] == e_ids, pad_starts, 0), axis=-1)
    block_start = jnp.arange(n_blocks, dtype=I32) * bm
    block_expert = jnp.minimum(
        jnp.sum((block_start[:, None] >= pad_ends[None, :]).astype(I32), axis=1), n_exp - 1)
    n_used = pad_ends[-1:] // bm
    present = counts > 0
    seg_ordinal = jnp.sum(jnp.where((e_ids[None, :] < e_ids[:, None]) & present[None, :], 1, 0), axis=1)
    following = jnp.min(jnp.where((e_ids[None, :] > e_ids[:, None]) & present[None, :], e_ids[None, :], n_exp),
                        axis=1)
    following = jnp.where(following == n_exp, -1, following)
    block_hot = block_expert[:, None] == e_ids[None, :]
    seg_slot = jnp.sum(jnp.where(block_hot, seg_ordinal[None, :] % 2, 0), axis=1)
    next_expert = jnp.sum(jnp.where(block_hot, following[None, :], 0), axis=1)
    block_valid = jnp.clip(
        jnp.sum(jnp.where(block_hot, (pad_starts + counts)[None, :], 0), axis=1) - block_start, 0, bm)

    wp = h2p.shape[2]
    plane = jnp.arange(SC_SPLIT, dtype=I32)[None, :, None] * n_rows
    dest_sub = dest[:, None, :] + plane
    return dict(b0=b0, n_tok=n_tok, n_rows=n_rows, x1=x1, h2p=h2p, gt=gt, dest_sub=dest_sub,
                tables=(block_expert, n_used, seg_slot, next_expert, block_valid))


def _experts_chunk(r, mod, out_prev, wgu, bgu, wd, bd, out_gain):
    n_tok, n_rows, dest_sub = r["n_tok"], r["n_rows"], r["dest_sub"]
    wp = r["h2p"].shape[2]
    xs = _sc_scatter_rows(r["h2p"].reshape(SC_SPLIT * n_tok, wp), dest_sub.reshape(TOP_K, SC_SPLIT * n_tok),
                          SC_SPLIT * n_rows).reshape(SC_SPLIT, n_rows, wp)
    ys = _moe_call(xs, *r["tables"], wgu, bgu, wd, bd)
    yg = _sc_gather_rows(ys.reshape(SC_SPLIT * n_rows, wp), dest_sub.reshape(1, TOP_K * SC_SPLIT * n_tok))
    return _combine_call(r["x1"], r["gt"], mod, out_gain, yg.reshape(TOP_K, SC_SPLIT, n_tok, wp), out_prev,
                         r["b0"])


def kernel(x, c, w_ada, b_ada, norm1_g, w_in, conv_w, conv_b, conv_ln_g, conv_ln_b, lru_conv_w, lru_conv_b,
           lru_w_a, lru_b_a, lru_w_x, lru_b_x, lru_lambda, mix_norm_g, w_out, norm2_g, w_router, b_router,
           w_gate_up, b_gate_up, w_down, b_down, final_norm_g):
    depth = w_ada.shape[0]
    assert depth == 1, "the final norm is fused into the (single) layer's combine kernel"
    bsz, seq, dm = x.shape
    mod = _ada_call(c, w_ada[0], b_ada[0]).reshape(bsz, N_MOD, dm)
    return _layer(x, mod, 0, w_in, norm1_g, conv_w, conv_b, conv_ln_g, conv_ln_b, lru_conv_w, lru_conv_b,
                  lru_w_a, lru_b_a, lru_w_x, lru_b_x, lru_lambda, mix_norm_g, w_out, norm2_g, w_router,
                  b_router, w_gate_up, b_gate_up, w_down, b_down, row_gain(final_norm_g))


def row_gain(g):
    return g.reshape(1, -1)
```

```python
import functools

import jax
import jax.numpy as jnp
from jax import lax
from jax.experimental import pallas as pl
from jax.experimental.pallas import tpu as pltpu
from jax.experimental.pallas import tpu_sc as plsc

F32 = jnp.float32
BF16 = jnp.bfloat16
I32 = jnp.int32

EPS = 1e-6
N_MOD = 6
LRU_C = 8.0
TOP_K = 4
SWIGLU_ALPHA = 1.702
SWIGLU_LIMIT = 7.0

LANES = 128
SUBLANES = 8
VMEM_LIMIT_BYTES = 56 * 1024 * 1024

SEQ_TILE = 512
TILES_PER_STEP = 2
COMBINE_TILE = 512
CONV_PAD = 32
CONV_ROWS = 128
MOE_BLOCK = 1024
MOE_SUB = 256
CHUNK_BATCHES = (5, 3)
HI_MASK = -65536

SC_CORES = 2
SC_SUBCORES = 16
SC_LANES = 16
SC_PACK_ROWS = 16
SC_PACK_COLS = 512
SC_WINDOW = 128
SC_SPLIT = 2


def _sigmoid(x):
    return 0.5 * jnp.tanh(0.5 * x) + 0.5


def _pack_bf16_pair(lo_f32, hi_f32):
    lo_bits = lax.bitcast_convert_type(lo_f32.astype(BF16).astype(F32), I32)
    hi_bits = lax.bitcast_convert_type(hi_f32.astype(BF16).astype(F32), I32)
    return lax.shift_right_logical(lo_bits, 16) | hi_bits


def _unpack_bf16_pair(p):
    lo = lax.bitcast_convert_type(lax.shift_left(p, 16), F32)
    hi = lax.bitcast_convert_type(p & HI_MASK, F32)
    return lo, hi


def _store_planes(ref, packed, rows=slice(None)):
    wp = packed.shape[1] // SC_SPLIT
    for s in range(SC_SPLIT):
        ref[s, rows, :] = packed[:, s * wp:(s + 1) * wp]


def _load_planes(ref):
    return jnp.concatenate([ref[s] for s in range(SC_SPLIT)], axis=1)


def _ada_kernel(c_ref, w_ref, b_ref, o_ref):
    c = c_ref[...]
    ca = c * _sigmoid(c)
    w = w_ref[...]
    c_hi = ca.astype(BF16)
    c_lo = (ca - c_hi.astype(F32)).astype(BF16)
    w_hi = w.astype(BF16)
    w_lo = (w - w_hi.astype(F32)).astype(BF16)
    acc = jnp.dot(c_hi, w_hi, preferred_element_type=F32)
    acc += jnp.dot(c_lo, w_hi, preferred_element_type=F32)
    acc += jnp.dot(c_hi, w_lo, preferred_element_type=F32)
    o_ref[...] = acc + b_ref[...]


def _ada_call(c, w_ada, b_ada):
    bsz, dm = c.shape
    n_out = w_ada.shape[1]
    tn = 1024
    return pl.pallas_call(
        _ada_kernel,
        grid=(n_out // tn,),
        in_specs=[
            pl.BlockSpec((bsz, dm), lambda n: (0, 0)),
            pl.BlockSpec((dm, tn), lambda n: (0, n)),
            pl.BlockSpec((1, tn), lambda n: (0, n)),
        ],
        out_specs=pl.BlockSpec((bsz, tn), lambda n: (0, n)),
        out_shape=jax.ShapeDtypeStruct((bsz, n_out), F32),
        compiler_params=pltpu.CompilerParams(dimension_semantics=("arbitrary",)),
        name="ada_mod",
    )(c, w_ada, b_ada.reshape(1, n_out))


def _rms(x, eps=EPS):
    return x * lax.rsqrt(jnp.mean(x * x, axis=-1, keepdims=True) + eps)


def _gelu_tanh(x):
    return 0.5 * x * (1.0 + jnp.tanh(0.7978845608028654 * (x + 0.044715 * (x * x * x))))


def _slab_store(buf, row0, val):
    for s in range(buf.shape[0]):
        buf[s, pl.ds(row0, val.shape[0]), :] = val[:, s * LANES:(s + 1) * LANES]


def _slab_keep_tail(buf, keep, ts):
    for s in range(buf.shape[0]):
        buf[s, pl.ds(0, keep), :] = buf[s, pl.ds(ts, keep), :]


def _causal_tap_sum(buf, w_ref, bias_row, first, n_taps, ts):
    w = w_ref[...]
    cols = []
    for s in range(buf.shape[0]):
        lanes = slice(s * LANES, (s + 1) * LANES)
        chunks = []
        for c in range(0, ts, CONV_ROWS):
            acc = jnp.broadcast_to(bias_row[:, lanes], (CONV_ROWS, LANES))
            for k in range(n_taps):
                acc = acc + w[k:k + 1, lanes] * buf[s, pl.ds(c + first + k, CONV_ROWS), :]
            chunks.append(acc)
        cols.append(jnp.concatenate(chunks, axis=0))
    return jnp.concatenate(cols, axis=1)


def _linear_recurrence(a, b, a_s, b_s, hcar):
    ts = a.shape[0]
    blk = ts // SUBLANES
    pitch = blk + 1
    cols = []
    for j in range(a_s.shape[0]):
        lanes = slice(j * LANES, (j + 1) * LANES)
        for k in range(SUBLANES):
            a_s[j, pl.ds(k * pitch, blk), :] = a[k * blk:(k + 1) * blk, lanes]
            b_s[j, pl.ds(k * pitch, blk), :] = b[k * blk:(k + 1) * blk, lanes]
        h = jnp.zeros((SUBLANES, LANES), F32)
        p = jnp.ones((SUBLANES, LANES), F32)
        for i in range(blk):
            row_i = pl.ds(i, SUBLANES, stride=pitch)
            a_i = a_s[j, row_i, :]
            h = a_i * h + b_s[j, row_i, :]
            p = a_i * p
            b_s[j, row_i, :] = h
            a_s[j, row_i, :] = p
        state = hcar[:, lanes]
        entering = []
        for k in range(SUBLANES):
            entering.append(state)
            state = p[k:k + 1, :] * state + h[k:k + 1, :]
        hcar[:, lanes] = state
        h_in = jnp.concatenate(entering, axis=0)
        for i in range(blk):
            row_i = pl.ds(i, SUBLANES, stride=pitch)
            b_s[j, row_i, :] = b_s[j, row_i, :] + a_s[j, row_i, :] * h_in
        cols.append(jnp.concatenate([b_s[j, pl.ds(k * pitch, blk), :] for k in range(SUBLANES)], axis=0))
    return jnp.concatenate(cols, axis=1)


def _mix_kernel(x_ref, xn_ref, mod_ref, modn_ref, g1_ref, win_ref, cw_ref, cb_ref, lng_ref, lnb_ref,
                lcw_ref, lcb_ref, wg_ref, bg_ref, lam_ref, mng_ref, wout_ref, g2_ref,
                wrt_ref, br_ref, anchor_ref,
                x1_ref, h2p_ref, idx_ref, rank_ref, gt_ref, cnt_ref,
                u_a, u_b, vbuf, rbuf, a_s, b_s, hcar, cnt_s,
                *, tile, n_tiles, steps_per_seq, d_conv, d_lru, conv_w, lru_cw, n_exp):
    del anchor_ref
    s = pl.program_id(0)

    @pl.when(s % steps_per_seq == 0)
    def _():
        _slab_store(vbuf, 0, jnp.zeros((CONV_PAD, d_conv), F32))
        _slab_store(rbuf, 0, jnp.zeros((SUBLANES, d_lru), F32))
        hcar[...] = jnp.zeros_like(hcar)

    mod = mod_ref[...]
    modn = modn_ref[...]
    shift1, gate1, shift2 = mod[0:1], mod[2:3], mod[3:4]
    gain1 = g1_ref[...] * (1.0 + mod[1:2])
    gain2 = g2_ref[...] * (1.0 + mod[4:5])
    lam = lam_ref[...]
    softplus_neg_lam = jnp.maximum(-lam, 0.0) + jnp.log1p(jnp.exp(-jnp.abs(lam)))

    def project(src_ref, r0, gain, shift, u_ref):
        h = _rms(src_ref[pl.ds(r0, tile), :]) * gain + shift
        u_ref[...] = jnp.dot(h.astype(BF16), win_ref[...], preferred_element_type=F32)

    def conv_group(r0, u_ref):
        return _mix_conv_group(r0, tile, u_ref, cw_ref, cb_ref, lng_ref, lnb_ref, vbuf, d_conv=d_conv,
                               conv_w=conv_w)

    def finish(r0, u_ref, yc):
        _mix_finish(r0, tile, x_ref, u_ref, yc, gate1, shift2, gain2, softplus_neg_lam,
                    lcw_ref, lcb_ref, wg_ref, bg_ref, mng_ref, wout_ref,
                    wrt_ref, br_ref, x1_ref, h2p_ref, idx_ref, rank_ref, gt_ref, rbuf, a_s, b_s, hcar,
                    cnt_s, d_conv=d_conv, d_lru=d_lru, lru_cw=lru_cw, n_exp=n_exp)

    @pl.when(s == 0)
    def _():
        cnt_s[...] = jnp.zeros_like(cnt_s)
        project(x_ref, 0, gain1, shift1, u_a)

    bufs = (u_a, u_b)
    for q in range(n_tiles):
        cur, nxt = bufs[q % 2], bufs[(q + 1) % 2]
        yc = conv_group(q * tile, cur)
        if q + 1 < n_tiles:
            project(x_ref, (q + 1) * tile, gain1, shift1, nxt)
        else:
            project(xn_ref, 0, g1_ref[...] * (1.0 + modn[1:2]), modn[0:1], nxt)
        finish(q * tile, cur, yc)
    _slab_keep_tail(vbuf, CONV_PAD, n_tiles * tile)
    _slab_keep_tail(rbuf, SUBLANES, n_tiles * tile)
    cnt_ref[...] = cnt_s[:, :LANES]


def _mix_conv_group(r0, ts, u_ref, cw_ref, cb_ref, lng_ref, lnb_ref, vbuf, *, d_conv, conv_w):
    v = u_ref[:, :d_conv] * _sigmoid(u_ref[:, d_conv:2 * d_conv])
    _slab_store(vbuf, CONV_PAD + r0, v)
    acc = _causal_tap_sum(vbuf, cw_ref, cb_ref[...], r0 + CONV_PAD - (conv_w - 1), conv_w, ts)
    mu = jnp.mean(acc, axis=-1, keepdims=True)
    cen = acc - mu
    var = jnp.mean(cen * cen, axis=-1, keepdims=True)
    yc = cen * lax.rsqrt(var + EPS) * lng_ref[...] + lnb_ref[...]
    return yc * _sigmoid(yc)


def _mix_finish(r0, ts, x_ref, u_ref, yc, gate1, shift2, gain2, softplus_neg_lam,
                lcw_ref, lcb_ref, wg_ref, bg_ref, mng_ref, wout_ref,
                wrt_ref, br_ref, x1_ref, h2p_ref, idx_ref, rank_ref, gt_ref, rbuf, a_s, b_s, hcar,
                cnt_s, *, d_conv, d_lru, lru_cw, n_exp):
    rows = pl.ds(r0, ts)
    x = x_ref[rows, :]
    u = u_ref

    u_gate = u[:, 2 * d_conv:2 * d_conv + d_lru]
    _slab_store(rbuf, SUBLANES + r0, u[:, 2 * d_conv + d_lru:])
    xr = _causal_tap_sum(rbuf, lcw_ref, lcb_ref[...], r0 + SUBLANES - (lru_cw - 1), lru_cw, ts)
    gates = jnp.dot(xr.astype(BF16), wg_ref[...], preferred_element_type=F32) + bg_ref[...]
    r = _sigmoid(gates[:, :d_lru])
    i_g = _sigmoid(gates[:, d_lru:])
    log_a = (-LRU_C) * r * softplus_neg_lam
    a = jnp.exp(log_a)
    inp = jnp.sqrt(1.0 - jnp.exp(2.0 * log_a)) * (i_g * xr)

    yl = _gelu_tanh(u_gate) * _linear_recurrence(a, inp, a_s, b_s, hcar)

    mng = mng_ref[...]
    yc_n = _rms(yc) * mng[:, :d_conv]
    yl_n = _rms(yl) * mng[:, d_conv:]
    mixo = jnp.dot(yc_n.astype(BF16), wout_ref[pl.ds(0, d_conv), :], preferred_element_type=F32)
    mixo += jnp.dot(yl_n.astype(BF16), wout_ref[pl.ds(d_conv, d_lru), :], preferred_element_type=F32)
    x1 = x + gate1 * mixo
    x1_ref[rows, :] = x1

    h2 = _rms(x1) * gain2 + shift2
    dh = h2.shape[1] // 2
    h2_hi = h2.astype(BF16)
    _store_planes(h2p_ref, _pack_bf16_pair(h2[:, :dh], h2[:, dh:]), rows)
    h2_lo = (h2 - h2_hi.astype(F32)).astype(BF16)
    nt_dims = (((1,), (1,)), ((), ()))
    wrt = wrt_ref[...]
    lg = lax.dot_general(wrt, h2_hi, nt_dims, preferred_element_type=F32)
    lg2 = lax.dot_general(wrt[:n_exp], h2_lo, nt_dims, preferred_element_type=F32)
    logits = lg[:n_exp] + lg[n_exp:] + lg2 + br_ref[...]

    eidx = lax.broadcasted_iota(I32, (n_exp, ts), 0)
    neg_inf = jnp.float32(-jnp.inf)
    work = logits
    vals, idxs, hots = [], [], []
    for _ in range(TOP_K):
        mval = jnp.max(work, axis=0, keepdims=True)
        midx = jnp.min(jnp.where(work == mval, eidx, n_exp), axis=0, keepdims=True)
        hot = eidx == midx
        vals.append(mval)
        idxs.append(midx)
        hots.append(hot)
        work = jnp.where(hot, neg_inf, work)
    exps = [jnp.exp(vk - vals[0]) for vk in vals]
    denom = exps[0] + exps[1] + exps[2] + exps[3]
    gate_rows = [ek / denom for ek in exps]

    sel = jnp.zeros((n_exp, ts), F32)
    for hot in hots:
        sel = sel + hot.astype(F32)
    tri = (lax.broadcasted_iota(I32, (ts, ts), 0) < lax.broadcasted_iota(I32, (ts, ts), 1)).astype(BF16)
    before = jnp.dot(sel.astype(BF16), tri, preferred_element_type=F32) + cnt_s[...]
    rank_rows = [jnp.sum(jnp.where(hot, before, 0.0), axis=0, keepdims=True) for hot in hots]
    cnt_s[...] = cnt_s[...] + jnp.sum(sel, axis=1, keepdims=True)

    idx_ref[:, rows] = jnp.concatenate(idxs, axis=0)
    rank_ref[:, rows] = jnp.concatenate(rank_rows, axis=0).astype(I32)
    g4 = jnp.concatenate(gate_rows, axis=0)
    g_pad = jnp.concatenate([g4, jnp.zeros((LANES - TOP_K, ts), F32)], axis=0)
    gt_ref[rows, :] = g_pad.T


def _mix_call(x, mod, p, b0, bsz, anchor):
    _, seq, dm = x.shape
    n_tiles = TILES_PER_STEP
    tile = min(SEQ_TILE, seq // n_tiles)
    ts = n_tiles * tile
    nj = seq // ts
    n_steps = bsz * nj
    tiles_per_seq = seq // tile
    d_conv = p["conv_w"].shape[1]
    d_lru = p["lru_conv_w"].shape[1]
    conv_w = p["conv_w"].shape[0]
    lru_cw = p["lru_conv_w"].shape[0]
    n_exp = p["wrt"].shape[0] // 2
    assert seq % ts == 0 and tile % LANES == 0 and tile % CONV_ROWS == 0 and n_tiles % 2 == 0
    assert conv_w - 1 <= CONV_PAD and lru_cw - 1 <= SUBLANES and d_conv % LANES == 0 and d_lru % LANES == 0

    def full(a):
        return pl.BlockSpec(a.shape, lambda s: (0,) * a.ndim)

    def next_tile(s):
        return jnp.minimum(n_tiles * (s + 1), bsz * tiles_per_seq - 1)

    weights = [p["g1"], p["win"], p["conv_w"], p["conv_b"], p["ln_g"], p["ln_b"], p["lru_conv_w"],
               p["lru_conv_b"], p["wg"], p["bg"], p["lam"], p["mng"], p["wout"], p["g2"], p["wrt"], p["br"]]
    kern = functools.partial(_mix_kernel, tile=tile, n_tiles=n_tiles, steps_per_seq=nj, d_conv=d_conv, d_lru=d_lru,
                             conv_w=conv_w, lru_cw=lru_cw, n_exp=n_exp)
    out_shape = (
        jax.ShapeDtypeStruct((bsz, seq, dm), F32),
        jax.ShapeDtypeStruct((SC_SPLIT, bsz * seq, dm // 2 // SC_SPLIT), I32),
        jax.ShapeDtypeStruct((TOP_K, bsz * seq), I32),
        jax.ShapeDtypeStruct((TOP_K, bsz * seq), I32),
        jax.ShapeDtypeStruct((bsz, seq, LANES), F32),
        jax.ShapeDtypeStruct((n_exp, LANES), F32),
    )
    return pl.pallas_call(
        kern,
        grid=(n_steps,),
        in_specs=[pl.BlockSpec((None, ts, dm), lambda s: (s // nj + b0, s % nj, 0)),
                  pl.BlockSpec((None, tile, dm),
                               lambda s: (next_tile(s) // tiles_per_seq + b0, next_tile(s) % tiles_per_seq, 0)),
                  pl.BlockSpec((None, N_MOD, dm), lambda s: (s // nj + b0, 0, 0)),
                  pl.BlockSpec((None, N_MOD, dm), lambda s: (next_tile(s) // tiles_per_seq + b0, 0, 0))]
        + [full(w) for w in weights] + [pl.BlockSpec(memory_space=pl.ANY)],
        out_specs=(
            pl.BlockSpec((None, ts, dm), lambda s: (s // nj, s % nj, 0)),
            pl.BlockSpec((SC_SPLIT, ts, dm // 2 // SC_SPLIT), lambda s: (0, s, 0)),
            pl.BlockSpec((TOP_K, ts), lambda s: (0, s)),
            pl.BlockSpec((TOP_K, ts), lambda s: (0, s)),
            pl.BlockSpec((None, ts, LANES), lambda s: (s // nj, s % nj, 0)),
            pl.BlockSpec((n_exp, LANES), lambda s: (0, 0)),
        ),
        out_shape=out_shape,
        scratch_shapes=[
            pltpu.VMEM((tile, 2 * d_conv + 2 * d_lru), F32),
            pltpu.VMEM((tile, 2 * d_conv + 2 * d_lru), F32),
            pltpu.VMEM((d_conv // LANES, CONV_PAD + ts, LANES), F32),
            pltpu.VMEM((d_lru // LANES, SUBLANES + ts, LANES), F32),
            pltpu.VMEM((d_lru // LANES, tile + SUBLANES, LANES), F32),
            pltpu.VMEM((d_lru // LANES, tile + SUBLANES, LANES), F32),
            pltpu.VMEM((1, d_lru), F32),
            pltpu.VMEM((n_exp, tile), F32),
        ],
        compiler_params=pltpu.CompilerParams(dimension_semantics=("arbitrary",),
                                             vmem_limit_bytes=VMEM_LIMIT_BYTES),
        name="token_mix_route",
    )(x, x, mod, mod, *weights, anchor)


def _sc_mesh():
    return plsc.VectorSubcoreMesh(core_axis_name="core", subcore_axis_name="subcore",
                                  num_cores=SC_CORES, num_subcores=SC_SUBCORES)


def _sc_scatter_rows(rows, idx, n_out, after):
    n, w = rows.shape
    n_k = idx.shape[0]
    assert n % (SC_WINDOW * SC_CORES * SC_SUBCORES) == 0

    @functools.partial(pl.kernel, out_type=jax.ShapeDtypeStruct((n_out, w), rows.dtype), mesh=_sc_mesh(),
                       scratch_types=[], name="moe_dispatch_sc")
    def scatter(x_hbm, i_hbm, after_hbm, o_hbm):
        del after_hbm

        def body(x_vmem, *i_vmems):
            for i_vmem in i_vmems:
                pltpu.sync_copy(x_vmem, o_hbm.at[i_vmem.at[0]])

        pltpu.emit_pipeline(
            body,
            grid=(n // SC_WINDOW,),
            in_specs=[pl.BlockSpec((SC_WINDOW, w), lambda i: (i, 0))]
            + [pl.BlockSpec((1, SC_WINDOW), functools.partial(lambda k, i: (k, i), k)) for k in range(n_k)],
            out_specs=[],
            core_axis_name=("core", "subcore"),
            dimension_semantics=(pltpu.PARALLEL,),
        )(x_hbm, *([i_hbm] * n_k))

    return scatter(rows, idx, after)


def _sc_gather_rows(table, idx):
    m = idx.shape[1]
    w = table.shape[1]
    assert m % (SC_WINDOW * SC_CORES * SC_SUBCORES) == 0

    @functools.partial(pl.kernel, out_type=jax.ShapeDtypeStruct((m, w), table.dtype), mesh=_sc_mesh(),
                       scratch_types=[], name="moe_gather_sc")
    def gather(x_hbm, i_hbm, o_hbm):
        def body(i_vmem, o_vmem):
            pltpu.sync_copy(x_hbm.at[i_vmem.at[0]], o_vmem)

        pltpu.emit_pipeline(
            body,
            grid=(m // SC_WINDOW,),
            in_specs=[pl.BlockSpec((1, SC_WINDOW), lambda i: (0, i))],
            out_specs=[pl.BlockSpec((SC_WINDOW, w), lambda i: (i, 0))],
            core_axis_name=("core", "subcore"),
            dimension_semantics=(pltpu.PARALLEL,),
        )(i_hbm, o_hbm)

    return gather(table, idx)


def _sc_pack_rows_bf16(w):
    r, c = w.shape
    assert r % SC_PACK_ROWS == 0 and c % SC_PACK_COLS == 0 and (r // SC_PACK_ROWS) % (SC_CORES * SC_SUBCORES) == 0
    lanes = SC_LANES

    @functools.partial(pl.kernel, out_type=jax.ShapeDtypeStruct((r // 2, c), I32), mesh=_sc_mesh(),
                       scratch_types=[], compiler_params=pltpu.CompilerParams(needs_layout_passes=False),
                       name="expert_weight_pack_sc")
    def pack(x_hbm, o_hbm):
        def body(x_vmem, o_vmem):
            @pl.loop(0, SC_PACK_ROWS // 2)
            def _(i):
                for j in range(0, SC_PACK_COLS, lanes):
                    pair = plsc.pack(x_vmem[2 * i, pl.ds(j, lanes)], x_vmem[2 * i + 1, pl.ds(j, lanes)],
                                     format=plsc.PackFormat.INTERLEAVED)
                    o_vmem[i, pl.ds(j, lanes)] = plsc.bitcast(pair, I32)

        pltpu.emit_pipeline(
            body,
            grid=(r // SC_PACK_ROWS, c // SC_PACK_COLS),
            in_specs=[pl.BlockSpec((SC_PACK_ROWS, SC_PACK_COLS), lambda i, j: (i, j))],
            out_specs=[pl.BlockSpec((SC_PACK_ROWS // 2, SC_PACK_COLS), lambda i, j: (i, j))],
            core_axis_name=("core", "subcore"),
            dimension_semantics=(pltpu.PARALLEL, pltpu.PARALLEL),
        )(x_hbm, o_hbm)

    return pack(w)


def _moe_kernel(be_ref, nb_ref, slot_ref, nxt_ref, nv_ref, xs_ref, wgu_hbm, bgu_ref, wd_hbm, bd_ref, ys_ref,
                wgu_s, wd_s, sems, *, d_ff):
    i = pl.program_id(0)
    live = i < nb_ref[0]
    new_expert = (i == 0) | (be_ref[i] != be_ref[jnp.maximum(i - 1, 0)])
    slot = slot_ref[i]

    def weight_copies(e, s):
        return (pltpu.make_async_copy(wgu_hbm.at[e], wgu_s.at[s], sems.at[0, s]),
                pltpu.make_async_copy(wd_hbm.at[e], wd_s.at[s], sems.at[1, s]))

    @pl.when(live & new_expert)
    def _():
        e, nx = be_ref[i], nxt_ref[i]

        @pl.when(i == 0)
        def _():
            for cp in weight_copies(e, slot):
                cp.start()

        for cp in weight_copies(e, slot):
            cp.wait()

        @pl.when(nx >= 0)
        def _():
            for cp in weight_copies(nx, 1 - slot):
                cp.start()

    def sub_block(s):
        rows = pl.ds(s * MOE_SUB, MOE_SUB)
        packed = jnp.concatenate([xs_ref[p, rows, :] for p in range(SC_SPLIT)], axis=1)
        x_lo, x_hi = _unpack_bf16_pair(packed)
        dh = x_lo.shape[1]
        w_top = pltpu.bitcast(wgu_s[slot, pl.ds(0, dh // 2), :], BF16)
        w_bot = pltpu.bitcast(wgu_s[slot, pl.ds(dh // 2, dh // 2), :], BF16)
        gu = jnp.dot(x_lo.astype(BF16), w_top, preferred_element_type=F32)
        gu += jnp.dot(x_hi.astype(BF16), w_bot, preferred_element_type=F32)
        gu += bgu_ref[...]
        g = jnp.minimum(gu[:, :d_ff], SWIGLU_LIMIT)
        u = jnp.clip(gu[:, d_ff:], -SWIGLU_LIMIT, SWIGLU_LIMIT)
        act = (u + 1.0) * (g * _sigmoid(SWIGLU_ALPHA * g))
        y = jnp.dot(act.astype(BF16), pltpu.bitcast(wd_s[slot], BF16), preferred_element_type=F32) + bd_ref[...]
        dm2 = y.shape[1] // 2
        _store_planes(ys_ref, _pack_bf16_pair(y[:, :dm2], y[:, dm2:]), rows)

    n_sub = xs_ref.shape[1] // MOE_SUB
    n_valid = nv_ref[i]

    @pl.when(live & (n_valid == n_sub * MOE_SUB))
    def _():
        for s in range(n_sub):
            sub_block(s)

    @pl.when(live & (n_valid < n_sub * MOE_SUB))
    def _():
        for s in range(n_sub):
            pl.when(s * MOE_SUB < n_valid)(functools.partial(sub_block, s))


def _moe_call(xs, block_expert, n_used, seg_slot, next_expert, block_valid, wgu, bgu, wd, bd):
    _, n_rows, wp = xs.shape
    n_exp, dm_half, d_ff2 = wgu.shape
    dm = 2 * dm_half
    d_ff = d_ff2 // 2
    bm = MOE_BLOCK
    n_blocks = n_rows // bm

    def row_map(i, be, nb, slot, nxt, nv):
        return (0, jnp.minimum(i, nb[0] - 1), 0)

    def b_map(i, be, nb, slot, nxt, nv):
        return (be[jnp.minimum(i, nb[0] - 1)], 0, 0)

    grid_spec = pltpu.PrefetchScalarGridSpec(
        num_scalar_prefetch=5,
        grid=(n_blocks,),
        in_specs=[
            pl.BlockSpec((SC_SPLIT, bm, wp), row_map),
            pl.BlockSpec(memory_space=pl.ANY),
            pl.BlockSpec((None, 1, d_ff2), b_map),
            pl.BlockSpec(memory_space=pl.ANY),
            pl.BlockSpec((None, 1, dm), b_map),
        ],
        out_specs=pl.BlockSpec((SC_SPLIT, bm, wp), row_map),
        scratch_shapes=[
            pltpu.VMEM((2,) + wgu.shape[1:], I32), pltpu.VMEM((2,) + wd.shape[1:], I32),
            pltpu.SemaphoreType.DMA((2, 2)),
        ],
    )
    return pl.pallas_call(
        functools.partial(_moe_kernel, d_ff=d_ff),
        grid_spec=grid_spec,
        out_shape=jax.ShapeDtypeStruct(xs.shape, I32),
        compiler_params=pltpu.CompilerParams(dimension_semantics=("arbitrary",),
                                             vmem_limit_bytes=VMEM_LIMIT_BYTES),
        name="moe_experts",
    )(block_expert, n_used, seg_slot, next_expert, block_valid, xs, wgu, bgu, wd, bd)


def _combine_kernel(x1_ref, gt_ref, mod_ref, gf_ref, yg_ref, *out_refs):
    o_ref = out_refs[-1]
    gt = gt_ref[...]
    ts = yg_ref.shape[2]
    dh = yg_ref.shape[3] * SC_SPLIT
    acc_lo = jnp.zeros((ts, dh), F32)
    acc_hi = jnp.zeros((ts, dh), F32)
    for k in range(TOP_K):
        lo, hi = _unpack_bf16_pair(_load_planes(yg_ref.at[k]))
        gk = gt[:, k:k + 1]
        acc_lo += gk * lo
        acc_hi += gk * hi
    y = jnp.concatenate([acc_lo, acc_hi], axis=1)
    gate2 = mod_ref[...][5:6]
    x2 = x1_ref[...] + gate2 * y
    o_ref[...] = _rms(x2) * gf_ref[...]


def _combine_call(x1, gt, mod, gf, yg, out_prev, b0):
    bsz, seq, dm = x1.shape
    ts = min(COMBINE_TILE, seq)
    assert seq % ts == 0
    nj = seq // ts
    in_specs = [
        pl.BlockSpec((None, ts, dm), lambda b, j: (b, j, 0)),
        pl.BlockSpec((None, ts, LANES), lambda b, j: (b, j, 0)),
        pl.BlockSpec((None, N_MOD, dm), lambda b, j: (b + b0, 0, 0)),
        pl.BlockSpec((1, dm), lambda b, j: (0, 0)),
        pl.BlockSpec((TOP_K, SC_SPLIT, ts, yg.shape[3]), lambda b, j: (0, 0, b * nj + j, 0)),
    ]
    args = [x1, gt, mod, gf, yg]
    aliases = {}
    if out_prev is not None:
        in_specs.append(pl.BlockSpec(memory_space=pl.ANY))
        args.append(out_prev)
        aliases = {len(args) - 1: 0}
    return pl.pallas_call(
        _combine_kernel,
        grid=(bsz, nj),
        in_specs=in_specs,
        out_specs=pl.BlockSpec((None, ts, dm), lambda b, j: (b + b0, j, 0)),
        out_shape=jax.ShapeDtypeStruct((mod.shape[0], seq, dm), F32),
        input_output_aliases=aliases,
        compiler_params=pltpu.CompilerParams(dimension_semantics=("arbitrary", "arbitrary"),
                                             vmem_limit_bytes=VMEM_LIMIT_BYTES),
        name="moe_combine",
    )(*args)


def _block_diag(w):
    n_h, d, _ = w.shape
    eye = jnp.eye(n_h, dtype=w.dtype)
    return (eye[:, None, :, None] * w[:, :, None, :]).reshape(n_h * d, n_h * d)


def _layer(x, mod, l, w_in, norm1_g, conv_w, conv_b, conv_ln_g, conv_ln_b, lru_conv_w, lru_conv_b,
           lru_w_a, lru_b_a, lru_w_x, lru_b_x, lru_lambda, mix_norm_g, w_out, norm2_g, w_router,
           b_router, w_gate_up, b_gate_up, w_down, b_down, out_gain):
    bsz, seq, dm = x.shape
    n_tok = bsz * seq
    n_exp = w_router.shape[-1]
    row = lambda a: a.reshape(1, -1)
    wr = w_router[l]
    wr_hi = wr.astype(BF16)
    wr_lo = (wr - wr_hi.astype(F32)).astype(BF16)
    params = dict(
        g1=row(norm1_g[l]), win=w_in[l].astype(BF16), conv_w=conv_w[l], conv_b=row(conv_b[l]),
        ln_g=row(conv_ln_g[l]), ln_b=row(conv_ln_b[l]), lru_conv_w=lru_conv_w[l], lru_conv_b=row(lru_conv_b[l]),
        wg=jnp.concatenate([_block_diag(lru_w_a[l]), _block_diag(lru_w_x[l])], axis=1).astype(BF16),
        bg=jnp.concatenate([lru_b_a[l].reshape(1, -1), lru_b_x[l].reshape(1, -1)], axis=1),
        lam=row(lru_lambda[l]), mng=row(mix_norm_g[l]), wout=w_out[l].astype(BF16), g2=row(norm2_g[l]),
        wrt=jnp.concatenate([wr_hi.T, wr_lo.T], axis=0), br=b_router[l].reshape(n_exp, 1),
    )
    _, d_in, d_ff2 = w_gate_up[l].shape
    _, d_ff, d_out = w_down[l].shape
    wgu_p = _sc_pack_rows_bf16(w_gate_up[l].reshape(n_exp * d_in, d_ff2)).reshape(n_exp, d_in // 2, d_ff2)
    wd_p = _sc_pack_rows_bf16(w_down[l].reshape(n_exp * d_ff, d_out)).reshape(n_exp, d_ff // 2, d_out)
    sizes = [s for s in CHUNK_BATCHES if s > 0] if sum(CHUNK_BATCHES) == bsz else [bsz]
    routed, b0, anchor = [], 0, mod
    for cb in sizes:
        r = _route_chunk(x, mod, params, b0, cb, anchor, n_exp)
        routed.append(r)
        anchor = r["dest_sub"]
        b0 += cb
    out = None
    sc_order = [wgu_p] + [wd_p] * (len(routed) - 1)
    for r, after in zip(routed, sc_order):
        out = _experts_chunk(r, mod, out, wgu_p, b_gate_up[l][:, None, :], wd_p, b_down[l][:, None, :], out_gain,
                             after)
    return out


def _route_chunk(x, mod, params, b0, cb, anchor, n_exp):
    seq, dm = x.shape[1:]
    n_tok = cb * seq
    x1, h2p, top_idx, rank, gt, cnt = _mix_call(x, mod, params, b0, cb, anchor)

    bm = MOE_BLOCK
    counts = cnt[:, 0].astype(I32)
    padded = (counts + bm - 1) // bm * bm
    e_ids = jnp.arange(n_exp, dtype=I32)
    pad_ends = jnp.sum(jnp.where(e_ids[None, :] <= e_ids[:, None], padded[None, :], 0), axis=1)
    pad_starts = pad_ends - padded
    n_rows = (n_tok * TOP_K // bm + n_exp) * bm
    n_blocks = n_rows // bm
    dest = rank + jnp.sum(jnp.where(top_idx[..., None] == e_ids, pad_starts, 0), axis=-1)
    block_start = jnp.arange(n_blocks, dtype=I32) * bm
    block_expert = jnp.minimum(
        jnp.sum((block_start[:, None] >= pad_ends[None, :]).astype(I32), axis=1), n_exp - 1)
    n_used = pad_ends[-1:] // bm
    present = counts > 0
    seg_ordinal = jnp.sum(jnp.where((e_ids[None, :] < e_ids[:, None]) & present[None, :], 1, 0), axis=1)
    following = jnp.min(jnp.where((e_ids[None, :] > e_ids[:, None]) & present[None, :], e_ids[None, :], n_exp),
                        axis=1)
    following = jnp.where(following == n_exp, -1, following)
    block_hot = block_expert[:, None] == e_ids[None, :]
    seg_slot = jnp.sum(jnp.where(block_hot, seg_ordinal[None, :] % 2, 0), axis=1)
    next_expert = jnp.sum(jnp.where(block_hot, following[None, :], 0), axis=1)
    block_valid = jnp.clip(
        jnp.sum(jnp.where(block_hot, (pad_starts + counts)[None, :], 0), axis=1) - block_start, 0, bm)

    wp = h2p.shape[2]
    plane = jnp.arange(SC_SPLIT, dtype=I32)[None, :, None] * n_rows
    dest_sub = dest[:, None, :] + plane
    return dict(b0=b0, n_tok=n_tok, n_rows=n_rows, x1=x1, h2p=h2p, gt=gt, dest_sub=dest_sub,
                tables=(block_expert, n_used, seg_slot, next_expert, block_valid))


def _experts_chunk(r, mod, out_prev, wgu, bgu, wd, bd, out_gain, dispatch_after):
    n_tok, n_rows, dest_sub = r["n_tok"], r["n_rows"], r["dest_sub"]
    wp = r["h2p"].shape[2]
    xs = _sc_scatter_rows(r["h2p"].reshape(SC_SPLIT * n_tok, wp), dest_sub.reshape(TOP_K, SC_SPLIT * n_tok),
                          SC_SPLIT * n_rows, dispatch_after).reshape(SC_SPLIT, n_rows, wp)
    ys = _moe_call(xs, *r["tables"], wgu, bgu, wd, bd)
    yg = _sc_gather_rows(ys.reshape(SC_SPLIT * n_rows, wp), dest_sub.reshape(1, TOP_K * SC_SPLIT * n_tok))
    return _combine_call(r["x1"], r["gt"], mod, out_gain, yg.reshape(TOP_K, SC_SPLIT, n_tok, wp), out_prev,
                         r["b0"])


def kernel(x, c, w_ada, b_ada, norm1_g, w_in, conv_w, conv_b, conv_ln_g, conv_ln_b, lru_conv_w, lru_conv_b,
           lru_w_a, lru_b_a, lru_w_x, lru_b_x, lru_lambda, mix_norm_g, w_out, norm2_g, w_router, b_router,
           w_gate_up, b_gate_up, w_down, b_down, final_norm_g):
    depth = w_ada.shape[0]
    assert depth == 1, "the final norm is fused into the (single) layer's combine kernel"
    bsz, seq, dm = x.shape
    mod = _ada_call(c, w_ada[0], b_ada[0]).reshape(bsz, N_MOD, dm)
    return _layer(x, mod, 0, w_in, norm1_g, conv_w, conv_b, conv_ln_g, conv_ln_b, lru_conv_w, lru_conv_b,
                  lru_w_a, lru_b_a, lru_w_x, lru_b_x, lru_lambda, mix_norm_g, w_out, norm2_g, w_router,
                  b_router, w_gate_up, b_gate_up, w_down, b_down, row_gain(final_norm_g))


def row_gain(g):
    return g.reshape(1, -1)
```

```python
import functools

import jax
import jax.numpy as jnp
from jax import lax
from jax.experimental import pallas as pl
from jax.experimental.pallas import tpu as pltpu
from jax.experimental.pallas import tpu_sc as plsc

F32 = jnp.float32
BF16 = jnp.bfloat16
I32 = jnp.int32

EPS = 1e-6
N_MOD = 6
LRU_C = 8.0
TOP_K = 4
SWIGLU_ALPHA = 1.702
SWIGLU_LIMIT = 7.0

LANES = 128
SUBLANES = 8
VMEM_LIMIT_BYTES = 56 * 1024 * 1024

SEQ_TILE = 512
TILES_PER_STEP = 2
COMBINE_TILE = 512
CONV_PAD = 32
CONV_ROWS = 128
MOE_BLOCK = 1024
MOE_SUB = 256
CHUNK_BATCHES = (6, 2)
HI_MASK = -65536

SC_CORES = 2
SC_SUBCORES = 16
SC_LANES = 16
SC_PACK_ROWS = 16
SC_PACK_COLS = 512
SC_WINDOW = 128
SC_SPLIT = 2


def _sigmoid(x):
    return 0.5 * jnp.tanh(0.5 * x) + 0.5


def _pack_bf16_pair(lo_f32, hi_f32):
    lo_bits = lax.bitcast_convert_type(lo_f32.astype(BF16).astype(F32), I32)
    hi_bits = lax.bitcast_convert_type(hi_f32.astype(BF16).astype(F32), I32)
    return lax.shift_right_logical(lo_bits, 16) | hi_bits


def _unpack_bf16_pair(p):
    lo = lax.bitcast_convert_type(lax.shift_left(p, 16), F32)
    hi = lax.bitcast_convert_type(p & HI_MASK, F32)
    return lo, hi


def _store_planes(ref, packed, rows=slice(None)):
    wp = packed.shape[1] // SC_SPLIT
    for s in range(SC_SPLIT):
        ref[s, rows, :] = packed[:, s * wp:(s + 1) * wp]


def _load_planes(ref):
    return jnp.concatenate([ref[s] for s in range(SC_SPLIT)], axis=1)


def _ada_kernel(c_ref, w_ref, b_ref, o_ref):
    c = c_ref[...]
    ca = c * _sigmoid(c)
    w = w_ref[...]
    c_hi = ca.astype(BF16)
    c_lo = (ca - c_hi.astype(F32)).astype(BF16)
    w_hi = w.astype(BF16)
    w_lo = (w - w_hi.astype(F32)).astype(BF16)
    acc = jnp.dot(c_hi, w_hi, preferred_element_type=F32)
    acc += jnp.dot(c_lo, w_hi, preferred_element_type=F32)
    acc += jnp.dot(c_hi, w_lo, preferred_element_type=F32)
    o_ref[...] = acc + b_ref[...]


def _ada_call(c, w_ada, b_ada):
    bsz, dm = c.shape
    n_out = w_ada.shape[1]
    tn = 1024
    return pl.pallas_call(
        _ada_kernel,
        grid=(n_out // tn,),
        in_specs=[
            pl.BlockSpec((bsz, dm), lambda n: (0, 0)),
            pl.BlockSpec((dm, tn), lambda n: (0, n)),
            pl.BlockSpec((1, tn), lambda n: (0, n)),
        ],
        out_specs=pl.BlockSpec((bsz, tn), lambda n: (0, n)),
        out_shape=jax.ShapeDtypeStruct((bsz, n_out), F32),
        compiler_params=pltpu.CompilerParams(dimension_semantics=("arbitrary",)),
        name="ada_mod",
    )(c, w_ada, b_ada.reshape(1, n_out))


def _rms(x, eps=EPS):
    return x * lax.rsqrt(jnp.mean(x * x, axis=-1, keepdims=True) + eps)


def _gelu_tanh(x):
    return 0.5 * x * (1.0 + jnp.tanh(0.7978845608028654 * (x + 0.044715 * (x * x * x))))


def _slab_store(buf, row0, val):
    for s in range(buf.shape[0]):
        buf[s, pl.ds(row0, val.shape[0]), :] = val[:, s * LANES:(s + 1) * LANES]


def _slab_keep_tail(buf, keep, ts):
    for s in range(buf.shape[0]):
        buf[s, pl.ds(0, keep), :] = buf[s, pl.ds(ts, keep), :]


def _causal_tap_sum(buf, w_ref, bias_row, first, n_taps, ts):
    w = w_ref[...]
    cols = []
    for s in range(buf.shape[0]):
        lanes = slice(s * LANES, (s + 1) * LANES)
        chunks = []
        for c in range(0, ts, CONV_ROWS):
            acc = jnp.broadcast_to(bias_row[:, lanes], (CONV_ROWS, LANES))
            for k in range(n_taps):
                acc = acc + w[k:k + 1, lanes] * buf[s, pl.ds(c + first + k, CONV_ROWS), :]
            chunks.append(acc)
        cols.append(jnp.concatenate(chunks, axis=0))
    return jnp.concatenate(cols, axis=1)


def _linear_recurrence(a, b, a_s, b_s, hcar):
    ts = a.shape[0]
    blk = ts // SUBLANES
    pitch = blk + 1
    cols = []
    for j in range(a_s.shape[0]):
        lanes = slice(j * LANES, (j + 1) * LANES)
        for k in range(SUBLANES):
            a_s[j, pl.ds(k * pitch, blk), :] = a[k * blk:(k + 1) * blk, lanes]
            b_s[j, pl.ds(k * pitch, blk), :] = b[k * blk:(k + 1) * blk, lanes]
        h = jnp.zeros((SUBLANES, LANES), F32)
        p = jnp.ones((SUBLANES, LANES), F32)
        for i in range(blk):
            row_i = pl.ds(i, SUBLANES, stride=pitch)
            a_i = a_s[j, row_i, :]
            h = a_i * h + b_s[j, row_i, :]
            p = a_i * p
            b_s[j, row_i, :] = h
            a_s[j, row_i, :] = p
        state = hcar[:, lanes]
        entering = []
        for k in range(SUBLANES):
            entering.append(state)
            state = p[k:k + 1, :] * state + h[k:k + 1, :]
        hcar[:, lanes] = state
        h_in = jnp.concatenate(entering, axis=0)
        for i in range(blk):
            row_i = pl.ds(i, SUBLANES, stride=pitch)
            b_s[j, row_i, :] = b_s[j, row_i, :] + a_s[j, row_i, :] * h_in
        cols.append(jnp.concatenate([b_s[j, pl.ds(k * pitch, blk), :] for k in range(SUBLANES)], axis=0))
    return jnp.concatenate(cols, axis=1)


def _mix_kernel(x_ref, xn_ref, mod_ref, modn_ref, g1_ref, win_ref, cw_ref, cb_ref, lng_ref, lnb_ref,
                lcw_ref, lcb_ref, wg_ref, bg_ref, lam_ref, mng_ref, wout_ref, g2_ref,
                wrt_ref, br_ref, anchor_ref,
                x1_ref, h2p_ref, idx_ref, rank_ref, gt_ref, cnt_ref,
                u_a, u_b, vbuf, rbuf, a_s, b_s, hcar, cnt_s,
                *, tile, n_tiles, steps_per_seq, d_conv, d_lru, conv_w, lru_cw, n_exp):
    del anchor_ref
    s = pl.program_id(0)

    @pl.when(s % steps_per_seq == 0)
    def _():
        _slab_store(vbuf, 0, jnp.zeros((CONV_PAD, d_conv), F32))
        _slab_store(rbuf, 0, jnp.zeros((SUBLANES, d_lru), F32))
        hcar[...] = jnp.zeros_like(hcar)

    mod = mod_ref[...]
    modn = modn_ref[...]
    shift1, gate1, shift2 = mod[0:1], mod[2:3], mod[3:4]
    gain1 = g1_ref[...] * (1.0 + mod[1:2])
    gain2 = g2_ref[...] * (1.0 + mod[4:5])
    lam = lam_ref[...]
    softplus_neg_lam = jnp.maximum(-lam, 0.0) + jnp.log1p(jnp.exp(-jnp.abs(lam)))

    def project(src_ref, r0, gain, shift, u_ref):
        h = _rms(src_ref[pl.ds(r0, tile), :]) * gain + shift
        u_ref[...] = jnp.dot(h.astype(BF16), win_ref[...], preferred_element_type=F32)

    def conv_group(r0, u_ref):
        return _mix_conv_group(r0, tile, u_ref, cw_ref, cb_ref, lng_ref, lnb_ref, vbuf, d_conv=d_conv,
                               conv_w=conv_w)

    def finish(r0, u_ref, yc):
        _mix_finish(r0, tile, x_ref, u_ref, yc, gate1, shift2, gain2, softplus_neg_lam,
                    lcw_ref, lcb_ref, wg_ref, bg_ref, mng_ref, wout_ref,
                    wrt_ref, br_ref, x1_ref, h2p_ref, idx_ref, rank_ref, gt_ref, rbuf, a_s, b_s, hcar,
                    cnt_s, d_conv=d_conv, d_lru=d_lru, lru_cw=lru_cw, n_exp=n_exp)

    @pl.when(s == 0)
    def _():
        cnt_s[...] = jnp.zeros_like(cnt_s)
        project(x_ref, 0, gain1, shift1, u_a)

    bufs = (u_a, u_b)
    for q in range(n_tiles):
        cur, nxt = bufs[q % 2], bufs[(q + 1) % 2]
        yc = conv_group(q * tile, cur)
        if q + 1 < n_tiles:
            project(x_ref, (q + 1) * tile, gain1, shift1, nxt)
        else:
            project(xn_ref, 0, g1_ref[...] * (1.0 + modn[1:2]), modn[0:1], nxt)
        finish(q * tile, cur, yc)
    _slab_keep_tail(vbuf, CONV_PAD, n_tiles * tile)
    _slab_keep_tail(rbuf, SUBLANES, n_tiles * tile)
    cnt_ref[...] = cnt_s[:, :LANES]


def _mix_conv_group(r0, ts, u_ref, cw_ref, cb_ref, lng_ref, lnb_ref, vbuf, *, d_conv, conv_w):
    v = u_ref[:, :d_conv] * _sigmoid(u_ref[:, d_conv:2 * d_conv])
    _slab_store(vbuf, CONV_PAD + r0, v)
    acc = _causal_tap_sum(vbuf, cw_ref, cb_ref[...], r0 + CONV_PAD - (conv_w - 1), conv_w, ts)
    mu = jnp.mean(acc, axis=-1, keepdims=True)
    cen = acc - mu
    var = jnp.mean(cen * cen, axis=-1, keepdims=True)
    yc = cen * lax.rsqrt(var + EPS) * lng_ref[...] + lnb_ref[...]
    return yc * _sigmoid(yc)


def _mix_finish(r0, ts, x_ref, u_ref, yc, gate1, shift2, gain2, softplus_neg_lam,
                lcw_ref, lcb_ref, wg_ref, bg_ref, mng_ref, wout_ref,
                wrt_ref, br_ref, x1_ref, h2p_ref, idx_ref, rank_ref, gt_ref, rbuf, a_s, b_s, hcar,
                cnt_s, *, d_conv, d_lru, lru_cw, n_exp):
    rows = pl.ds(r0, ts)
    x = x_ref[rows, :]
    u = u_ref

    u_gate = u[:, 2 * d_conv:2 * d_conv + d_lru]
    _slab_store(rbuf, SUBLANES + r0, u[:, 2 * d_conv + d_lru:])
    xr = _causal_tap_sum(rbuf, lcw_ref, lcb_ref[...], r0 + SUBLANES - (lru_cw - 1), lru_cw, ts)
    gates = jnp.dot(xr.astype(BF16), wg_ref[...], preferred_element_type=F32) + bg_ref[...]
    r = _sigmoid(gates[:, :d_lru])
    i_g = _sigmoid(gates[:, d_lru:])
    log_a = (-LRU_C) * r * softplus_neg_lam
    a = jnp.exp(log_a)
    inp = jnp.sqrt(1.0 - jnp.exp(2.0 * log_a)) * (i_g * xr)

    yl = _gelu_tanh(u_gate) * _linear_recurrence(a, inp, a_s, b_s, hcar)

    mng = mng_ref[...]
    yc_n = _rms(yc) * mng[:, :d_conv]
    yl_n = _rms(yl) * mng[:, d_conv:]
    mixo = jnp.dot(yc_n.astype(BF16), wout_ref[pl.ds(0, d_conv), :], preferred_element_type=F32)
    mixo += jnp.dot(yl_n.astype(BF16), wout_ref[pl.ds(d_conv, d_lru), :], preferred_element_type=F32)
    x1 = x + gate1 * mixo
    x1_ref[rows, :] = x1

    h2 = _rms(x1) * gain2 + shift2
    dh = h2.shape[1] // 2
    h2_hi = h2.astype(BF16)
    _store_planes(h2p_ref, _pack_bf16_pair(h2[:, :dh], h2[:, dh:]), rows)
    h2_lo = (h2 - h2_hi.astype(F32)).astype(BF16)
    nt_dims = (((1,), (1,)), ((), ()))
    wrt = wrt_ref[...]
    lg = lax.dot_general(wrt, h2_hi, nt_dims, preferred_element_type=F32)
    lg2 = lax.dot_general(wrt[:n_exp], h2_lo, nt_dims, preferred_element_type=F32)
    logits = lg[:n_exp] + lg[n_exp:] + lg2 + br_ref[...]

    eidx = lax.broadcasted_iota(I32, (n_exp, ts), 0)
    neg_inf = jnp.float32(-jnp.inf)
    work = logits
    vals, idxs, hots = [], [], []
    for _ in range(TOP_K):
        mval = jnp.max(work, axis=0, keepdims=True)
        midx = jnp.min(jnp.where(work == mval, eidx, n_exp), axis=0, keepdims=True)
        hot = eidx == midx
        vals.append(mval)
        idxs.append(midx)
        hots.append(hot)
        work = jnp.where(hot, neg_inf, work)
    exps = [jnp.exp(vk - vals[0]) for vk in vals]
    denom = exps[0] + exps[1] + exps[2] + exps[3]
    gate_rows = [ek / denom for ek in exps]

    sel = jnp.zeros((n_exp, ts), F32)
    for hot in hots:
        sel = sel + hot.astype(F32)
    tri = (lax.broadcasted_iota(I32, (ts, ts), 0) < lax.broadcasted_iota(I32, (ts, ts), 1)).astype(BF16)
    before = jnp.dot(sel.astype(BF16), tri, preferred_element_type=F32) + cnt_s[...]
    rank_rows = [jnp.sum(jnp.where(hot, before, 0.0), axis=0, keepdims=True) for hot in hots]
    cnt_s[...] = cnt_s[...] + jnp.sum(sel, axis=1, keepdims=True)

    idx_ref[:, rows] = jnp.concatenate(idxs, axis=0)
    rank_ref[:, rows] = jnp.concatenate(rank_rows, axis=0).astype(I32)
    g4 = jnp.concatenate(gate_rows, axis=0)
    g_pad = jnp.concatenate([g4, jnp.zeros((LANES - TOP_K, ts), F32)], axis=0)
    gt_ref[rows, :] = g_pad.T


def _mix_call(x, mod, p, b0, bsz, anchor):
    _, seq, dm = x.shape
    n_tiles = TILES_PER_STEP
    tile = min(SEQ_TILE, seq // n_tiles)
    ts = n_tiles * tile
    nj = seq // ts
    n_steps = bsz * nj
    tiles_per_seq = seq // tile
    d_conv = p["conv_w"].shape[1]
    d_lru = p["lru_conv_w"].shape[1]
    conv_w = p["conv_w"].shape[0]
    lru_cw = p["lru_conv_w"].shape[0]
    n_exp = p["wrt"].shape[0] // 2
    assert seq % ts == 0 and tile % LANES == 0 and tile % CONV_ROWS == 0 and n_tiles % 2 == 0
    assert conv_w - 1 <= CONV_PAD and lru_cw - 1 <= SUBLANES and d_conv % LANES == 0 and d_lru % LANES == 0

    def full(a):
        return pl.BlockSpec(a.shape, lambda s: (0,) * a.ndim)

    def next_tile(s):
        return jnp.minimum(n_tiles * (s + 1), bsz * tiles_per_seq - 1)

    weights = [p["g1"], p["win"], p["conv_w"], p["conv_b"], p["ln_g"], p["ln_b"], p["lru_conv_w"],
               p["lru_conv_b"], p["wg"], p["bg"], p["lam"], p["mng"], p["wout"], p["g2"], p["wrt"], p["br"]]
    kern = functools.partial(_mix_kernel, tile=tile, n_tiles=n_tiles, steps_per_seq=nj, d_conv=d_conv, d_lru=d_lru,
                             conv_w=conv_w, lru_cw=lru_cw, n_exp=n_exp)
    out_shape = (
        jax.ShapeDtypeStruct((bsz, seq, dm), F32),
        jax.ShapeDtypeStruct((SC_SPLIT, bsz * seq, dm // 2 // SC_SPLIT), I32),
        jax.ShapeDtypeStruct((TOP_K, bsz * seq), I32),
        jax.ShapeDtypeStruct((TOP_K, bsz * seq), I32),
        jax.ShapeDtypeStruct((bsz, seq, LANES), F32),
        jax.ShapeDtypeStruct((n_exp, LANES), F32),
    )
    return pl.pallas_call(
        kern,
        grid=(n_steps,),
        in_specs=[pl.BlockSpec((None, ts, dm), lambda s: (s // nj + b0, s % nj, 0)),
                  pl.BlockSpec((None, tile, dm),
                               lambda s: (next_tile(s) // tiles_per_seq + b0, next_tile(s) % tiles_per_seq, 0)),
                  pl.BlockSpec((None, N_MOD, dm), lambda s: (s // nj + b0, 0, 0)),
                  pl.BlockSpec((None, N_MOD, dm), lambda s: (next_tile(s) // tiles_per_seq + b0, 0, 0))]
        + [full(w) for w in weights] + [pl.BlockSpec(memory_space=pl.ANY)],
        out_specs=(
            pl.BlockSpec((None, ts, dm), lambda s: (s // nj, s % nj, 0)),
            pl.BlockSpec((SC_SPLIT, ts, dm // 2 // SC_SPLIT), lambda s: (0, s, 0)),
            pl.BlockSpec((TOP_K, ts), lambda s: (0, s)),
            pl.BlockSpec((TOP_K, ts), lambda s: (0, s)),
            pl.BlockSpec((None, ts, LANES), lambda s: (s // nj, s % nj, 0)),
            pl.BlockSpec((n_exp, LANES), lambda s: (0, 0)),
        ),
        out_shape=out_shape,
        scratch_shapes=[
            pltpu.VMEM((tile, 2 * d_conv + 2 * d_lru), F32),
            pltpu.VMEM((tile, 2 * d_conv + 2 * d_lru), F32),
            pltpu.VMEM((d_conv // LANES, CONV_PAD + ts, LANES), F32),
            pltpu.VMEM((d_lru // LANES, SUBLANES + ts, LANES), F32),
            pltpu.VMEM((d_lru // LANES, tile + SUBLANES, LANES), F32),
            pltpu.VMEM((d_lru // LANES, tile + SUBLANES, LANES), F32),
            pltpu.VMEM((1, d_lru), F32),
            pltpu.VMEM((n_exp, tile), F32),
        ],
        compiler_params=pltpu.CompilerParams(dimension_semantics=("arbitrary",),
                                             vmem_limit_bytes=VMEM_LIMIT_BYTES),
        name="token_mix_route",
    )(x, x, mod, mod, *weights, anchor)


def _sc_mesh():
    return plsc.VectorSubcoreMesh(core_axis_name="core", subcore_axis_name="subcore",
                                  num_cores=SC_CORES, num_subcores=SC_SUBCORES)


def _sc_scatter_rows(rows, idx, n_out, after):
    n, w = rows.shape
    n_k = idx.shape[0]
    assert n % (SC_WINDOW * SC_CORES * SC_SUBCORES) == 0

    @functools.partial(pl.kernel, out_type=jax.ShapeDtypeStruct((n_out, w), rows.dtype), mesh=_sc_mesh(),
                       scratch_types=[], name="moe_dispatch_sc")
    def scatter(x_hbm, i_hbm, after_hbm, o_hbm):
        del after_hbm

        def body(x_vmem, *i_vmems):
            for i_vmem in i_vmems:
                pltpu.sync_copy(x_vmem, o_hbm.at[i_vmem.at[0]])

        pltpu.emit_pipeline(
            body,
            grid=(n // SC_WINDOW,),
            in_specs=[pl.BlockSpec((SC_WINDOW, w), lambda i: (i, 0))]
            + [pl.BlockSpec((1, SC_WINDOW), functools.partial(lambda k, i: (k, i), k)) for k in range(n_k)],
            out_specs=[],
            core_axis_name=("core", "subcore"),
            dimension_semantics=(pltpu.PARALLEL,),
        )(x_hbm, *([i_hbm] * n_k))

    return scatter(rows, idx, after)


def _sc_gather_rows(table, idx):
    m = idx.shape[1]
    w = table.shape[1]
    assert m % (SC_WINDOW * SC_CORES * SC_SUBCORES) == 0

    @functools.partial(pl.kernel, out_type=jax.ShapeDtypeStruct((m, w), table.dtype), mesh=_sc_mesh(),
                       scratch_types=[], name="moe_gather_sc")
    def gather(x_hbm, i_hbm, o_hbm):
        def body(i_vmem, o_vmem):
            pltpu.sync_copy(x_hbm.at[i_vmem.at[0]], o_vmem)

        pltpu.emit_pipeline(
            body,
            grid=(m // SC_WINDOW,),
            in_specs=[pl.BlockSpec((1, SC_WINDOW), lambda i: (0, i))],
            out_specs=[pl.BlockSpec((SC_WINDOW, w), lambda i: (i, 0))],
            core_axis_name=("core", "subcore"),
            dimension_semantics=(pltpu.PARALLEL,),
        )(i_hbm, o_hbm)

    return gather(table, idx)


def _sc_pack_rows_bf16(w):
    r, c = w.shape
    assert r % SC_PACK_ROWS == 0 and c % SC_PACK_COLS == 0 and (r // SC_PACK_ROWS) % (SC_CORES * SC_SUBCORES) == 0
    lanes = SC_LANES

    @functools.partial(pl.kernel, out_type=jax.ShapeDtypeStruct((r // 2, c), I32), mesh=_sc_mesh(),
                       scratch_types=[], compiler_params=pltpu.CompilerParams(needs_layout_passes=False),
                       name="expert_weight_pack_sc")
    def pack(x_hbm, o_hbm):
        def body(x_vmem, o_vmem):
            @pl.loop(0, SC_PACK_ROWS // 2)
            def _(i):
                for j in range(0, SC_PACK_COLS, lanes):
                    pair = plsc.pack(x_vmem[2 * i, pl.ds(j, lanes)], x_vmem[2 * i + 1, pl.ds(j, lanes)],
                                     format=plsc.PackFormat.INTERLEAVED)
                    o_vmem[i, pl.ds(j, lanes)] = plsc.bitcast(pair, I32)

        pltpu.emit_pipeline(
            body,
            grid=(r // SC_PACK_ROWS, c // SC_PACK_COLS),
            in_specs=[pl.BlockSpec((SC_PACK_ROWS, SC_PACK_COLS), lambda i, j: (i, j))],
            out_specs=[pl.BlockSpec((SC_PACK_ROWS // 2, SC_PACK_COLS), lambda i, j: (i, j))],
            core_axis_name=("core", "subcore"),
            dimension_semantics=(pltpu.PARALLEL, pltpu.PARALLEL),
        )(x_hbm, o_hbm)

    return pack(w)


def _moe_kernel(be_ref, nb_ref, slot_ref, nxt_ref, nv_ref, xs_ref, wgu_hbm, bgu_ref, wd_hbm, bd_ref, ys_ref,
                wgu_s, wd_s, sems, *, d_ff):
    i = pl.program_id(0)
    live = i < nb_ref[0]
    new_expert = (i == 0) | (be_ref[i] != be_ref[jnp.maximum(i - 1, 0)])
    slot = slot_ref[i]

    def weight_copies(e, s):
        return (pltpu.make_async_copy(wgu_hbm.at[e], wgu_s.at[s], sems.at[0, s]),
                pltpu.make_async_copy(wd_hbm.at[e], wd_s.at[s], sems.at[1, s]))

    @pl.when(live & new_expert)
    def _():
        e, nx = be_ref[i], nxt_ref[i]

        @pl.when(i == 0)
        def _():
            for cp in weight_copies(e, slot):
                cp.start()

        for cp in weight_copies(e, slot):
            cp.wait()

        @pl.when(nx >= 0)
        def _():
            for cp in weight_copies(nx, 1 - slot):
                cp.start()

    def sub_block(s):
        rows = pl.ds(s * MOE_SUB, MOE_SUB)
        packed = jnp.concatenate([xs_ref[p, rows, :] for p in range(SC_SPLIT)], axis=1)
        x_lo, x_hi = _unpack_bf16_pair(packed)
        dh = x_lo.shape[1]
        w_top = pltpu.bitcast(wgu_s[slot, pl.ds(0, dh // 2), :], BF16)
        w_bot = pltpu.bitcast(wgu_s[slot, pl.ds(dh // 2, dh // 2), :], BF16)
        gu = jnp.dot(x_lo.astype(BF16), w_top, preferred_element_type=F32)
        gu += jnp.dot(x_hi.astype(BF16), w_bot, preferred_element_type=F32)
        gu += bgu_ref[...]
        g = jnp.minimum(gu[:, :d_ff], SWIGLU_LIMIT)
        u = jnp.clip(gu[:, d_ff:], -SWIGLU_LIMIT, SWIGLU_LIMIT)
        act = (u + 1.0) * (g * _sigmoid(SWIGLU_ALPHA * g))
        y = jnp.dot(act.astype(BF16), pltpu.bitcast(wd_s[slot], BF16), preferred_element_type=F32) + bd_ref[...]
        dm2 = y.shape[1] // 2
        _store_planes(ys_ref, _pack_bf16_pair(y[:, :dm2], y[:, dm2:]), rows)

    n_sub = xs_ref.shape[1] // MOE_SUB
    n_valid = nv_ref[i]

    @pl.when(live & (n_valid == n_sub * MOE_SUB))
    def _():
        for s in range(n_sub):
            sub_block(s)

    @pl.when(live & (n_valid < n_sub * MOE_SUB))
    def _():
        for s in range(n_sub):
            pl.when(s * MOE_SUB < n_valid)(functools.partial(sub_block, s))


def _moe_call(xs, block_expert, n_used, seg_slot, next_expert, block_valid, wgu, bgu, wd, bd):
    _, n_rows, wp = xs.shape
    n_exp, dm_half, d_ff2 = wgu.shape
    dm = 2 * dm_half
    d_ff = d_ff2 // 2
    bm = MOE_BLOCK
    n_blocks = n_rows // bm

    def row_map(i, be, nb, slot, nxt, nv):
        return (0, jnp.minimum(i, nb[0] - 1), 0)

    def b_map(i, be, nb, slot, nxt, nv):
        return (be[jnp.minimum(i, nb[0] - 1)], 0, 0)

    grid_spec = pltpu.PrefetchScalarGridSpec(
        num_scalar_prefetch=5,
        grid=(n_blocks,),
        in_specs=[
            pl.BlockSpec((SC_SPLIT, bm, wp), row_map),
            pl.BlockSpec(memory_space=pl.ANY),
            pl.BlockSpec((None, 1, d_ff2), b_map),
            pl.BlockSpec(memory_space=pl.ANY),
            pl.BlockSpec((None, 1, dm), b_map),
        ],
        out_specs=pl.BlockSpec((SC_SPLIT, bm, wp), row_map),
        scratch_shapes=[
            pltpu.VMEM((2,) + wgu.shape[1:], I32), pltpu.VMEM((2,) + wd.shape[1:], I32),
            pltpu.SemaphoreType.DMA((2, 2)),
        ],
    )
    return pl.pallas_call(
        functools.partial(_moe_kernel, d_ff=d_ff),
        grid_spec=grid_spec,
        out_shape=jax.ShapeDtypeStruct(xs.shape, I32),
        compiler_params=pltpu.CompilerParams(dimension_semantics=("arbitrary",),
                                             vmem_limit_bytes=VMEM_LIMIT_BYTES),
        name="moe_experts",
    )(block_expert, n_used, seg_slot, next_expert, block_valid, xs, wgu, bgu, wd, bd)


def _combine_kernel(x1_ref, gt_ref, mod_ref, gf_ref, yg_ref, *out_refs):
    o_ref = out_refs[-1]
    gt = gt_ref[...]
    ts = yg_ref.shape[2]
    dh = yg_ref.shape[3] * SC_SPLIT
    acc_lo = jnp.zeros((ts, dh), F32)
    acc_hi = jnp.zeros((ts, dh), F32)
    for k in range(TOP_K):
        lo, hi = _unpack_bf16_pair(_load_planes(yg_ref.at[k]))
        gk = gt[:, k:k + 1]
        acc_lo += gk * lo
        acc_hi += gk * hi
    y = jnp.concatenate([acc_lo, acc_hi], axis=1)
    gate2 = mod_ref[...][5:6]
    x2 = x1_ref[...] + gate2 * y
    o_ref[...] = _rms(x2) * gf_ref[...]


def _combine_call(x1, gt, mod, gf, yg, out_prev, b0):
    bsz, seq, dm = x1.shape
    ts = min(COMBINE_TILE, seq)
    assert seq % ts == 0
    nj = seq // ts
    in_specs = [
        pl.BlockSpec((None, ts, dm), lambda b, j: (b, j, 0)),
        pl.BlockSpec((None, ts, LANES), lambda b, j: (b, j, 0)),
        pl.BlockSpec((None, N_MOD, dm), lambda b, j: (b + b0, 0, 0)),
        pl.BlockSpec((1, dm), lambda b, j: (0, 0)),
        pl.BlockSpec((TOP_K, SC_SPLIT, ts, yg.shape[3]), lambda b, j: (0, 0, b * nj + j, 0)),
    ]
    args = [x1, gt, mod, gf, yg]
    aliases = {}
    if out_prev is not None:
        in_specs.append(pl.BlockSpec(memory_space=pl.ANY))
        args.append(out_prev)
        aliases = {len(args) - 1: 0}
    return pl.pallas_call(
        _combine_kernel,
        grid=(bsz, nj),
        in_specs=in_specs,
        out_specs=pl.BlockSpec((None, ts, dm), lambda b, j: (b + b0, j, 0)),
        out_shape=jax.ShapeDtypeStruct((mod.shape[0], seq, dm), F32),
        input_output_aliases=aliases,
        compiler_params=pltpu.CompilerParams(dimension_semantics=("arbitrary", "arbitrary"),
                                             vmem_limit_bytes=VMEM_LIMIT_BYTES),
        name="moe_combine",
    )(*args)


def _block_diag(w):
    n_h, d, _ = w.shape
    eye = jnp.eye(n_h, dtype=w.dtype)
    return (eye[:, None, :, None] * w[:, :, None, :]).reshape(n_h * d, n_h * d)


def _layer(x, mod, l, w_in, norm1_g, conv_w, conv_b, conv_ln_g, conv_ln_b, lru_conv_w, lru_conv_b,
           lru_w_a, lru_b_a, lru_w_x, lru_b_x, lru_lambda, mix_norm_g, w_out, norm2_g, w_router,
           b_router, w_gate_up, b_gate_up, w_down, b_down, out_gain):
    bsz, seq, dm = x.shape
    n_tok = bsz * seq
    n_exp = w_router.shape[-1]
    row = lambda a: a.reshape(1, -1)
    wr = w_router[l]
    wr_hi = wr.astype(BF16)
    wr_lo = (wr - wr_hi.astype(F32)).astype(BF16)
    params = dict(
        g1=row(norm1_g[l]), win=w_in[l].astype(BF16), conv_w=conv_w[l], conv_b=row(conv_b[l]),
        ln_g=row(conv_ln_g[l]), ln_b=row(conv_ln_b[l]), lru_conv_w=lru_conv_w[l], lru_conv_b=row(lru_conv_b[l]),
        wg=jnp.concatenate([_block_diag(lru_w_a[l]), _block_diag(lru_w_x[l])], axis=1).astype(BF16),
        bg=jnp.concatenate([lru_b_a[l].reshape(1, -1), lru_b_x[l].reshape(1, -1)], axis=1),
        lam=row(lru_lambda[l]), mng=row(mix_norm_g[l]), wout=w_out[l].astype(BF16), g2=row(norm2_g[l]),
        wrt=jnp.concatenate([wr_hi.T, wr_lo.T], axis=0), br=b_router[l].reshape(n_exp, 1),
    )
    _, d_in, d_ff2 = w_gate_up[l].shape
    _, d_ff, d_out = w_down[l].shape
    wgu_p = _sc_pack_rows_bf16(w_gate_up[l].reshape(n_exp * d_in, d_ff2)).reshape(n_exp, d_in // 2, d_ff2)
    wd_p = _sc_pack_rows_bf16(w_down[l].reshape(n_exp * d_ff, d_out)).reshape(n_exp, d_ff // 2, d_out)
    sizes = [s for s in CHUNK_BATCHES if s > 0] if sum(CHUNK_BATCHES) == bsz else [bsz]
    routed, b0, anchor = [], 0, mod
    for cb in sizes:
        r = _route_chunk(x, mod, params, b0, cb, anchor, n_exp)
        routed.append(r)
        anchor = r["dest_sub"]
        b0 += cb
    out = None
    sc_order = [wgu_p] + [wd_p] * (len(routed) - 1)
    for r, after in zip(routed, sc_order):
        out = _experts_chunk(r, mod, out, wgu_p, b_gate_up[l][:, None, :], wd_p, b_down[l][:, None, :], out_gain,
                             after)
    return out


def _route_chunk(x, mod, params, b0, cb, anchor, n_exp):
    seq, dm = x.shape[1:]
    n_tok = cb * seq
    x1, h2p, top_idx, rank, gt, cnt = _mix_call(x, mod, params, b0, cb, anchor)

    bm = MOE_BLOCK
    counts = cnt[:, 0].astype(I32)
    padded = (counts + bm - 1) // bm * bm
    e_ids = jnp.arange(n_exp, dtype=I32)
    pad_ends = jnp.sum(jnp.where(e_ids[None, :] <= e_ids[:, None], padded[None, :], 0), axis=1)
    pad_starts = pad_ends - padded
    n_rows = (n_tok * TOP_K // bm + n_exp) * bm
    n_blocks = n_rows // bm
    dest = rank + jnp.sum(jnp.where(top_idx[..., None] == e_ids, pad_starts, 0), axis=-1)
    block_start = jnp.arange(n_blocks, dtype=I32) * bm
    block_expert = jnp.minimum(
        jnp.sum((block_start[:, None] >= pad_ends[None, :]).astype(I32), axis=1), n_exp - 1)
    n_used = pad_ends[-1:] // bm
    present = counts > 0
    seg_ordinal = jnp.sum(jnp.where((e_ids[None, :] < e_ids[:, None]) & present[None, :], 1, 0), axis=1)
    following = jnp.min(jnp.where((e_ids[None, :] > e_ids[:, None]) & present[None, :], e_ids[None, :], n_exp),
                        axis=1)
    following = jnp.where(following == n_exp, -1, following)
    block_hot = block_expert[:, None] == e_ids[None, :]
    seg_slot = jnp.sum(jnp.where(block_hot, seg_ordinal[None, :] % 2, 0), axis=1)
    next_expert = jnp.sum(jnp.where(block_hot, following[None, :], 0), axis=1)
    block_valid = jnp.clip(
        jnp.sum(jnp.where(block_hot, (pad_starts + counts)[None, :], 0), axis=1) - block_start, 0, bm)

    wp = h2p.shape[2]
    plane = jnp.arange(SC_SPLIT, dtype=I32)[None, :, None] * n_rows
    dest_sub = dest[:, None, :] + plane
    return dict(b0=b0, n_tok=n_tok, n_rows=n_rows, x1=x1, h2p=h2p, gt=gt, dest_sub=dest_sub,
                tables=(block_expert, n_used, seg_slot, next_expert, block_valid))


def _experts_chunk(r, mod, out_prev, wgu, bgu, wd, bd, out_gain, dispatch_after):
    n_tok, n_rows, dest_sub = r["n_tok"], r["n_rows"], r["dest_sub"]
    wp = r["h2p"].shape[2]
    xs = _sc_scatter_rows(r["h2p"].reshape(SC_SPLIT * n_tok, wp), dest_sub.reshape(TOP_K, SC_SPLIT * n_tok),
                          SC_SPLIT * n_rows, dispatch_after).reshape(SC_SPLIT, n_rows, wp)
    ys = _moe_call(xs, *r["tables"], wgu, bgu, wd, bd)
    yg = _sc_gather_rows(ys.reshape(SC_SPLIT * n_rows, wp), dest_sub.reshape(1, TOP_K * SC_SPLIT * n_tok))
    return _combine_call(r["x1"], r["gt"], mod, out_gain, yg.reshape(TOP_K, SC_SPLIT, n_tok, wp), out_prev,
                         r["b0"])


def kernel(x, c, w_ada, b_ada, norm1_g, w_in, conv_w, conv_b, conv_ln_g, conv_ln_b, lru_conv_w, lru_conv_b,
           lru_w_a, lru_b_a, lru_w_x, lru_b_x, lru_lambda, mix_norm_g, w_out, norm2_g, w_router, b_router,
           w_gate_up, b_gate_up, w_down, b_down, final_norm_g):
    depth = w_ada.shape[0]
    assert depth == 1, "the final norm is fused into the (single) layer's combine kernel"
    bsz, seq, dm = x.shape
    mod = _ada_call(c, w_ada[0], b_ada[0]).reshape(bsz, N_MOD, dm)
    return _layer(x, mod, 0, w_in, norm1_g, conv_w, conv_b, conv_ln_g, conv_ln_b, lru_conv_w, lru_conv_b,
                  lru_w_a, lru_b_a, lru_w_x, lru_b_x, lru_lambda, mix_norm_g, w_out, norm2_g, w_router,
                  b_router, w_gate_up, b_gate_up, w_down, b_down, row_gain(final_norm_g))


def row_gain(g):
    return g.reshape(1, -1)
```

```python
import functools

import jax
import jax.numpy as jnp
from jax import lax
from jax.experimental import pallas as pl
from jax.experimental.pallas import tpu as pltpu
from jax.experimental.pallas import tpu_sc as plsc

F32 = jnp.float32
BF16 = jnp.bfloat16
I32 = jnp.int32

EPS = 1e-6
N_MOD = 6
LRU_C = 8.0
TOP_K = 4
SWIGLU_ALPHA = 1.702
SWIGLU_LIMIT = 7.0

LANES = 128
SUBLANES = 8
VMEM_LIMIT_BYTES = 56 * 1024 * 1024

SEQ_TILE = 512
TILES_PER_STEP = 2
COMBINE_TILE = 512
CONV_PAD = 32
CONV_ROWS = 128
MOE_BLOCK = 2048
MOE_SUB = 256
CHUNK_BATCHES = (6, 2)
HI_MASK = -65536

SC_CORES = 2
SC_SUBCORES = 16
SC_LANES = 16
SC_PACK_ROWS = 16
SC_PACK_COLS = 512
SC_WINDOW = 128
SC_SPLIT = 2


def _sigmoid(x):
    return 0.5 * jnp.tanh(0.5 * x) + 0.5


def _pack_bf16_pair(lo_f32, hi_f32):
    lo_bits = lax.bitcast_convert_type(lo_f32.astype(BF16).astype(F32), I32)
    hi_bits = lax.bitcast_convert_type(hi_f32.astype(BF16).astype(F32), I32)
    return lax.shift_right_logical(lo_bits, 16) | hi_bits


def _unpack_bf16_pair(p):
    lo = lax.bitcast_convert_type(lax.shift_left(p, 16), F32)
    hi = lax.bitcast_convert_type(p & HI_MASK, F32)
    return lo, hi


def _store_planes(ref, packed, rows=slice(None)):
    wp = packed.shape[1] // SC_SPLIT
    for s in range(SC_SPLIT):
        ref[s, rows, :] = packed[:, s * wp:(s + 1) * wp]


def _load_planes(ref):
    return jnp.concatenate([ref[s] for s in range(SC_SPLIT)], axis=1)


def _ada_kernel(c_ref, w_ref, b_ref, o_ref):
    c = c_ref[...]
    ca = c * _sigmoid(c)
    w = w_ref[...]
    c_hi = ca.astype(BF16)
    c_lo = (ca - c_hi.astype(F32)).astype(BF16)
    w_hi = w.astype(BF16)
    w_lo = (w - w_hi.astype(F32)).astype(BF16)
    acc = jnp.dot(c_hi, w_hi, preferred_element_type=F32)
    acc += jnp.dot(c_lo, w_hi, preferred_element_type=F32)
    acc += jnp.dot(c_hi, w_lo, preferred_element_type=F32)
    o_ref[...] = acc + b_ref[...]


def _ada_call(c, w_ada, b_ada):
    bsz, dm = c.shape
    n_out = w_ada.shape[1]
    tn = 1024
    return pl.pallas_call(
        _ada_kernel,
        grid=(n_out // tn,),
        in_specs=[
            pl.BlockSpec((bsz, dm), lambda n: (0, 0)),
            pl.BlockSpec((dm, tn), lambda n: (0, n)),
            pl.BlockSpec((1, tn), lambda n: (0, n)),
        ],
        out_specs=pl.BlockSpec((bsz, tn), lambda n: (0, n)),
        out_shape=jax.ShapeDtypeStruct((bsz, n_out), F32),
        compiler_params=pltpu.CompilerParams(dimension_semantics=("arbitrary",)),
        name="ada_mod",
    )(c, w_ada, b_ada.reshape(1, n_out))


def _rms(x, eps=EPS):
    return x * lax.rsqrt(jnp.mean(x * x, axis=-1, keepdims=True) + eps)


def _gelu_tanh(x):
    return 0.5 * x * (1.0 + jnp.tanh(0.7978845608028654 * (x + 0.044715 * (x * x * x))))


def _slab_store(buf, row0, val):
    for s in range(buf.shape[0]):
        buf[s, pl.ds(row0, val.shape[0]), :] = val[:, s * LANES:(s + 1) * LANES]


def _slab_keep_tail(buf, keep, ts):
    for s in range(buf.shape[0]):
        buf[s, pl.ds(0, keep), :] = buf[s, pl.ds(ts, keep), :]


def _causal_tap_sum(buf, w_ref, bias_row, first, n_taps, ts):
    w = w_ref[...]
    cols = []
    for s in range(buf.shape[0]):
        lanes = slice(s * LANES, (s + 1) * LANES)
        chunks = []
        for c in range(0, ts, CONV_ROWS):
            acc = jnp.broadcast_to(bias_row[:, lanes], (CONV_ROWS, LANES))
            for k in range(n_taps):
                acc = acc + w[k:k + 1, lanes] * buf[s, pl.ds(c + first + k, CONV_ROWS), :]
            chunks.append(acc)
        cols.append(jnp.concatenate(chunks, axis=0))
    return jnp.concatenate(cols, axis=1)


def _linear_recurrence(a, b, a_s, b_s, hcar):
    ts = a.shape[0]
    blk = ts // SUBLANES
    pitch = blk + 1
    cols = []
    for j in range(a_s.shape[0]):
        lanes = slice(j * LANES, (j + 1) * LANES)
        for k in range(SUBLANES):
            a_s[j, pl.ds(k * pitch, blk), :] = a[k * blk:(k + 1) * blk, lanes]
            b_s[j, pl.ds(k * pitch, blk), :] = b[k * blk:(k + 1) * blk, lanes]
        h = jnp.zeros((SUBLANES, LANES), F32)
        p = jnp.ones((SUBLANES, LANES), F32)
        for i in range(blk):
            row_i = pl.ds(i, SUBLANES, stride=pitch)
            a_i = a_s[j, row_i, :]
            h = a_i * h + b_s[j, row_i, :]
            p = a_i * p
            b_s[j, row_i, :] = h
            a_s[j, row_i, :] = p
        state = hcar[:, lanes]
        entering = []
        for k in range(SUBLANES):
            entering.append(state)
            state = p[k:k + 1, :] * state + h[k:k + 1, :]
        hcar[:, lanes] = state
        h_in = jnp.concatenate(entering, axis=0)
        for i in range(blk):
            row_i = pl.ds(i, SUBLANES, stride=pitch)
            b_s[j, row_i, :] = b_s[j, row_i, :] + a_s[j, row_i, :] * h_in
        cols.append(jnp.concatenate([b_s[j, pl.ds(k * pitch, blk), :] for k in range(SUBLANES)], axis=0))
    return jnp.concatenate(cols, axis=1)


def _mix_kernel(x_ref, xn_ref, mod_ref, modn_ref, g1_ref, win_ref, cw_ref, cb_ref, lng_ref, lnb_ref,
                lcw_ref, lcb_ref, wg_ref, bg_ref, lam_ref, mng_ref, wout_ref, g2_ref,
                wrt_ref, br_ref, anchor_ref,
                x1_ref, h2p_ref, idx_ref, rank_ref, gt_ref, cnt_ref,
                u_a, u_b, vbuf, rbuf, a_s, b_s, hcar, cnt_s,
                *, tile, n_tiles, steps_per_seq, d_conv, d_lru, conv_w, lru_cw, n_exp):
    del anchor_ref
    s = pl.program_id(0)

    @pl.when(s % steps_per_seq == 0)
    def _():
        _slab_store(vbuf, 0, jnp.zeros((CONV_PAD, d_conv), F32))
        _slab_store(rbuf, 0, jnp.zeros((SUBLANES, d_lru), F32))
        hcar[...] = jnp.zeros_like(hcar)

    mod = mod_ref[...]
    modn = modn_ref[...]
    shift1, gate1, shift2 = mod[0:1], mod[2:3], mod[3:4]
    gain1 = g1_ref[...] * (1.0 + mod[1:2])
    gain2 = g2_ref[...] * (1.0 + mod[4:5])
    lam = lam_ref[...]
    softplus_neg_lam = jnp.maximum(-lam, 0.0) + jnp.log1p(jnp.exp(-jnp.abs(lam)))

    def project(src_ref, r0, gain, shift, u_ref):
        h = _rms(src_ref[pl.ds(r0, tile), :]) * gain + shift
        u_ref[...] = jnp.dot(h.astype(BF16), win_ref[...], preferred_element_type=F32)

    def conv_group(r0, u_ref):
        return _mix_conv_group(r0, tile, u_ref, cw_ref, cb_ref, lng_ref, lnb_ref, vbuf, d_conv=d_conv,
                               conv_w=conv_w)

    def finish(r0, u_ref, yc):
        _mix_finish(r0, tile, x_ref, u_ref, yc, gate1, shift2, gain2, softplus_neg_lam,
                    lcw_ref, lcb_ref, wg_ref, bg_ref, mng_ref, wout_ref,
                    wrt_ref, br_ref, x1_ref, h2p_ref, idx_ref, rank_ref, gt_ref, rbuf, a_s, b_s, hcar,
                    cnt_s, d_conv=d_conv, d_lru=d_lru, lru_cw=lru_cw, n_exp=n_exp)

    @pl.when(s == 0)
    def _():
        cnt_s[...] = jnp.zeros_like(cnt_s)
        project(x_ref, 0, gain1, shift1, u_a)

    bufs = (u_a, u_b)
    for q in range(n_tiles):
        cur, nxt = bufs[q % 2], bufs[(q + 1) % 2]
        yc = conv_group(q * tile, cur)
        if q + 1 < n_tiles:
            project(x_ref, (q + 1) * tile, gain1, shift1, nxt)
        else:
            project(xn_ref, 0, g1_ref[...] * (1.0 + modn[1:2]), modn[0:1], nxt)
        finish(q * tile, cur, yc)
    _slab_keep_tail(vbuf, CONV_PAD, n_tiles * tile)
    _slab_keep_tail(rbuf, SUBLANES, n_tiles * tile)
    cnt_ref[...] = cnt_s[:, :LANES]


def _mix_conv_group(r0, ts, u_ref, cw_ref, cb_ref, lng_ref, lnb_ref, vbuf, *, d_conv, conv_w):
    v = u_ref[:, :d_conv] * _sigmoid(u_ref[:, d_conv:2 * d_conv])
    _slab_store(vbuf, CONV_PAD + r0, v)
    acc = _causal_tap_sum(vbuf, cw_ref, cb_ref[...], r0 + CONV_PAD - (conv_w - 1), conv_w, ts)
    mu = jnp.mean(acc, axis=-1, keepdims=True)
    cen = acc - mu
    var = jnp.mean(cen * cen, axis=-1, keepdims=True)
    yc = cen * lax.rsqrt(var + EPS) * lng_ref[...] + lnb_ref[...]
    return yc * _sigmoid(yc)


def _mix_finish(r0, ts, x_ref, u_ref, yc, gate1, shift2, gain2, softplus_neg_lam,
                lcw_ref, lcb_ref, wg_ref, bg_ref, mng_ref, wout_ref,
                wrt_ref, br_ref, x1_ref, h2p_ref, idx_ref, rank_ref, gt_ref, rbuf, a_s, b_s, hcar,
                cnt_s, *, d_conv, d_lru, lru_cw, n_exp):
    rows = pl.ds(r0, ts)
    x = x_ref[rows, :]
    u = u_ref

    u_gate = u[:, 2 * d_conv:2 * d_conv + d_lru]
    _slab_store(rbuf, SUBLANES + r0, u[:, 2 * d_conv + d_lru:])
    xr = _causal_tap_sum(rbuf, lcw_ref, lcb_ref[...], r0 + SUBLANES - (lru_cw - 1), lru_cw, ts)
    gates = jnp.dot(xr.astype(BF16), wg_ref[...], preferred_element_type=F32) + bg_ref[...]
    r = _sigmoid(gates[:, :d_lru])
    i_g = _sigmoid(gates[:, d_lru:])
    log_a = (-LRU_C) * r * softplus_neg_lam
    a = jnp.exp(log_a)
    inp = jnp.sqrt(1.0 - jnp.exp(2.0 * log_a)) * (i_g * xr)

    yl = _gelu_tanh(u_gate) * _linear_recurrence(a, inp, a_s, b_s, hcar)

    mng = mng_ref[...]
    yc_n = _rms(yc) * mng[:, :d_conv]
    yl_n = _rms(yl) * mng[:, d_conv:]
    mixo = jnp.dot(yc_n.astype(BF16), wout_ref[pl.ds(0, d_conv), :], preferred_element_type=F32)
    mixo += jnp.dot(yl_n.astype(BF16), wout_ref[pl.ds(d_conv, d_lru), :], preferred_element_type=F32)
    x1 = x + gate1 * mixo
    x1_ref[rows, :] = x1

    h2 = _rms(x1) * gain2 + shift2
    dh = h2.shape[1] // 2
    h2_hi = h2.astype(BF16)
    _store_planes(h2p_ref, _pack_bf16_pair(h2[:, :dh], h2[:, dh:]), rows)
    h2_lo = (h2 - h2_hi.astype(F32)).astype(BF16)
    nt_dims = (((1,), (1,)), ((), ()))
    wrt = wrt_ref[...]
    lg = lax.dot_general(wrt, h2_hi, nt_dims, preferred_element_type=F32)
    lg2 = lax.dot_general(wrt[:n_exp], h2_lo, nt_dims, preferred_element_type=F32)
    logits = lg[:n_exp] + lg[n_exp:] + lg2 + br_ref[...]

    eidx = lax.broadcasted_iota(I32, (n_exp, ts), 0)
    neg_inf = jnp.float32(-jnp.inf)
    work = logits
    vals, idxs, hots = [], [], []
    for _ in range(TOP_K):
        mval = jnp.max(work, axis=0, keepdims=True)
        midx = jnp.min(jnp.where(work == mval, eidx, n_exp), axis=0, keepdims=True)
        hot = eidx == midx
        vals.append(mval)
        idxs.append(midx)
        hots.append(hot)
        work = jnp.where(hot, neg_inf, work)
    exps = [jnp.exp(vk - vals[0]) for vk in vals]
    denom = exps[0] + exps[1] + exps[2] + exps[3]
    gate_rows = [ek / denom for ek in exps]

    sel = jnp.zeros((n_exp, ts), F32)
    for hot in hots:
        sel = sel + hot.astype(F32)
    tri = (lax.broadcasted_iota(I32, (ts, ts), 0) < lax.broadcasted_iota(I32, (ts, ts), 1)).astype(BF16)
    before = jnp.dot(sel.astype(BF16), tri, preferred_element_type=F32) + cnt_s[...]
    rank_rows = [jnp.sum(jnp.where(hot, before, 0.0), axis=0, keepdims=True) for hot in hots]
    cnt_s[...] = cnt_s[...] + jnp.sum(sel, axis=1, keepdims=True)

    idx_ref[:, rows] = jnp.concatenate(idxs, axis=0)
    rank_ref[:, rows] = jnp.concatenate(rank_rows, axis=0).astype(I32)
    g4 = jnp.concatenate(gate_rows, axis=0)
    g_pad = jnp.concatenate([g4, jnp.zeros((LANES - TOP_K, ts), F32)], axis=0)
    gt_ref[rows, :] = g_pad.T


def _mix_call(x, mod, p, b0, bsz, anchor):
    _, seq, dm = x.shape
    n_tiles = TILES_PER_STEP
    tile = min(SEQ_TILE, seq // n_tiles)
    ts = n_tiles * tile
    nj = seq // ts
    n_steps = bsz * nj
    tiles_per_seq = seq // tile
    d_conv = p["conv_w"].shape[1]
    d_lru = p["lru_conv_w"].shape[1]
    conv_w = p["conv_w"].shape[0]
    lru_cw = p["lru_conv_w"].shape[0]
    n_exp = p["wrt"].shape[0] // 2
    assert seq % ts == 0 and tile % LANES == 0 and tile % CONV_ROWS == 0 and n_tiles % 2 == 0
    assert conv_w - 1 <= CONV_PAD and lru_cw - 1 <= SUBLANES and d_conv % LANES == 0 and d_lru % LANES == 0

    def full(a):
        return pl.BlockSpec(a.shape, lambda s: (0,) * a.ndim)

    def next_tile(s):
        return jnp.minimum(n_tiles * (s + 1), bsz * tiles_per_seq - 1)

    weights = [p["g1"], p["win"], p["conv_w"], p["conv_b"], p["ln_g"], p["ln_b"], p["lru_conv_w"],
               p["lru_conv_b"], p["wg"], p["bg"], p["lam"], p["mng"], p["wout"], p["g2"], p["wrt"], p["br"]]
    kern = functools.partial(_mix_kernel, tile=tile, n_tiles=n_tiles, steps_per_seq=nj, d_conv=d_conv, d_lru=d_lru,
                             conv_w=conv_w, lru_cw=lru_cw, n_exp=n_exp)
    out_shape = (
        jax.ShapeDtypeStruct((bsz, seq, dm), F32),
        jax.ShapeDtypeStruct((SC_SPLIT, bsz * seq, dm // 2 // SC_SPLIT), I32),
        jax.ShapeDtypeStruct((TOP_K, bsz * seq), I32),
        jax.ShapeDtypeStruct((TOP_K, bsz * seq), I32),
        jax.ShapeDtypeStruct((bsz, seq, LANES), F32),
        jax.ShapeDtypeStruct((n_exp, LANES), F32),
    )
    return pl.pallas_call(
        kern,
        grid=(n_steps,),
        in_specs=[pl.BlockSpec((None, ts, dm), lambda s: (s // nj + b0, s % nj, 0)),
                  pl.BlockSpec((None, tile, dm),
                               lambda s: (next_tile(s) // tiles_per_seq + b0, next_tile(s) % tiles_per_seq, 0)),
                  pl.BlockSpec((None, N_MOD, dm), lambda s: (s // nj + b0, 0, 0)),
                  pl.BlockSpec((None, N_MOD, dm), lambda s: (next_tile(s) // tiles_per_seq + b0, 0, 0))]
        + [full(w) for w in weights] + [pl.BlockSpec(memory_space=pl.ANY)],
        out_specs=(
            pl.BlockSpec((None, ts, dm), lambda s: (s // nj, s % nj, 0)),
            pl.BlockSpec((SC_SPLIT, ts, dm // 2 // SC_SPLIT), lambda s: (0, s, 0)),
            pl.BlockSpec((TOP_K, ts), lambda s: (0, s)),
            pl.BlockSpec((TOP_K, ts), lambda s: (0, s)),
            pl.BlockSpec((None, ts, LANES), lambda s: (s // nj, s % nj, 0)),
            pl.BlockSpec((n_exp, LANES), lambda s: (0, 0)),
        ),
        out_shape=out_shape,
        scratch_shapes=[
            pltpu.VMEM((tile, 2 * d_conv + 2 * d_lru), F32),
            pltpu.VMEM((tile, 2 * d_conv + 2 * d_lru), F32),
            pltpu.VMEM((d_conv // LANES, CONV_PAD + ts, LANES), F32),
            pltpu.VMEM((d_lru // LANES, SUBLANES + ts, LANES), F32),
            pltpu.VMEM((d_lru // LANES, tile + SUBLANES, LANES), F32),
            pltpu.VMEM((d_lru // LANES, tile + SUBLANES, LANES), F32),
            pltpu.VMEM((1, d_lru), F32),
            pltpu.VMEM((n_exp, tile), F32),
        ],
        compiler_params=pltpu.CompilerParams(dimension_semantics=("arbitrary",),
                                             vmem_limit_bytes=VMEM_LIMIT_BYTES),
        name="token_mix_route",
    )(x, x, mod, mod, *weights, anchor)


def _sc_mesh():
    return plsc.VectorSubcoreMesh(core_axis_name="core", subcore_axis_name="subcore",
                                  num_cores=SC_CORES, num_subcores=SC_SUBCORES)


def _sc_scatter_rows(rows, idx, n_out, after):
    n, w = rows.shape
    n_k = idx.shape[0]
    assert n % (SC_WINDOW * SC_CORES * SC_SUBCORES) == 0

    @functools.partial(pl.kernel, out_type=jax.ShapeDtypeStruct((n_out, w), rows.dtype), mesh=_sc_mesh(),
                       scratch_types=[], name="moe_dispatch_sc")
    def scatter(x_hbm, i_hbm, after_hbm, o_hbm):
        del after_hbm

        def body(x_vmem, *i_vmems):
            for i_vmem in i_vmems:
                pltpu.sync_copy(x_vmem, o_hbm.at[i_vmem.at[0]])

        pltpu.emit_pipeline(
            body,
            grid=(n // SC_WINDOW,),
            in_specs=[pl.BlockSpec((SC_WINDOW, w), lambda i: (i, 0))]
            + [pl.BlockSpec((1, SC_WINDOW), functools.partial(lambda k, i: (k, i), k)) for k in range(n_k)],
            out_specs=[],
            core_axis_name=("core", "subcore"),
            dimension_semantics=(pltpu.PARALLEL,),
        )(x_hbm, *([i_hbm] * n_k))

    return scatter(rows, idx, after)


def _sc_gather_rows(table, idx):
    m = idx.shape[1]
    w = table.shape[1]
    assert m % (SC_WINDOW * SC_CORES * SC_SUBCORES) == 0

    @functools.partial(pl.kernel, out_type=jax.ShapeDtypeStruct((m, w), table.dtype), mesh=_sc_mesh(),
                       scratch_types=[], name="moe_gather_sc")
    def gather(x_hbm, i_hbm, o_hbm):
        def body(i_vmem, o_vmem):
            pltpu.sync_copy(x_hbm.at[i_vmem.at[0]], o_vmem)

        pltpu.emit_pipeline(
            body,
            grid=(m // SC_WINDOW,),
            in_specs=[pl.BlockSpec((1, SC_WINDOW), lambda i: (0, i))],
            out_specs=[pl.BlockSpec((SC_WINDOW, w), lambda i: (i, 0))],
            core_axis_name=("core", "subcore"),
            dimension_semantics=(pltpu.PARALLEL,),
        )(i_hbm, o_hbm)

    return gather(table, idx)


def _sc_pack_rows_bf16(w):
    r, c = w.shape
    assert r % SC_PACK_ROWS == 0 and c % SC_PACK_COLS == 0 and (r // SC_PACK_ROWS) % (SC_CORES * SC_SUBCORES) == 0
    lanes = SC_LANES

    @functools.partial(pl.kernel, out_type=jax.ShapeDtypeStruct((r // 2, c), I32), mesh=_sc_mesh(),
                       scratch_types=[], compiler_params=pltpu.CompilerParams(needs_layout_passes=False),
                       name="expert_weight_pack_sc")
    def pack(x_hbm, o_hbm):
        def body(x_vmem, o_vmem):
            @pl.loop(0, SC_PACK_ROWS // 2)
            def _(i):
                for j in range(0, SC_PACK_COLS, lanes):
                    pair = plsc.pack(x_vmem[2 * i, pl.ds(j, lanes)], x_vmem[2 * i + 1, pl.ds(j, lanes)],
                                     format=plsc.PackFormat.INTERLEAVED)
                    o_vmem[i, pl.ds(j, lanes)] = plsc.bitcast(pair, I32)

        pltpu.emit_pipeline(
            body,
            grid=(r // SC_PACK_ROWS, c // SC_PACK_COLS),
            in_specs=[pl.BlockSpec((SC_PACK_ROWS, SC_PACK_COLS), lambda i, j: (i, j))],
            out_specs=[pl.BlockSpec((SC_PACK_ROWS // 2, SC_PACK_COLS), lambda i, j: (i, j))],
            core_axis_name=("core", "subcore"),
            dimension_semantics=(pltpu.PARALLEL, pltpu.PARALLEL),
        )(x_hbm, o_hbm)

    return pack(w)


def _moe_kernel(be_ref, nb_ref, slot_ref, nxt_ref, nv_ref, xs_ref, wgu_hbm, bgu_ref, wd_hbm, bd_ref, ys_ref,
                wgu_s, wd_s, sems, *, d_ff):
    i = pl.program_id(0)
    live = i < nb_ref[0]
    new_expert = (i == 0) | (be_ref[i] != be_ref[jnp.maximum(i - 1, 0)])
    slot = slot_ref[i]

    def weight_copies(e, s):
        return (pltpu.make_async_copy(wgu_hbm.at[e], wgu_s.at[s], sems.at[0, s]),
                pltpu.make_async_copy(wd_hbm.at[e], wd_s.at[s], sems.at[1, s]))

    @pl.when(live & new_expert)
    def _():
        e, nx = be_ref[i], nxt_ref[i]

        @pl.when(i == 0)
        def _():
            for cp in weight_copies(e, slot):
                cp.start()

        for cp in weight_copies(e, slot):
            cp.wait()

        @pl.when(nx >= 0)
        def _():
            for cp in weight_copies(nx, 1 - slot):
                cp.start()

    def sub_block(s):
        rows = pl.ds(s * MOE_SUB, MOE_SUB)
        packed = jnp.concatenate([xs_ref[p, rows, :] for p in range(SC_SPLIT)], axis=1)
        x_lo, x_hi = _unpack_bf16_pair(packed)
        dh = x_lo.shape[1]
        w_top = pltpu.bitcast(wgu_s[slot, pl.ds(0, dh // 2), :], BF16)
        w_bot = pltpu.bitcast(wgu_s[slot, pl.ds(dh // 2, dh // 2), :], BF16)
        gu = jnp.dot(x_lo.astype(BF16), w_top, preferred_element_type=F32)
        gu += jnp.dot(x_hi.astype(BF16), w_bot, preferred_element_type=F32)
        gu += bgu_ref[...]
        g = jnp.minimum(gu[:, :d_ff], SWIGLU_LIMIT)
        u = jnp.clip(gu[:, d_ff:], -SWIGLU_LIMIT, SWIGLU_LIMIT)
        act = (u + 1.0) * (g * _sigmoid(SWIGLU_ALPHA * g))
        y = jnp.dot(act.astype(BF16), pltpu.bitcast(wd_s[slot], BF16), preferred_element_type=F32) + bd_ref[...]
        dm2 = y.shape[1] // 2
        _store_planes(ys_ref, _pack_bf16_pair(y[:, :dm2], y[:, dm2:]), rows)

    n_sub = xs_ref.shape[1] // MOE_SUB
    n_valid = nv_ref[i]

    @pl.when(live & (n_valid == n_sub * MOE_SUB))
    def _():
        for s in range(n_sub):
            sub_block(s)

    @pl.when(live & (n_valid < n_sub * MOE_SUB))
    def _():
        for s in range(n_sub):
            pl.when(s * MOE_SUB < n_valid)(functools.partial(sub_block, s))


def _moe_call(xs, block_expert, n_used, seg_slot, next_expert, block_valid, wgu, bgu, wd, bd):
    _, n_rows, wp = xs.shape
    n_exp, dm_half, d_ff2 = wgu.shape
    dm = 2 * dm_half
    d_ff = d_ff2 // 2
    bm = MOE_BLOCK
    n_blocks = n_rows // bm

    def row_map(i, be, nb, slot, nxt, nv):
        return (0, jnp.minimum(i, nb[0] - 1), 0)

    def b_map(i, be, nb, slot, nxt, nv):
        return (be[jnp.minimum(i, nb[0] - 1)], 0, 0)

    grid_spec = pltpu.PrefetchScalarGridSpec(
        num_scalar_prefetch=5,
        grid=(n_blocks,),
        in_specs=[
            pl.BlockSpec((SC_SPLIT, bm, wp), row_map),
            pl.BlockSpec(memory_space=pl.ANY),
            pl.BlockSpec((None, 1, d_ff2), b_map),
            pl.BlockSpec(memory_space=pl.ANY),
            pl.BlockSpec((None, 1, dm), b_map),
        ],
        out_specs=pl.BlockSpec((SC_SPLIT, bm, wp), row_map),
        scratch_shapes=[
            pltpu.VMEM((2,) + wgu.shape[1:], I32), pltpu.VMEM((2,) + wd.shape[1:], I32),
            pltpu.SemaphoreType.DMA((2, 2)),
        ],
    )
    return pl.pallas_call(
        functools.partial(_moe_kernel, d_ff=d_ff),
        grid_spec=grid_spec,
        out_shape=jax.ShapeDtypeStruct(xs.shape, I32),
        compiler_params=pltpu.CompilerParams(dimension_semantics=("arbitrary",),
                                             vmem_limit_bytes=VMEM_LIMIT_BYTES),
        name="moe_experts",
    )(block_expert, n_used, seg_slot, next_expert, block_valid, xs, wgu, bgu, wd, bd)


def _combine_kernel(x1_ref, gt_ref, mod_ref, gf_ref, yg_ref, *out_refs):
    o_ref = out_refs[-1]
    gt = gt_ref[...]
    ts = yg_ref.shape[2]
    dh = yg_ref.shape[3] * SC_SPLIT
    acc_lo = jnp.zeros((ts, dh), F32)
    acc_hi = jnp.zeros((ts, dh), F32)
    for k in range(TOP_K):
        lo, hi = _unpack_bf16_pair(_load_planes(yg_ref.at[k]))
        gk = gt[:, k:k + 1]
        acc_lo += gk * lo
        acc_hi += gk * hi
    y = jnp.concatenate([acc_lo, acc_hi], axis=1)
    gate2 = mod_ref[...][5:6]
    x2 = x1_ref[...] + gate2 * y
    o_ref[...] = _rms(x2) * gf_ref[...]


def _combine_call(x1, gt, mod, gf, yg, out_prev, b0):
    bsz, seq, dm = x1.shape
    ts = min(COMBINE_TILE, seq)
    assert seq % ts == 0
    nj = seq // ts
    in_specs = [
        pl.BlockSpec((None, ts, dm), lambda b, j: (b, j, 0)),
        pl.BlockSpec((None, ts, LANES), lambda b, j: (b, j, 0)),
        pl.BlockSpec((None, N_MOD, dm), lambda b, j: (b + b0, 0, 0)),
        pl.BlockSpec((1, dm), lambda b, j: (0, 0)),
        pl.BlockSpec((TOP_K, SC_SPLIT, ts, yg.shape[3]), lambda b, j: (0, 0, b * nj + j, 0)),
    ]
    args = [x1, gt, mod, gf, yg]
    aliases = {}
    if out_prev is not None:
        in_specs.append(pl.BlockSpec(memory_space=pl.ANY))
        args.append(out_prev)
        aliases = {len(args) - 1: 0}
    return pl.pallas_call(
        _combine_kernel,
        grid=(bsz, nj),
        in_specs=in_specs,
        out_specs=pl.BlockSpec((None, ts, dm), lambda b, j: (b + b0, j, 0)),
        out_shape=jax.ShapeDtypeStruct((mod.shape[0], seq, dm), F32),
        input_output_aliases=aliases,
        compiler_params=pltpu.CompilerParams(dimension_semantics=("arbitrary", "arbitrary"),
                                             vmem_limit_bytes=VMEM_LIMIT_BYTES),
        name="moe_combine",
    )(*args)


def _block_diag(w):
    n_h, d, _ = w.shape
    eye = jnp.eye(n_h, dtype=w.dtype)
    return (eye[:, None, :, None] * w[:, :, None, :]).reshape(n_h * d, n_h * d)


def _layer(x, mod, l, w_in, norm1_g, conv_w, conv_b, conv_ln_g, conv_ln_b, lru_conv_w, lru_conv_b,
           lru_w_a, lru_b_a, lru_w_x, lru_b_x, lru_lambda, mix_norm_g, w_out, norm2_g, w_router,
           b_router, w_gate_up, b_gate_up, w_down, b_down, out_gain):
    bsz, seq, dm = x.shape
    n_tok = bsz * seq
    n_exp = w_router.shape[-1]
    row = lambda a: a.reshape(1, -1)
    wr = w_router[l]
    wr_hi = wr.astype(BF16)
    wr_lo = (wr - wr_hi.astype(F32)).astype(BF16)
    params = dict(
        g1=row(norm1_g[l]), win=w_in[l].astype(BF16), conv_w=conv_w[l], conv_b=row(conv_b[l]),
        ln_g=row(conv_ln_g[l]), ln_b=row(conv_ln_b[l]), lru_conv_w=lru_conv_w[l], lru_conv_b=row(lru_conv_b[l]),
        wg=jnp.concatenate([_block_diag(lru_w_a[l]), _block_diag(lru_w_x[l])], axis=1).astype(BF16),
        bg=jnp.concatenate([lru_b_a[l].reshape(1, -1), lru_b_x[l].reshape(1, -1)], axis=1),
        lam=row(lru_lambda[l]), mng=row(mix_norm_g[l]), wout=w_out[l].astype(BF16), g2=row(norm2_g[l]),
        wrt=jnp.concatenate([wr_hi.T, wr_lo.T], axis=0), br=b_router[l].reshape(n_exp, 1),
    )
    _, d_in, d_ff2 = w_gate_up[l].shape
    _, d_ff, d_out = w_down[l].shape
    wgu_p = _sc_pack_rows_bf16(w_gate_up[l].reshape(n_exp * d_in, d_ff2)).reshape(n_exp, d_in // 2, d_ff2)
    wd_p = _sc_pack_rows_bf16(w_down[l].reshape(n_exp * d_ff, d_out)).reshape(n_exp, d_ff // 2, d_out)
    sizes = [s for s in CHUNK_BATCHES if s > 0] if sum(CHUNK_BATCHES) == bsz else [bsz]
    routed, b0, anchor = [], 0, mod
    for cb in sizes:
        r = _route_chunk(x, mod, params, b0, cb, anchor, n_exp)
        routed.append(r)
        anchor = r["dest_sub"]
        b0 += cb
    out = None
    sc_order = [wgu_p] + [wd_p] * (len(routed) - 1)
    for r, after in zip(routed, sc_order):
        out = _experts_chunk(r, mod, out, wgu_p, b_gate_up[l][:, None, :], wd_p, b_down[l][:, None, :], out_gain,
                             after)
    return out


def _route_chunk(x, mod, params, b0, cb, anchor, n_exp):
    seq, dm = x.shape[1:]
    n_tok = cb * seq
    x1, h2p, top_idx, rank, gt, cnt = _mix_call(x, mod, params, b0, cb, anchor)

    bm = MOE_BLOCK
    counts = cnt[:, 0].astype(I32)
    padded = (counts + bm - 1) // bm * bm
    e_ids = jnp.arange(n_exp, dtype=I32)
    pad_ends = jnp.sum(jnp.where(e_ids[None, :] <= e_ids[:, None], padded[None, :], 0), axis=1)
    pad_starts = pad_ends - padded
    n_rows = (n_tok * TOP_K // bm + n_exp) * bm
    n_blocks = n_rows // bm
    dest = rank + jnp.sum(jnp.where(top_idx[..., None] == e_ids, pad_starts, 0), axis=-1)
    block_start = jnp.arange(n_blocks, dtype=I32) * bm
    block_expert = jnp.minimum(
        jnp.sum((block_start[:, None] >= pad_ends[None, :]).astype(I32), axis=1), n_exp - 1)
    n_used = pad_ends[-1:] // bm
    present = counts > 0
    seg_ordinal = jnp.sum(jnp.where((e_ids[None, :] < e_ids[:, None]) & present[None, :], 1, 0), axis=1)
    following = jnp.min(jnp.where((e_ids[None, :] > e_ids[:, None]) & present[None, :], e_ids[None, :], n_exp),
                        axis=1)
    following = jnp.where(following == n_exp, -1, following)
    block_hot = block_expert[:, None] == e_ids[None, :]
    seg_slot = jnp.sum(jnp.where(block_hot, seg_ordinal[None, :] % 2, 0), axis=1)
    next_expert = jnp.sum(jnp.where(block_hot, following[None, :], 0), axis=1)
    block_valid = jnp.clip(
        jnp.sum(jnp.where(block_hot, (pad_starts + counts)[None, :], 0), axis=1) - block_start, 0, bm)

    wp = h2p.shape[2]
    plane = jnp.arange(SC_SPLIT, dtype=I32)[None, :, None] * n_rows
    dest_sub = dest[:, None, :] + plane
    return dict(b0=b0, n_tok=n_tok, n_rows=n_rows, x1=x1, h2p=h2p, gt=gt, dest_sub=dest_sub,
                tables=(block_expert, n_used, seg_slot, next_expert, block_valid))


def _experts_chunk(r, mod, out_prev, wgu, bgu, wd, bd, out_gain, dispatch_after):
    n_tok, n_rows, dest_sub = r["n_tok"], r["n_rows"], r["dest_sub"]
    wp = r["h2p"].shape[2]
    xs = _sc_scatter_rows(r["h2p"].reshape(SC_SPLIT * n_tok, wp), dest_sub.reshape(TOP_K, SC_SPLIT * n_tok),
                          SC_SPLIT * n_rows, dispatch_after).reshape(SC_SPLIT, n_rows, wp)
    ys = _moe_call(xs, *r["tables"], wgu, bgu, wd, bd)
    yg = _sc_gather_rows(ys.reshape(SC_SPLIT * n_rows, wp), dest_sub.reshape(1, TOP_K * SC_SPLIT * n_tok))
    return _combine_call(r["x1"], r["gt"], mod, out_gain, yg.reshape(TOP_K, SC_SPLIT, n_tok, wp), out_prev,
                         r["b0"])


def kernel(x, c, w_ada, b_ada, norm1_g, w_in, conv_w, conv_b, conv_ln_g, conv_ln_b, lru_conv_w, lru_conv_b,
           lru_w_a, lru_b_a, lru_w_x, lru_b_x, lru_lambda, mix_norm_g, w_out, norm2_g, w_router, b_router,
           w_gate_up, b_gate_up, w_down, b_down, final_norm_g):
    depth = w_ada.shape[0]
    assert depth == 1, "the final norm is fused into the (single) layer's combine kernel"
    bsz, seq, dm = x.shape
    mod = _ada_call(c, w_ada[0], b_ada[0]).reshape(bsz, N_MOD, dm)
    return _layer(x, mod, 0, w_in, norm1_g, conv_w, conv_b, conv_ln_g, conv_ln_b, lru_conv_w, lru_conv_b,
                  lru_w_a, lru_b_a, lru_w_x, lru_b_x, lru_lambda, mix_norm_g, w_out, norm2_g, w_router,
                  b_router, w_gate_up, b_gate_up, w_down, b_down, row_gain(final_norm_g))


def row_gain(g):
    return g.reshape(1, -1)
```

```python
import functools

import jax
import jax.numpy as jnp
from jax import lax
from jax.experimental import pallas as pl
from jax.experimental.pallas import tpu as pltpu
from jax.experimental.pallas import tpu_sc as plsc

F32 = jnp.float32
BF16 = jnp.bfloat16
I32 = jnp.int32

EPS = 1e-6
N_MOD = 6
LRU_C = 8.0
TOP_K = 4
SWIGLU_ALPHA = 1.702
SWIGLU_LIMIT = 7.0

LANES = 128
SUBLANES = 8
VMEM_LIMIT_BYTES = 56 * 1024 * 1024

SEQ_TILE = 512
TILES_PER_STEP = 2
COMBINE_TILE = 512
CONV_PAD = 32
CONV_ROWS = 128
MOE_BLOCK = 1024
MOE_SUB = 256
CHUNK_BATCHES = (6, 2)
HI_MASK = -65536

SC_CORES = 2
SC_SUBCORES = 16
SC_LANES = 16
SC_PACK_ROWS = 16
SC_PACK_COLS = 512
SC_WINDOW = 128
SC_SPLIT = 2


def _sigmoid(x):
    return 0.5 * jnp.tanh(0.5 * x) + 0.5


def _pack_bf16_pair(lo_f32, hi_f32):
    lo_bits = lax.bitcast_convert_type(lo_f32.astype(BF16).astype(F32), I32)
    hi_bits = lax.bitcast_convert_type(hi_f32.astype(BF16).astype(F32), I32)
    return lax.shift_right_logical(lo_bits, 16) | hi_bits


def _unpack_bf16_pair(p):
    lo = lax.bitcast_convert_type(lax.shift_left(p, 16), F32)
    hi = lax.bitcast_convert_type(p & HI_MASK, F32)
    return lo, hi


def _store_planes(ref, packed, rows=slice(None)):
    wp = packed.shape[1] // SC_SPLIT
    for s in range(SC_SPLIT):
        ref[s, rows, :] = packed[:, s * wp:(s + 1) * wp]


def _load_planes(ref):
    return jnp.concatenate([ref[s] for s in range(SC_SPLIT)], axis=1)


def _ada_kernel(c_ref, w_ref, b_ref, o_ref):
    c = c_ref[...]
    ca = c * _sigmoid(c)
    w = w_ref[...]
    c_hi = ca.astype(BF16)
    c_lo = (ca - c_hi.astype(F32)).astype(BF16)
    w_hi = w.astype(BF16)
    w_lo = (w - w_hi.astype(F32)).astype(BF16)
    acc = jnp.dot(c_hi, w_hi, preferred_element_type=F32)
    acc += jnp.dot(c_lo, w_hi, preferred_element_type=F32)
    acc += jnp.dot(c_hi, w_lo, preferred_element_type=F32)
    o_ref[...] = acc + b_ref[...]


def _ada_call(c, w_ada, b_ada):
    bsz, dm = c.shape
    n_out = w_ada.shape[1]
    tn = 1024
    return pl.pallas_call(
        _ada_kernel,
        grid=(n_out // tn,),
        in_specs=[
            pl.BlockSpec((bsz, dm), lambda n: (0, 0)),
            pl.BlockSpec((dm, tn), lambda n: (0, n)),
            pl.BlockSpec((1, tn), lambda n: (0, n)),
        ],
        out_specs=pl.BlockSpec((bsz, tn), lambda n: (0, n)),
        out_shape=jax.ShapeDtypeStruct((bsz, n_out), F32),
        compiler_params=pltpu.CompilerParams(dimension_semantics=("arbitrary",)),
        name="ada_mod",
    )(c, w_ada, b_ada.reshape(1, n_out))


def _rms(x, eps=EPS):
    return x * lax.rsqrt(jnp.mean(x * x, axis=-1, keepdims=True) + eps)


def _gelu_tanh(x):
    return 0.5 * x * (1.0 + jnp.tanh(0.7978845608028654 * (x + 0.044715 * (x * x * x))))


def _slab_store(buf, row0, val):
    for s in range(buf.shape[0]):
        buf[s, pl.ds(row0, val.shape[0]), :] = val[:, s * LANES:(s + 1) * LANES]


def _slab_keep_tail(buf, keep, ts):
    for s in range(buf.shape[0]):
        buf[s, pl.ds(0, keep), :] = buf[s, pl.ds(ts, keep), :]


def _causal_tap_sum(buf, w_ref, bias_row, first, n_taps, ts):
    w = w_ref[...]
    cols = []
    for s in range(buf.shape[0]):
        lanes = slice(s * LANES, (s + 1) * LANES)
        chunks = []
        for c in range(0, ts, CONV_ROWS):
            acc = jnp.broadcast_to(bias_row[:, lanes], (CONV_ROWS, LANES))
            for k in range(n_taps):
                acc = acc + w[k:k + 1, lanes] * buf[s, pl.ds(c + first + k, CONV_ROWS), :]
            chunks.append(acc)
        cols.append(jnp.concatenate(chunks, axis=0))
    return jnp.concatenate(cols, axis=1)


def _linear_recurrence(a, b, a_s, b_s, hcar):
    ts = a.shape[0]
    blk = ts // SUBLANES
    pitch = blk + 1
    cols = []
    for j in range(a_s.shape[0]):
        lanes = slice(j * LANES, (j + 1) * LANES)
        for k in range(SUBLANES):
            a_s[j, pl.ds(k * pitch, blk), :] = a[k * blk:(k + 1) * blk, lanes]
            b_s[j, pl.ds(k * pitch, blk), :] = b[k * blk:(k + 1) * blk, lanes]
        h = jnp.zeros((SUBLANES, LANES), F32)
        p = jnp.ones((SUBLANES, LANES), F32)
        for i in range(blk):
            row_i = pl.ds(i, SUBLANES, stride=pitch)
            a_i = a_s[j, row_i, :]
            h = a_i * h + b_s[j, row_i, :]
            p = a_i * p
            b_s[j, row_i, :] = h
            a_s[j, row_i, :] = p
        state = hcar[:, lanes]
        entering = []
        for k in range(SUBLANES):
            entering.append(state)
            state = p[k:k + 1, :] * state + h[k:k + 1, :]
        hcar[:, lanes] = state
        h_in = jnp.concatenate(entering, axis=0)
        for i in range(blk):
            row_i = pl.ds(i, SUBLANES, stride=pitch)
            b_s[j, row_i, :] = b_s[j, row_i, :] + a_s[j, row_i, :] * h_in
        cols.append(jnp.concatenate([b_s[j, pl.ds(k * pitch, blk), :] for k in range(SUBLANES)], axis=0))
    return jnp.concatenate(cols, axis=1)


def _mix_kernel(x_ref, xn_ref, mod_ref, modn_ref, g1_ref, win_ref, cw_ref, cb_ref, lng_ref, lnb_ref,
                lcw_ref, lcb_ref, wg_ref, bg_ref, lam_ref, mng_ref, wout_ref, g2_ref,
                wrt_ref, br_ref, anchor_ref,
                x1_ref, h2p_ref, idx_ref, rank_ref, gt_ref, cnt_ref,
                u_a, u_b, vbuf, rbuf, a_s, b_s, hcar, cnt_s,
                *, tile, n_tiles, steps_per_seq, d_conv, d_lru, conv_w, lru_cw, n_exp):
    del anchor_ref
    s = pl.program_id(0)

    @pl.when(s % steps_per_seq == 0)
    def _():
        _slab_store(vbuf, 0, jnp.zeros((CONV_PAD, d_conv), F32))
        _slab_store(rbuf, 0, jnp.zeros((SUBLANES, d_lru), F32))
        hcar[...] = jnp.zeros_like(hcar)

    mod = mod_ref[...]
    modn = modn_ref[...]
    shift1, gate1, shift2 = mod[0:1], mod[2:3], mod[3:4]
    gain1 = g1_ref[...] * (1.0 + mod[1:2])
    gain2 = g2_ref[...] * (1.0 + mod[4:5])
    lam = lam_ref[...]
    softplus_neg_lam = jnp.maximum(-lam, 0.0) + jnp.log1p(jnp.exp(-jnp.abs(lam)))

    def project(src_ref, r0, gain, shift, u_ref):
        h = _rms(src_ref[pl.ds(r0, tile), :]) * gain + shift
        u_ref[...] = jnp.dot(h.astype(BF16), win_ref[...], preferred_element_type=F32)

    def conv_group(r0, u_ref):
        return _mix_conv_group(r0, tile, u_ref, cw_ref, cb_ref, lng_ref, lnb_ref, vbuf, d_conv=d_conv,
                               conv_w=conv_w)

    def finish(r0, u_ref, yc):
        _mix_finish(r0, tile, x_ref, u_ref, yc, gate1, shift2, gain2, softplus_neg_lam,
                    lcw_ref, lcb_ref, wg_ref, bg_ref, mng_ref, wout_ref,
                    wrt_ref, br_ref, x1_ref, h2p_ref, idx_ref, rank_ref, gt_ref, rbuf, a_s, b_s, hcar,
                    cnt_s, d_conv=d_conv, d_lru=d_lru, lru_cw=lru_cw, n_exp=n_exp)

    @pl.when(s == 0)
    def _():
        cnt_s[...] = jnp.zeros_like(cnt_s)
        project(x_ref, 0, gain1, shift1, u_a)

    bufs = (u_a, u_b)
    for q in range(n_tiles):
        cur, nxt = bufs[q % 2], bufs[(q + 1) % 2]
        yc = conv_group(q * tile, cur)
        if q + 1 < n_tiles:
            project(x_ref, (q + 1) * tile, gain1, shift1, nxt)
        else:
            project(xn_ref, 0, g1_ref[...] * (1.0 + modn[1:2]), modn[0:1], nxt)
        finish(q * tile, cur, yc)
    _slab_keep_tail(vbuf, CONV_PAD, n_tiles * tile)
    _slab_keep_tail(rbuf, SUBLANES, n_tiles * tile)
    cnt_ref[...] = cnt_s[:, :LANES]


def _mix_conv_group(r0, ts, u_ref, cw_ref, cb_ref, lng_ref, lnb_ref, vbuf, *, d_conv, conv_w):
    v = u_ref[:, :d_conv] * _sigmoid(u_ref[:, d_conv:2 * d_conv])
    _slab_store(vbuf, CONV_PAD + r0, v)
    acc = _causal_tap_sum(vbuf, cw_ref, cb_ref[...], r0 + CONV_PAD - (conv_w - 1), conv_w, ts)
    mu = jnp.mean(acc, axis=-1, keepdims=True)
    cen = acc - mu
    var = jnp.mean(cen * cen, axis=-1, keepdims=True)
    yc = cen * lax.rsqrt(var + EPS) * lng_ref[...] + lnb_ref[...]
    return yc * _sigmoid(yc)


def _mix_finish(r0, ts, x_ref, u_ref, yc, gate1, shift2, gain2, softplus_neg_lam,
                lcw_ref, lcb_ref, wg_ref, bg_ref, mng_ref, wout_ref,
                wrt_ref, br_ref, x1_ref, h2p_ref, idx_ref, rank_ref, gt_ref, rbuf, a_s, b_s, hcar,
                cnt_s, *, d_conv, d_lru, lru_cw, n_exp):
    rows = pl.ds(r0, ts)
    x = x_ref[rows, :]
    u = u_ref

    u_gate = u[:, 2 * d_conv:2 * d_conv + d_lru]
    _slab_store(rbuf, SUBLANES + r0, u[:, 2 * d_conv + d_lru:])
    xr = _causal_tap_sum(rbuf, lcw_ref, lcb_ref[...], r0 + SUBLANES - (lru_cw - 1), lru_cw, ts)
    gates = jnp.dot(xr.astype(BF16), wg_ref[...], preferred_element_type=F32) + bg_ref[...]
    r = _sigmoid(gates[:, :d_lru])
    i_g = _sigmoid(gates[:, d_lru:])
    log_a = (-LRU_C) * r * softplus_neg_lam
    a = jnp.exp(log_a)
    inp = jnp.sqrt(1.0 - jnp.exp(2.0 * log_a)) * (i_g * xr)

    yl = _gelu_tanh(u_gate) * _linear_recurrence(a, inp, a_s, b_s, hcar)

    mng = mng_ref[...]
    yc_n = _rms(yc) * mng[:, :d_conv]
    yl_n = _rms(yl) * mng[:, d_conv:]
    mixo = jnp.dot(yc_n.astype(BF16), wout_ref[pl.ds(0, d_conv), :], preferred_element_type=F32)
    mixo += jnp.dot(yl_n.astype(BF16), wout_ref[pl.ds(d_conv, d_lru), :], preferred_element_type=F32)
    x1 = x + gate1 * mixo
    x1_ref[rows, :] = x1

    h2 = _rms(x1) * gain2 + shift2
    dh = h2.shape[1] // 2
    h2_hi = h2.astype(BF16)
    _store_planes(h2p_ref, _pack_bf16_pair(h2[:, :dh], h2[:, dh:]), rows)
    h2_lo = (h2 - h2_hi.astype(F32)).astype(BF16)
    nt_dims = (((1,), (1,)), ((), ()))
    wrt = wrt_ref[...]
    lg = lax.dot_general(wrt, h2_hi, nt_dims, preferred_element_type=F32)
    lg2 = lax.dot_general(wrt[:n_exp], h2_lo, nt_dims, preferred_element_type=F32)
    logits = lg[:n_exp] + lg[n_exp:] + lg2 + br_ref[...]

    eidx = lax.broadcasted_iota(I32, (n_exp, ts), 0)
    neg_inf = jnp.float32(-jnp.inf)
    work = logits
    vals, idxs, hots = [], [], []
    for _ in range(TOP_K):
        mval = jnp.max(work, axis=0, keepdims=True)
        midx = jnp.min(jnp.where(work == mval, eidx, n_exp), axis=0, keepdims=True)
        hot = eidx == midx
        vals.append(mval)
        idxs.append(midx)
        hots.append(hot)
        work = jnp.where(hot, neg_inf, work)
    exps = [jnp.exp(vk - vals[0]) for vk in vals]
    denom = exps[0] + exps[1] + exps[2] + exps[3]
    gate_rows = [ek / denom for ek in exps]

    sel = jnp.zeros((n_exp, ts), F32)
    for hot in hots:
        sel = sel + hot.astype(F32)
    tri = (lax.broadcasted_iota(I32, (ts, ts), 0) < lax.broadcasted_iota(I32, (ts, ts), 1)).astype(BF16)
    before = jnp.dot(sel.astype(BF16), tri, preferred_element_type=F32) + cnt_s[...]
    rank_rows = [jnp.sum(jnp.where(hot, before, 0.0), axis=0, keepdims=True) for hot in hots]
    cnt_s[...] = cnt_s[...] + jnp.sum(sel, axis=1, keepdims=True)

    idx_ref[:, rows] = jnp.concatenate(idxs, axis=0)
    rank_ref[:, rows] = jnp.concatenate(rank_rows, axis=0).astype(I32)
    g4 = jnp.concatenate(gate_rows, axis=0)
    g_pad = jnp.concatenate([g4, jnp.zeros((LANES - TOP_K, ts), F32)], axis=0)
    gt_ref[rows, :] = g_pad.T


def _mix_call(x, mod, p, b0, bsz, anchor):
    _, seq, dm = x.shape
    n_tiles = TILES_PER_STEP
    tile = min(SEQ_TILE, seq // n_tiles)
    ts = n_tiles * tile
    nj = seq // ts
    n_steps = bsz * nj
    tiles_per_seq = seq // tile
    d_conv = p["conv_w"].shape[1]
    d_lru = p["lru_conv_w"].shape[1]
    conv_w = p["conv_w"].shape[0]
    lru_cw = p["lru_conv_w"].shape[0]
    n_exp = p["wrt"].shape[0] // 2
    assert seq % ts == 0 and tile % LANES == 0 and tile % CONV_ROWS == 0 and n_tiles % 2 == 0
    assert conv_w - 1 <= CONV_PAD and lru_cw - 1 <= SUBLANES and d_conv % LANES == 0 and d_lru % LANES == 0

    def full(a):
        return pl.BlockSpec(a.shape, lambda s: (0,) * a.ndim)

    def next_tile(s):
        return jnp.minimum(n_tiles * (s + 1), bsz * tiles_per_seq - 1)

    weights = [p["g1"], p["win"], p["conv_w"], p["conv_b"], p["ln_g"], p["ln_b"], p["lru_conv_w"],
               p["lru_conv_b"], p["wg"], p["bg"], p["lam"], p["mng"], p["wout"], p["g2"], p["wrt"], p["br"]]
    kern = functools.partial(_mix_kernel, tile=tile, n_tiles=n_tiles, steps_per_seq=nj, d_conv=d_conv, d_lru=d_lru,
                             conv_w=conv_w, lru_cw=lru_cw, n_exp=n_exp)
    out_shape = (
        jax.ShapeDtypeStruct((bsz, seq, dm), F32),
        jax.ShapeDtypeStruct((SC_SPLIT, bsz * seq, dm // 2 // SC_SPLIT), I32),
        jax.ShapeDtypeStruct((TOP_K, bsz * seq), I32),
        jax.ShapeDtypeStruct((TOP_K, bsz * seq), I32),
        jax.ShapeDtypeStruct((bsz, seq, LANES), F32),
        jax.ShapeDtypeStruct((n_exp, LANES), F32),
    )
    return pl.pallas_call(
        kern,
        grid=(n_steps,),
        in_specs=[pl.BlockSpec((None, ts, dm), lambda s: (s // nj + b0, s % nj, 0)),
                  pl.BlockSpec((None, tile, dm),
                               lambda s: (next_tile(s) // tiles_per_seq + b0, next_tile(s) % tiles_per_seq, 0)),
                  pl.BlockSpec((None, N_MOD, dm), lambda s: (s // nj + b0, 0, 0)),
                  pl.BlockSpec((None, N_MOD, dm), lambda s: (next_tile(s) // tiles_per_seq + b0, 0, 0))]
        + [full(w) for w in weights] + [pl.BlockSpec(memory_space=pl.ANY)],
        out_specs=(
            pl.BlockSpec((None, ts, dm), lambda s: (s // nj, s % nj, 0)),
            pl.BlockSpec((SC_SPLIT, ts, dm // 2 // SC_SPLIT), lambda s: (0, s, 0)),
            pl.BlockSpec((TOP_K, ts), lambda s: (0, s)),
            pl.BlockSpec((TOP_K, ts), lambda s: (0, s)),
            pl.BlockSpec((None, ts, LANES), lambda s: (s // nj, s % nj, 0)),
            pl.BlockSpec((n_exp, LANES), lambda s: (0, 0)),
        ),
        out_shape=out_shape,
        scratch_shapes=[
            pltpu.VMEM((tile, 2 * d_conv + 2 * d_lru), F32),
            pltpu.VMEM((tile, 2 * d_conv + 2 * d_lru), F32),
            pltpu.VMEM((d_conv // LANES, CONV_PAD + ts, LANES), F32),
            pltpu.VMEM((d_lru // LANES, SUBLANES + ts, LANES), F32),
            pltpu.VMEM((d_lru // LANES, tile + SUBLANES, LANES), F32),
            pltpu.VMEM((d_lru // LANES, tile + SUBLANES, LANES), F32),
            pltpu.VMEM((1, d_lru), F32),
            pltpu.VMEM((n_exp, tile), F32),
        ],
        compiler_params=pltpu.CompilerParams(dimension_semantics=("arbitrary",),
                                             vmem_limit_bytes=VMEM_LIMIT_BYTES),
        name="token_mix_route",
    )(x, x, mod, mod, *weights, anchor)


def _sc_mesh():
    return plsc.VectorSubcoreMesh(core_axis_name="core", subcore_axis_name="subcore",
                                  num_cores=SC_CORES, num_subcores=SC_SUBCORES)


def _sc_scatter_rows(rows, idx, n_out, after):
    n, w = rows.shape
    n_k = idx.shape[0]
    assert n % (SC_WINDOW * SC_CORES * SC_SUBCORES) == 0

    @functools.partial(pl.kernel, out_type=jax.ShapeDtypeStruct((n_out, w), rows.dtype), mesh=_sc_mesh(),
                       scratch_types=[], name="moe_dispatch_sc")
    def scatter(x_hbm, i_hbm, after_hbm, o_hbm):
        del after_hbm

        def body(x_vmem, *i_vmems):
            for i_vmem in i_vmems:
                pltpu.sync_copy(x_vmem, o_hbm.at[i_vmem.at[0]])

        pltpu.emit_pipeline(
            body,
            grid=(n // SC_WINDOW,),
            in_specs=[pl.BlockSpec((SC_WINDOW, w), lambda i: (i, 0))]
            + [pl.BlockSpec((1, SC_WINDOW), functools.partial(lambda k, i: (k, i), k)) for k in range(n_k)],
            out_specs=[],
            core_axis_name=("core", "subcore"),
            dimension_semantics=(pltpu.PARALLEL,),
        )(x_hbm, *([i_hbm] * n_k))

    return scatter(rows, idx, after)


def _sc_gather_rows(table, idx):
    m = idx.shape[1]
    w = table.shape[1]
    assert m % (SC_WINDOW * SC_CORES * SC_SUBCORES) == 0

    @functools.partial(pl.kernel, out_type=jax.ShapeDtypeStruct((m, w), table.dtype), mesh=_sc_mesh(),
                       scratch_types=[], name="moe_gather_sc")
    def gather(x_hbm, i_hbm, o_hbm):
        def body(i_vmem, o_vmem):
            pltpu.sync_copy(x_hbm.at[i_vmem.at[0]], o_vmem)

        pltpu.emit_pipeline(
            body,
            grid=(m // SC_WINDOW,),
            in_specs=[pl.BlockSpec((1, SC_WINDOW), lambda i: (0, i))],
            out_specs=[pl.BlockSpec((SC_WINDOW, w), lambda i: (i, 0))],
            core_axis_name=("core", "subcore"),
            dimension_semantics=(pltpu.PARALLEL,),
        )(i_hbm, o_hbm)

    return gather(table, idx)


def _sc_pack_rows_bf16(w):
    r, c = w.shape
    assert r % SC_PACK_ROWS == 0 and c % SC_PACK_COLS == 0 and (r // SC_PACK_ROWS) % (SC_CORES * SC_SUBCORES) == 0
    lanes = SC_LANES

    @functools.partial(pl.kernel, out_type=jax.ShapeDtypeStruct((r // 2, c), I32), mesh=_sc_mesh(),
                       scratch_types=[], compiler_params=pltpu.CompilerParams(needs_layout_passes=False),
                       name="expert_weight_pack_sc")
    def pack(x_hbm, o_hbm):
        def body(x_vmem, o_vmem):
            @pl.loop(0, SC_PACK_ROWS // 2)
            def _(i):
                for j in range(0, SC_PACK_COLS, lanes):
                    pair = plsc.pack(x_vmem[2 * i, pl.ds(j, lanes)], x_vmem[2 * i + 1, pl.ds(j, lanes)],
                                     format=plsc.PackFormat.INTERLEAVED)
                    o_vmem[i, pl.ds(j, lanes)] = plsc.bitcast(pair, I32)

        pltpu.emit_pipeline(
            body,
            grid=(r // SC_PACK_ROWS, c // SC_PACK_COLS),
            in_specs=[pl.BlockSpec((SC_PACK_ROWS, SC_PACK_COLS), lambda i, j: (i, j))],
            out_specs=[pl.BlockSpec((SC_PACK_ROWS // 2, SC_PACK_COLS), lambda i, j: (i, j))],
            core_axis_name=("core", "subcore"),
            dimension_semantics=(pltpu.PARALLEL, pltpu.PARALLEL),
        )(x_hbm, o_hbm)

    return pack(w)


def _moe_kernel(be_ref, nb_ref, slot_ref, nxt_ref, nv_ref, xs_ref, wgu_hbm, bgu_ref, wd_hbm, bd_ref, ys_ref,
                wgu_s, wd_s, sems, *, d_ff):
    i = pl.program_id(0)
    live = i < nb_ref[0]
    new_expert = (i == 0) | (be_ref[i] != be_ref[jnp.maximum(i - 1, 0)])
    slot = slot_ref[i]

    def weight_copies(e, s):
        return (pltpu.make_async_copy(wgu_hbm.at[e], wgu_s.at[s], sems.at[0, s]),
                pltpu.make_async_copy(wd_hbm.at[e], wd_s.at[s], sems.at[1, s]))

    @pl.when(live & new_expert)
    def _():
        e, nx = be_ref[i], nxt_ref[i]

        @pl.when(i == 0)
        def _():
            for cp in weight_copies(e, slot):
                cp.start()

        for cp in weight_copies(e, slot):
            cp.wait()

        @pl.when(nx >= 0)
        def _():
            for cp in weight_copies(nx, 1 - slot):
                cp.start()

    def sub_block(s):
        rows = pl.ds(s * MOE_SUB, MOE_SUB)
        packed = jnp.concatenate([xs_ref[p, rows, :] for p in range(SC_SPLIT)], axis=1)
        x_lo, x_hi = _unpack_bf16_pair(packed)
        dh = x_lo.shape[1]
        w_top = pltpu.bitcast(wgu_s[slot, pl.ds(0, dh // 2), :], BF16)
        w_bot = pltpu.bitcast(wgu_s[slot, pl.ds(dh // 2, dh // 2), :], BF16)
        gu = jnp.dot(x_lo.astype(BF16), w_top, preferred_element_type=F32)
        gu += jnp.dot(x_hi.astype(BF16), w_bot, preferred_element_type=F32)
        gu += bgu_ref[...]
        g = jnp.minimum(gu[:, :d_ff], SWIGLU_LIMIT)
        u = jnp.clip(gu[:, d_ff:], -SWIGLU_LIMIT, SWIGLU_LIMIT)
        act = (u + 1.0) * (g * _sigmoid(SWIGLU_ALPHA * g))
        y = jnp.dot(act.astype(BF16), pltpu.bitcast(wd_s[slot], BF16), preferred_element_type=F32) + bd_ref[...]
        dm2 = y.shape[1] // 2
        _store_planes(ys_ref, _pack_bf16_pair(y[:, :dm2], y[:, dm2:]), rows)

    n_sub = xs_ref.shape[1] // MOE_SUB
    n_live_sub = (nv_ref[i] + (MOE_SUB - 1)) // MOE_SUB

    for k in range(1, n_sub + 1):
        @pl.when(live & (n_live_sub == k))
        def _():
            for s in range(k):
                sub_block(s)


def _moe_call(xs, block_expert, n_used, seg_slot, next_expert, block_valid, wgu, bgu, wd, bd):
    _, n_rows, wp = xs.shape
    n_exp, dm_half, d_ff2 = wgu.shape
    dm = 2 * dm_half
    d_ff = d_ff2 // 2
    bm = MOE_BLOCK
    n_blocks = n_rows // bm

    def row_map(i, be, nb, slot, nxt, nv):
        return (0, jnp.minimum(i, nb[0] - 1), 0)

    def b_map(i, be, nb, slot, nxt, nv):
        return (be[jnp.minimum(i, nb[0] - 1)], 0, 0)

    grid_spec = pltpu.PrefetchScalarGridSpec(
        num_scalar_prefetch=5,
        grid=(n_blocks,),
        in_specs=[
            pl.BlockSpec((SC_SPLIT, bm, wp), row_map),
            pl.BlockSpec(memory_space=pl.ANY),
            pl.BlockSpec((None, 1, d_ff2), b_map),
            pl.BlockSpec(memory_space=pl.ANY),
            pl.BlockSpec((None, 1, dm), b_map),
        ],
        out_specs=pl.BlockSpec((SC_SPLIT, bm, wp), row_map),
        scratch_shapes=[
            pltpu.VMEM((2,) + wgu.shape[1:], I32), pltpu.VMEM((2,) + wd.shape[1:], I32),
            pltpu.SemaphoreType.DMA((2, 2)),
        ],
    )
    return pl.pallas_call(
        functools.partial(_moe_kernel, d_ff=d_ff),
        grid_spec=grid_spec,
        out_shape=jax.ShapeDtypeStruct(xs.shape, I32),
        compiler_params=pltpu.CompilerParams(dimension_semantics=("arbitrary",),
                                             vmem_limit_bytes=VMEM_LIMIT_BYTES),
        name="moe_experts",
    )(block_expert, n_used, seg_slot, next_expert, block_valid, xs, wgu, bgu, wd, bd)


def _combine_kernel(x1_ref, gt_ref, mod_ref, gf_ref, yg_ref, *out_refs):
    o_ref = out_refs[-1]
    gt = gt_ref[...]
    ts = yg_ref.shape[2]
    dh = yg_ref.shape[3] * SC_SPLIT
    acc_lo = jnp.zeros((ts, dh), F32)
    acc_hi = jnp.zeros((ts, dh), F32)
    for k in range(TOP_K):
        lo, hi = _unpack_bf16_pair(_load_planes(yg_ref.at[k]))
        gk = gt[:, k:k + 1]
        acc_lo += gk * lo
        acc_hi += gk * hi
    y = jnp.concatenate([acc_lo, acc_hi], axis=1)
    gate2 = mod_ref[...][5:6]
    x2 = x1_ref[...] + gate2 * y
    o_ref[...] = _rms(x2) * gf_ref[...]


def _combine_call(x1, gt, mod, gf, yg, out_prev, b0):
    bsz, seq, dm = x1.shape
    ts = min(COMBINE_TILE, seq)
    assert seq % ts == 0
    nj = seq // ts
    in_specs = [
        pl.BlockSpec((None, ts, dm), lambda b, j: (b, j, 0)),
        pl.BlockSpec((None, ts, LANES), lambda b, j: (b, j, 0)),
        pl.BlockSpec((None, N_MOD, dm), lambda b, j: (b + b0, 0, 0)),
        pl.BlockSpec((1, dm), lambda b, j: (0, 0)),
        pl.BlockSpec((TOP_K, SC_SPLIT, ts, yg.shape[3]), lambda b, j: (0, 0, b * nj + j, 0)),
    ]
    args = [x1, gt, mod, gf, yg]
    aliases = {}
    if out_prev is not None:
        in_specs.append(pl.BlockSpec(memory_space=pl.ANY))
        args.append(out_prev)
        aliases = {len(args) - 1: 0}
    return pl.pallas_call(
        _combine_kernel,
        grid=(bsz, nj),
        in_specs=in_specs,
        out_specs=pl.BlockSpec((None, ts, dm), lambda b, j: (b + b0, j, 0)),
        out_shape=jax.ShapeDtypeStruct((mod.shape[0], seq, dm), F32),
        input_output_aliases=aliases,
        compiler_params=pltpu.CompilerParams(dimension_semantics=("arbitrary", "arbitrary"),
                                             vmem_limit_bytes=VMEM_LIMIT_BYTES),
        name="moe_combine",
    )(*args)


def _block_diag(w):
    n_h, d, _ = w.shape
    eye = jnp.eye(n_h, dtype=w.dtype)
    return (eye[:, None, :, None] * w[:, :, None, :]).reshape(n_h * d, n_h * d)


def _layer(x, mod, l, w_in, norm1_g, conv_w, conv_b, conv_ln_g, conv_ln_b, lru_conv_w, lru_conv_b,
           lru_w_a, lru_b_a, lru_w_x, lru_b_x, lru_lambda, mix_norm_g, w_out, norm2_g, w_router,
           b_router, w_gate_up, b_gate_up, w_down, b_down, out_gain):
    bsz, seq, dm = x.shape
    n_tok = bsz * seq
    n_exp = w_router.shape[-1]
    row = lambda a: a.reshape(1, -1)
    wr = w_router[l]
    wr_hi = wr.astype(BF16)
    wr_lo = (wr - wr_hi.astype(F32)).astype(BF16)
    params = dict(
        g1=row(norm1_g[l]), win=w_in[l].astype(BF16), conv_w=conv_w[l], conv_b=row(conv_b[l]),
        ln_g=row(conv_ln_g[l]), ln_b=row(conv_ln_b[l]), lru_conv_w=lru_conv_w[l], lru_conv_b=row(lru_conv_b[l]),
        wg=jnp.concatenate([_block_diag(lru_w_a[l]), _block_diag(lru_w_x[l])], axis=1).astype(BF16),
        bg=jnp.concatenate([lru_b_a[l].reshape(1, -1), lru_b_x[l].reshape(1, -1)], axis=1),
        lam=row(lru_lambda[l]), mng=row(mix_norm_g[l]), wout=w_out[l].astype(BF16), g2=row(norm2_g[l]),
        wrt=jnp.concatenate([wr_hi.T, wr_lo.T], axis=0), br=b_router[l].reshape(n_exp, 1),
    )
    _, d_in, d_ff2 = w_gate_up[l].shape
    _, d_ff, d_out = w_down[l].shape
    wgu_p = _sc_pack_rows_bf16(w_gate_up[l].reshape(n_exp * d_in, d_ff2)).reshape(n_exp, d_in // 2, d_ff2)
    wd_p = _sc_pack_rows_bf16(w_down[l].reshape(n_exp * d_ff, d_out)).reshape(n_exp, d_ff // 2, d_out)
    sizes = [s for s in CHUNK_BATCHES if s > 0] if sum(CHUNK_BATCHES) == bsz else [bsz]
    routed, b0, anchor = [], 0, mod
    for cb in sizes:
        r = _route_chunk(x, mod, params, b0, cb, anchor, n_exp)
        routed.append(r)
        anchor = r["dest_sub"]
        b0 += cb
    out = None
    sc_order = [wgu_p] + [wd_p] * (len(routed) - 1)
    for r, after in zip(routed, sc_order):
        out = _experts_chunk(r, mod, out, wgu_p, b_gate_up[l][:, None, :], wd_p, b_down[l][:, None, :], out_gain,
                             after)
    return out


def _route_chunk(x, mod, params, b0, cb, anchor, n_exp):
    seq, dm = x.shape[1:]
    n_tok = cb * seq
    x1, h2p, top_idx, rank, gt, cnt = _mix_call(x, mod, params, b0, cb, anchor)

    bm = MOE_BLOCK
    counts = cnt[:, 0].astype(I32)
    padded = (counts + bm - 1) // bm * bm
    e_ids = jnp.arange(n_exp, dtype=I32)
    pad_ends = jnp.sum(jnp.where(e_ids[None, :] <= e_ids[:, None], padded[None, :], 0), axis=1)
    pad_starts = pad_ends - padded
    n_rows = (n_tok * TOP_K // bm + n_exp) * bm
    n_blocks = n_rows // bm
    dest = rank + jnp.sum(jnp.where(top_idx[..., None] == e_ids, pad_starts, 0), axis=-1)
    block_start = jnp.arange(n_blocks, dtype=I32) * bm
    block_expert = jnp.minimum(
        jnp.sum((block_start[:, None] >= pad_ends[None, :]).astype(I32), axis=1), n_exp - 1)
    n_used = pad_ends[-1:] // bm
    present = counts > 0
    seg_ordinal = jnp.sum(jnp.where((e_ids[None, :] < e_ids[:, None]) & present[None, :], 1, 0), axis=1)
    following = jnp.min(jnp.where((e_ids[None, :] > e_ids[:, None]) & present[None, :], e_ids[None, :], n_exp),
                        axis=1)
    following = jnp.where(following == n_exp, -1, following)
    block_hot = block_expert[:, None] == e_ids[None, :]
    seg_slot = jnp.sum(jnp.where(block_hot, seg_ordinal[None, :] % 2, 0), axis=1)
    next_expert = jnp.sum(jnp.where(block_hot, following[None, :], 0), axis=1)
    block_valid = jnp.clip(
        jnp.sum(jnp.where(block_hot, (pad_starts + counts)[None, :], 0), axis=1) - block_start, 0, bm)

    wp = h2p.shape[2]
    plane = jnp.arange(SC_SPLIT, dtype=I32)[None, :, None] * n_rows
    dest_sub = dest[:, None, :] + plane
    return dict(b0=b0, n_tok=n_tok, n_rows=n_rows, x1=x1, h2p=h2p, gt=gt, dest_sub=dest_sub,
                tables=(block_expert, n_used, seg_slot, next_expert, block_valid))


def _experts_chunk(r, mod, out_prev, wgu, bgu, wd, bd, out_gain, dispatch_after):
    n_tok, n_rows, dest_sub = r["n_tok"], r["n_rows"], r["dest_sub"]
    wp = r["h2p"].shape[2]
    xs = _sc_scatter_rows(r["h2p"].reshape(SC_SPLIT * n_tok, wp), dest_sub.reshape(TOP_K, SC_SPLIT * n_tok),
                          SC_SPLIT * n_rows, dispatch_after).reshape(SC_SPLIT, n_rows, wp)
    ys = _moe_call(xs, *r["tables"], wgu, bgu, wd, bd)
    yg = _sc_gather_rows(ys.reshape(SC_SPLIT * n_rows, wp), dest_sub.reshape(1, TOP_K * SC_SPLIT * n_tok))
    return _combine_call(r["x1"], r["gt"], mod, out_gain, yg.reshape(TOP_K, SC_SPLIT, n_tok, wp), out_prev,
                         r["b0"])


def kernel(x, c, w_ada, b_ada, norm1_g, w_in, conv_w, conv_b, conv_ln_g, conv_ln_b, lru_conv_w, lru_conv_b,
           lru_w_a, lru_b_a, lru_w_x, lru_b_x, lru_lambda, mix_norm_g, w_out, norm2_g, w_router, b_router,
           w_gate_up, b_gate_up, w_down, b_down, final_norm_g):
    depth = w_ada.shape[0]
    assert depth == 1, "the final norm is fused into the (single) layer's combine kernel"
    bsz, seq, dm = x.shape
    mod = _ada_call(c, w_ada[0], b_ada[0]).reshape(bsz, N_MOD, dm)
    return _layer(x, mod, 0, w_in, norm1_g, conv_w, conv_b, conv_ln_g, conv_ln_b, lru_conv_w, lru_conv_b,
                  lru_w_a, lru_b_a, lru_w_x, lru_b_x, lru_lambda, mix_norm_g, w_out, norm2_g, w_router,
                  b_router, w_gate_up, b_gate_up, w_down, b_down, row_gain(final_norm_g))


def row_gain(g):
    return g.reshape(1, -1)
```

```python
import functools

import jax
import jax.numpy as jnp
from jax import lax
from jax.experimental import pallas as pl
from jax.experimental.pallas import tpu as pltpu
from jax.experimental.pallas import tpu_sc as plsc

F32 = jnp.float32
BF16 = jnp.bfloat16
I32 = jnp.int32

EPS = 1e-6
N_MOD = 6
LRU_C = 8.0
TOP_K = 4
SWIGLU_ALPHA = 1.702
SWIGLU_LIMIT = 7.0

LANES = 128
SUBLANES = 8
VMEM_LIMIT_BYTES = 56 * 1024 * 1024

SEQ_TILE = 512
TILES_PER_STEP = 2
COMBINE_TILE = 512
CONV_PAD = 32
CONV_ROWS = 128
MOE_BLOCK = 1024
MOE_SUB = 256
CHUNK_BATCHES = (5, 3)
HI_MASK = -65536

SC_CORES = 2
SC_SUBCORES = 16
SC_LANES = 16
SC_PACK_ROWS = 16
SC_PACK_COLS = 512
SC_WINDOW = 128
SC_SPLIT = 2


def _sigmoid(x):
    return 0.5 * jnp.tanh(0.5 * x) + 0.5


def _pack_bf16_pair(lo_f32, hi_f32):
    lo_bits = lax.bitcast_convert_type(lo_f32.astype(BF16).astype(F32), I32)
    hi_bits = lax.bitcast_convert_type(hi_f32.astype(BF16).astype(F32), I32)
    return lax.shift_right_logical(lo_bits, 16) | hi_bits


def _unpack_bf16_pair(p):
    lo = lax.bitcast_convert_type(lax.shift_left(p, 16), F32)
    hi = lax.bitcast_convert_type(p & HI_MASK, F32)
    return lo, hi


def _store_planes(ref, packed, rows=slice(None)):
    wp = packed.shape[1] // SC_SPLIT
    for s in range(SC_SPLIT):
        ref[s, rows, :] = packed[:, s * wp:(s + 1) * wp]


def _load_planes(ref):
    return jnp.concatenate([ref[s] for s in range(SC_SPLIT)], axis=1)


def _ada_kernel(c_ref, w_ref, b_ref, o_ref):
    c = c_ref[...]
    ca = c * _sigmoid(c)
    w = w_ref[...]
    c_hi = ca.astype(BF16)
    c_lo = (ca - c_hi.astype(F32)).astype(BF16)
    w_hi = w.astype(BF16)
    w_lo = (w - w_hi.astype(F32)).astype(BF16)
    acc = jnp.dot(c_hi, w_hi, preferred_element_type=F32)
    acc += jnp.dot(c_lo, w_hi, preferred_element_type=F32)
    acc += jnp.dot(c_hi, w_lo, preferred_element_type=F32)
    o_ref[...] = acc + b_ref[...]


def _ada_call(c, w_ada, b_ada):
    bsz, dm = c.shape
    n_out = w_ada.shape[1]
    tn = 1024
    return pl.pallas_call(
        _ada_kernel,
        grid=(n_out // tn,),
        in_specs=[
            pl.BlockSpec((bsz, dm), lambda n: (0, 0)),
            pl.BlockSpec((dm, tn), lambda n: (0, n)),
            pl.BlockSpec((1, tn), lambda n: (0, n)),
        ],
        out_specs=pl.BlockSpec((bsz, tn), lambda n: (0, n)),
        out_shape=jax.ShapeDtypeStruct((bsz, n_out), F32),
        compiler_params=pltpu.CompilerParams(dimension_semantics=("arbitrary",)),
        name="ada_mod",
    )(c, w_ada, b_ada.reshape(1, n_out))


def _rms(x, eps=EPS):
    return x * lax.rsqrt(jnp.mean(x * x, axis=-1, keepdims=True) + eps)


def _gelu_tanh(x):
    return 0.5 * x * (1.0 + jnp.tanh(0.7978845608028654 * (x + 0.044715 * (x * x * x))))


def _slab_store(buf, row0, val):
    for s in range(buf.shape[0]):
        buf[s, pl.ds(row0, val.shape[0]), :] = val[:, s * LANES:(s + 1) * LANES]


def _slab_keep_tail(buf, keep, ts):
    for s in range(buf.shape[0]):
        buf[s, pl.ds(0, keep), :] = buf[s, pl.ds(ts, keep), :]


def _causal_tap_sum(buf, w_ref, bias_row, first, n_taps, ts):
    w = w_ref[...]
    cols = []
    for s in range(buf.shape[0]):
        lanes = slice(s * LANES, (s + 1) * LANES)
        chunks = []
        for c in range(0, ts, CONV_ROWS):
            acc = jnp.broadcast_to(bias_row[:, lanes], (CONV_ROWS, LANES))
            for k in range(n_taps):
                acc = acc + w[k:k + 1, lanes] * buf[s, pl.ds(c + first + k, CONV_ROWS), :]
            chunks.append(acc)
        cols.append(jnp.concatenate(chunks, axis=0))
    return jnp.concatenate(cols, axis=1)


def _linear_recurrence(a, b, a_s, b_s, hcar):
    ts = a.shape[0]
    blk = ts // SUBLANES
    pitch = blk + 1
    cols = []
    for j in range(a_s.shape[0]):
        lanes = slice(j * LANES, (j + 1) * LANES)
        for k in range(SUBLANES):
            a_s[j, pl.ds(k * pitch, blk), :] = a[k * blk:(k + 1) * blk, lanes]
            b_s[j, pl.ds(k * pitch, blk), :] = b[k * blk:(k + 1) * blk, lanes]
        h = jnp.zeros((SUBLANES, LANES), F32)
        p = jnp.ones((SUBLANES, LANES), F32)
        for i in range(blk):
            row_i = pl.ds(i, SUBLANES, stride=pitch)
            a_i = a_s[j, row_i, :]
            h = a_i * h + b_s[j, row_i, :]
            p = a_i * p
            b_s[j, row_i, :] = h
            a_s[j, row_i, :] = p
        state = hcar[:, lanes]
        entering = []
        for k in range(SUBLANES):
            entering.append(state)
            state = p[k:k + 1, :] * state + h[k:k + 1, :]
        hcar[:, lanes] = state
        h_in = jnp.concatenate(entering, axis=0)
        for i in range(blk):
            row_i = pl.ds(i, SUBLANES, stride=pitch)
            b_s[j, row_i, :] = b_s[j, row_i, :] + a_s[j, row_i, :] * h_in
        cols.append(jnp.concatenate([b_s[j, pl.ds(k * pitch, blk), :] for k in range(SUBLANES)], axis=0))
    return jnp.concatenate(cols, axis=1)


def _mix_kernel(x_ref, xn_ref, mod_ref, modn_ref, g1_ref, win_ref, cw_ref, cb_ref, lng_ref, lnb_ref,
                lcw_ref, lcb_ref, wg_ref, bg_ref, lam_ref, mng_ref, wout_ref, g2_ref,
                wrt_ref, br_ref, anchor_ref,
                x1_ref, h2p_ref, idx_ref, rank_ref, gt_ref, cnt_ref,
                u_a, u_b, vbuf, rbuf, a_s, b_s, hcar, cnt_s,
                *, tile, n_tiles, steps_per_seq, d_conv, d_lru, conv_w, lru_cw, n_exp):
    del anchor_ref
    s = pl.program_id(0)

    @pl.when(s % steps_per_seq == 0)
    def _():
        _slab_store(vbuf, 0, jnp.zeros((CONV_PAD, d_conv), F32))
        _slab_store(rbuf, 0, jnp.zeros((SUBLANES, d_lru), F32))
        hcar[...] = jnp.zeros_like(hcar)

    mod = mod_ref[...]
    modn = modn_ref[...]
    shift1, gate1, shift2 = mod[0:1], mod[2:3], mod[3:4]
    gain1 = g1_ref[...] * (1.0 + mod[1:2])
    gain2 = g2_ref[...] * (1.0 + mod[4:5])
    lam = lam_ref[...]
    softplus_neg_lam = jnp.maximum(-lam, 0.0) + jnp.log1p(jnp.exp(-jnp.abs(lam)))

    def project(src_ref, r0, gain, shift, u_ref):
        h = _rms(src_ref[pl.ds(r0, tile), :]) * gain + shift
        u_ref[...] = jnp.dot(h.astype(BF16), win_ref[...], preferred_element_type=F32)

    def conv_group(r0, u_ref):
        return _mix_conv_group(r0, tile, u_ref, cw_ref, cb_ref, lng_ref, lnb_ref, vbuf, d_conv=d_conv,
                               conv_w=conv_w)

    def finish(r0, u_ref, yc):
        _mix_finish(r0, tile, x_ref, u_ref, yc, gate1, shift2, gain2, softplus_neg_lam,
                    lcw_ref, lcb_ref, wg_ref, bg_ref, mng_ref, wout_ref,
                    wrt_ref, br_ref, x1_ref, h2p_ref, idx_ref, rank_ref, gt_ref, rbuf, a_s, b_s, hcar,
                    cnt_s, d_conv=d_conv, d_lru=d_lru, lru_cw=lru_cw, n_exp=n_exp)

    @pl.when(s == 0)
    def _():
        cnt_s[...] = jnp.zeros_like(cnt_s)
        project(x_ref, 0, gain1, shift1, u_a)

    bufs = (u_a, u_b)
    for q in range(n_tiles):
        cur, nxt = bufs[q % 2], bufs[(q + 1) % 2]
        yc = conv_group(q * tile, cur)
        if q + 1 < n_tiles:
            project(x_ref, (q + 1) * tile, gain1, shift1, nxt)
        else:
            project(xn_ref, 0, g1_ref[...] * (1.0 + modn[1:2]), modn[0:1], nxt)
        finish(q * tile, cur, yc)
    _slab_keep_tail(vbuf, CONV_PAD, n_tiles * tile)
    _slab_keep_tail(rbuf, SUBLANES, n_tiles * tile)
    cnt_ref[...] = cnt_s[:, :LANES]


def _mix_conv_group(r0, ts, u_ref, cw_ref, cb_ref, lng_ref, lnb_ref, vbuf, *, d_conv, conv_w):
    v = u_ref[:, :d_conv] * _sigmoid(u_ref[:, d_conv:2 * d_conv])
    _slab_store(vbuf, CONV_PAD + r0, v)
    acc = _causal_tap_sum(vbuf, cw_ref, cb_ref[...], r0 + CONV_PAD - (conv_w - 1), conv_w, ts)
    mu = jnp.mean(acc, axis=-1, keepdims=True)
    cen = acc - mu
    var = jnp.mean(cen * cen, axis=-1, keepdims=True)
    yc = cen * lax.rsqrt(var + EPS) * lng_ref[...] + lnb_ref[...]
    return yc * _sigmoid(yc)


def _mix_finish(r0, ts, x_ref, u_ref, yc, gate1, shift2, gain2, softplus_neg_lam,
                lcw_ref, lcb_ref, wg_ref, bg_ref, mng_ref, wout_ref,
                wrt_ref, br_ref, x1_ref, h2p_ref, idx_ref, rank_ref, gt_ref, rbuf, a_s, b_s, hcar,
                cnt_s, *, d_conv, d_lru, lru_cw, n_exp):
    rows = pl.ds(r0, ts)
    x = x_ref[rows, :]
    u = u_ref

    u_gate = u[:, 2 * d_conv:2 * d_conv + d_lru]
    _slab_store(rbuf, SUBLANES + r0, u[:, 2 * d_conv + d_lru:])
    xr = _causal_tap_sum(rbuf, lcw_ref, lcb_ref[...], r0 + SUBLANES - (lru_cw - 1), lru_cw, ts)
    gates = jnp.dot(xr.astype(BF16), wg_ref[...], preferred_element_type=F32) + bg_ref[...]
    r = _sigmoid(gates[:, :d_lru])
    i_g = _sigmoid(gates[:, d_lru:])
    log_a = (-LRU_C) * r * softplus_neg_lam
    a = jnp.exp(log_a)
    inp = jnp.sqrt(1.0 - jnp.exp(2.0 * log_a)) * (i_g * xr)

    yl = _gelu_tanh(u_gate) * _linear_recurrence(a, inp, a_s, b_s, hcar)

    mng = mng_ref[...]
    yc_n = _rms(yc) * mng[:, :d_conv]
    yl_n = _rms(yl) * mng[:, d_conv:]
    mixo = jnp.dot(yc_n.astype(BF16), wout_ref[pl.ds(0, d_conv), :], preferred_element_type=F32)
    mixo += jnp.dot(yl_n.astype(BF16), wout_ref[pl.ds(d_conv, d_lru), :], preferred_element_type=F32)
    x1 = x + gate1 * mixo
    x1_ref[rows, :] = x1

    h2 = _rms(x1) * gain2 + shift2
    dh = h2.shape[1] // 2
    h2_hi = h2.astype(BF16)
    _store_planes(h2p_ref, _pack_bf16_pair(h2[:, :dh], h2[:, dh:]), rows)
    h2_lo = (h2 - h2_hi.astype(F32)).astype(BF16)
    nt_dims = (((1,), (1,)), ((), ()))
    wrt = wrt_ref[...]
    lg = lax.dot_general(wrt, h2_hi, nt_dims, preferred_element_type=F32)
    lg2 = lax.dot_general(wrt[:n_exp], h2_lo, nt_dims, preferred_element_type=F32)
    logits = lg[:n_exp] + lg[n_exp:] + lg2 + br_ref[...]

    eidx = lax.broadcasted_iota(I32, (n_exp, ts), 0)
    neg_inf = jnp.float32(-jnp.inf)
    work = logits
    vals, idxs, hots = [], [], []
    for _ in range(TOP_K):
        mval = jnp.max(work, axis=0, keepdims=True)
        midx = jnp.min(jnp.where(work == mval, eidx, n_exp), axis=0, keepdims=True)
        hot = eidx == midx
        vals.append(mval)
        idxs.append(midx)
        hots.append(hot)
        work = jnp.where(hot, neg_inf, work)
    exps = [jnp.exp(vk - vals[0]) for vk in vals]
    denom = exps[0] + exps[1] + exps[2] + exps[3]
    gate_rows = [ek / denom for ek in exps]

    sel = jnp.zeros((n_exp, ts), F32)
    for hot in hots:
        sel = sel + hot.astype(F32)
    tri = (lax.broadcasted_iota(I32, (ts, ts), 0) < lax.broadcasted_iota(I32, (ts, ts), 1)).astype(BF16)
    before = jnp.dot(sel.astype(BF16), tri, preferred_element_type=F32) + cnt_s[...]
    rank_rows = [jnp.sum(jnp.where(hot, before, 0.0), axis=0, keepdims=True) for hot in hots]
    cnt_s[...] = cnt_s[...] + jnp.sum(sel, axis=1, keepdims=True)

    idx_ref[:, rows] = jnp.concatenate(idxs, axis=0)
    rank_ref[:, rows] = jnp.concatenate(rank_rows, axis=0).astype(I32)
    g4 = jnp.concatenate(gate_rows, axis=0)
    g_pad = jnp.concatenate([g4, jnp.zeros((LANES - TOP_K, ts), F32)], axis=0)
    gt_ref[rows, :] = g_pad.T


def _mix_call(x, mod, p, b0, bsz, anchor):
    _, seq, dm = x.shape
    n_tiles = TILES_PER_STEP
    tile = min(SEQ_TILE, seq // n_tiles)
    ts = n_tiles * tile
    nj = seq // ts
    n_steps = bsz * nj
    tiles_per_seq = seq // tile
    d_conv = p["conv_w"].shape[1]
    d_lru = p["lru_conv_w"].shape[1]
    conv_w = p["conv_w"].shape[0]
    lru_cw = p["lru_conv_w"].shape[0]
    n_exp = p["wrt"].shape[0] // 2
    assert seq % ts == 0 and tile % LANES == 0 and tile % CONV_ROWS == 0 and n_tiles % 2 == 0
    assert conv_w - 1 <= CONV_PAD and lru_cw - 1 <= SUBLANES and d_conv % LANES == 0 and d_lru % LANES == 0

    def full(a):
        return pl.BlockSpec(a.shape, lambda s: (0,) * a.ndim)

    def next_tile(s):
        return jnp.minimum(n_tiles * (s + 1), bsz * tiles_per_seq - 1)

    weights = [p["g1"], p["win"], p["conv_w"], p["conv_b"], p["ln_g"], p["ln_b"], p["lru_conv_w"],
               p["lru_conv_b"], p["wg"], p["bg"], p["lam"], p["mng"], p["wout"], p["g2"], p["wrt"], p["br"]]
    kern = functools.partial(_mix_kernel, tile=tile, n_tiles=n_tiles, steps_per_seq=nj, d_conv=d_conv, d_lru=d_lru,
                             conv_w=conv_w, lru_cw=lru_cw, n_exp=n_exp)
    out_shape = (
        jax.ShapeDtypeStruct((bsz, seq, dm), F32),
        jax.ShapeDtypeStruct((SC_SPLIT, bsz * seq, dm // 2 // SC_SPLIT), I32),
        jax.ShapeDtypeStruct((TOP_K, bsz * seq), I32),
        jax.ShapeDtypeStruct((TOP_K, bsz * seq), I32),
        jax.ShapeDtypeStruct((bsz, seq, LANES), F32),
        jax.ShapeDtypeStruct((n_exp, LANES), F32),
    )
    return pl.pallas_call(
        kern,
        grid=(n_steps,),
        in_specs=[pl.BlockSpec((None, ts, dm), lambda s: (s // nj + b0, s % nj, 0)),
                  pl.BlockSpec((None, tile, dm),
                               lambda s: (next_tile(s) // tiles_per_seq + b0, next_tile(s) % tiles_per_seq, 0)),
                  pl.BlockSpec((None, N_MOD, dm), lambda s: (s // nj + b0, 0, 0)),
                  pl.BlockSpec((None, N_MOD, dm), lambda s: (next_tile(s) // tiles_per_seq + b0, 0, 0))]
        + [full(w) for w in weights] + [pl.BlockSpec(memory_space=pl.ANY)],
        out_specs=(
            pl.BlockSpec((None, ts, dm), lambda s: (s // nj, s % nj, 0)),
            pl.BlockSpec((SC_SPLIT, ts, dm // 2 // SC_SPLIT), lambda s: (0, s, 0)),
            pl.BlockSpec((TOP_K, ts), lambda s: (0, s)),
            pl.BlockSpec((TOP_K, ts), lambda s: (0, s)),
            pl.BlockSpec((None, ts, LANES), lambda s: (s // nj, s % nj, 0)),
            pl.BlockSpec((n_exp, LANES), lambda s: (0, 0)),
        ),
        out_shape=out_shape,
        scratch_shapes=[
            pltpu.VMEM((tile, 2 * d_conv + 2 * d_lru), F32),
            pltpu.VMEM((tile, 2 * d_conv + 2 * d_lru), F32),
            pltpu.VMEM((d_conv // LANES, CONV_PAD + ts, LANES), F32),
            pltpu.VMEM((d_lru // LANES, SUBLANES + ts, LANES), F32),
            pltpu.VMEM((d_lru // LANES, tile + SUBLANES, LANES), F32),
            pltpu.VMEM((d_lru // LANES, tile + SUBLANES, LANES), F32),
            pltpu.VMEM((1, d_lru), F32),
            pltpu.VMEM((n_exp, tile), F32),
        ],
        compiler_params=pltpu.CompilerParams(dimension_semantics=("arbitrary",),
                                             vmem_limit_bytes=VMEM_LIMIT_BYTES),
        name="token_mix_route",
    )(x, x, mod, mod, *weights, anchor)


def _sc_mesh():
    return plsc.VectorSubcoreMesh(core_axis_name="core", subcore_axis_name="subcore",
                                  num_cores=SC_CORES, num_subcores=SC_SUBCORES)


def _sc_scatter_rows(rows, idx, n_out, after):
    n, w = rows.shape
    n_k = idx.shape[0]
    assert n % (SC_WINDOW * SC_CORES * SC_SUBCORES) == 0

    @functools.partial(pl.kernel, out_type=jax.ShapeDtypeStruct((n_out, w), rows.dtype), mesh=_sc_mesh(),
                       scratch_types=[], name="moe_dispatch_sc")
    def scatter(x_hbm, i_hbm, after_hbm, o_hbm):
        del after_hbm

        def body(x_vmem, *i_vmems):
            for i_vmem in i_vmems:
                pltpu.sync_copy(x_vmem, o_hbm.at[i_vmem.at[0]])

        pltpu.emit_pipeline(
            body,
            grid=(n // SC_WINDOW,),
            in_specs=[pl.BlockSpec((SC_WINDOW, w), lambda i: (i, 0))]
            + [pl.BlockSpec((1, SC_WINDOW), functools.partial(lambda k, i: (k, i), k)) for k in range(n_k)],
            out_specs=[],
            core_axis_name=("core", "subcore"),
            dimension_semantics=(pltpu.PARALLEL,),
        )(x_hbm, *([i_hbm] * n_k))

    return scatter(rows, idx, after)


def _sc_gather_rows(table, idx):
    m = idx.shape[1]
    w = table.shape[1]
    assert m % (SC_WINDOW * SC_CORES * SC_SUBCORES) == 0

    @functools.partial(pl.kernel, out_type=jax.ShapeDtypeStruct((m, w), table.dtype), mesh=_sc_mesh(),
                       scratch_types=[], name="moe_gather_sc")
    def gather(x_hbm, i_hbm, o_hbm):
        def body(i_vmem, o_vmem):
            pltpu.sync_copy(x_hbm.at[i_vmem.at[0]], o_vmem)

        pltpu.emit_pipeline(
            body,
            grid=(m // SC_WINDOW,),
            in_specs=[pl.BlockSpec((1, SC_WINDOW), lambda i: (0, i))],
            out_specs=[pl.BlockSpec((SC_WINDOW, w), lambda i: (i, 0))],
            core_axis_name=("core", "subcore"),
            dimension_semantics=(pltpu.PARALLEL,),
        )(i_hbm, o_hbm)

    return gather(table, idx)


def _sc_pack_rows_bf16(w):
    r, c = w.shape
    assert r % SC_PACK_ROWS == 0 and c % SC_PACK_COLS == 0 and (r // SC_PACK_ROWS) % (SC_CORES * SC_SUBCORES) == 0
    lanes = SC_LANES

    @functools.partial(pl.kernel, out_type=jax.ShapeDtypeStruct((r // 2, c), I32), mesh=_sc_mesh(),
                       scratch_types=[], compiler_params=pltpu.CompilerParams(needs_layout_passes=False),
                       name="expert_weight_pack_sc")
    def pack(x_hbm, o_hbm):
        def body(x_vmem, o_vmem):
            @pl.loop(0, SC_PACK_ROWS // 2)
            def _(i):
                for j in range(0, SC_PACK_COLS, lanes):
                    pair = plsc.pack(x_vmem[2 * i, pl.ds(j, lanes)], x_vmem[2 * i + 1, pl.ds(j, lanes)],
                                     format=plsc.PackFormat.INTERLEAVED)
                    o_vmem[i, pl.ds(j, lanes)] = plsc.bitcast(pair, I32)

        pltpu.emit_pipeline(
            body,
            grid=(r // SC_PACK_ROWS, c // SC_PACK_COLS),
            in_specs=[pl.BlockSpec((SC_PACK_ROWS, SC_PACK_COLS), lambda i, j: (i, j))],
            out_specs=[pl.BlockSpec((SC_PACK_ROWS // 2, SC_PACK_COLS), lambda i, j: (i, j))],
            core_axis_name=("core", "subcore"),
            dimension_semantics=(pltpu.PARALLEL, pltpu.PARALLEL),
        )(x_hbm, o_hbm)

    return pack(w)


def _moe_kernel(be_ref, nb_ref, slot_ref, nxt_ref, nv_ref, xs_ref, wgu_hbm, bgu_ref, wd_hbm, bd_ref, ys_ref,
                wgu_s, wd_s, sems, *, d_ff):
    i = pl.program_id(0)
    live = i < nb_ref[0]
    new_expert = (i == 0) | (be_ref[i] != be_ref[jnp.maximum(i - 1, 0)])
    slot = slot_ref[i]

    def weight_copies(e, s):
        return (pltpu.make_async_copy(wgu_hbm.at[e], wgu_s.at[s], sems.at[0, s]),
                pltpu.make_async_copy(wd_hbm.at[e], wd_s.at[s], sems.at[1, s]))

    @pl.when(live & new_expert)
    def _():
        e, nx = be_ref[i], nxt_ref[i]

        @pl.when(i == 0)
        def _():
            for cp in weight_copies(e, slot):
                cp.start()

        for cp in weight_copies(e, slot):
            cp.wait()

        @pl.when(nx >= 0)
        def _():
            for cp in weight_copies(nx, 1 - slot):
                cp.start()

    def sub_block(s):
        rows = pl.ds(s * MOE_SUB, MOE_SUB)
        packed = jnp.concatenate([xs_ref[p, rows, :] for p in range(SC_SPLIT)], axis=1)
        x_lo, x_hi = _unpack_bf16_pair(packed)
        dh = x_lo.shape[1]
        w_top = pltpu.bitcast(wgu_s[slot, pl.ds(0, dh // 2), :], BF16)
        w_bot = pltpu.bitcast(wgu_s[slot, pl.ds(dh // 2, dh // 2), :], BF16)
        gu = jnp.dot(x_lo.astype(BF16), w_top, preferred_element_type=F32)
        gu += jnp.dot(x_hi.astype(BF16), w_bot, preferred_element_type=F32)
        gu += bgu_ref[...]
        g = jnp.minimum(gu[:, :d_ff], SWIGLU_LIMIT)
        u = jnp.clip(gu[:, d_ff:], -SWIGLU_LIMIT, SWIGLU_LIMIT)
        act = (u + 1.0) * (g * _sigmoid(SWIGLU_ALPHA * g))
        y = jnp.dot(act.astype(BF16), pltpu.bitcast(wd_s[slot], BF16), preferred_element_type=F32) + bd_ref[...]
        dm2 = y.shape[1] // 2
        _store_planes(ys_ref, _pack_bf16_pair(y[:, :dm2], y[:, dm2:]), rows)

    n_sub = xs_ref.shape[1] // MOE_SUB
    n_live_sub = (nv_ref[i] + (MOE_SUB - 1)) // MOE_SUB

    for k in range(1, n_sub + 1):
        @pl.when(live & (n_live_sub == k))
        def _():
            for s in range(k):
                sub_block(s)


def _moe_call(xs, block_expert, n_used, seg_slot, next_expert, block_valid, wgu, bgu, wd, bd):
    _, n_rows, wp = xs.shape
    n_exp, dm_half, d_ff2 = wgu.shape
    dm = 2 * dm_half
    d_ff = d_ff2 // 2
    bm = MOE_BLOCK
    n_blocks = n_rows // bm

    def row_map(i, be, nb, slot, nxt, nv):
        return (0, jnp.minimum(i, nb[0] - 1), 0)

    def b_map(i, be, nb, slot, nxt, nv):
        return (be[jnp.minimum(i, nb[0] - 1)], 0, 0)

    grid_spec = pltpu.PrefetchScalarGridSpec(
        num_scalar_prefetch=5,
        grid=(n_blocks,),
        in_specs=[
            pl.BlockSpec((SC_SPLIT, bm, wp), row_map),
            pl.BlockSpec(memory_space=pl.ANY),
            pl.BlockSpec((None, 1, d_ff2), b_map),
            pl.BlockSpec(memory_space=pl.ANY),
            pl.BlockSpec((None, 1, dm), b_map),
        ],
        out_specs=pl.BlockSpec((SC_SPLIT, bm, wp), row_map),
        scratch_shapes=[
            pltpu.VMEM((2,) + wgu.shape[1:], I32), pltpu.VMEM((2,) + wd.shape[1:], I32),
            pltpu.SemaphoreType.DMA((2, 2)),
        ],
    )
    return pl.pallas_call(
        functools.partial(_moe_kernel, d_ff=d_ff),
        grid_spec=grid_spec,
        out_shape=jax.ShapeDtypeStruct(xs.shape, I32),
        compiler_params=pltpu.CompilerParams(dimension_semantics=("arbitrary",),
                                             vmem_limit_bytes=VMEM_LIMIT_BYTES),
        name="moe_experts",
    )(block_expert, n_used, seg_slot, next_expert, block_valid, xs, wgu, bgu, wd, bd)


def _combine_kernel(x1_ref, gt_ref, mod_ref, gf_ref, yg_ref, *out_refs):
    o_ref = out_refs[-1]
    gt = gt_ref[...]
    ts = yg_ref.shape[2]
    dh = yg_ref.shape[3] * SC_SPLIT
    acc_lo = jnp.zeros((ts, dh), F32)
    acc_hi = jnp.zeros((ts, dh), F32)
    for k in range(TOP_K):
        lo, hi = _unpack_bf16_pair(_load_planes(yg_ref.at[k]))
        gk = gt[:, k:k + 1]
        acc_lo += gk * lo
        acc_hi += gk * hi
    y = jnp.concatenate([acc_lo, acc_hi], axis=1)
    gate2 = mod_ref[...][5:6]
    x2 = x1_ref[...] + gate2 * y
    o_ref[...] = _rms(x2) * gf_ref[...]


def _combine_call(x1, gt, mod, gf, yg, out_prev, b0):
    bsz, seq, dm = x1.shape
    ts = min(COMBINE_TILE, seq)
    assert seq % ts == 0
    nj = seq // ts
    in_specs = [
        pl.BlockSpec((None, ts, dm), lambda b, j: (b, j, 0)),
        pl.BlockSpec((None, ts, LANES), lambda b, j: (b, j, 0)),
        pl.BlockSpec((None, N_MOD, dm), lambda b, j: (b + b0, 0, 0)),
        pl.BlockSpec((1, dm), lambda b, j: (0, 0)),
        pl.BlockSpec((TOP_K, SC_SPLIT, ts, yg.shape[3]), lambda b, j: (0, 0, b * nj + j, 0)),
    ]
    args = [x1, gt, mod, gf, yg]
    aliases = {}
    if out_prev is not None:
        in_specs.append(pl.BlockSpec(memory_space=pl.ANY))
        args.append(out_prev)
        aliases = {len(args) - 1: 0}
    return pl.pallas_call(
        _combine_kernel,
        grid=(bsz, nj),
        in_specs=in_specs,
        out_specs=pl.BlockSpec((None, ts, dm), lambda b, j: (b + b0, j, 0)),
        out_shape=jax.ShapeDtypeStruct((mod.shape[0], seq, dm), F32),
        input_output_aliases=aliases,
        compiler_params=pltpu.CompilerParams(dimension_semantics=("arbitrary", "arbitrary"),
                                             vmem_limit_bytes=VMEM_LIMIT_BYTES),
        name="moe_combine",
    )(*args)


def _block_diag(w):
    n_h, d, _ = w.shape
    eye = jnp.eye(n_h, dtype=w.dtype)
    return (eye[:, None, :, None] * w[:, :, None, :]).reshape(n_h * d, n_h * d)


def _layer(x, mod, l, w_in, norm1_g, conv_w, conv_b, conv_ln_g, conv_ln_b, lru_conv_w, lru_conv_b,
           lru_w_a, lru_b_a, lru_w_x, lru_b_x, lru_lambda, mix_norm_g, w_out, norm2_g, w_router,
           b_router, w_gate_up, b_gate_up, w_down, b_down, out_gain):
    bsz, seq, dm = x.shape
    n_tok = bsz * seq
    n_exp = w_router.shape[-1]
    row = lambda a: a.reshape(1, -1)
    wr = w_router[l]
    wr_hi = wr.astype(BF16)
    wr_lo = (wr - wr_hi.astype(F32)).astype(BF16)
    params = dict(
        g1=row(norm1_g[l]), win=w_in[l].astype(BF16), conv_w=conv_w[l], conv_b=row(conv_b[l]),
        ln_g=row(conv_ln_g[l]), ln_b=row(conv_ln_b[l]), lru_conv_w=lru_conv_w[l], lru_conv_b=row(lru_conv_b[l]),
        wg=jnp.concatenate([_block_diag(lru_w_a[l]), _block_diag(lru_w_x[l])], axis=1).astype(BF16),
        bg=jnp.concatenate([lru_b_a[l].reshape(1, -1), lru_b_x[l].reshape(1, -1)], axis=1),
        lam=row(lru_lambda[l]), mng=row(mix_norm_g[l]), wout=w_out[l].astype(BF16), g2=row(norm2_g[l]),
        wrt=jnp.concatenate([wr_hi.T, wr_lo.T], axis=0), br=b_router[l].reshape(n_exp, 1),
    )
    _, d_in, d_ff2 = w_gate_up[l].shape
    _, d_ff, d_out = w_down[l].shape
    wgu_p = _sc_pack_rows_bf16(w_gate_up[l].reshape(n_exp * d_in, d_ff2)).reshape(n_exp, d_in // 2, d_ff2)
    wd_p = _sc_pack_rows_bf16(w_down[l].reshape(n_exp * d_ff, d_out)).reshape(n_exp, d_ff // 2, d_out)
    sizes = [s for s in CHUNK_BATCHES if s > 0] if sum(CHUNK_BATCHES) == bsz else [bsz]
    routed, b0, anchor = [], 0, mod
    for cb in sizes:
        r = _route_chunk(x, mod, params, b0, cb, anchor, n_exp)
        routed.append(r)
        anchor = r["dest_sub"]
        b0 += cb
    out = None
    sc_order = [wgu_p] + [wd_p] * (len(routed) - 1)
    for r, after in zip(routed, sc_order):
        out = _experts_chunk(r, mod, out, wgu_p, b_gate_up[l][:, None, :], wd_p, b_down[l][:, None, :], out_gain,
                             after)
    return out


def _route_chunk(x, mod, params, b0, cb, anchor, n_exp):
    seq, dm = x.shape[1:]
    n_tok = cb * seq
    x1, h2p, top_idx, rank, gt, cnt = _mix_call(x, mod, params, b0, cb, anchor)

    bm = MOE_BLOCK
    counts = cnt[:, 0].astype(I32)
    padded = (counts + bm - 1) // bm * bm
    e_ids = jnp.arange(n_exp, dtype=I32)
    pad_ends = jnp.sum(jnp.where(e_ids[None, :] <= e_ids[:, None], padded[None, :], 0), axis=1)
    pad_starts = pad_ends - padded
    n_rows = (n_tok * TOP_K // bm + n_exp) * bm
    n_blocks = n_rows // bm
    dest = rank + jnp.sum(jnp.where(top_idx[..., None] == e_ids, pad_starts, 0), axis=-1)
    block_start = jnp.arange(n_blocks, dtype=I32) * bm
    block_expert = jnp.minimum(
        jnp.sum((block_start[:, None] >= pad_ends[None, :]).astype(I32), axis=1), n_exp - 1)
    n_used = pad_ends[-1:] // bm
    present = counts > 0
    seg_ordinal = jnp.sum(jnp.where((e_ids[None, :] < e_ids[:, None]) & present[None, :], 1, 0), axis=1)
    following = jnp.min(jnp.where((e_ids[None, :] > e_ids[:, None]) & present[None, :], e_ids[None, :], n_exp),
                        axis=1)
    following = jnp.where(following == n_exp, -1, following)
    block_hot = block_expert[:, None] == e_ids[None, :]
    seg_slot = jnp.sum(jnp.where(block_hot, seg_ordinal[None, :] % 2, 0), axis=1)
    next_expert = jnp.sum(jnp.where(block_hot, following[None, :], 0), axis=1)
    block_valid = jnp.clip(
        jnp.sum(jnp.where(block_hot, (pad_starts + counts)[None, :], 0), axis=1) - block_start, 0, bm)

    wp = h2p.shape[2]
    plane = jnp.arange(SC_SPLIT, dtype=I32)[None, :, None] * n_rows
    dest_sub = dest[:, None, :] + plane
    return dict(b0=b0, n_tok=n_tok, n_rows=n_rows, x1=x1, h2p=h2p, gt=gt, dest_sub=dest_sub,
                tables=(block_expert, n_used, seg_slot, next_expert, block_valid))


def _experts_chunk(r, mod, out_prev, wgu, bgu, wd, bd, out_gain, dispatch_after):
    n_tok, n_rows, dest_sub = r["n_tok"], r["n_rows"], r["dest_sub"]
    wp = r["h2p"].shape[2]
    xs = _sc_scatter_rows(r["h2p"].reshape(SC_SPLIT * n_tok, wp), dest_sub.reshape(TOP_K, SC_SPLIT * n_tok),
                          SC_SPLIT * n_rows, dispatch_after).reshape(SC_SPLIT, n_rows, wp)
    ys = _moe_call(xs, *r["tables"], wgu, bgu, wd, bd)
    yg = _sc_gather_rows(ys.reshape(SC_SPLIT * n_rows, wp), dest_sub.reshape(1, TOP_K * SC_SPLIT * n_tok))
    return _combine_call(r["x1"], r["gt"], mod, out_gain, yg.reshape(TOP_K, SC_SPLIT, n_tok, wp), out_prev,
                         r["b0"])


def kernel(x, c, w_ada, b_ada, norm1_g, w_in, conv_w, conv_b, conv_ln_g, conv_ln_b, lru_conv_w, lru_conv_b,
           lru_w_a, lru_b_a, lru_w_x, lru_b_x, lru_lambda, mix_norm_g, w_out, norm2_g, w_router, b_router,
           w_gate_up, b_gate_up, w_down, b_down, final_norm_g):
    depth = w_ada.shape[0]
    assert depth == 1, "the final norm is fused into the (single) layer's combine kernel"
    bsz, seq, dm = x.shape
    mod = _ada_call(c, w_ada[0], b_ada[0]).reshape(bsz, N_MOD, dm)
    return _layer(x, mod, 0, w_in, norm1_g, conv_w, conv_b, conv_ln_g, conv_ln_b, lru_conv_w, lru_conv_b,
                  lru_w_a, lru_b_a, lru_w_x, lru_b_x, lru_lambda, mix_norm_g, w_out, norm2_g, w_router,
                  b_router, w_gate_up, b_gate_up, w_down, b_down, row_gain(final_norm_g))


def row_gain(g):
    return g.reshape(1, -1)
```

```python
import functools

import jax
import jax.numpy as jnp
from jax import lax
from jax.experimental import pallas as pl
from jax.experimental.pallas import tpu as pltpu
from jax.experimental.pallas import tpu_sc as plsc

F32 = jnp.float32
BF16 = jnp.bfloat16
I32 = jnp.int32

EPS = 1e-6
N_MOD = 6
LRU_C = 8.0
TOP_K = 4
SWIGLU_ALPHA = 1.702
SWIGLU_LIMIT = 7.0

LANES = 128
SUBLANES = 8
VMEM_LIMIT_BYTES = 56 * 1024 * 1024

SEQ_TILE = 512
TILES_PER_STEP = 2
COMBINE_TILE = 512
CONV_PAD = 32
CONV_ROWS = 128
MOE_BLOCK = 1024
MOE_SUB = 256
CHUNK_BATCHES = (4, 3, 1)
HI_MASK = -65536

SC_CORES = 2
SC_SUBCORES = 16
SC_LANES = 16
SC_PACK_ROWS = 16
SC_PACK_COLS = 512
SC_WINDOW = 128
SC_SPLIT = 2


def _sigmoid(x):
    return 0.5 * jnp.tanh(0.5 * x) + 0.5


def _pack_bf16_pair(lo_f32, hi_f32):
    lo_bits = lax.bitcast_convert_type(lo_f32.astype(BF16).astype(F32), I32)
    hi_bits = lax.bitcast_convert_type(hi_f32.astype(BF16).astype(F32), I32)
    return lax.shift_right_logical(lo_bits, 16) | hi_bits


def _unpack_bf16_pair(p):
    lo = lax.bitcast_convert_type(lax.shift_left(p, 16), F32)
    hi = lax.bitcast_convert_type(p & HI_MASK, F32)
    return lo, hi


def _store_planes(ref, packed, rows=slice(None)):
    wp = packed.shape[1] // SC_SPLIT
    for s in range(SC_SPLIT):
        ref[s, rows, :] = packed[:, s * wp:(s + 1) * wp]


def _load_planes(ref):
    return jnp.concatenate([ref[s] for s in range(SC_SPLIT)], axis=1)


def _ada_kernel(c_ref, w_ref, b_ref, o_ref):
    c = c_ref[...]
    ca = c * _sigmoid(c)
    w = w_ref[...]
    c_hi = ca.astype(BF16)
    c_lo = (ca - c_hi.astype(F32)).astype(BF16)
    w_hi = w.astype(BF16)
    w_lo = (w - w_hi.astype(F32)).astype(BF16)
    acc = jnp.dot(c_hi, w_hi, preferred_element_type=F32)
    acc += jnp.dot(c_lo, w_hi, preferred_element_type=F32)
    acc += jnp.dot(c_hi, w_lo, preferred_element_type=F32)
    o_ref[...] = acc + b_ref[...]


def _ada_call(c, w_ada, b_ada):
    bsz, dm = c.shape
    n_out = w_ada.shape[1]
    tn = 1024
    return pl.pallas_call(
        _ada_kernel,
        grid=(n_out // tn,),
        in_specs=[
            pl.BlockSpec((bsz, dm), lambda n: (0, 0)),
            pl.BlockSpec((dm, tn), lambda n: (0, n)),
            pl.BlockSpec((1, tn), lambda n: (0, n)),
        ],
        out_specs=pl.BlockSpec((bsz, tn), lambda n: (0, n)),
        out_shape=jax.ShapeDtypeStruct((bsz, n_out), F32),
        compiler_params=pltpu.CompilerParams(dimension_semantics=("arbitrary",)),
        name="ada_mod",
    )(c, w_ada, b_ada.reshape(1, n_out))


def _rms(x, eps=EPS):
    return x * lax.rsqrt(jnp.mean(x * x, axis=-1, keepdims=True) + eps)


def _gelu_tanh(x):
    return 0.5 * x * (1.0 + jnp.tanh(0.7978845608028654 * (x + 0.044715 * (x * x * x))))


def _slab_store(buf, row0, val):
    for s in range(buf.shape[0]):
        buf[s, pl.ds(row0, val.shape[0]), :] = val[:, s * LANES:(s + 1) * LANES]


def _slab_keep_tail(buf, keep, ts):
    for s in range(buf.shape[0]):
        buf[s, pl.ds(0, keep), :] = buf[s, pl.ds(ts, keep), :]


def _causal_tap_sum(buf, w_ref, bias_row, first, n_taps, ts):
    w = w_ref[...]
    cols = []
    for s in range(buf.shape[0]):
        lanes = slice(s * LANES, (s + 1) * LANES)
        chunks = []
        for c in range(0, ts, CONV_ROWS):
            acc = jnp.broadcast_to(bias_row[:, lanes], (CONV_ROWS, LANES))
            for k in range(n_taps):
                acc = acc + w[k:k + 1, lanes] * buf[s, pl.ds(c + first + k, CONV_ROWS), :]
            chunks.append(acc)
        cols.append(jnp.concatenate(chunks, axis=0))
    return jnp.concatenate(cols, axis=1)


def _linear_recurrence(a, b, a_s, b_s, hcar):
    ts = a.shape[0]
    blk = ts // SUBLANES
    pitch = blk + 1
    cols = []
    for j in range(a_s.shape[0]):
        lanes = slice(j * LANES, (j + 1) * LANES)
        for k in range(SUBLANES):
            a_s[j, pl.ds(k * pitch, blk), :] = a[k * blk:(k + 1) * blk, lanes]
            b_s[j, pl.ds(k * pitch, blk), :] = b[k * blk:(k + 1) * blk, lanes]
        h = jnp.zeros((SUBLANES, LANES), F32)
        p = jnp.ones((SUBLANES, LANES), F32)
        for i in range(blk):
            row_i = pl.ds(i, SUBLANES, stride=pitch)
            a_i = a_s[j, row_i, :]
            h = a_i * h + b_s[j, row_i, :]
            p = a_i * p
            b_s[j, row_i, :] = h
            a_s[j, row_i, :] = p
        state = hcar[:, lanes]
        entering = []
        for k in range(SUBLANES):
            entering.append(state)
            state = p[k:k + 1, :] * state + h[k:k + 1, :]
        hcar[:, lanes] = state
        h_in = jnp.concatenate(entering, axis=0)
        for i in range(blk):
            row_i = pl.ds(i, SUBLANES, stride=pitch)
            b_s[j, row_i, :] = b_s[j, row_i, :] + a_s[j, row_i, :] * h_in
        cols.append(jnp.concatenate([b_s[j, pl.ds(k * pitch, blk), :] for k in range(SUBLANES)], axis=0))
    return jnp.concatenate(cols, axis=1)


def _mix_kernel(x_ref, xn_ref, mod_ref, modn_ref, g1_ref, win_ref, cw_ref, cb_ref, lng_ref, lnb_ref,
                lcw_ref, lcb_ref, wg_ref, bg_ref, lam_ref, mng_ref, wout_ref, g2_ref,
                wrt_ref, br_ref, anchor_ref,
                x1_ref, h2p_ref, idx_ref, rank_ref, gt_ref, cnt_ref,
                u_a, u_b, vbuf, rbuf, a_s, b_s, hcar, cnt_s,
                *, tile, n_tiles, steps_per_seq, d_conv, d_lru, conv_w, lru_cw, n_exp):
    del anchor_ref
    s = pl.program_id(0)

    @pl.when(s % steps_per_seq == 0)
    def _():
        _slab_store(vbuf, 0, jnp.zeros((CONV_PAD, d_conv), F32))
        _slab_store(rbuf, 0, jnp.zeros((SUBLANES, d_lru), F32))
        hcar[...] = jnp.zeros_like(hcar)

    mod = mod_ref[...]
    modn = modn_ref[...]
    shift1, gate1, shift2 = mod[0:1], mod[2:3], mod[3:4]
    gain1 = g1_ref[...] * (1.0 + mod[1:2])
    gain2 = g2_ref[...] * (1.0 + mod[4:5])
    lam = lam_ref[...]
    softplus_neg_lam = jnp.maximum(-lam, 0.0) + jnp.log1p(jnp.exp(-jnp.abs(lam)))

    def project(src_ref, r0, gain, shift, u_ref):
        h = _rms(src_ref[pl.ds(r0, tile), :]) * gain + shift
        u_ref[...] = jnp.dot(h.astype(BF16), win_ref[...], preferred_element_type=F32)

    def conv_group(r0, u_ref):
        return _mix_conv_group(r0, tile, u_ref, cw_ref, cb_ref, lng_ref, lnb_ref, vbuf, d_conv=d_conv,
                               conv_w=conv_w)

    def finish(r0, u_ref, yc):
        _mix_finish(r0, tile, x_ref, u_ref, yc, gate1, shift2, gain2, softplus_neg_lam,
                    lcw_ref, lcb_ref, wg_ref, bg_ref, mng_ref, wout_ref,
                    wrt_ref, br_ref, x1_ref, h2p_ref, idx_ref, rank_ref, gt_ref, rbuf, a_s, b_s, hcar,
                    cnt_s, d_conv=d_conv, d_lru=d_lru, lru_cw=lru_cw, n_exp=n_exp)

    @pl.when(s == 0)
    def _():
        cnt_s[...] = jnp.zeros_like(cnt_s)
        project(x_ref, 0, gain1, shift1, u_a)

    bufs = (u_a, u_b)
    for q in range(n_tiles):
        cur, nxt = bufs[q % 2], bufs[(q + 1) % 2]
        yc = conv_group(q * tile, cur)
        if q + 1 < n_tiles:
            project(x_ref, (q + 1) * tile, gain1, shift1, nxt)
        else:
            project(xn_ref, 0, g1_ref[...] * (1.0 + modn[1:2]), modn[0:1], nxt)
        finish(q * tile, cur, yc)
    _slab_keep_tail(vbuf, CONV_PAD, n_tiles * tile)
    _slab_keep_tail(rbuf, SUBLANES, n_tiles * tile)
    cnt_ref[...] = cnt_s[:, :LANES]


def _mix_conv_group(r0, ts, u_ref, cw_ref, cb_ref, lng_ref, lnb_ref, vbuf, *, d_conv, conv_w):
    v = u_ref[:, :d_conv] * _sigmoid(u_ref[:, d_conv:2 * d_conv])
    _slab_store(vbuf, CONV_PAD + r0, v)
    acc = _causal_tap_sum(vbuf, cw_ref, cb_ref[...], r0 + CONV_PAD - (conv_w - 1), conv_w, ts)
    mu = jnp.mean(acc, axis=-1, keepdims=True)
    cen = acc - mu
    var = jnp.mean(cen * cen, axis=-1, keepdims=True)
    yc = cen * lax.rsqrt(var + EPS) * lng_ref[...] + lnb_ref[...]
    return yc * _sigmoid(yc)


def _mix_finish(r0, ts, x_ref, u_ref, yc, gate1, shift2, gain2, softplus_neg_lam,
                lcw_ref, lcb_ref, wg_ref, bg_ref, mng_ref, wout_ref,
                wrt_ref, br_ref, x1_ref, h2p_ref, idx_ref, rank_ref, gt_ref, rbuf, a_s, b_s, hcar,
                cnt_s, *, d_conv, d_lru, lru_cw, n_exp):
    rows = pl.ds(r0, ts)
    x = x_ref[rows, :]
    u = u_ref

    u_gate = u[:, 2 * d_conv:2 * d_conv + d_lru]
    _slab_store(rbuf, SUBLANES + r0, u[:, 2 * d_conv + d_lru:])
    xr = _causal_tap_sum(rbuf, lcw_ref, lcb_ref[...], r0 + SUBLANES - (lru_cw - 1), lru_cw, ts)
    gates = jnp.dot(xr.astype(BF16), wg_ref[...], preferred_element_type=F32) + bg_ref[...]
    r = _sigmoid(gates[:, :d_lru])
    i_g = _sigmoid(gates[:, d_lru:])
    log_a = (-LRU_C) * r * softplus_neg_lam
    a = jnp.exp(log_a)
    inp = jnp.sqrt(1.0 - jnp.exp(2.0 * log_a)) * (i_g * xr)

    yl = _gelu_tanh(u_gate) * _linear_recurrence(a, inp, a_s, b_s, hcar)

    mng = mng_ref[...]
    yc_n = _rms(yc) * mng[:, :d_conv]
    yl_n = _rms(yl) * mng[:, d_conv:]
    mixo = jnp.dot(yc_n.astype(BF16), wout_ref[pl.ds(0, d_conv), :], preferred_element_type=F32)
    mixo += jnp.dot(yl_n.astype(BF16), wout_ref[pl.ds(d_conv, d_lru), :], preferred_element_type=F32)
    x1 = x + gate1 * mixo
    x1_ref[rows, :] = x1

    h2 = _rms(x1) * gain2 + shift2
    dh = h2.shape[1] // 2
    h2_hi = h2.astype(BF16)
    _store_planes(h2p_ref, _pack_bf16_pair(h2[:, :dh], h2[:, dh:]), rows)
    h2_lo = (h2 - h2_hi.astype(F32)).astype(BF16)
    nt_dims = (((1,), (1,)), ((), ()))
    wrt = wrt_ref[...]
    lg = lax.dot_general(wrt, h2_hi, nt_dims, preferred_element_type=F32)
    lg2 = lax.dot_general(wrt[:n_exp], h2_lo, nt_dims, preferred_element_type=F32)
    logits = lg[:n_exp] + lg[n_exp:] + lg2 + br_ref[...]

    eidx = lax.broadcasted_iota(I32, (n_exp, ts), 0)
    neg_inf = jnp.float32(-jnp.inf)
    work = logits
    vals, idxs, hots = [], [], []
    for _ in range(TOP_K):
        mval = jnp.max(work, axis=0, keepdims=True)
        midx = jnp.min(jnp.where(work == mval, eidx, n_exp), axis=0, keepdims=True)
        hot = eidx == midx
        vals.append(mval)
        idxs.append(midx)
        hots.append(hot)
        work = jnp.where(hot, neg_inf, work)
    exps = [jnp.exp(vk - vals[0]) for vk in vals]
    denom = exps[0] + exps[1] + exps[2] + exps[3]
    gate_rows = [ek / denom for ek in exps]

    sel = jnp.zeros((n_exp, ts), F32)
    for hot in hots:
        sel = sel + hot.astype(F32)
    tri = (lax.broadcasted_iota(I32, (ts, ts), 0) < lax.broadcasted_iota(I32, (ts, ts), 1)).astype(BF16)
    before = jnp.dot(sel.astype(BF16), tri, preferred_element_type=F32) + cnt_s[...]
    rank_rows = [jnp.sum(jnp.where(hot, before, 0.0), axis=0, keepdims=True) for hot in hots]
    cnt_s[...] = cnt_s[...] + jnp.sum(sel, axis=1, keepdims=True)

    idx_ref[:, rows] = jnp.concatenate(idxs, axis=0)
    rank_ref[:, rows] = jnp.concatenate(rank_rows, axis=0).astype(I32)
    g4 = jnp.concatenate(gate_rows, axis=0)
    g_pad = jnp.concatenate([g4, jnp.zeros((LANES - TOP_K, ts), F32)], axis=0)
    gt_ref[rows, :] = g_pad.T


def _mix_call(x, mod, p, b0, bsz, anchor):
    _, seq, dm = x.shape
    n_tiles = TILES_PER_STEP
    tile = min(SEQ_TILE, seq // n_tiles)
    ts = n_tiles * tile
    nj = seq // ts
    n_steps = bsz * nj
    tiles_per_seq = seq // tile
    d_conv = p["conv_w"].shape[1]
    d_lru = p["lru_conv_w"].shape[1]
    conv_w = p["conv_w"].shape[0]
    lru_cw = p["lru_conv_w"].shape[0]
    n_exp = p["wrt"].shape[0] // 2
    assert seq % ts == 0 and tile % LANES == 0 and tile % CONV_ROWS == 0 and n_tiles % 2 == 0
    assert conv_w - 1 <= CONV_PAD and lru_cw - 1 <= SUBLANES and d_conv % LANES == 0 and d_lru % LANES == 0

    def full(a):
        return pl.BlockSpec(a.shape, lambda s: (0,) * a.ndim)

    def next_tile(s):
        return jnp.minimum(n_tiles * (s + 1), bsz * tiles_per_seq - 1)

    weights = [p["g1"], p["win"], p["conv_w"], p["conv_b"], p["ln_g"], p["ln_b"], p["lru_conv_w"],
               p["lru_conv_b"], p["wg"], p["bg"], p["lam"], p["mng"], p["wout"], p["g2"], p["wrt"], p["br"]]
    kern = functools.partial(_mix_kernel, tile=tile, n_tiles=n_tiles, steps_per_seq=nj, d_conv=d_conv, d_lru=d_lru,
                             conv_w=conv_w, lru_cw=lru_cw, n_exp=n_exp)
    out_shape = (
        jax.ShapeDtypeStruct((bsz, seq, dm), F32),
        jax.ShapeDtypeStruct((SC_SPLIT, bsz * seq, dm // 2 // SC_SPLIT), I32),
        jax.ShapeDtypeStruct((TOP_K, bsz * seq), I32),
        jax.ShapeDtypeStruct((TOP_K, bsz * seq), I32),
        jax.ShapeDtypeStruct((bsz, seq, LANES), F32),
        jax.ShapeDtypeStruct((n_exp, LANES), F32),
    )
    return pl.pallas_call(
        kern,
        grid=(n_steps,),
        in_specs=[pl.BlockSpec((None, ts, dm), lambda s: (s // nj + b0, s % nj, 0)),
                  pl.BlockSpec((None, tile, dm),
                               lambda s: (next_tile(s) // tiles_per_seq + b0, next_tile(s) % tiles_per_seq, 0)),
                  pl.BlockSpec((None, N_MOD, dm), lambda s: (s // nj + b0, 0, 0)),
                  pl.BlockSpec((None, N_MOD, dm), lambda s: (next_tile(s) // tiles_per_seq + b0, 0, 0))]
        + [full(w) for w in weights] + [pl.BlockSpec(memory_space=pl.ANY)],
        out_specs=(
            pl.BlockSpec((None, ts, dm), lambda s: (s // nj, s % nj, 0)),
            pl.BlockSpec((SC_SPLIT, ts, dm // 2 // SC_SPLIT), lambda s: (0, s, 0)),
            pl.BlockSpec((TOP_K, ts), lambda s: (0, s)),
            pl.BlockSpec((TOP_K, ts), lambda s: (0, s)),
            pl.BlockSpec((None, ts, LANES), lambda s: (s // nj, s % nj, 0)),
            pl.BlockSpec((n_exp, LANES), lambda s: (0, 0)),
        ),
        out_shape=out_shape,
        scratch_shapes=[
            pltpu.VMEM((tile, 2 * d_conv + 2 * d_lru), F32),
            pltpu.VMEM((tile, 2 * d_conv + 2 * d_lru), F32),
            pltpu.VMEM((d_conv // LANES, CONV_PAD + ts, LANES), F32),
            pltpu.VMEM((d_lru // LANES, SUBLANES + ts, LANES), F32),
            pltpu.VMEM((d_lru // LANES, tile + SUBLANES, LANES), F32),
            pltpu.VMEM((d_lru // LANES, tile + SUBLANES, LANES), F32),
            pltpu.VMEM((1, d_lru), F32),
            pltpu.VMEM((n_exp, tile), F32),
        ],
        compiler_params=pltpu.CompilerParams(dimension_semantics=("arbitrary",),
                                             vmem_limit_bytes=VMEM_LIMIT_BYTES),
        name="token_mix_route",
    )(x, x, mod, mod, *weights, anchor)


def _sc_mesh():
    return plsc.VectorSubcoreMesh(core_axis_name="core", subcore_axis_name="subcore",
                                  num_cores=SC_CORES, num_subcores=SC_SUBCORES)


def _sc_scatter_rows(rows, idx, n_out, after):
    n, w = rows.shape
    n_k = idx.shape[0]
    assert n % (SC_WINDOW * SC_CORES * SC_SUBCORES) == 0

    @functools.partial(pl.kernel, out_type=jax.ShapeDtypeStruct((n_out, w), rows.dtype), mesh=_sc_mesh(),
                       scratch_types=[], name="moe_dispatch_sc")
    def scatter(x_hbm, i_hbm, after_hbm, o_hbm):
        del after_hbm

        def body(x_vmem, *i_vmems):
            for i_vmem in i_vmems:
                pltpu.sync_copy(x_vmem, o_hbm.at[i_vmem.at[0]])

        pltpu.emit_pipeline(
            body,
            grid=(n // SC_WINDOW,),
            in_specs=[pl.BlockSpec((SC_WINDOW, w), lambda i: (i, 0))]
            + [pl.BlockSpec((1, SC_WINDOW), functools.partial(lambda k, i: (k, i), k)) for k in range(n_k)],
            out_specs=[],
            core_axis_name=("core", "subcore"),
            dimension_semantics=(pltpu.PARALLEL,),
        )(x_hbm, *([i_hbm] * n_k))

    return scatter(rows, idx, after)


def _sc_gather_rows(table, idx):
    m = idx.shape[1]
    w = table.shape[1]
    assert m % (SC_WINDOW * SC_CORES * SC_SUBCORES) == 0

    @functools.partial(pl.kernel, out_type=jax.ShapeDtypeStruct((m, w), table.dtype), mesh=_sc_mesh(),
                       scratch_types=[], name="moe_gather_sc")
    def gather(x_hbm, i_hbm, o_hbm):
        def body(i_vmem, o_vmem):
            pltpu.sync_copy(x_hbm.at[i_vmem.at[0]], o_vmem)

        pltpu.emit_pipeline(
            body,
            grid=(m // SC_WINDOW,),
            in_specs=[pl.BlockSpec((1, SC_WINDOW), lambda i: (0, i))],
            out_specs=[pl.BlockSpec((SC_WINDOW, w), lambda i: (i, 0))],
            core_axis_name=("core", "subcore"),
            dimension_semantics=(pltpu.PARALLEL,),
        )(i_hbm, o_hbm)

    return gather(table, idx)


def _sc_pack_rows_bf16(w):
    r, c = w.shape
    assert r % SC_PACK_ROWS == 0 and c % SC_PACK_COLS == 0 and (r // SC_PACK_ROWS) % (SC_CORES * SC_SUBCORES) == 0
    lanes = SC_LANES

    @functools.partial(pl.kernel, out_type=jax.ShapeDtypeStruct((r // 2, c), I32), mesh=_sc_mesh(),
                       scratch_types=[], compiler_params=pltpu.CompilerParams(needs_layout_passes=False),
                       name="expert_weight_pack_sc")
    def pack(x_hbm, o_hbm):
        def body(x_vmem, o_vmem):
            @pl.loop(0, SC_PACK_ROWS // 2)
            def _(i):
                for j in range(0, SC_PACK_COLS, lanes):
                    pair = plsc.pack(x_vmem[2 * i, pl.ds(j, lanes)], x_vmem[2 * i + 1, pl.ds(j, lanes)],
                                     format=plsc.PackFormat.INTERLEAVED)
                    o_vmem[i, pl.ds(j, lanes)] = plsc.bitcast(pair, I32)

        pltpu.emit_pipeline(
            body,
            grid=(r // SC_PACK_ROWS, c // SC_PACK_COLS),
            in_specs=[pl.BlockSpec((SC_PACK_ROWS, SC_PACK_COLS), lambda i, j: (i, j))],
            out_specs=[pl.BlockSpec((SC_PACK_ROWS // 2, SC_PACK_COLS), lambda i, j: (i, j))],
            core_axis_name=("core", "subcore"),
            dimension_semantics=(pltpu.PARALLEL, pltpu.PARALLEL),
        )(x_hbm, o_hbm)

    return pack(w)


def _moe_kernel(be_ref, nb_ref, slot_ref, nxt_ref, nv_ref, xs_ref, wgu_hbm, bgu_ref, wd_hbm, bd_ref, ys_ref,
                wgu_s, wd_s, sems, *, d_ff):
    i = pl.program_id(0)
    live = i < nb_ref[0]
    new_expert = (i == 0) | (be_ref[i] != be_ref[jnp.maximum(i - 1, 0)])
    slot = slot_ref[i]

    def weight_copies(e, s):
        return (pltpu.make_async_copy(wgu_hbm.at[e], wgu_s.at[s], sems.at[0, s]),
                pltpu.make_async_copy(wd_hbm.at[e], wd_s.at[s], sems.at[1, s]))

    @pl.when(live & new_expert)
    def _():
        e, nx = be_ref[i], nxt_ref[i]

        @pl.when(i == 0)
        def _():
            for cp in weight_copies(e, slot):
                cp.start()

        for cp in weight_copies(e, slot):
            cp.wait()

        @pl.when(nx >= 0)
        def _():
            for cp in weight_copies(nx, 1 - slot):
                cp.start()

    def sub_block(s):
        rows = pl.ds(s * MOE_SUB, MOE_SUB)
        packed = jnp.concatenate([xs_ref[p, rows, :] for p in range(SC_SPLIT)], axis=1)
        x_lo, x_hi = _unpack_bf16_pair(packed)
        dh = x_lo.shape[1]
        w_top = pltpu.bitcast(wgu_s[slot, pl.ds(0, dh // 2), :], BF16)
        w_bot = pltpu.bitcast(wgu_s[slot, pl.ds(dh // 2, dh // 2), :], BF16)
        gu = jnp.dot(x_lo.astype(BF16), w_top, preferred_element_type=F32)
        gu += jnp.dot(x_hi.astype(BF16), w_bot, preferred_element_type=F32)
        gu += bgu_ref[...]
        g = jnp.minimum(gu[:, :d_ff], SWIGLU_LIMIT)
        u = jnp.clip(gu[:, d_ff:], -SWIGLU_LIMIT, SWIGLU_LIMIT)
        act = (u + 1.0) * (g * _sigmoid(SWIGLU_ALPHA * g))
        y = jnp.dot(act.astype(BF16), pltpu.bitcast(wd_s[slot], BF16), preferred_element_type=F32) + bd_ref[...]
        dm2 = y.shape[1] // 2
        _store_planes(ys_ref, _pack_bf16_pair(y[:, :dm2], y[:, dm2:]), rows)

    n_sub = xs_ref.shape[1] // MOE_SUB
    n_live_sub = (nv_ref[i] + (MOE_SUB - 1)) // MOE_SUB

    for k in range(1, n_sub + 1):
        @pl.when(live & (n_live_sub == k))
        def _():
            for s in range(k):
                sub_block(s)


def _moe_call(xs, block_expert, n_used, seg_slot, next_expert, block_valid, wgu, bgu, wd, bd):
    _, n_rows, wp = xs.shape
    n_exp, dm_half, d_ff2 = wgu.shape
    dm = 2 * dm_half
    d_ff = d_ff2 // 2
    bm = MOE_BLOCK
    n_blocks = n_rows // bm

    def row_map(i, be, nb, slot, nxt, nv):
        return (0, jnp.minimum(i, nb[0] - 1), 0)

    def b_map(i, be, nb, slot, nxt, nv):
        return (be[jnp.minimum(i, nb[0] - 1)], 0, 0)

    grid_spec = pltpu.PrefetchScalarGridSpec(
        num_scalar_prefetch=5,
        grid=(n_blocks,),
        in_specs=[
            pl.BlockSpec((SC_SPLIT, bm, wp), row_map),
            pl.BlockSpec(memory_space=pl.ANY),
            pl.BlockSpec((None, 1, d_ff2), b_map),
            pl.BlockSpec(memory_space=pl.ANY),
            pl.BlockSpec((None, 1, dm), b_map),
        ],
        out_specs=pl.BlockSpec((SC_SPLIT, bm, wp), row_map),
        scratch_shapes=[
            pltpu.VMEM((2,) + wgu.shape[1:], I32), pltpu.VMEM((2,) + wd.shape[1:], I32),
            pltpu.SemaphoreType.DMA((2, 2)),
        ],
    )
    return pl.pallas_call(
        functools.partial(_moe_kernel, d_ff=d_ff),
        grid_spec=grid_spec,
        out_shape=jax.ShapeDtypeStruct(xs.shape, I32),
        compiler_params=pltpu.CompilerParams(dimension_semantics=("arbitrary",),
                                             vmem_limit_bytes=VMEM_LIMIT_BYTES),
        name="moe_experts",
    )(block_expert, n_used, seg_slot, next_expert, block_valid, xs, wgu, bgu, wd, bd)


def _combine_kernel(x1_ref, gt_ref, mod_ref, gf_ref, yg_ref, *out_refs):
    o_ref = out_refs[-1]
    gt = gt_ref[...]
    ts = yg_ref.shape[2]
    dh = yg_ref.shape[3] * SC_SPLIT
    acc_lo = jnp.zeros((ts, dh), F32)
    acc_hi = jnp.zeros((ts, dh), F32)
    for k in range(TOP_K):
        lo, hi = _unpack_bf16_pair(_load_planes(yg_ref.at[k]))
        gk = gt[:, k:k + 1]
        acc_lo += gk * lo
        acc_hi += gk * hi
    y = jnp.concatenate([acc_lo, acc_hi], axis=1)
    gate2 = mod_ref[...][5:6]
    x2 = x1_ref[...] + gate2 * y
    o_ref[...] = _rms(x2) * gf_ref[...]


def _combine_call(x1, gt, mod, gf, yg, out_prev, b0):
    bsz, seq, dm = x1.shape
    ts = min(COMBINE_TILE, seq)
    assert seq % ts == 0
    nj = seq // ts
    in_specs = [
        pl.BlockSpec((None, ts, dm), lambda b, j: (b, j, 0)),
        pl.BlockSpec((None, ts, LANES), lambda b, j: (b, j, 0)),
        pl.BlockSpec((None, N_MOD, dm), lambda b, j: (b + b0, 0, 0)),
        pl.BlockSpec((1, dm), lambda b, j: (0, 0)),
        pl.BlockSpec((TOP_K, SC_SPLIT, ts, yg.shape[3]), lambda b, j: (0, 0, b * nj + j, 0)),
    ]
    args = [x1, gt, mod, gf, yg]
    aliases = {}
    if out_prev is not None:
        in_specs.append(pl.BlockSpec(memory_space=pl.ANY))
        args.append(out_prev)
        aliases = {len(args) - 1: 0}
    return pl.pallas_call(
        _combine_kernel,
        grid=(bsz, nj),
        in_specs=in_specs,
        out_specs=pl.BlockSpec((None, ts, dm), lambda b, j: (b + b0, j, 0)),
        out_shape=jax.ShapeDtypeStruct((mod.shape[0], seq, dm), F32),
        input_output_aliases=aliases,
        compiler_params=pltpu.CompilerParams(dimension_semantics=("arbitrary", "arbitrary"),
                                             vmem_limit_bytes=VMEM_LIMIT_BYTES),
        name="moe_combine",
    )(*args)


def _block_diag(w):
    n_h, d, _ = w.shape
    eye = jnp.eye(n_h, dtype=w.dtype)
    return (eye[:, None, :, None] * w[:, :, None, :]).reshape(n_h * d, n_h * d)


def _layer(x, mod, l, w_in, norm1_g, conv_w, conv_b, conv_ln_g, conv_ln_b, lru_conv_w, lru_conv_b,
           lru_w_a, lru_b_a, lru_w_x, lru_b_x, lru_lambda, mix_norm_g, w_out, norm2_g, w_router,
           b_router, w_gate_up, b_gate_up, w_down, b_down, out_gain):
    bsz, seq, dm = x.shape
    n_tok = bsz * seq
    n_exp = w_router.shape[-1]
    row = lambda a: a.reshape(1, -1)
    wr = w_router[l]
    wr_hi = wr.astype(BF16)
    wr_lo = (wr - wr_hi.astype(F32)).astype(BF16)
    params = dict(
        g1=row(norm1_g[l]), win=w_in[l].astype(BF16), conv_w=conv_w[l], conv_b=row(conv_b[l]),
        ln_g=row(conv_ln_g[l]), ln_b=row(conv_ln_b[l]), lru_conv_w=lru_conv_w[l], lru_conv_b=row(lru_conv_b[l]),
        wg=jnp.concatenate([_block_diag(lru_w_a[l]), _block_diag(lru_w_x[l])], axis=1).astype(BF16),
        bg=jnp.concatenate([lru_b_a[l].reshape(1, -1), lru_b_x[l].reshape(1, -1)], axis=1),
        lam=row(lru_lambda[l]), mng=row(mix_norm_g[l]), wout=w_out[l].astype(BF16), g2=row(norm2_g[l]),
        wrt=jnp.concatenate([wr_hi.T, wr_lo.T], axis=0), br=b_router[l].reshape(n_exp, 1),
    )
    _, d_in, d_ff2 = w_gate_up[l].shape
    _, d_ff, d_out = w_down[l].shape
    wgu_p = _sc_pack_rows_bf16(w_gate_up[l].reshape(n_exp * d_in, d_ff2)).reshape(n_exp, d_in // 2, d_ff2)
    wd_p = _sc_pack_rows_bf16(w_down[l].reshape(n_exp * d_ff, d_out)).reshape(n_exp, d_ff // 2, d_out)
    sizes = [s for s in CHUNK_BATCHES if s > 0] if sum(CHUNK_BATCHES) == bsz else [bsz]
    routed, b0, anchor = [], 0, mod
    for cb in sizes:
        r = _route_chunk(x, mod, params, b0, cb, anchor, n_exp)
        routed.append(r)
        anchor = r["dest_sub"]
        b0 += cb
    out = None
    sc_order = [wgu_p] + [wd_p] * (len(routed) - 1)
    for r, after in zip(routed, sc_order):
        out = _experts_chunk(r, mod, out, wgu_p, b_gate_up[l][:, None, :], wd_p, b_down[l][:, None, :], out_gain,
                             after)
    return out


def _route_chunk(x, mod, params, b0, cb, anchor, n_exp):
    seq, dm = x.shape[1:]
    n_tok = cb * seq
    x1, h2p, top_idx, rank, gt, cnt = _mix_call(x, mod, params, b0, cb, anchor)

    bm = MOE_BLOCK
    counts = cnt[:, 0].astype(I32)
    padded = (counts + bm - 1) // bm * bm
    e_ids = jnp.arange(n_exp, dtype=I32)
    pad_ends = jnp.sum(jnp.where(e_ids[None, :] <= e_ids[:, None], padded[None, :], 0), axis=1)
    pad_starts = pad_ends - padded
    n_rows = (n_tok * TOP_K // bm + n_exp) * bm
    n_blocks = n_rows // bm
    dest = rank + jnp.sum(jnp.where(top_idx[..., None] == e_ids, pad_starts, 0), axis=-1)
    block_start = jnp.arange(n_blocks, dtype=I32) * bm
    block_expert = jnp.minimum(
        jnp.sum((block_start[:, None] >= pad_ends[None, :]).astype(I32), axis=1), n_exp - 1)
    n_used = pad_ends[-1:] // bm
    present = counts > 0
    seg_ordinal = jnp.sum(jnp.where((e_ids[None, :] < e_ids[:, None]) & present[None, :], 1, 0), axis=1)
    following = jnp.min(jnp.where((e_ids[None, :] > e_ids[:, None]) & present[None, :], e_ids[None, :], n_exp),
                        axis=1)
    following = jnp.where(following == n_exp, -1, following)
    block_hot = block_expert[:, None] == e_ids[None, :]
    seg_slot = jnp.sum(jnp.where(block_hot, seg_ordinal[None, :] % 2, 0), axis=1)
    next_expert = jnp.sum(jnp.where(block_hot, following[None, :], 0), axis=1)
    block_valid = jnp.clip(
        jnp.sum(jnp.where(block_hot, (pad_starts + counts)[None, :], 0), axis=1) - block_start, 0, bm)

    wp = h2p.shape[2]
    plane = jnp.arange(SC_SPLIT, dtype=I32)[None, :, None] * n_rows
    dest_sub = dest[:, None, :] + plane
    return dict(b0=b0, n_tok=n_tok, n_rows=n_rows, x1=x1, h2p=h2p, gt=gt, dest_sub=dest_sub,
                tables=(block_expert, n_used, seg_slot, next_expert, block_valid))


def _experts_chunk(r, mod, out_prev, wgu, bgu, wd, bd, out_gain, dispatch_after):
    n_tok, n_rows, dest_sub = r["n_tok"], r["n_rows"], r["dest_sub"]
    wp = r["h2p"].shape[2]
    xs = _sc_scatter_rows(r["h2p"].reshape(SC_SPLIT * n_tok, wp), dest_sub.reshape(TOP_K, SC_SPLIT * n_tok),
                          SC_SPLIT * n_rows, dispatch_after).reshape(SC_SPLIT, n_rows, wp)
    ys = _moe_call(xs, *r["tables"], wgu, bgu, wd, bd)
    yg = _sc_gather_rows(ys.reshape(SC_SPLIT * n_rows, wp), dest_sub.reshape(1, TOP_K * SC_SPLIT * n_tok))
    return _combine_call(r["x1"], r["gt"], mod, out_gain, yg.reshape(TOP_K, SC_SPLIT, n_tok, wp), out_prev,
                         r["b0"])


def kernel(x, c, w_ada, b_ada, norm1_g, w_in, conv_w, conv_b, conv_ln_g, conv_ln_b, lru_conv_w, lru_conv_b,
           lru_w_a, lru_b_a, lru_w_x, lru_b_x, lru_lambda, mix_norm_g, w_out, norm2_g, w_router, b_router,
           w_gate_up, b_gate_up, w_down, b_down, final_norm_g):
    depth = w_ada.shape[0]
    assert depth == 1, "the final norm is fused into the (single) layer's combine kernel"
    bsz, seq, dm = x.shape
    mod = _ada_call(c, w_ada[0], b_ada[0]).reshape(bsz, N_MOD, dm)
    return _layer(x, mod, 0, w_in, norm1_g, conv_w, conv_b, conv_ln_g, conv_ln_b, lru_conv_w, lru_conv_b,
                  lru_w_a, lru_b_a, lru_w_x, lru_b_x, lru_lambda, mix_norm_g, w_out, norm2_g, w_router,
                  b_router, w_gate_up, b_gate_up, w_down, b_down, row_gain(final_norm_g))


def row_gain(g):
    return g.reshape(1, -1)
```

```python
import functools

import jax
import jax.numpy as jnp
from jax import lax
from jax.experimental import pallas as pl
from jax.experimental.pallas import tpu as pltpu
from jax.experimental.pallas import tpu_sc as plsc

F32 = jnp.float32
BF16 = jnp.bfloat16
I32 = jnp.int32

EPS = 1e-6
N_MOD = 6
LRU_C = 8.0
TOP_K = 4
SWIGLU_ALPHA = 1.702
SWIGLU_LIMIT = 7.0

LANES = 128
SUBLANES = 8
VMEM_LIMIT_BYTES = 56 * 1024 * 1024

SEQ_TILE = 512
TILES_PER_STEP = 2
COMBINE_TILE = 512
CONV_PAD = 32
CONV_ROWS = 128
MOE_BLOCK = 1024
MOE_SUB = 256
CHUNK_BATCHES = (4, 4)
HI_MASK = -65536

SC_CORES = 2
SC_SUBCORES = 16
SC_LANES = 16
SC_PACK_ROWS = 16
SC_PACK_COLS = 512
SC_WINDOW = 128
SC_SPLIT = 2


def _sigmoid(x):
    return 0.5 * jnp.tanh(0.5 * x) + 0.5


def _pack_bf16_pair(lo_f32, hi_f32):
    lo_bits = lax.bitcast_convert_type(lo_f32.astype(BF16).astype(F32), I32)
    hi_bits = lax.bitcast_convert_type(hi_f32.astype(BF16).astype(F32), I32)
    return lax.shift_right_logical(lo_bits, 16) | hi_bits


def _unpack_bf16_pair(p):
    lo = lax.bitcast_convert_type(lax.shift_left(p, 16), F32)
    hi = lax.bitcast_convert_type(p & HI_MASK, F32)
    return lo, hi


def _store_planes(ref, packed, rows=slice(None)):
    wp = packed.shape[1] // SC_SPLIT
    for s in range(SC_SPLIT):
        ref[s, rows, :] = packed[:, s * wp:(s + 1) * wp]


def _load_planes(ref):
    return jnp.concatenate([ref[s] for s in range(SC_SPLIT)], axis=1)


def _ada_kernel(c_ref, w_ref, b_ref, o_ref):
    c = c_ref[...]
    ca = c * _sigmoid(c)
    w = w_ref[...]
    c_hi = ca.astype(BF16)
    c_lo = (ca - c_hi.astype(F32)).astype(BF16)
    w_hi = w.astype(BF16)
    w_lo = (w - w_hi.astype(F32)).astype(BF16)
    acc = jnp.dot(c_hi, w_hi, preferred_element_type=F32)
    acc += jnp.dot(c_lo, w_hi, preferred_element_type=F32)
    acc += jnp.dot(c_hi, w_lo, preferred_element_type=F32)
    o_ref[...] = acc + b_ref[...]


def _ada_call(c, w_ada, b_ada):
    bsz, dm = c.shape
    n_out = w_ada.shape[1]
    tn = 1024
    return pl.pallas_call(
        _ada_kernel,
        grid=(n_out // tn,),
        in_specs=[
            pl.BlockSpec((bsz, dm), lambda n: (0, 0)),
            pl.BlockSpec((dm, tn), lambda n: (0, n)),
            pl.BlockSpec((1, tn), lambda n: (0, n)),
        ],
        out_specs=pl.BlockSpec((bsz, tn), lambda n: (0, n)),
        out_shape=jax.ShapeDtypeStruct((bsz, n_out), F32),
        compiler_params=pltpu.CompilerParams(dimension_semantics=("arbitrary",)),
        name="ada_mod",
    )(c, w_ada, b_ada.reshape(1, n_out))


def _rms(x, eps=EPS):
    return x * lax.rsqrt(jnp.mean(x * x, axis=-1, keepdims=True) + eps)


def _gelu_tanh(x):
    return 0.5 * x * (1.0 + jnp.tanh(0.7978845608028654 * (x + 0.044715 * (x * x * x))))


def _slab_store(buf, row0, val):
    for s in range(buf.shape[0]):
        buf[s, pl.ds(row0, val.shape[0]), :] = val[:, s * LANES:(s + 1) * LANES]


def _slab_keep_tail(buf, keep, ts):
    for s in range(buf.shape[0]):
        buf[s, pl.ds(0, keep), :] = buf[s, pl.ds(ts, keep), :]


def _causal_tap_sum(buf, w_ref, bias_row, first, n_taps, ts):
    w = w_ref[...]
    cols = []
    for s in range(buf.shape[0]):
        lanes = slice(s * LANES, (s + 1) * LANES)
        chunks = []
        for c in range(0, ts, CONV_ROWS):
            acc = jnp.broadcast_to(bias_row[:, lanes], (CONV_ROWS, LANES))
            for k in range(n_taps):
                acc = acc + w[k:k + 1, lanes] * buf[s, pl.ds(c + first + k, CONV_ROWS), :]
            chunks.append(acc)
        cols.append(jnp.concatenate(chunks, axis=0))
    return jnp.concatenate(cols, axis=1)


def _linear_recurrence(a, b, a_s, b_s, hcar):
    ts = a.shape[0]
    blk = ts // SUBLANES
    pitch = blk + 1
    cols = []
    for j in range(a_s.shape[0]):
        lanes = slice(j * LANES, (j + 1) * LANES)
        for k in range(SUBLANES):
            a_s[j, pl.ds(k * pitch, blk), :] = a[k * blk:(k + 1) * blk, lanes]
            b_s[j, pl.ds(k * pitch, blk), :] = b[k * blk:(k + 1) * blk, lanes]
        h = jnp.zeros((SUBLANES, LANES), F32)
        p = jnp.ones((SUBLANES, LANES), F32)
        for i in range(blk):
            row_i = pl.ds(i, SUBLANES, stride=pitch)
            a_i = a_s[j, row_i, :]
            h = a_i * h + b_s[j, row_i, :]
            p = a_i * p
            b_s[j, row_i, :] = h
            a_s[j, row_i, :] = p
        state = hcar[:, lanes]
        entering = []
        for k in range(SUBLANES):
            entering.append(state)
            state = p[k:k + 1, :] * state + h[k:k + 1, :]
        hcar[:, lanes] = state
        h_in = jnp.concatenate(entering, axis=0)
        for i in range(blk):
            row_i = pl.ds(i, SUBLANES, stride=pitch)
            b_s[j, row_i, :] = b_s[j, row_i, :] + a_s[j, row_i, :] * h_in
        cols.append(jnp.concatenate([b_s[j, pl.ds(k * pitch, blk), :] for k in range(SUBLANES)], axis=0))
    return jnp.concatenate(cols, axis=1)


def _mix_kernel(x_ref, xn_ref, mod_ref, modn_ref, g1_ref, win_ref, cw_ref, cb_ref, lng_ref, lnb_ref,
                lcw_ref, lcb_ref, wg_ref, bg_ref, lam_ref, mng_ref, wout_ref, g2_ref,
                wrt_ref, br_ref, anchor_ref,
                x1_ref, h2p_ref, idx_ref, rank_ref, gt_ref, cnt_ref,
                u_a, u_b, vbuf, rbuf, a_s, b_s, hcar, cnt_s,
                *, tile, n_tiles, steps_per_seq, d_conv, d_lru, conv_w, lru_cw, n_exp):
    del anchor_ref
    s = pl.program_id(0)

    @pl.when(s % steps_per_seq == 0)
    def _():
        _slab_store(vbuf, 0, jnp.zeros((CONV_PAD, d_conv), F32))
        _slab_store(rbuf, 0, jnp.zeros((SUBLANES, d_lru), F32))
        hcar[...] = jnp.zeros_like(hcar)

    mod = mod_ref[...]
    modn = modn_ref[...]
    shift1, gate1, shift2 = mod[0:1], mod[2:3], mod[3:4]
    gain1 = g1_ref[...] * (1.0 + mod[1:2])
    gain2 = g2_ref[...] * (1.0 + mod[4:5])
    lam = lam_ref[...]
    softplus_neg_lam = jnp.maximum(-lam, 0.0) + jnp.log1p(jnp.exp(-jnp.abs(lam)))

    def project(src_ref, r0, gain, shift, u_ref):
        h = _rms(src_ref[pl.ds(r0, tile), :]) * gain + shift
        u_ref[...] = jnp.dot(h.astype(BF16), win_ref[...], preferred_element_type=F32)

    def conv_group(r0, u_ref):
        return _mix_conv_group(r0, tile, u_ref, cw_ref, cb_ref, lng_ref, lnb_ref, vbuf, d_conv=d_conv,
                               conv_w=conv_w)

    def finish(r0, u_ref, yc):
        _mix_finish(r0, tile, x_ref, u_ref, yc, gate1, shift2, gain2, softplus_neg_lam,
                    lcw_ref, lcb_ref, wg_ref, bg_ref, mng_ref, wout_ref,
                    wrt_ref, br_ref, x1_ref, h2p_ref, idx_ref, rank_ref, gt_ref, rbuf, a_s, b_s, hcar,
                    cnt_s, d_conv=d_conv, d_lru=d_lru, lru_cw=lru_cw, n_exp=n_exp)

    @pl.when(s == 0)
    def _():
        cnt_s[...] = jnp.zeros_like(cnt_s)
        project(x_ref, 0, gain1, shift1, u_a)

    bufs = (u_a, u_b)
    for q in range(n_tiles):
        cur, nxt = bufs[q % 2], bufs[(q + 1) % 2]
        yc = conv_group(q * tile, cur)
        if q + 1 < n_tiles:
            project(x_ref, (q + 1) * tile, gain1, shift1, nxt)
        else:
            project(xn_ref, 0, g1_ref[...] * (1.0 + modn[1:2]), modn[0:1], nxt)
        finish(q * tile, cur, yc)
    _slab_keep_tail(vbuf, CONV_PAD, n_tiles * tile)
    _slab_keep_tail(rbuf, SUBLANES, n_tiles * tile)
    cnt_ref[...] = cnt_s[:, :LANES]


def _mix_conv_group(r0, ts, u_ref, cw_ref, cb_ref, lng_ref, lnb_ref, vbuf, *, d_conv, conv_w):
    v = u_ref[:, :d_conv] * _sigmoid(u_ref[:, d_conv:2 * d_conv])
    _slab_store(vbuf, CONV_PAD + r0, v)
    acc = _causal_tap_sum(vbuf, cw_ref, cb_ref[...], r0 + CONV_PAD - (conv_w - 1), conv_w, ts)
    mu = jnp.mean(acc, axis=-1, keepdims=True)
    cen = acc - mu
    var = jnp.mean(cen * cen, axis=-1, keepdims=True)
    yc = cen * lax.rsqrt(var + EPS) * lng_ref[...] + lnb_ref[...]
    return yc * _sigmoid(yc)


def _mix_finish(r0, ts, x_ref, u_ref, yc, gate1, shift2, gain2, softplus_neg_lam,
                lcw_ref, lcb_ref, wg_ref, bg_ref, mng_ref, wout_ref,
                wrt_ref, br_ref, x1_ref, h2p_ref, idx_ref, rank_ref, gt_ref, rbuf, a_s, b_s, hcar,
                cnt_s, *, d_conv, d_lru, lru_cw, n_exp):
    rows = pl.ds(r0, ts)
    x = x_ref[rows, :]
    u = u_ref

    u_gate = u[:, 2 * d_conv:2 * d_conv + d_lru]
    _slab_store(rbuf, SUBLANES + r0, u[:, 2 * d_conv + d_lru:])
    xr = _causal_tap_sum(rbuf, lcw_ref, lcb_ref[...], r0 + SUBLANES - (lru_cw - 1), lru_cw, ts)
    gates = jnp.dot(xr.astype(BF16), wg_ref[...], preferred_element_type=F32) + bg_ref[...]
    r = _sigmoid(gates[:, :d_lru])
    i_g = _sigmoid(gates[:, d_lru:])
    log_a = (-LRU_C) * r * softplus_neg_lam
    a = jnp.exp(log_a)
    inp = jnp.sqrt(1.0 - jnp.exp(2.0 * log_a)) * (i_g * xr)

    yl = _gelu_tanh(u_gate) * _linear_recurrence(a, inp, a_s, b_s, hcar)

    mng = mng_ref[...]
    yc_n = _rms(yc) * mng[:, :d_conv]
    yl_n = _rms(yl) * mng[:, d_conv:]
    mixo = jnp.dot(yc_n.astype(BF16), wout_ref[pl.ds(0, d_conv), :], preferred_element_type=F32)
    mixo += jnp.dot(yl_n.astype(BF16), wout_ref[pl.ds(d_conv, d_lru), :], preferred_element_type=F32)
    x1 = x + gate1 * mixo
    x1_ref[rows, :] = x1

    h2 = _rms(x1) * gain2 + shift2
    dh = h2.shape[1] // 2
    h2_hi = h2.astype(BF16)
    _store_planes(h2p_ref, _pack_bf16_pair(h2[:, :dh], h2[:, dh:]), rows)
    h2_lo = (h2 - h2_hi.astype(F32)).astype(BF16)
    nt_dims = (((1,), (1,)), ((), ()))
    wrt = wrt_ref[...]
    lg = lax.dot_general(wrt, h2_hi, nt_dims, preferred_element_type=F32)
    lg2 = lax.dot_general(wrt[:n_exp], h2_lo, nt_dims, preferred_element_type=F32)
    logits = lg[:n_exp] + lg[n_exp:] + lg2 + br_ref[...]

    eidx = lax.broadcasted_iota(I32, (n_exp, ts), 0)
    neg_inf = jnp.float32(-jnp.inf)
    work = logits
    vals, idxs, hots = [], [], []
    for _ in range(TOP_K):
        mval = jnp.max(work, axis=0, keepdims=True)
        midx = jnp.min(jnp.where(work == mval, eidx, n_exp), axis=0, keepdims=True)
        hot = eidx == midx
        vals.append(mval)
        idxs.append(midx)
        hots.append(hot)
        work = jnp.where(hot, neg_inf, work)
    exps = [jnp.exp(vk - vals[0]) for vk in vals]
    denom = exps[0] + exps[1] + exps[2] + exps[3]
    gate_rows = [ek / denom for ek in exps]

    sel = jnp.zeros((n_exp, ts), F32)
    for hot in hots:
        sel = sel + hot.astype(F32)
    tri = (lax.broadcasted_iota(I32, (ts, ts), 0) < lax.broadcasted_iota(I32, (ts, ts), 1)).astype(BF16)
    before = jnp.dot(sel.astype(BF16), tri, preferred_element_type=F32) + cnt_s[...]
    rank_rows = [jnp.sum(jnp.where(hot, before, 0.0), axis=0, keepdims=True) for hot in hots]
    cnt_s[...] = cnt_s[...] + jnp.sum(sel, axis=1, keepdims=True)

    idx_ref[:, rows] = jnp.concatenate(idxs, axis=0)
    rank_ref[:, rows] = jnp.concatenate(rank_rows, axis=0).astype(I32)
    g4 = jnp.concatenate(gate_rows, axis=0)
    g_pad = jnp.concatenate([g4, jnp.zeros((LANES - TOP_K, ts), F32)], axis=0)
    gt_ref[rows, :] = g_pad.T


def _mix_call(x, mod, p, b0, bsz, anchor):
    _, seq, dm = x.shape
    n_tiles = TILES_PER_STEP
    tile = min(SEQ_TILE, seq // n_tiles)
    ts = n_tiles * tile
    nj = seq // ts
    n_steps = bsz * nj
    tiles_per_seq = seq // tile
    d_conv = p["conv_w"].shape[1]
    d_lru = p["lru_conv_w"].shape[1]
    conv_w = p["conv_w"].shape[0]
    lru_cw = p["lru_conv_w"].shape[0]
    n_exp = p["wrt"].shape[0] // 2
    assert seq % ts == 0 and tile % LANES == 0 and tile % CONV_ROWS == 0 and n_tiles % 2 == 0
    assert conv_w - 1 <= CONV_PAD and lru_cw - 1 <= SUBLANES and d_conv % LANES == 0 and d_lru % LANES == 0

    def full(a):
        return pl.BlockSpec(a.shape, lambda s: (0,) * a.ndim)

    def next_tile(s):
        return jnp.minimum(n_tiles * (s + 1), bsz * tiles_per_seq - 1)

    weights = [p["g1"], p["win"], p["conv_w"], p["conv_b"], p["ln_g"], p["ln_b"], p["lru_conv_w"],
               p["lru_conv_b"], p["wg"], p["bg"], p["lam"], p["mng"], p["wout"], p["g2"], p["wrt"], p["br"]]
    kern = functools.partial(_mix_kernel, tile=tile, n_tiles=n_tiles, steps_per_seq=nj, d_conv=d_conv, d_lru=d_lru,
                             conv_w=conv_w, lru_cw=lru_cw, n_exp=n_exp)
    out_shape = (
        jax.ShapeDtypeStruct((bsz, seq, dm), F32),
        jax.ShapeDtypeStruct((SC_SPLIT, bsz * seq, dm // 2 // SC_SPLIT), I32),
        jax.ShapeDtypeStruct((TOP_K, bsz * seq), I32),
        jax.ShapeDtypeStruct((TOP_K, bsz * seq), I32),
        jax.ShapeDtypeStruct((bsz, seq, LANES), F32),
        jax.ShapeDtypeStruct((n_exp, LANES), F32),
    )
    return pl.pallas_call(
        kern,
        grid=(n_steps,),
        in_specs=[pl.BlockSpec((None, ts, dm), lambda s: (s // nj + b0, s % nj, 0)),
                  pl.BlockSpec((None, tile, dm),
                               lambda s: (next_tile(s) // tiles_per_seq + b0, next_tile(s) % tiles_per_seq, 0)),
                  pl.BlockSpec((None, N_MOD, dm), lambda s: (s // nj + b0, 0, 0)),
                  pl.BlockSpec((None, N_MOD, dm), lambda s: (next_tile(s) // tiles_per_seq + b0, 0, 0))]
        + [full(w) for w in weights] + [pl.BlockSpec(memory_space=pl.ANY)],
        out_specs=(
            pl.BlockSpec((None, ts, dm), lambda s: (s // nj, s % nj, 0)),
            pl.BlockSpec((SC_SPLIT, ts, dm // 2 // SC_SPLIT), lambda s: (0, s, 0)),
            pl.BlockSpec((TOP_K, ts), lambda s: (0, s)),
            pl.BlockSpec((TOP_K, ts), lambda s: (0, s)),
            pl.BlockSpec((None, ts, LANES), lambda s: (s // nj, s % nj, 0)),
            pl.BlockSpec((n_exp, LANES), lambda s: (0, 0)),
        ),
        out_shape=out_shape,
        scratch_shapes=[
            pltpu.VMEM((tile, 2 * d_conv + 2 * d_lru), F32),
            pltpu.VMEM((tile, 2 * d_conv + 2 * d_lru), F32),
            pltpu.VMEM((d_conv // LANES, CONV_PAD + ts, LANES), F32),
            pltpu.VMEM((d_lru // LANES, SUBLANES + ts, LANES), F32),
            pltpu.VMEM((d_lru // LANES, tile + SUBLANES, LANES), F32),
            pltpu.VMEM((d_lru // LANES, tile + SUBLANES, LANES), F32),
            pltpu.VMEM((1, d_lru), F32),
            pltpu.VMEM((n_exp, tile), F32),
        ],
        compiler_params=pltpu.CompilerParams(dimension_semantics=("arbitrary",),
                                             vmem_limit_bytes=VMEM_LIMIT_BYTES),
        name="token_mix_route",
    )(x, x, mod, mod, *weights, anchor)


def _sc_mesh():
    return plsc.VectorSubcoreMesh(core_axis_name="core", subcore_axis_name="subcore",
                                  num_cores=SC_CORES, num_subcores=SC_SUBCORES)


def _sc_scatter_rows(rows, idx, n_out, after):
    n, w = rows.shape
    n_k = idx.shape[0]
    assert n % (SC_WINDOW * SC_CORES * SC_SUBCORES) == 0

    @functools.partial(pl.kernel, out_type=jax.ShapeDtypeStruct((n_out, w), rows.dtype), mesh=_sc_mesh(),
                       scratch_types=[], name="moe_dispatch_sc")
    def scatter(x_hbm, i_hbm, after_hbm, o_hbm):
        del after_hbm

        def body(x_vmem, *i_vmems):
            for i_vmem in i_vmems:
                pltpu.sync_copy(x_vmem, o_hbm.at[i_vmem.at[0]])

        pltpu.emit_pipeline(
            body,
            grid=(n // SC_WINDOW,),
            in_specs=[pl.BlockSpec((SC_WINDOW, w), lambda i: (i, 0))]
            + [pl.BlockSpec((1, SC_WINDOW), functools.partial(lambda k, i: (k, i), k)) for k in range(n_k)],
            out_specs=[],
            core_axis_name=("core", "subcore"),
            dimension_semantics=(pltpu.PARALLEL,),
        )(x_hbm, *([i_hbm] * n_k))

    return scatter(rows, idx, after)


def _sc_gather_rows(table, idx):
    m = idx.shape[1]
    w = table.shape[1]
    assert m % (SC_WINDOW * SC_CORES * SC_SUBCORES) == 0

    @functools.partial(pl.kernel, out_type=jax.ShapeDtypeStruct((m, w), table.dtype), mesh=_sc_mesh(),
                       scratch_types=[], name="moe_gather_sc")
    def gather(x_hbm, i_hbm, o_hbm):
        def body(i_vmem, o_vmem):
            pltpu.sync_copy(x_hbm.at[i_vmem.at[0]], o_vmem)

        pltpu.emit_pipeline(
            body,
            grid=(m // SC_WINDOW,),
            in_specs=[pl.BlockSpec((1, SC_WINDOW), lambda i: (0, i))],
            out_specs=[pl.BlockSpec((SC_WINDOW, w), lambda i: (i, 0))],
            core_axis_name=("core", "subcore"),
            dimension_semantics=(pltpu.PARALLEL,),
        )(i_hbm, o_hbm)

    return gather(table, idx)


def _sc_pack_rows_bf16(w):
    r, c = w.shape
    assert r % SC_PACK_ROWS == 0 and c % SC_PACK_COLS == 0 and (r // SC_PACK_ROWS) % (SC_CORES * SC_SUBCORES) == 0
    lanes = SC_LANES

    @functools.partial(pl.kernel, out_type=jax.ShapeDtypeStruct((r // 2, c), I32), mesh=_sc_mesh(),
                       scratch_types=[], compiler_params=pltpu.CompilerParams(needs_layout_passes=False),
                       name="expert_weight_pack_sc")
    def pack(x_hbm, o_hbm):
        def body(x_vmem, o_vmem):
            @pl.loop(0, SC_PACK_ROWS // 2)
            def _(i):
                for j in range(0, SC_PACK_COLS, lanes):
                    pair = plsc.pack(x_vmem[2 * i, pl.ds(j, lanes)], x_vmem[2 * i + 1, pl.ds(j, lanes)],
                                     format=plsc.PackFormat.INTERLEAVED)
                    o_vmem[i, pl.ds(j, lanes)] = plsc.bitcast(pair, I32)

        pltpu.emit_pipeline(
            body,
            grid=(r // SC_PACK_ROWS, c // SC_PACK_COLS),
            in_specs=[pl.BlockSpec((SC_PACK_ROWS, SC_PACK_COLS), lambda i, j: (i, j))],
            out_specs=[pl.BlockSpec((SC_PACK_ROWS // 2, SC_PACK_COLS), lambda i, j: (i, j))],
            core_axis_name=("core", "subcore"),
            dimension_semantics=(pltpu.PARALLEL, pltpu.PARALLEL),
        )(x_hbm, o_hbm)

    return pack(w)


def _moe_kernel(be_ref, nb_ref, slot_ref, nxt_ref, nv_ref, xs_ref, wgu_hbm, bgu_ref, wd_hbm, bd_ref, ys_ref,
                wgu_s, wd_s, sems, *, d_ff):
    i = pl.program_id(0)
    live = i < nb_ref[0]
    new_expert = (i == 0) | (be_ref[i] != be_ref[jnp.maximum(i - 1, 0)])
    slot = slot_ref[i]

    def weight_copies(e, s):
        return (pltpu.make_async_copy(wgu_hbm.at[e], wgu_s.at[s], sems.at[0, s]),
                pltpu.make_async_copy(wd_hbm.at[e], wd_s.at[s], sems.at[1, s]))

    @pl.when(live & new_expert)
    def _():
        e, nx = be_ref[i], nxt_ref[i]

        @pl.when(i == 0)
        def _():
            for cp in weight_copies(e, slot):
                cp.start()

        for cp in weight_copies(e, slot):
            cp.wait()

        @pl.when(nx >= 0)
        def _():
            for cp in weight_copies(nx, 1 - slot):
                cp.start()

    def sub_block(s):
        rows = pl.ds(s * MOE_SUB, MOE_SUB)
        packed = jnp.concatenate([xs_ref[p, rows, :] for p in range(SC_SPLIT)], axis=1)
        x_lo, x_hi = _unpack_bf16_pair(packed)
        dh = x_lo.shape[1]
        w_top = pltpu.bitcast(wgu_s[slot, pl.ds(0, dh // 2), :], BF16)
        w_bot = pltpu.bitcast(wgu_s[slot, pl.ds(dh // 2, dh // 2), :], BF16)
        gu = jnp.dot(x_lo.astype(BF16), w_top, preferred_element_type=F32)
        gu += jnp.dot(x_hi.astype(BF16), w_bot, preferred_element_type=F32)
        gu += bgu_ref[...]
        g = jnp.minimum(gu[:, :d_ff], SWIGLU_LIMIT)
        u = jnp.clip(gu[:, d_ff:], -SWIGLU_LIMIT, SWIGLU_LIMIT)
        act = (u + 1.0) * (g * _sigmoid(SWIGLU_ALPHA * g))
        y = jnp.dot(act.astype(BF16), pltpu.bitcast(wd_s[slot], BF16), preferred_element_type=F32) + bd_ref[...]
        dm2 = y.shape[1] // 2
        _store_planes(ys_ref, _pack_bf16_pair(y[:, :dm2], y[:, dm2:]), rows)

    n_sub = xs_ref.shape[1] // MOE_SUB
    n_live_sub = (nv_ref[i] + (MOE_SUB - 1)) // MOE_SUB

    for k in range(1, n_sub + 1):
        @pl.when(live & (n_live_sub == k))
        def _():
            for s in range(k):
                sub_block(s)


def _moe_call(xs, block_expert, n_used, seg_slot, next_expert, block_valid, wgu, bgu, wd, bd):
    _, n_rows, wp = xs.shape
    n_exp, dm_half, d_ff2 = wgu.shape
    dm = 2 * dm_half
    d_ff = d_ff2 // 2
    bm = MOE_BLOCK
    n_blocks = n_rows // bm

    def row_map(i, be, nb, slot, nxt, nv):
        return (0, jnp.minimum(i, nb[0] - 1), 0)

    def b_map(i, be, nb, slot, nxt, nv):
        return (be[jnp.minimum(i, nb[0] - 1)], 0, 0)

    grid_spec = pltpu.PrefetchScalarGridSpec(
        num_scalar_prefetch=5,
        grid=(n_blocks,),
        in_specs=[
            pl.BlockSpec((SC_SPLIT, bm, wp), row_map),
            pl.BlockSpec(memory_space=pl.ANY),
            pl.BlockSpec((None, 1, d_ff2), b_map),
            pl.BlockSpec(memory_space=pl.ANY),
            pl.BlockSpec((None, 1, dm), b_map),
        ],
        out_specs=pl.BlockSpec((SC_SPLIT, bm, wp), row_map),
        scratch_shapes=[
            pltpu.VMEM((2,) + wgu.shape[1:], I32), pltpu.VMEM((2,) + wd.shape[1:], I32),
            pltpu.SemaphoreType.DMA((2, 2)),
        ],
    )
    return pl.pallas_call(
        functools.partial(_moe_kernel, d_ff=d_ff),
        grid_spec=grid_spec,
        out_shape=jax.ShapeDtypeStruct(xs.shape, I32),
        compiler_params=pltpu.CompilerParams(dimension_semantics=("arbitrary",),
                                             vmem_limit_bytes=VMEM_LIMIT_BYTES),
        name="moe_experts",
    )(block_expert, n_used, seg_slot, next_expert, block_valid, xs, wgu, bgu, wd, bd)


def _combine_kernel(x1_ref, gt_ref, mod_ref, gf_ref, yg_ref, *out_refs):
    o_ref = out_refs[-1]
    gt = gt_ref[...]
    ts = yg_ref.shape[2]
    dh = yg_ref.shape[3] * SC_SPLIT
    acc_lo = jnp.zeros((ts, dh), F32)
    acc_hi = jnp.zeros((ts, dh), F32)
    for k in range(TOP_K):
        lo, hi = _unpack_bf16_pair(_load_planes(yg_ref.at[k]))
        gk = gt[:, k:k + 1]
        acc_lo += gk * lo
        acc_hi += gk * hi
    y = jnp.concatenate([acc_lo, acc_hi], axis=1)
    gate2 = mod_ref[...][5:6]
    x2 = x1_ref[...] + gate2 * y
    o_ref[...] = _rms(x2) * gf_ref[...]


def _combine_call(x1, gt, mod, gf, yg, out_prev, b0):
    bsz, seq, dm = x1.shape
    ts = min(COMBINE_TILE, seq)
    assert seq % ts == 0
    nj = seq // ts
    in_specs = [
        pl.BlockSpec((None, ts, dm), lambda b, j: (b, j, 0)),
        pl.BlockSpec((None, ts, LANES), lambda b, j: (b, j, 0)),
        pl.BlockSpec((None, N_MOD, dm), lambda b, j: (b + b0, 0, 0)),
        pl.BlockSpec((1, dm), lambda b, j: (0, 0)),
        pl.BlockSpec((TOP_K, SC_SPLIT, ts, yg.shape[3]), lambda b, j: (0, 0, b * nj + j, 0)),
    ]
    args = [x1, gt, mod, gf, yg]
    aliases = {}
    if out_prev is not None:
        in_specs.append(pl.BlockSpec(memory_space=pl.ANY))
        args.append(out_prev)
        aliases = {len(args) - 1: 0}
    return pl.pallas_call(
        _combine_kernel,
        grid=(bsz, nj),
        in_specs=in_specs,
        out_specs=pl.BlockSpec((None, ts, dm), lambda b, j: (b + b0, j, 0)),
        out_shape=jax.ShapeDtypeStruct((mod.shape[0], seq, dm), F32),
        input_output_aliases=aliases,
        compiler_params=pltpu.CompilerParams(dimension_semantics=("arbitrary", "arbitrary"),
                                             vmem_limit_bytes=VMEM_LIMIT_BYTES),
        name="moe_combine",
    )(*args)


def _block_diag(w):
    n_h, d, _ = w.shape
    eye = jnp.eye(n_h, dtype=w.dtype)
    return (eye[:, None, :, None] * w[:, :, None, :]).reshape(n_h * d, n_h * d)


def _layer(x, mod, l, w_in, norm1_g, conv_w, conv_b, conv_ln_g, conv_ln_b, lru_conv_w, lru_conv_b,
           lru_w_a, lru_b_a, lru_w_x, lru_b_x, lru_lambda, mix_norm_g, w_out, norm2_g, w_router,
           b_router, w_gate_up, b_gate_up, w_down, b_down, out_gain):
    bsz, seq, dm = x.shape
    n_tok = bsz * seq
    n_exp = w_router.shape[-1]
    row = lambda a: a.reshape(1, -1)
    wr = w_router[l]
    wr_hi = wr.astype(BF16)
    wr_lo = (wr - wr_hi.astype(F32)).astype(BF16)
    params = dict(
        g1=row(norm1_g[l]), win=w_in[l].astype(BF16), conv_w=conv_w[l], conv_b=row(conv_b[l]),
        ln_g=row(conv_ln_g[l]), ln_b=row(conv_ln_b[l]), lru_conv_w=lru_conv_w[l], lru_conv_b=row(lru_conv_b[l]),
        wg=jnp.concatenate([_block_diag(lru_w_a[l]), _block_diag(lru_w_x[l])], axis=1).astype(BF16),
        bg=jnp.concatenate([lru_b_a[l].reshape(1, -1), lru_b_x[l].reshape(1, -1)], axis=1),
        lam=row(lru_lambda[l]), mng=row(mix_norm_g[l]), wout=w_out[l].astype(BF16), g2=row(norm2_g[l]),
        wrt=jnp.concatenate([wr_hi.T, wr_lo.T], axis=0), br=b_router[l].reshape(n_exp, 1),
    )
    _, d_in, d_ff2 = w_gate_up[l].shape
    _, d_ff, d_out = w_down[l].shape
    wgu_p = _sc_pack_rows_bf16(w_gate_up[l].reshape(n_exp * d_in, d_ff2)).reshape(n_exp, d_in // 2, d_ff2)
    wd_p = _sc_pack_rows_bf16(w_down[l].reshape(n_exp * d_ff, d_out)).reshape(n_exp, d_ff // 2, d_out)
    sizes = [s for s in CHUNK_BATCHES if s > 0] if sum(CHUNK_BATCHES) == bsz else [bsz]
    routed, b0, anchor = [], 0, mod
    for cb in sizes:
        r = _route_chunk(x, mod, params, b0, cb, anchor, n_exp)
        routed.append(r)
        anchor = r["dest_sub"]
        b0 += cb
    out = None
    sc_order = [wgu_p] + [wd_p] * (len(routed) - 1)
    for r, after in zip(routed, sc_order):
        out = _experts_chunk(r, mod, out, wgu_p, b_gate_up[l][:, None, :], wd_p, b_down[l][:, None, :], out_gain,
                             after)
    return out


def _route_chunk(x, mod, params, b0, cb, anchor, n_exp):
    seq, dm = x.shape[1:]
    n_tok = cb * seq
    x1, h2p, top_idx, rank, gt, cnt = _mix_call(x, mod, params, b0, cb, anchor)

    bm = MOE_BLOCK
    counts = cnt[:, 0].astype(I32)
    padded = (counts + bm - 1) // bm * bm
    e_ids = jnp.arange(n_exp, dtype=I32)
    pad_ends = jnp.sum(jnp.where(e_ids[None, :] <= e_ids[:, None], padded[None, :], 0), axis=1)
    pad_starts = pad_ends - padded
    n_rows = (n_tok * TOP_K // bm + n_exp) * bm
    n_blocks = n_rows // bm
    dest = rank + jnp.sum(jnp.where(top_idx[..., None] == e_ids, pad_starts, 0), axis=-1)
    block_start = jnp.arange(n_blocks, dtype=I32) * bm
    block_expert = jnp.minimum(
        jnp.sum((block_start[:, None] >= pad_ends[None, :]).astype(I32), axis=1), n_exp - 1)
    n_used = pad_ends[-1:] // bm
    present = counts > 0
    seg_ordinal = jnp.sum(jnp.where((e_ids[None, :] < e_ids[:, None]) & present[None, :], 1, 0), axis=1)
    following = jnp.min(jnp.where((e_ids[None, :] > e_ids[:, None]) & present[None, :], e_ids[None, :], n_exp),
                        axis=1)
    following = jnp.where(following == n_exp, -1, following)
    block_hot = block_expert[:, None] == e_ids[None, :]
    seg_slot = jnp.sum(jnp.where(block_hot, seg_ordinal[None, :] % 2, 0), axis=1)
    next_expert = jnp.sum(jnp.where(block_hot, following[None, :], 0), axis=1)
    block_valid = jnp.clip(
        jnp.sum(jnp.where(block_hot, (pad_starts + counts)[None, :], 0), axis=1) - block_start, 0, bm)

    wp = h2p.shape[2]
    plane = jnp.arange(SC_SPLIT, dtype=I32)[None, :, None] * n_rows
    dest_sub = dest[:, None, :] + plane
    return dict(b0=b0, n_tok=n_tok, n_rows=n_rows, x1=x1, h2p=h2p, gt=gt, dest_sub=dest_sub,
                tables=(block_expert, n_used, seg_slot, next_expert, block_valid))


def _experts_chunk(r, mod, out_prev, wgu, bgu, wd, bd, out_gain, dispatch_after):
    n_tok, n_rows, dest_sub = r["n_tok"], r["n_rows"], r["dest_sub"]
    wp = r["h2p"].shape[2]
    xs = _sc_scatter_rows(r["h2p"].reshape(SC_SPLIT * n_tok, wp), dest_sub.reshape(TOP_K, SC_SPLIT * n_tok),
                          SC_SPLIT * n_rows, dispatch_after).reshape(SC_SPLIT, n_rows, wp)
    ys = _moe_call(xs, *r["tables"], wgu, bgu, wd, bd)
    yg = _sc_gather_rows(ys.reshape(SC_SPLIT * n_rows, wp), dest_sub.reshape(1, TOP_K * SC_SPLIT * n_tok))
    return _combine_call(r["x1"], r["gt"], mod, out_gain, yg.reshape(TOP_K, SC_SPLIT, n_tok, wp), out_prev,
                         r["b0"])


def kernel(x, c, w_ada, b_ada, norm1_g, w_in, conv_w, conv_b, conv_ln_g, conv_ln_b, lru_conv_w, lru_conv_b,
           lru_w_a, lru_b_a, lru_w_x, lru_b_x, lru_lambda, mix_norm_g, w_out, norm2_g, w_router, b_router,
           w_gate_up, b_gate_up, w_down, b_down, final_norm_g):
    depth = w_ada.shape[0]
    assert depth == 1, "the final norm is fused into the (single) layer's combine kernel"
    bsz, seq, dm = x.shape
    mod = _ada_call(c, w_ada[0], b_ada[0]).reshape(bsz, N_MOD, dm)
    return _layer(x, mod, 0, w_in, norm1_g, conv_w, conv_b, conv_ln_g, conv_ln_b, lru_conv_w, lru_conv_b,
                  lru_w_a, lru_b_a, lru_w_x, lru_b_x, lru_lambda, mix_norm_g, w_out, norm2_g, w_router,
                  b_router, w_gate_up, b_gate_up, w_down, b_down, row_gain(final_norm_g))


def row_gain(g):
    return g.reshape(1, -1)
```
